```python
import math
import jax, jax.numpy as jnp
from jax import lax
import numpy as np

D_MODEL = 2048
BATCH = 8
SEQ = 2048
DEPTH = 2

HEAD_DIM = 64
BLOCK_Q = 128
EPS = 1e-6
NEG = -1e30
BIG = 1e4
N_BUCKETS = 32
MAX_DISTANCE = 128
A_HEADS = 16
A_KV_HEADS = 2
A_WINDOW = 128
B_HEADS = 16
B_KV_GROUPS = 2
CMP_BLOCK = 32
CMP_STRIDE = 16
CMP_HIDDEN = 256
SLC_BLOCK = 64
SLC_TOPK = 8
B_WINDOW = 512
N_GATES = 3
BIAS_HEADS = A_HEADS + B_HEADS
A_Q = A_HEADS * HEAD_DIM
A_KV = A_KV_HEADS * HEAD_DIM
B_Q = B_HEADS * HEAD_DIM
B_KV = B_KV_GROUPS * HEAD_DIM
EVEN_SPLITS = [A_Q, A_KV, A_KV, B_Q, B_KV, B_KV, B_KV, B_KV, B_KV, B_KV, B_HEADS * N_GATES]
EVEN_IN = sum(EVEN_SPLITS)
EVEN_OUT = A_Q + B_Q
C_HEADS = 16
Q_LORA = 768
KV_LORA = 512
NOPE_DIM = 128
ROPE_DIM = 64
V_DIM = 128
ROPE_THETA = 10000.0
ODD_IN = Q_LORA + KV_LORA + ROPE_DIM
ODD_OUT = C_HEADS * V_DIM
D_FF = 4 * D_MODEL

kernel_name = "hybrid_swa_nsa_mla_block"


def rmsnorm(x, g):
    xf = x.astype(jnp.float32)
    y = xf * lax.rsqrt(jnp.mean(xf * xf, axis=-1, keepdims=True) + EPS)
    return (y * g.astype(jnp.float32)).astype(x.dtype)


def t5_bucket(dist):
    dist = jnp.maximum(dist, 0)
    max_exact = N_BUCKETS // 2
    d = jnp.maximum(dist, 1).astype(jnp.float32)
    large = max_exact + (jnp.log(d / max_exact) / math.log(MAX_DISTANCE / max_exact)
                         * (N_BUCKETS - max_exact)).astype(jnp.int32)
    large = jnp.minimum(large, N_BUCKETS - 1)
    return jnp.where(dist < max_exact, dist, large)


def softmax_f32(logits):
    return jax.nn.softmax(logits.astype(jnp.float32), axis=-1)


def band_keys(k, n_prev):
    b, s, g, d = k.shape
    nb = s // BLOCK_Q
    kb = k.reshape(b, nb, BLOCK_Q, g, d)
    kp = jnp.pad(kb, ((0, 0), (n_prev, 0), (0, 0), (0, 0), (0, 0)))
    return jnp.concatenate([kp[:, j:j + nb] for j in range(n_prev + 1)], axis=2)


def banded_gqa(q, k, v, window, bias_table, sinks=None):
    b, s, h, d = q.shape
    g = k.shape[2]
    hpg = h // g
    nb = s // BLOCK_Q
    n_prev = (window - 1 + BLOCK_Q - 1) // BLOCK_Q
    kl = (n_prev + 1) * BLOCK_Q
    kb = band_keys(k, n_prev)
    vb = band_keys(v, n_prev)
    qb = q.reshape(b, nb, BLOCK_Q, g, hpg, d)
    logits = jnp.einsum('bnqgjd,bnkgd->bngjqk', qb, kb).astype(jnp.float32) * (d ** -0.5)
    qpos = jnp.arange(nb)[:, None] * BLOCK_Q + jnp.arange(BLOCK_Q)[None]
    kpos = (jnp.arange(nb)[:, None] - n_prev) * BLOCK_Q + jnp.arange(kl)[None]
    dist = qpos[:, :, None] - kpos[:, None, :]
    valid = (dist >= 0) & (dist < window) & (kpos[:, None, :] >= 0)
    bias = bias_table[t5_bucket(dist)].astype(jnp.float32)
    bias = bias.reshape(nb, BLOCK_Q, kl, g, hpg).transpose(0, 3, 4, 1, 2)
    logits = jnp.where(valid[:, None, None], logits + bias, NEG)
    if sinks is not None:
        sink = sinks.astype(jnp.float32).reshape(g, hpg)[None, None, :, :, None, None]
        sink = jnp.broadcast_to(sink, logits.shape[:-1] + (1,))
        p = softmax_f32(jnp.concatenate([logits, sink], axis=-1))[..., :-1]
    else:
        p = softmax_f32(logits)
    out = jnp.einsum('bngjqk,bnkgd->bnqgjd', p.astype(v.dtype), vb)
    return out.reshape(b, s, h, d)


def compress_blocks(k, pos_emb, w1, w2):
    b, s, g, d = k.shape
    nc = (s - CMP_BLOCK) // CMP_STRIDE + 1
    idx = jnp.arange(nc)[:, None] * CMP_STRIDE + jnp.arange(CMP_BLOCK)[None]
    blocks = k[:, idx] + pos_emb[:, None, :]
    blocks = blocks.transpose(0, 1, 3, 2, 4).reshape(b, nc, g, CMP_BLOCK * d)
    return jax.nn.gelu(blocks @ w1) @ w2


def nsa_attention(q, kc_raw, vc_raw, ks, vs, kw, vw, gates, bias_table,
                  cmp_pos_k, cmp_pos_v, cmp_k_w1, cmp_k_w2, cmp_v_w1, cmp_v_w2):
    b, s, h, d = q.shape
    g = kc_raw.shape[2]
    hpg = h // g
    scale = d ** -0.5
    tpos = jnp.arange(s)
    kc = compress_blocks(kc_raw, cmp_pos_k, cmp_k_w1, cmp_k_w2)
    vc = compress_blocks(vc_raw, cmp_pos_v, cmp_v_w1, cmp_v_w2)
    nc = kc.shape[1]
    qg = q.reshape(b, s, g, hpg, d)
    logits = jnp.einsum('bsgjd,bcgd->bgjsc', qg, kc).astype(jnp.float32) * scale
    cend = jnp.arange(nc) * CMP_STRIDE + CMP_BLOCK - 1
    cdist = tpos[:, None] - cend[None, :]
    cbias = bias_table[t5_bucket(cdist)].astype(jnp.float32)
    cbias = cbias.reshape(s, nc, g, hpg).transpose(2, 3, 0, 1)
    logits = jnp.where(cdist >= 0, logits + cbias, NEG)
    any_visible = (tpos >= CMP_BLOCK - 1).astype(jnp.float32)[:, None]
    p_cmp = softmax_f32(logits) * any_visible
    o_cmp = jnp.einsum('bgjsc,bcgd->bsgjd', p_cmp.astype(vc.dtype), vc).reshape(b, s, h, d)
    ns = s // SLC_BLOCK
    c_start = np.arange(nc) * CMP_STRIDE
    s_start = np.arange(ns) * SLC_BLOCK
    overlap = ((c_start[:, None] <= s_start[None] + SLC_BLOCK - 1) &
               (c_start[:, None] + CMP_BLOCK - 1 >= s_start[None])).astype(np.float32)
    p_slc = jnp.einsum('bgsc,cn->bgsn', p_cmp.sum(axis=2), jnp.asarray(overlap))
    blk = jnp.arange(ns)[None, :]
    cur = tpos[:, None] // SLC_BLOCK
    forced = (blk == 0) | (blk == cur) | (blk == cur - 1)
    future = blk * SLC_BLOCK > tpos[:, None]
    score = jnp.where(future, NEG, jnp.where(forced, BIG, p_slc))
    k_eff = min(SLC_TOPK, ns)
    _, sel = lax.top_k(score, k_eff)
    kblk = ks.reshape(b, ns, SLC_BLOCK, g, d).transpose(0, 3, 1, 2, 4)
    vblk = vs.reshape(b, ns, SLC_BLOCK, g, d).transpose(0, 3, 1, 2, 4)
    bi = jnp.arange(b)[:, None, None, None]
    gi = jnp.arange(g)[None, :, None, None]
    table_g = bias_table.reshape(N_BUCKETS, g, hpg)
    nb = s // BLOCK_Q

    def slc_chunk(args):
        qc, selc, qpos = args
        kg = kblk[bi, gi, selc]
        vg = vblk[bi, gi, selc]
        lg = jnp.einsum('bqgjd,bgqkld->bgjqkl', qc, kg).astype(jnp.float32) * scale
        kpos = selc[..., None] * SLC_BLOCK + jnp.arange(SLC_BLOCK)
        dist = qpos[None, None, :, None, None] - kpos
        bias = table_g[t5_bucket(dist), gi[..., None]].astype(jnp.float32)
        bias = bias.transpose(0, 1, 5, 2, 3, 4)
        lg = jnp.where((dist >= 0)[:, :, None], lg + bias, NEG)
        shp = lg.shape
        p = softmax_f32(lg.reshape(shp[:4] + (shp[4] * shp[5],))).reshape(shp)
        return jnp.einsum('bgjqkl,bgqkld->bqgjd', p.astype(vg.dtype), vg)

    q_chunks = qg.reshape(b, nb, BLOCK_Q, g, hpg, d).transpose(1, 0, 2, 3, 4, 5)
    sel_chunks = sel.reshape(b, g, nb, BLOCK_Q, k_eff).transpose(2, 0, 1, 3, 4)
    pos_chunks = tpos.reshape(nb, BLOCK_Q)
    o_slc = lax.map(slc_chunk, (q_chunks, sel_chunks, pos_chunks))
    o_slc = o_slc.transpose(1, 0, 2, 3, 4, 5).reshape(b, s, h, d)
    o_win = banded_gqa(q, kw, vw, B_WINDOW, bias_table)
    gt = jax.nn.sigmoid(gates.astype(jnp.float32)).astype(q.dtype)
    return gt[..., 0:1] * o_cmp + gt[..., 1:2] * o_slc + gt[..., 2:3] * o_win


def even_mixer(h, w_in, sinks, rel_bias, cmp_pos_k, cmp_pos_v, cmp_k_w1, cmp_k_w2,
               cmp_v_w1, cmp_v_w2, w_out):
    b, s, _ = h.shape
    proj = h @ w_in
    cuts = [int(c) for c in np.cumsum(EVEN_SPLITS)[:-1]]
    qa, ka, va, qb, kcb, vcb, ksb, vsb, kwb, vwb, gate = jnp.split(proj, cuts, axis=-1)
    heads = lambda t, n: t.reshape(b, s, n, HEAD_DIM)
    o_a = banded_gqa(heads(qa, A_HEADS), heads(ka, A_KV_HEADS), heads(va, A_KV_HEADS),
                     A_WINDOW, rel_bias[:, :A_HEADS], sinks)
    o_b = nsa_attention(heads(qb, B_HEADS), heads(kcb, B_KV_GROUPS), heads(vcb, B_KV_GROUPS),
                        heads(ksb, B_KV_GROUPS), heads(vsb, B_KV_GROUPS),
                        heads(kwb, B_KV_GROUPS), heads(vwb, B_KV_GROUPS),
                        gate.reshape(b, s, B_HEADS, N_GATES), rel_bias[:, A_HEADS:],
                        cmp_pos_k, cmp_pos_v, cmp_k_w1, cmp_k_w2, cmp_v_w1, cmp_v_w2)
    o = jnp.concatenate([o_a.reshape(b, s, A_Q), o_b.reshape(b, s, B_Q)], axis=-1)
    return o @ w_out


def apply_rope(x, cos, sin):
    half = x.shape[-1] // 2
    x1, x2 = x[..., :half], x[..., half:]
    return jnp.concatenate([x1 * cos - x2 * sin, x2 * cos + x1 * sin], axis=-1).astype(x.dtype)


def odd_mixer(h, w_in, q_norm, w_q_up, kv_norm, w_kv_up, w_out):
    b, s, _ = h.shape
    proj = h @ w_in
    cq, ckv, k_rope = jnp.split(proj, [Q_LORA, Q_LORA + KV_LORA], axis=-1)
    q = (rmsnorm(cq, q_norm) @ w_q_up).reshape(b, s, C_HEADS, NOPE_DIM + ROPE_DIM)
    q_nope, q_rope = q[..., :NOPE_DIM], q[..., NOPE_DIM:]
    kv = (rmsnorm(ckv, kv_norm) @ w_kv_up).reshape(b, s, C_HEADS, NOPE_DIM + V_DIM)
    k_nope, v = kv[..., :NOPE_DIM], kv[..., NOPE_DIM:]
    inv = 1.0 / (ROPE_THETA ** (jnp.arange(0, ROPE_DIM, 2, dtype=jnp.float32) / ROPE_DIM))
    ang = jnp.arange(s, dtype=jnp.float32)[:, None] * inv[None]
    cos, sin = jnp.cos(ang), jnp.sin(ang)
    q_rope = apply_rope(q_rope, cos[None, :, None, :], sin[None, :, None, :])
    k_rope = apply_rope(k_rope, cos[None], sin[None])
    scale = (NOPE_DIM + ROPE_DIM) ** -0.5
    kpos = jnp.arange(s)
    nb = s // BLOCK_Q

    def att_chunk(args):
        qn, qr, qpos = args
        lg = (jnp.einsum('bqhd,bkhd->bhqk', qn, k_nope) +
              jnp.einsum('bqhr,bkr->bhqk', qr, k_rope)).astype(jnp.float32) * scale
        lg = jnp.where(kpos[None, :] <= qpos[:, None], lg, NEG)
        p = softmax_f32(lg)
        return jnp.einsum('bhqk,bkhd->bqhd', p.astype(v.dtype), v)

    qn_c = q_nope.reshape(b, nb, BLOCK_Q, C_HEADS, NOPE_DIM).transpose(1, 0, 2, 3, 4)
    qr_c = q_rope.reshape(b, nb, BLOCK_Q, C_HEADS, ROPE_DIM).transpose(1, 0, 2, 3, 4)
    o = lax.map(att_chunk, (qn_c, qr_c, kpos.reshape(nb, BLOCK_Q)))
    o = o.transpose(1, 0, 2, 3, 4).reshape(b, s, ODD_OUT)
    return o @ w_out


def sqrelu_mlp(h, w_up, w_down):
    return jnp.square(jax.nn.relu(h @ w_up)) @ w_down


def setup_inputs(seed: int = 0) -> dict:
    key = jax.random.key(seed)
    ks = iter(jax.random.split(key, 40))
    ne = (DEPTH + 1) // 2
    no = DEPTH // 2
    f32 = jnp.float32

    def nrm(shape, fan_in):
        return jax.random.normal(next(ks), shape, f32) * (fan_in ** -0.5)

    def gain(shape):
        return 1.0 + 0.02 * jax.random.normal(next(ks), shape, f32)

    def small(shape, s):
        return s * jax.random.normal(next(ks), shape, f32)

    return {
        "x": jax.random.normal(next(ks), (BATCH, SEQ, D_MODEL), f32),
        "rel_bias": small((N_BUCKETS, BIAS_HEADS), 0.5),
        "norm_mix_e": gain((ne, D_MODEL)),
        "w_in_e": nrm((ne, D_MODEL, EVEN_IN), D_MODEL),
        "sinks": small((ne, A_HEADS), 0.5),
        "cmp_pos_k": small((ne, CMP_BLOCK, HEAD_DIM), 0.1),
        "cmp_pos_v": small((ne, CMP_BLOCK, HEAD_DIM), 0.1),
        "cmp_k_w1": nrm((ne, CMP_BLOCK * HEAD_DIM, CMP_HIDDEN), CMP_BLOCK * HEAD_DIM),
        "cmp_k_w2": nrm((ne, CMP_HIDDEN, HEAD_DIM), CMP_HIDDEN),
        "cmp_v_w1": nrm((ne, CMP_BLOCK * HEAD_DIM, CMP_HIDDEN), CMP_BLOCK * HEAD_DIM),
        "cmp_v_w2": nrm((ne, CMP_HIDDEN, HEAD_DIM), CMP_HIDDEN),
        "w_out_e": nrm((ne, EVEN_OUT, D_MODEL), EVEN_OUT),
        "norm_mix_o": gain((no, D_MODEL)),
        "w_in_o": nrm((no, D_MODEL, ODD_IN), D_MODEL),
        "q_norm": gain((no, Q_LORA)),
        "w_q_up": nrm((no, Q_LORA, C_HEADS * (NOPE_DIM + ROPE_DIM)), Q_LORA),
        "kv_norm": gain((no, KV_LORA)),
        "w_kv_up": nrm((no, KV_LORA, C_HEADS * (NOPE_DIM + V_DIM)), KV_LORA),
        "w_out_o": nrm((no, ODD_OUT, D_MODEL), ODD_OUT),
        "norm_mlp": gain((DEPTH, D_MODEL)),
        "w_up": nrm((DEPTH, D_MODEL, D_FF), D_MODEL),
        "w_down": nrm((DEPTH, D_FF, D_MODEL), D_FF),
        "norm_final": gain((D_MODEL,)),
    }


def reference(x, rel_bias, norm_mix_e, w_in_e, sinks, cmp_pos_k, cmp_pos_v, cmp_k_w1,
              cmp_k_w2, cmp_v_w1, cmp_v_w2, w_out_e, norm_mix_o, w_in_o, q_norm, w_q_up,
              kv_norm, w_kv_up, w_out_o, norm_mlp, w_up, w_down, norm_final):
    for layer in range(DEPTH):
        i = layer // 2
        if layer % 2 == 0:
            h = rmsnorm(x, norm_mix_e[i])
            x = x + even_mixer(h, w_in_e[i], sinks[i], rel_bias, cmp_pos_k[i], cmp_pos_v[i],
                               cmp_k_w1[i], cmp_k_w2[i], cmp_v_w1[i], cmp_v_w2[i], w_out_e[i])
        else:
            h = rmsnorm(x, norm_mix_o[i])
            x = x + odd_mixer(h, w_in_o[i], q_norm[i], w_q_up[i], kv_norm[i], w_kv_up[i],
                              w_out_o[i])
        x = x + sqrelu_mlp(rmsnorm(x, norm_mlp[layer]), w_up[layer], w_down[layer])
    return rmsnorm(x, norm_final)
```

```python
import functools
import math

import numpy as np
import jax
import jax.numpy as jnp
from jax import lax
from jax.experimental import pallas as pl
from jax.experimental.pallas import tpu as pltpu

F32 = jnp.float32
BF16 = jnp.bfloat16

LANES = 128
VMEM_LIMIT_BYTES = 56 * 1024 * 1024

EPS = 1e-6
NEG = -1e30
BIG = 1e4
HEAD_DIM = 64
TQ = 128
N_BUCKETS = 32
MAX_DISTANCE = 128
A_HEADS = 16
B_HEADS = 16
GROUPS = 2
PAIRS = 4
CMP_BLOCK = 32
CMP_STRIDE = 16
CMP_HIDDEN = 256
SLC_BLOCK = 64
SLC_TOPK = 8
A_WINDOW = 128
B_WINDOW = 512
C_HEADS = 16
Q_LORA = 768
KV_LORA = 512
NOPE_DIM = 128
ROPE_DIM = 64
V_DIM = 128
ROPE_THETA = 10000.0

COL_QA, COL_QB = 0, 1024
SEG_KA, SEG_VA, SEG_KC, SEG_VC, SEG_KS, SEG_VS, SEG_KW, SEG_VW, SEG_GATE = range(16, 25)
EVEN_COLS = 25 * LANES


def _params(*sem):
    return pltpu.CompilerParams(dimension_semantics=sem, vmem_limit_bytes=VMEM_LIMIT_BYTES)


def _rms(x, g):
    return x * lax.rsqrt(jnp.mean(x * x, axis=-1, keepdims=True) + EPS) * g


def _dot(a, b):
    return jnp.dot(a, b, preferred_element_type=F32)


def _dot_nt(a, b):
    return lax.dot_general(a, b, (((1,), (1,)), ((), ())), preferred_element_type=F32)


def _bucket_np(dist):
    dist = np.maximum(dist, 0)
    max_exact = N_BUCKETS // 2
    d = np.maximum(dist, 1).astype(np.float32)
    large = max_exact + (np.log(d / np.float32(max_exact)) / np.float32(math.log(MAX_DISTANCE / max_exact))
                         * np.float32(N_BUCKETS - max_exact)).astype(np.int32)
    large = np.minimum(large, N_BUCKETS - 1)
    return np.where(dist < max_exact, dist, large).astype(np.int32)


def _bias_kernel(tab_ref, bm_ref, o_ref, *, head0):
    h = pl.program_id(0) + head0
    bm = bm_ref[...]
    acc = jnp.zeros(bm.shape, F32)
    for b in range(N_BUCKETS):
        acc = jnp.where(bm == b, tab_ref[b, h], acc)
    o_ref[0] = acc


def _bias_tiles(rel_bias, bucket_map, head0, n_heads):
    shp = bucket_map.shape
    nd = len(shp)
    return pl.pallas_call(
        functools.partial(_bias_kernel, head0=head0),
        grid=(n_heads,),
        in_specs=[pl.BlockSpec(memory_space=pltpu.SMEM),
                  pl.BlockSpec(shp, lambda h: (0,) * nd)],
        out_specs=pl.BlockSpec((1,) + shp, lambda h: (h,) + (0,) * nd),
        out_shape=jax.ShapeDtypeStruct((n_heads,) + shp, F32),
        compiler_params=_params("arbitrary"),
    )(rel_bias, jnp.asarray(bucket_map))


def _norm_matmul_kernel(x_ref, g_ref, w_ref, o_ref, xn_ref):
    @pl.when(pl.program_id(1) == 0)
    def _():
        xn_ref[...] = _rms(x_ref[...], g_ref[...]).astype(BF16)

    o_ref[...] = _dot(xn_ref[...], w_ref[...]).astype(o_ref.dtype)


def _norm_matmul(x, g, w, tm, tn, out_dtype):
    m, d = x.shape
    n = w.shape[1]
    return pl.pallas_call(
        _norm_matmul_kernel,
        grid=(m // tm, n // tn),
        in_specs=[pl.BlockSpec((tm, d), lambda i, j: (i, 0)),
                  pl.BlockSpec((1, d), lambda i, j: (0, 0)),
                  pl.BlockSpec((d, tn), lambda i, j: (0, j))],
        out_specs=pl.BlockSpec((tm, tn), lambda i, j: (i, j)),
        out_shape=jax.ShapeDtypeStruct((m, n), out_dtype),
        scratch_shapes=[pltpu.VMEM((tm, d), BF16)],
        compiler_params=_params("parallel", "arbitrary"),
    )(x, g.reshape(1, d), w)


def _out_proj_kernel(*refs, n_in):
    x_ref = refs[0]
    o_ref = refs[2 * n_in + 1]
    acc = x_ref[...]
    for t in range(n_in):
        acc = acc + _dot(refs[1 + t][...], refs[1 + n_in + t][...])
    o_ref[...] = acc


def _out_proj(x, acts, ws, tm):
    m, d = x.shape
    n_in = len(acts)
    in_specs = [pl.BlockSpec((tm, d), lambda i: (i, 0))]
    in_specs += [pl.BlockSpec((tm, a.shape[1]), lambda i: (i, 0)) for a in acts]
    in_specs += [pl.BlockSpec(w.shape, lambda i: (0, 0)) for w in ws]
    return pl.pallas_call(
        functools.partial(_out_proj_kernel, n_in=n_in),
        grid=(m // tm,),
        in_specs=in_specs,
        out_specs=pl.BlockSpec((tm, d), lambda i: (i, 0)),
        out_shape=jax.ShapeDtypeStruct((m, d), F32),
        compiler_params=_params("parallel"),
    )(x, *acts, *ws)


def _mlp_kernel(x_ref, g_ref, wu_ref, wd_ref, gf_ref, o_ref, xn_ref, *, final_norm):
    j = pl.program_id(1)

    @pl.when(j == 0)
    def _():
        x = x_ref[...]
        xn_ref[...] = _rms(x, g_ref[...]).astype(BF16)
        o_ref[...] = x

    h = _dot(xn_ref[...], wu_ref[...])
    a = jnp.square(jnp.maximum(h, 0.0)).astype(BF16)
    o_ref[...] += _dot(a, wd_ref[...])

    if final_norm:
        @pl.when(j == pl.num_programs(1) - 1)
        def _():
            o_ref[...] = _rms(o_ref[...], gf_ref[...])


def _mlp(x, g, w_up, w_down, g_final, tm, tf, final_norm):
    m, d = x.shape
    ff = w_up.shape[1]
    return pl.pallas_call(
        functools.partial(_mlp_kernel, final_norm=final_norm),
        grid=(m // tm, ff // tf),
        in_specs=[pl.BlockSpec((tm, d), lambda i, j: (i, 0)),
                  pl.BlockSpec((1, d), lambda i, j: (0, 0)),
                  pl.BlockSpec((d, tf), lambda i, j: (0, j)),
                  pl.BlockSpec((tf, d), lambda i, j: (j, 0)),
                  pl.BlockSpec((1, d), lambda i, j: (0, 0))],
        out_specs=pl.BlockSpec((tm, d), lambda i, j: (i, 0)),
        out_shape=jax.ShapeDtypeStruct((m, d), F32),
        scratch_shapes=[pltpu.VMEM((tm, d), BF16)],
        compiler_params=_params("parallel", "arbitrary"),
    )(x, g.reshape(1, d), w_up, w_down, g_final.reshape(1, d))


def _gelu_tanh(x):
    return 0.5 * x * (1.0 + jnp.tanh(math.sqrt(2.0 / math.pi) * (x + 0.044715 * (x * x * x))))


def _compress_kernel(h_ref, pa_ref, pb_ref, wa_ref, wb_ref, w2_ref, o_ref):
    h = h_ref[0, 0].astype(F32)
    ha = _dot((h + pa_ref[0]).astype(BF16), wa_ref[0])
    hb = _dot((h + pb_ref[0]).astype(BF16), wb_ref[0])
    n = hb.shape[0]
    pre = ha + pltpu.roll(hb, n - 1, 0)
    o_ref[0, 0] = _dot(_gelu_tanh(pre).astype(BF16), w2_ref[0]).astype(o_ref.dtype)


def _compress(hkv, pa, pb, wa, wb, w2):
    _, b, r, w = hkv.shape
    hid2 = wa.shape[2]
    return pl.pallas_call(
        _compress_kernel,
        grid=(2, b),
        in_specs=[pl.BlockSpec((1, 1, r, w), lambda t, i: (t, i, 0, 0)),
                  pl.BlockSpec((1, 1, w), lambda t, i: (t, 0, 0)),
                  pl.BlockSpec((1, 1, w), lambda t, i: (t, 0, 0)),
                  pl.BlockSpec((1, w, hid2), lambda t, i: (t, 0, 0)),
                  pl.BlockSpec((1, w, hid2), lambda t, i: (t, 0, 0)),
                  pl.BlockSpec((1, hid2, LANES), lambda t, i: (t, 0, 0))],
        out_specs=pl.BlockSpec((1, 1, r, LANES), lambda t, i: (t, i, 0, 0)),
        out_shape=jax.ShapeDtypeStruct((2, b, r, LANES), BF16),
        compiler_params=_params("arbitrary", "arbitrary"),
    )(hkv, pa, pb, wa, wb, w2)


def _stack_pairs(q_ref):
    scale = HEAD_DIM ** -0.5
    return jnp.concatenate(
        [q_ref[0, :, p * LANES:(p + 1) * LANES] for p in range(PAIRS)], axis=0) * jnp.asarray(scale, BF16)


def _lane_lo():
    return lax.broadcasted_iota(jnp.int32, (TQ, LANES), 1) < HEAD_DIM


def _flash_tile(s, bias_of, mask, vbd, state):
    m, l, acc = state
    lo = _lane_lo()
    new_m, new_l, p_rows, alphas = [], [], [], []
    for p in range(PAIRS):
        halves = []
        al = []
        for hf in range(2):
            u = 2 * p + hf
            sh = s[p * TQ:(p + 1) * TQ, hf * LANES:(hf + 1) * LANES] + bias_of(u)
            if mask is not None:
                sh = jnp.where(mask, sh, NEG)
            mn = jnp.maximum(m[u], jnp.max(sh, axis=1, keepdims=True))
            a = jnp.exp(m[u] - mn)
            pe = jnp.exp(sh - mn)
            new_l.append(a * l[u] + jnp.sum(pe, axis=1, keepdims=True))
            new_m.append(mn)
            halves.append(pe.astype(BF16))
            al.append(a)
        p_rows.append(jnp.concatenate(halves, axis=1))
        alphas.append(jnp.where(lo, al[0], al[1]))
    pv = _dot(jnp.concatenate(p_rows, axis=0), vbd)
    new_acc = [alphas[p] * acc[p] + pv[p * TQ:(p + 1) * TQ] for p in range(PAIRS)]
    return tuple(new_m), tuple(new_l), tuple(new_acc)


def _flash_init():
    m = tuple(jnp.full((TQ, 1), NEG, F32) for _ in range(2 * PAIRS))
    l = tuple(jnp.zeros((TQ, 1), F32) for _ in range(2 * PAIRS))
    acc = tuple(jnp.zeros((TQ, LANES), F32) for _ in range(PAIRS))
    return m, l, acc


def _flash_finish(state, extra_l=None):
    m, l, acc = state
    lo = _lane_lo()
    outs = []
    for p in range(PAIRS):
        inv = []
        for hf in range(2):
            u = 2 * p + hf
            den = l[u] if extra_l is None else l[u] + extra_l(u, m[u])
            inv.append(1.0 / den)
        outs.append(acc[p] * jnp.where(lo, inv[0], inv[1]))
    return outs


def _even_attn_kernel(tab_ref, sink_ref, qa_ref, qb_ref, gate_ref,
                      ka_ref, va_ref, ks_ref, vs_ref, kw_ref, vw_ref, kc_ref, vc_ref,
                      t_ref, cb_ref, e3_ref, esel_ref, ovt_ref,
                      oa_ref, ob_ref, bd_ref, cbd_ref, msk_ref, *, n_tiles):
    g = pl.program_id(1)
    i = pl.program_id(2)
    lo_s = lax.broadcasted_iota(jnp.int32, (TQ, LANES), 1) < HEAD_DIM

    def block_diag(x):
        r = pltpu.roll(x, HEAD_DIM, 1)
        lo = lax.broadcasted_iota(jnp.int32, x.shape, 1) < HEAD_DIM
        first = g == 0
        top = jnp.where(lo, jnp.where(first, x, r), 0.0)
        bot = jnp.where(lo, 0.0, jnp.where(first, r, x))
        return top.astype(BF16), bot.astype(BF16)

    @pl.when(i == 0)
    def _build():
        for t, ref in enumerate((ka_ref, va_ref, ks_ref, vs_ref, kw_ref, vw_ref)):
            def body(j, c, t=t, ref=ref):
                x = ref[0, pl.ds(pl.multiple_of(j * TQ, TQ), TQ), :].astype(F32)
                top, bot = block_diag(x)
                bd_ref[t, j, 0:TQ, :] = top
                bd_ref[t, j, TQ:2 * TQ, :] = bot
                return c
            lax.fori_loop(0, n_tiles, body, 0)
        for t, ref in enumerate((kc_ref, vc_ref)):
            top, bot = block_diag(ref[0, 0].astype(F32))
            cbd_ref[t, 0:TQ, :] = top
            cbd_ref[t, TQ:2 * TQ, :] = bot

    qi = lax.broadcasted_iota(jnp.int32, (TQ, LANES), 0)
    ki = lax.broadcasted_iota(jnp.int32, (TQ, LANES), 1)
    causal = qi >= ki
    upper = ki > qi

    def far_bias(head0):
        return lambda u: tab_ref[N_BUCKETS - 1, head0 + u]

    def near_bias(head0, delta):
        return lambda u: t_ref[head0 + u, delta]

    head_a = 8 * g
    q4 = _stack_pairs(qa_ref)
    st = _flash_tile(_dot_nt(q4, bd_ref[0, i]), near_bias(head_a, 0), causal, bd_ref[1, i], _flash_init())

    def swa_prev(d, st):
        return _flash_tile(_dot_nt(q4, bd_ref[0, i - 1]), near_bias(head_a, 1), upper, bd_ref[1, i - 1], st)

    st = lax.fori_loop(0, jnp.minimum(i, 1), swa_prev, st)
    outs = _flash_finish(st, lambda u, m: jnp.exp(sink_ref[head_a + u] - m))
    for p in range(PAIRS):
        oa_ref[0, :, p * LANES:(p + 1) * LANES] = outs[p].astype(oa_ref.dtype)

    head_b = A_HEADS + 8 * g
    q4 = _stack_pairs(qb_ref)

    s = _dot_nt(q4, cbd_ref[0])
    t_pos = i * TQ + qi
    cvis = (t_pos - CMP_STRIDE * ki - (CMP_BLOCK - 1)) >= 0
    anyvis = jnp.where(i * TQ + lax.broadcasted_iota(jnp.int32, (TQ, 1), 0) >= CMP_BLOCK - 1, 1.0, 0.0)
    psum = jnp.zeros((TQ, LANES), F32)
    p_rows = []
    for p in range(PAIRS):
        halves = []
        for hf in range(2):
            u = 2 * p + hf
            sh = jnp.where(cvis, s[p * TQ:(p + 1) * TQ, hf * LANES:(hf + 1) * LANES] + cb_ref[u], NEG)
            pe = jnp.exp(sh - jnp.max(sh, axis=1, keepdims=True))
            pc = pe * (anyvis / jnp.sum(pe, axis=1, keepdims=True))
            psum = psum + pc
            halves.append(pc.astype(BF16))
        p_rows.append(jnp.concatenate(halves, axis=1))
    o_cmp = _dot(jnp.concatenate(p_rows, axis=0), cbd_ref[1])

    ph = psum.astype(BF16)
    r1 = psum - ph.astype(F32)
    pm = r1.astype(BF16)
    pl_ = (r1 - pm.astype(F32)).astype(BF16)
    ovt = ovt_ref[...]
    pslc = _dot_nt(ovt, ph) + _dot_nt(ovt, pm) + _dot_nt(ovt, pl_)
    ns = pslc.shape[0]
    nb = lax.broadcasted_iota(jnp.int32, (ns, LANES), 0)
    tq = i * TQ + lax.broadcasted_iota(jnp.int32, (ns, LANES), 1)
    cur = tq // SLC_BLOCK
    forced = jnp.where(nb == 0, 1.0, 0.0) + jnp.where(nb == cur, 1.0, 0.0) + jnp.where(nb == cur - 1, 1.0, 0.0)
    score = jnp.where(nb * SLC_BLOCK > tq, NEG, jnp.where(forced > 0.0, BIG, pslc))
    rank = jnp.zeros((ns, LANES), F32)
    for mth in range(ns):
        sm = score[mth:mth + 1, :]
        tie = jnp.where(nb > mth, 1.0, 0.0)
        rank = rank + jnp.where(sm > score, 1.0, jnp.where(sm == score, tie, 0.0))
    sel_t = jnp.where(rank < SLC_TOPK, 1.0, 0.0)
    sel_q = jnp.concatenate([sel_t, jnp.zeros((LANES - ns, LANES), F32)], axis=0).T
    maskf = _dot(sel_q.astype(BF16), esel_ref[...])
    for j in range(n_tiles):
        msk_ref[j] = maskf[:, j * TQ:(j + 1) * TQ]

    sel_d = msk_ref[i] > 0.5
    st = _flash_tile(_dot_nt(q4, bd_ref[2, i]), near_bias(head_b, 0),
                     jnp.logical_and(causal, sel_d), bd_ref[3, i], _flash_init())

    def slc_near(d, st):
        return _flash_tile(_dot_nt(q4, bd_ref[2, i - 1]), near_bias(head_b, 1),
                           msk_ref[i - 1] > 0.5, bd_ref[3, i - 1], st)

    st = lax.fori_loop(0, jnp.minimum(i, 1), slc_near, st)

    def slc_far(d, st):
        j = i - d
        return _flash_tile(_dot_nt(q4, bd_ref[2, j]), far_bias(head_b), msk_ref[j] > 0.5, bd_ref[3, j], st)

    st = lax.fori_loop(2, i + 1, slc_far, st)
    o_slc = _flash_finish(st)

    st = _flash_tile(_dot_nt(q4, bd_ref[4, i]), near_bias(head_b, 0), causal, bd_ref[5, i], _flash_init())

    def win_near(d, st):
        return _flash_tile(_dot_nt(q4, bd_ref[4, i - 1]), near_bias(head_b, 1), None, bd_ref[5, i - 1], st)

    st = lax.fori_loop(0, jnp.minimum(i, 1), win_near, st)

    def win_mid(d, st):
        j = i - d
        return _flash_tile(_dot_nt(q4, bd_ref[4, j]), far_bias(head_b), None, bd_ref[5, j], st)

    n_full = B_WINDOW // TQ
    st = lax.fori_loop(2, jnp.minimum(i, n_full - 1) + 1, win_mid, st)

    def win_edge(d, st):
        j = i - n_full
        return _flash_tile(_dot_nt(q4, bd_ref[4, j]), far_bias(head_b), upper, bd_ref[5, j], st)

    st = lax.fori_loop(0, jnp.where(i >= n_full, 1, 0), win_edge, st)
    o_win = _flash_finish(st)

    sg = jax.nn.sigmoid(gate_ref[0].astype(F32))
    sg_hi = sg.astype(BF16)
    sg_lo = (sg - sg_hi.astype(F32)).astype(BF16)
    for p in range(PAIRS):
        cols = slice(p * LANES, (p + 1) * LANES)
        gts = [_dot(sg_hi, e3_ref[0, br, :, cols]) + _dot(sg_lo, e3_ref[0, br, :, cols]) for br in range(3)]
        o = gts[0] * o_cmp[p * TQ:(p + 1) * TQ] + gts[1] * o_slc[p] + gts[2] * o_win[p]
        ob_ref[0, :, cols] = o.astype(ob_ref.dtype)


def _even_attention(pe, kvc, rel_bias, sinks, t_tiles, cbias, e3, esel, ovt):
    b, s, _ = pe.shape
    n_tiles = s // TQ
    half = PAIRS * LANES

    def seg(k):
        return pl.BlockSpec((1, s, LANES), lambda bi, g, i, k=k: (bi, 0, k))

    in_specs = [
        pl.BlockSpec(memory_space=pltpu.SMEM),
        pl.BlockSpec(memory_space=pltpu.SMEM),
        pl.BlockSpec((1, TQ, half), lambda bi, g, i: (bi, i, COL_QA // half + g)),
        pl.BlockSpec((1, TQ, half), lambda bi, g, i: (bi, i, COL_QB // half + g)),
        pl.BlockSpec((1, TQ, LANES), lambda bi, g, i: (bi, i, SEG_GATE)),
        seg(SEG_KA), seg(SEG_VA), seg(SEG_KS), seg(SEG_VS), seg(SEG_KW), seg(SEG_VW),
        pl.BlockSpec((1, 1, TQ, LANES), lambda bi, g, i: (0, bi, 0, 0)),
        pl.BlockSpec((1, 1, TQ, LANES), lambda bi, g, i: (1, bi, 0, 0)),
        pl.BlockSpec(t_tiles.shape, lambda bi, g, i: (0, 0, 0, 0)),
        pl.BlockSpec((2 * PAIRS, TQ, LANES), lambda bi, g, i: (g, i, 0)),
        pl.BlockSpec((1, 3, LANES, half), lambda bi, g, i: (g, 0, 0, 0)),
        pl.BlockSpec(esel.shape, lambda bi, g, i: (0, 0)),
        pl.BlockSpec(ovt.shape, lambda bi, g, i: (0, 0)),
    ]
    out_spec = pl.BlockSpec((1, TQ, half), lambda bi, g, i: (bi, i, g))
    return pl.pallas_call(
        functools.partial(_even_attn_kernel, n_tiles=n_tiles),
        grid=(b, GROUPS, n_tiles),
        in_specs=in_specs,
        out_specs=[out_spec, out_spec],
        out_shape=[jax.ShapeDtypeStruct((b, s, GROUPS * half), BF16)] * 2,
        scratch_shapes=[pltpu.VMEM((6, n_tiles, 2 * TQ, LANES), BF16),
                        pltpu.VMEM((2, 2 * TQ, LANES), BF16),
                        pltpu.VMEM((n_tiles, TQ, TQ), F32)],
        compiler_params=_params("arbitrary", "arbitrary", "arbitrary"),
    )(rel_bias, sinks, pe, pe, pe, pe, pe, pe, pe, pe, pe, kvc, kvc, t_tiles, cbias, e3, esel, ovt)


def _mla_prep_kernel(x_ref, g_ref, win_ref, qn_ref, kvn_ref, wq_ref, wkv_ref, cos_ref, sin_ref,
                     q_ref, k_ref, v_ref):
    xn = _rms(x_ref[...], g_ref[...]).astype(BF16)
    proj = _dot(xn, win_ref[...])
    cq = _rms(proj[:, :Q_LORA], qn_ref[...]).astype(BF16)
    ckv = _rms(proj[:, Q_LORA:Q_LORA + KV_LORA], kvn_ref[...]).astype(BF16)
    cos = cos_ref[...]
    sin = sin_ref[...]

    def rope(x):
        return x * cos + pltpu.roll(x, HEAD_DIM, 1) * sin

    kr = rope(proj[:, Q_LORA + KV_LORA:]).astype(BF16)
    scale = (NOPE_DIM + ROPE_DIM) ** -0.5
    for h in range(C_HEADS):
        q = _dot(cq, wq_ref[:, h * 256:(h + 1) * 256])
        q_ref[:, h * 256:h * 256 + LANES] = (q[:, :LANES] * scale).astype(BF16)
        q_ref[:, h * 256 + LANES:(h + 1) * 256] = (rope(q[:, LANES:]) * scale).astype(BF16)
        kv = _dot(ckv, wkv_ref[:, h * 256:(h + 1) * 256])
        k_ref[:, h * 256:h * 256 + LANES] = kv[:, :LANES].astype(BF16)
        k_ref[:, h * 256 + LANES:(h + 1) * 256] = kr
        v_ref[:, h * LANES:(h + 1) * LANES] = kv[:, LANES:].astype(BF16)


def _mla_prep(x, g, w_in, qn, kvn, wq, wkv, cos, sin, tm, s):
    m, d = x.shape
    pos_tiles = s // tm
    full = lambda a: pl.BlockSpec(a.shape, lambda i: (0,) * a.ndim)
    return pl.pallas_call(
        _mla_prep_kernel,
        grid=(m // tm,),
        in_specs=[pl.BlockSpec((tm, d), lambda i: (i, 0)),
                  pl.BlockSpec((1, d), lambda i: (0, 0)),
                  full(w_in), pl.BlockSpec((1, Q_LORA), lambda i: (0, 0)),
                  pl.BlockSpec((1, KV_LORA), lambda i: (0, 0)), full(wq), full(wkv),
                  pl.BlockSpec((tm, LANES), lambda i: (i % pos_tiles, 0)),
                  pl.BlockSpec((tm, LANES), lambda i: (i % pos_tiles, 0))],
        out_specs=[pl.BlockSpec((tm, C_HEADS * 256), lambda i: (i, 0)),
                   pl.BlockSpec((tm, C_HEADS * 256), lambda i: (i, 0)),
                   pl.BlockSpec((tm, C_HEADS * LANES), lambda i: (i, 0))],
        out_shape=[jax.ShapeDtypeStruct((m, C_HEADS * 256), BF16),
                   jax.ShapeDtypeStruct((m, C_HEADS * 256), BF16),
                   jax.ShapeDtypeStruct((m, C_HEADS * LANES), BF16)],
        compiler_params=_params("parallel"),
    )(x, g.reshape(1, d), w_in, qn.reshape(1, -1), kvn.reshape(1, -1), wq, wkv, cos, sin)


def _mla_attn_kernel(q_ref, k_ref, v_ref, o_ref, *, n_tiles, tq):
    qi = lax.broadcasted_iota(jnp.int32, (tq, tq), 0)
    ki = lax.broadcasted_iota(jnp.int32, (tq, tq), 1)
    causal = qi >= ki

    def q_tile(t, c):
        row = pl.ds(pl.multiple_of(t * tq, tq), tq)
        q = q_ref[0, row, :]
        s = jnp.where(causal, _dot_nt(q, k_ref[0, row, :]), NEG)
        m = jnp.max(s, axis=1, keepdims=True)
        p = jnp.exp(s - m)
        l = jnp.sum(p, axis=1, keepdims=True)
        acc = _dot(p.astype(BF16), v_ref[0, row, :])

        def k_tile(j, st):
            m, l, acc = st
            rk = pl.ds(pl.multiple_of(j * tq, tq), tq)
            s = _dot_nt(q, k_ref[0, rk, :])
            mn = jnp.maximum(m, jnp.max(s, axis=1, keepdims=True))
            a = jnp.exp(m - mn)
            p = jnp.exp(s - mn)
            return mn, a * l + jnp.sum(p, axis=1, keepdims=True), a * acc + _dot(p.astype(BF16), v_ref[0, rk, :])

        m, l, acc = lax.fori_loop(0, t, k_tile, (m, l, acc))
        o_ref[0, row, :] = (acc * (1.0 / l)).astype(o_ref.dtype)
        return c

    lax.fori_loop(0, n_tiles, q_tile, 0)


def _mla_attention(q, k, v, tq):
    b, s, _ = q.shape
    return pl.pallas_call(
        functools.partial(_mla_attn_kernel, n_tiles=s // tq, tq=tq),
        grid=(b, C_HEADS),
        in_specs=[pl.BlockSpec((1, s, 256), lambda bi, h: (bi, 0, h)),
                  pl.BlockSpec((1, s, 256), lambda bi, h: (bi, 0, h)),
                  pl.BlockSpec((1, s, LANES), lambda bi, h: (bi, 0, h))],
        out_specs=pl.BlockSpec((1, s, LANES), lambda bi, h: (bi, 0, h)),
        out_shape=jax.ShapeDtypeStruct((b, s, C_HEADS * V_DIM), BF16),
        compiler_params=_params("parallel", "arbitrary"),
    )(q, k, v)


def _even_in_weight(w):
    cuts = np.cumsum([1024, 128, 128, 1024, 128, 128, 128, 128, 128, 128, 48])[:-1]
    qa, ka, va, qb, kc, vc, ks, vs, kw, vw, gate = jnp.split(w, [int(c) for c in cuts], axis=1)
    gate = jnp.pad(gate, ((0, 0), (0, LANES - gate.shape[1])))
    return jnp.concatenate([qa, qb, ka, va, kc, vc, ks, vs, kw, vw, gate], axis=1).astype(BF16)


def _compress_weights(pos, w1, w2):
    half = CMP_BLOCK // 2
    eye = jnp.eye(GROUPS, dtype=w1.dtype)
    w1r = w1.reshape(2, half, HEAD_DIM, CMP_HIDDEN)
    wab = jnp.einsum('rldh,gk->rlgdkh', w1r, eye).reshape(2, half * GROUPS * HEAD_DIM, GROUPS * CMP_HIDDEN)
    posr = jnp.broadcast_to(pos.reshape(2, half, 1, HEAD_DIM), (2, half, GROUPS, HEAD_DIM)).reshape(2, 1, -1)
    w2bd = jnp.einsum('hd,gk->ghkd', w2, eye).reshape(GROUPS * CMP_HIDDEN, GROUPS * HEAD_DIM)
    return posr[0], posr[1], wab[0].astype(BF16), wab[1].astype(BF16), w2bd.astype(BF16)


def _rope_chunk_cols(w):
    z = jnp.zeros((w.shape[0], ROPE_DIM // 2), w.dtype)
    return jnp.concatenate([w[:, :ROPE_DIM // 2], z, w[:, ROPE_DIM // 2:], z], axis=1)


def _mla_weights(w_in, w_q_up, w_kv_up):
    w_in2 = jnp.concatenate([w_in[:, :Q_LORA + KV_LORA], _rope_chunk_cols(w_in[:, Q_LORA + KV_LORA:])], axis=1)
    wq = w_q_up.reshape(Q_LORA, C_HEADS, NOPE_DIM + ROPE_DIM)
    z = jnp.zeros((Q_LORA, C_HEADS, ROPE_DIM // 2), w_q_up.dtype)
    wq2 = jnp.concatenate([wq[..., :NOPE_DIM], wq[..., NOPE_DIM:NOPE_DIM + ROPE_DIM // 2], z,
                           wq[..., NOPE_DIM + ROPE_DIM // 2:], z], axis=-1).reshape(Q_LORA, C_HEADS * 256)
    return w_in2.astype(BF16), wq2.astype(BF16), w_kv_up.astype(BF16)


def _rope_tables(s):
    inv = 1.0 / (ROPE_THETA ** (jnp.arange(0, ROPE_DIM, 2, dtype=F32) / ROPE_DIM))
    ang = jnp.arange(s, dtype=F32)[:, None] * inv[None]
    cos, sin = jnp.cos(ang), jnp.sin(ang)
    z = jnp.zeros_like(cos)
    return jnp.concatenate([cos, z, cos, z], axis=1), jnp.concatenate([-sin, z, sin, z], axis=1)


def _static_tables(s):
    q = np.arange(TQ)[:, None]
    k = np.arange(TQ)[None, :]
    near = np.stack([_bucket_np(q - k), _bucket_np(q - k + TQ)])
    t = np.arange(s)[:, None]
    c = np.arange(LANES)[None, :]
    cmp_map = _bucket_np(t - (c * CMP_STRIDE + CMP_BLOCK - 1))
    ns = s // SLC_BLOCK
    nc = (s - CMP_BLOCK) // CMP_STRIDE + 1
    c_start = np.arange(LANES) * CMP_STRIDE
    s_start = np.arange(ns) * SLC_BLOCK
    ovt = ((c_start[None, :] <= s_start[:, None] + SLC_BLOCK - 1) &
           (c_start[None, :] + CMP_BLOCK - 1 >= s_start[:, None]) &
           (np.arange(LANES)[None, :] < nc)).astype(np.float32)
    esel = (np.arange(s)[None, :] // SLC_BLOCK == np.arange(LANES)[:, None]).astype(np.float32)
    e3 = np.zeros((GROUPS, 3, LANES, PAIRS * LANES), np.float32)
    for g in range(GROUPS):
        for u in range(2 * PAIRS):
            for br in range(3):
                e3[g, br, (8 * g + u) * 3 + br, u * HEAD_DIM:(u + 1) * HEAD_DIM] = 1.0
    return near, cmp_map, ovt, esel, e3


def _even_layer(x, b, s, rel_bias, norm, w_in, sinks, pos_k, pos_v, k_w1, k_w2, v_w1, v_w2, w_out):
    near, cmp_map, ovt, esel, e3 = _static_tables(s)
    t_tiles = _bias_tiles(rel_bias, near, 0, A_HEADS + B_HEADS)
    cbias = _bias_tiles(rel_bias, cmp_map, A_HEADS, B_HEADS)

    pe = _norm_matmul(x, norm, _even_in_weight(w_in), 1024, 640, BF16).reshape(b, s, EVEN_COLS)

    rows = s // (CMP_BLOCK // 2)
    hkv = jnp.stack([pe[:, :, SEG_KC * LANES:(SEG_KC + 1) * LANES],
                     pe[:, :, SEG_VC * LANES:(SEG_VC + 1) * LANES]]).reshape(2, b, rows, -1)
    ck = _compress_weights(pos_k, k_w1, k_w2)
    cv = _compress_weights(pos_v, v_w1, v_w2)
    kvc = _compress(hkv, *[jnp.stack([a, c]) for a, c in zip(ck, cv)])
    assert rows <= TQ
    kvc = jnp.pad(kvc, ((0, 0), (0, 0), (0, TQ - rows), (0, 0)))

    oa, ob = _even_attention(pe, kvc, rel_bias, sinks, t_tiles, cbias,
                             jnp.asarray(e3, BF16), jnp.asarray(esel, BF16), jnp.asarray(ovt, BF16))
    w_out = w_out.astype(BF16)
    n_a = A_HEADS * HEAD_DIM
    return _out_proj(x, [oa.reshape(b * s, -1), ob.reshape(b * s, -1)], [w_out[:n_a], w_out[n_a:]], 512)


def _odd_layer(x, b, s, norm, w_in, q_norm, w_q_up, kv_norm, w_kv_up, w_out):
    w_in2, wq2, wkv = _mla_weights(w_in, w_q_up, w_kv_up)
    cos, sin = _rope_tables(s)
    q, k, v = _mla_prep(x, norm, w_in2, q_norm, kv_norm, wq2, wkv, cos, sin, 256, s)
    o = _mla_attention(q.reshape(b, s, -1), k.reshape(b, s, -1), v.reshape(b, s, -1), 256)
    return _out_proj(x, [o.reshape(b * s, -1)], [w_out.astype(BF16)], 512)


def kernel(x, rel_bias, norm_mix_e, w_in_e, sinks, cmp_pos_k, cmp_pos_v, cmp_k_w1, cmp_k_w2, cmp_v_w1, cmp_v_w2, w_out_e, norm_mix_o, w_in_o, q_norm, w_q_up, kv_norm, w_kv_up, w_out_o, norm_mlp, w_up, w_down, norm_final):
    b, s, d = x.shape
    depth = norm_mlp.shape[0]
    h = x.reshape(b * s, d)
    for layer in range(depth):
        i = layer // 2
        if layer % 2 == 0:
            h = _even_layer(h, b, s, rel_bias, norm_mix_e[i], w_in_e[i], sinks[i], cmp_pos_k[i], cmp_pos_v[i],
                            cmp_k_w1[i], cmp_k_w2[i], cmp_v_w1[i], cmp_v_w2[i], w_out_e[i])
        else:
            h = _odd_layer(h, b, s, norm_mix_o[i], w_in_o[i], q_norm[i], w_q_up[i], kv_norm[i], w_kv_up[i],
                           w_out_o[i])
        h = _mlp(h, norm_mlp[layer], w_up[layer].astype(BF16), w_down[layer].astype(BF16), norm_final,
                 512, 512, layer == depth - 1)
    return h.reshape(b, s, d)
```

```python
import functools
import math

import numpy as np
import jax
import jax.numpy as jnp
from jax import lax
from jax.experimental import pallas as pl
from jax.experimental.pallas import tpu as pltpu

F32 = jnp.float32
BF16 = jnp.bfloat16

LANES = 128
VMEM_LIMIT_BYTES = 56 * 1024 * 1024

EPS = 1e-6
NEG = -1e30
BIG = 1e4
HEAD_DIM = 64
TQ = 128
MLA_TQ = 512
MLA_TK = 256
N_BUCKETS = 32
MAX_DISTANCE = 128
A_HEADS = 16
B_HEADS = 16
GROUPS = 2
PAIRS = 4
CMP_BLOCK = 32
CMP_STRIDE = 16
CMP_HIDDEN = 256
SLC_BLOCK = 64
SLC_TOPK = 8
A_WINDOW = 128
B_WINDOW = 512
C_HEADS = 16
Q_LORA = 768
KV_LORA = 512
NOPE_DIM = 128
ROPE_DIM = 64
V_DIM = 128
ROPE_THETA = 10000.0

COL_QA, COL_QB = 0, 1024
SEG_KA, SEG_VA, SEG_KC, SEG_VC, SEG_KS, SEG_VS, SEG_KW, SEG_VW, SEG_GATE = range(16, 25)
EVEN_COLS = 25 * LANES


def _params(*sem):
    return pltpu.CompilerParams(dimension_semantics=sem, vmem_limit_bytes=VMEM_LIMIT_BYTES)


def _rms(x, g):
    return x * lax.rsqrt(jnp.mean(x * x, axis=-1, keepdims=True) + EPS) * g


def _dot(a, b):
    return jnp.dot(a, b, preferred_element_type=F32)


def _dot_nt(a, b):
    return lax.dot_general(a, b, (((1,), (1,)), ((), ())), preferred_element_type=F32)


def _bucket_np(dist):
    dist = np.maximum(dist, 0)
    max_exact = N_BUCKETS // 2
    d = np.maximum(dist, 1).astype(np.float32)
    large = max_exact + (np.log(d / np.float32(max_exact)) / np.float32(math.log(MAX_DISTANCE / max_exact))
                         * np.float32(N_BUCKETS - max_exact)).astype(np.int32)
    large = np.minimum(large, N_BUCKETS - 1)
    return np.where(dist < max_exact, dist, large).astype(np.int32)


def _bias_kernel(tab_ref, bm_ref, o_ref, *, head0):
    h = pl.program_id(0) + head0
    bm = bm_ref[...]
    acc = jnp.zeros(bm.shape, F32)
    for b in range(N_BUCKETS):
        acc = jnp.where(bm == b, tab_ref[b, h], acc)
    o_ref[0] = acc


def _bias_tiles(rel_bias, bucket_map, head0, n_heads):
    shp = bucket_map.shape
    nd = len(shp)
    return pl.pallas_call(
        functools.partial(_bias_kernel, head0=head0),
        grid=(n_heads,),
        in_specs=[pl.BlockSpec(memory_space=pltpu.SMEM),
                  pl.BlockSpec(shp, lambda h: (0,) * nd)],
        out_specs=pl.BlockSpec((1,) + shp, lambda h: (h,) + (0,) * nd),
        out_shape=jax.ShapeDtypeStruct((n_heads,) + shp, F32),
        compiler_params=_params("arbitrary"),
    )(rel_bias, jnp.asarray(bucket_map))


def _norm_matmul_kernel(x_ref, g_ref, w_ref, o_ref, xn_ref):
    @pl.when(pl.program_id(1) == 0)
    def _():
        xn_ref[...] = _rms(x_ref[...], g_ref[...]).astype(BF16)

    o_ref[...] = _dot(xn_ref[...], w_ref[...]).astype(o_ref.dtype)


def _norm_matmul(x, g, w, tm, tn, out_dtype):
    m, d = x.shape
    n = w.shape[1]
    return pl.pallas_call(
        _norm_matmul_kernel,
        grid=(m // tm, n // tn),
        in_specs=[pl.BlockSpec((tm, d), lambda i, j: (i, 0)),
                  pl.BlockSpec((1, d), lambda i, j: (0, 0)),
                  pl.BlockSpec((d, tn), lambda i, j: (0, j))],
        out_specs=pl.BlockSpec((tm, tn), lambda i, j: (i, j)),
        out_shape=jax.ShapeDtypeStruct((m, n), out_dtype),
        scratch_shapes=[pltpu.VMEM((tm, d), BF16)],
        compiler_params=_params("parallel", "arbitrary"),
    )(x, g.reshape(1, d), w)


def _out_proj_kernel(*refs, n_in):
    x_ref = refs[0]
    o_ref = refs[2 * n_in + 1]
    acc = x_ref[...]
    for t in range(n_in):
        acc = acc + _dot(refs[1 + t][...], refs[1 + n_in + t][...])
    o_ref[...] = acc


def _out_proj(x, acts, ws, tm):
    m, d = x.shape
    n_in = len(acts)
    in_specs = [pl.BlockSpec((tm, d), lambda i: (i, 0))]
    in_specs += [pl.BlockSpec((tm, a.shape[1]), lambda i: (i, 0)) for a in acts]
    in_specs += [pl.BlockSpec(w.shape, lambda i: (0, 0)) for w in ws]
    return pl.pallas_call(
        functools.partial(_out_proj_kernel, n_in=n_in),
        grid=(m // tm,),
        in_specs=in_specs,
        out_specs=pl.BlockSpec((tm, d), lambda i: (i, 0)),
        out_shape=jax.ShapeDtypeStruct((m, d), F32),
        compiler_params=_params("parallel"),
    )(x, *acts, *ws)


def _mlp_kernel(x_ref, g_ref, wu_ref, wd_ref, gf_ref, o_ref, xn_ref, *, final_norm):
    j = pl.program_id(1)

    @pl.when(j == 0)
    def _():
        x = x_ref[...]
        xn_ref[...] = _rms(x, g_ref[...]).astype(BF16)
        o_ref[...] = x

    h = _dot(xn_ref[...], wu_ref[...])
    a = jnp.square(jnp.maximum(h, 0.0)).astype(BF16)
    o_ref[...] += _dot(a, wd_ref[...])

    if final_norm:
        @pl.when(j == pl.num_programs(1) - 1)
        def _():
            o_ref[...] = _rms(o_ref[...], gf_ref[...])


def _mlp(x, g, w_up, w_down, g_final, tm, tf, final_norm):
    m, d = x.shape
    ff = w_up.shape[1]
    return pl.pallas_call(
        functools.partial(_mlp_kernel, final_norm=final_norm),
        grid=(m // tm, ff // tf),
        in_specs=[pl.BlockSpec((tm, d), lambda i, j: (i, 0)),
                  pl.BlockSpec((1, d), lambda i, j: (0, 0)),
                  pl.BlockSpec((d, tf), lambda i, j: (0, j)),
                  pl.BlockSpec((tf, d), lambda i, j: (j, 0)),
                  pl.BlockSpec((1, d), lambda i, j: (0, 0))],
        out_specs=pl.BlockSpec((tm, d), lambda i, j: (i, 0)),
        out_shape=jax.ShapeDtypeStruct((m, d), F32),
        scratch_shapes=[pltpu.VMEM((tm, d), BF16)],
        compiler_params=_params("parallel", "arbitrary"),
    )(x, g.reshape(1, d), w_up, w_down, g_final.reshape(1, d))


def _gelu_tanh(x):
    return 0.5 * x * (1.0 + jnp.tanh(math.sqrt(2.0 / math.pi) * (x + 0.044715 * (x * x * x))))


def _compress_kernel(h_ref, pa_ref, pb_ref, wa_ref, wb_ref, w2_ref, o_ref):
    h = h_ref[0, 0].astype(F32)
    ha = _dot((h + pa_ref[0]).astype(BF16), wa_ref[0])
    hb = _dot((h + pb_ref[0]).astype(BF16), wb_ref[0])
    n = hb.shape[0]
    pre = ha + pltpu.roll(hb, n - 1, 0)
    o_ref[0, 0] = _dot(_gelu_tanh(pre).astype(BF16), w2_ref[0]).astype(o_ref.dtype)


def _compress(hkv, pa, pb, wa, wb, w2):
    _, b, r, w = hkv.shape
    hid2 = wa.shape[2]
    return pl.pallas_call(
        _compress_kernel,
        grid=(2, b),
        in_specs=[pl.BlockSpec((1, 1, r, w), lambda t, i: (t, i, 0, 0)),
                  pl.BlockSpec((1, 1, w), lambda t, i: (t, 0, 0)),
                  pl.BlockSpec((1, 1, w), lambda t, i: (t, 0, 0)),
                  pl.BlockSpec((1, w, hid2), lambda t, i: (t, 0, 0)),
                  pl.BlockSpec((1, w, hid2), lambda t, i: (t, 0, 0)),
                  pl.BlockSpec((1, hid2, LANES), lambda t, i: (t, 0, 0))],
        out_specs=pl.BlockSpec((1, 1, r, LANES), lambda t, i: (t, i, 0, 0)),
        out_shape=jax.ShapeDtypeStruct((2, b, r, LANES), BF16),
        compiler_params=_params("arbitrary", "arbitrary"),
    )(hkv, pa, pb, wa, wb, w2)


def _stack_pairs(q_ref):
    scale = HEAD_DIM ** -0.5
    return jnp.concatenate(
        [q_ref[0, :, p * LANES:(p + 1) * LANES] for p in range(PAIRS)], axis=0) * jnp.asarray(scale, BF16)


def _lane_lo():
    return lax.broadcasted_iota(jnp.int32, (TQ, LANES), 1) < HEAD_DIM


def _flash_tile(s, bias_of, mask, vbd, state):
    m, l, acc = state
    lo = _lane_lo()
    new_m, new_l, p_rows, alphas = [], [], [], []
    for p in range(PAIRS):
        halves = []
        al = []
        for hf in range(2):
            u = 2 * p + hf
            sh = s[p * TQ:(p + 1) * TQ, hf * LANES:(hf + 1) * LANES] + bias_of(u)
            if mask is not None:
                sh = jnp.where(mask, sh, NEG)
            mn = jnp.maximum(m[u], jnp.max(sh, axis=1, keepdims=True))
            a = jnp.exp(m[u] - mn)
            pe = jnp.exp(sh - mn)
            new_l.append(a * l[u] + jnp.sum(pe, axis=1, keepdims=True))
            new_m.append(mn)
            halves.append(pe.astype(BF16))
            al.append(a)
        p_rows.append(jnp.concatenate(halves, axis=1))
        alphas.append(jnp.where(lo, al[0], al[1]))
    pv = _dot(jnp.concatenate(p_rows, axis=0), vbd)
    new_acc = [alphas[p] * acc[p] + pv[p * TQ:(p + 1) * TQ] for p in range(PAIRS)]
    return tuple(new_m), tuple(new_l), tuple(new_acc)


def _flash_init():
    m = tuple(jnp.full((TQ, 1), NEG, F32) for _ in range(2 * PAIRS))
    l = tuple(jnp.zeros((TQ, 1), F32) for _ in range(2 * PAIRS))
    acc = tuple(jnp.zeros((TQ, LANES), F32) for _ in range(PAIRS))
    return m, l, acc


def _flash_finish(state, extra_l=None):
    m, l, acc = state
    lo = _lane_lo()
    outs = []
    for p in range(PAIRS):
        inv = []
        for hf in range(2):
            u = 2 * p + hf
            den = l[u] if extra_l is None else l[u] + extra_l(u, m[u])
            inv.append(1.0 / den)
        outs.append(acc[p] * jnp.where(lo, inv[0], inv[1]))
    return outs


def _even_attn_kernel(tab_ref, sink_ref, qa_ref, qb_ref, gate_ref,
                      ka_ref, va_ref, ks_ref, vs_ref, kw_ref, vw_ref, kc_ref, vc_ref,
                      t_ref, cb_ref, e3_ref, esel_ref, ovt_ref,
                      oa_ref, ob_ref, bd_ref, cbd_ref, msk_ref, *, n_tiles):
    g = pl.program_id(1)
    i = pl.program_id(2)
    lo_s = lax.broadcasted_iota(jnp.int32, (TQ, LANES), 1) < HEAD_DIM

    def block_diag(x):
        r = pltpu.roll(x, HEAD_DIM, 1)
        lo = lax.broadcasted_iota(jnp.int32, x.shape, 1) < HEAD_DIM
        first = g == 0
        top = jnp.where(lo, jnp.where(first, x, r), 0.0)
        bot = jnp.where(lo, 0.0, jnp.where(first, r, x))
        return top.astype(BF16), bot.astype(BF16)

    @pl.when(i == 0)
    def _build():
        for t, ref in enumerate((ka_ref, va_ref, ks_ref, vs_ref, kw_ref, vw_ref)):
            def body(j, c, t=t, ref=ref):
                x = ref[0, pl.ds(pl.multiple_of(j * TQ, TQ), TQ), :].astype(F32)
                top, bot = block_diag(x)
                bd_ref[t, j, 0:TQ, :] = top
                bd_ref[t, j, TQ:2 * TQ, :] = bot
                return c
            lax.fori_loop(0, n_tiles, body, 0)
        for t, ref in enumerate((kc_ref, vc_ref)):
            top, bot = block_diag(ref[0, 0].astype(F32))
            cbd_ref[t, 0:TQ, :] = top
            cbd_ref[t, TQ:2 * TQ, :] = bot

    qi = lax.broadcasted_iota(jnp.int32, (TQ, LANES), 0)
    ki = lax.broadcasted_iota(jnp.int32, (TQ, LANES), 1)
    causal = qi >= ki
    upper = ki > qi

    def far_bias(head0):
        return lambda u: tab_ref[N_BUCKETS - 1, head0 + u]

    def near_bias(head0, delta):
        return lambda u: t_ref[head0 + u, delta]

    head_a = 8 * g
    q4 = _stack_pairs(qa_ref)
    st = _flash_tile(_dot_nt(q4, bd_ref[0, i]), near_bias(head_a, 0), causal, bd_ref[1, i], _flash_init())

    def swa_prev(d, st):
        return _flash_tile(_dot_nt(q4, bd_ref[0, i - 1]), near_bias(head_a, 1), upper, bd_ref[1, i - 1], st)

    st = lax.fori_loop(0, jnp.minimum(i, 1), swa_prev, st)
    outs = _flash_finish(st, lambda u, m: jnp.exp(sink_ref[head_a + u] - m))
    for p in range(PAIRS):
        oa_ref[0, :, p * LANES:(p + 1) * LANES] = outs[p].astype(oa_ref.dtype)

    head_b = A_HEADS + 8 * g
    q4 = _stack_pairs(qb_ref)

    s = _dot_nt(q4, cbd_ref[0])
    t_pos = i * TQ + qi
    cvis = (t_pos - CMP_STRIDE * ki - (CMP_BLOCK - 1)) >= 0
    anyvis = jnp.where(i * TQ + lax.broadcasted_iota(jnp.int32, (TQ, 1), 0) >= CMP_BLOCK - 1, 1.0, 0.0)
    psum = jnp.zeros((TQ, LANES), F32)
    p_rows = []
    for p in range(PAIRS):
        halves = []
        for hf in range(2):
            u = 2 * p + hf
            sh = jnp.where(cvis, s[p * TQ:(p + 1) * TQ, hf * LANES:(hf + 1) * LANES] + cb_ref[u], NEG)
            pe = jnp.exp(sh - jnp.max(sh, axis=1, keepdims=True))
            pc = pe * (anyvis / jnp.sum(pe, axis=1, keepdims=True))
            psum = psum + pc
            halves.append(pc.astype(BF16))
        p_rows.append(jnp.concatenate(halves, axis=1))
    o_cmp = _dot(jnp.concatenate(p_rows, axis=0), cbd_ref[1])

    ph = psum.astype(BF16)
    r1 = psum - ph.astype(F32)
    pm = r1.astype(BF16)
    pl_ = (r1 - pm.astype(F32)).astype(BF16)
    ovt = ovt_ref[...]
    pslc = _dot_nt(ovt, ph) + _dot_nt(ovt, pm) + _dot_nt(ovt, pl_)
    ns = pslc.shape[0]
    nb = lax.broadcasted_iota(jnp.int32, (ns, LANES), 0)
    tq = i * TQ + lax.broadcasted_iota(jnp.int32, (ns, LANES), 1)
    cur = tq // SLC_BLOCK
    forced = jnp.where(nb == 0, 1.0, 0.0) + jnp.where(nb == cur, 1.0, 0.0) + jnp.where(nb == cur - 1, 1.0, 0.0)
    score = jnp.where(nb * SLC_BLOCK > tq, NEG, jnp.where(forced > 0.0, BIG, pslc))
    rank = jnp.zeros((ns, LANES), F32)
    for mth in range(ns):
        sm = score[mth:mth + 1, :]
        tie = jnp.where(nb > mth, 1.0, 0.0)
        rank = rank + jnp.where(sm > score, 1.0, jnp.where(sm == score, tie, 0.0))
    sel_t = jnp.where(rank < SLC_TOPK, 1.0, 0.0)
    sel_q = jnp.concatenate([sel_t, jnp.zeros((LANES - ns, LANES), F32)], axis=0).T
    maskf = _dot(sel_q.astype(BF16), esel_ref[...])
    for j in range(n_tiles):
        msk_ref[j] = maskf[:, j * TQ:(j + 1) * TQ]

    sel_d = msk_ref[i] > 0.5
    st = _flash_tile(_dot_nt(q4, bd_ref[2, i]), near_bias(head_b, 0),
                     jnp.logical_and(causal, sel_d), bd_ref[3, i], _flash_init())

    def slc_near(d, st):
        return _flash_tile(_dot_nt(q4, bd_ref[2, i - 1]), near_bias(head_b, 1),
                           msk_ref[i - 1] > 0.5, bd_ref[3, i - 1], st)

    st = lax.fori_loop(0, jnp.minimum(i, 1), slc_near, st)

    def slc_far(d, st):
        j = i - d
        return _flash_tile(_dot_nt(q4, bd_ref[2, j]), far_bias(head_b), msk_ref[j] > 0.5, bd_ref[3, j], st)

    st = lax.fori_loop(2, i + 1, slc_far, st)
    o_slc = _flash_finish(st)

    st = _flash_tile(_dot_nt(q4, bd_ref[4, i]), near_bias(head_b, 0), causal, bd_ref[5, i], _flash_init())

    def win_near(d, st):
        return _flash_tile(_dot_nt(q4, bd_ref[4, i - 1]), near_bias(head_b, 1), None, bd_ref[5, i - 1], st)

    st = lax.fori_loop(0, jnp.minimum(i, 1), win_near, st)

    def win_mid(d, st):
        j = i - d
        return _flash_tile(_dot_nt(q4, bd_ref[4, j]), far_bias(head_b), None, bd_ref[5, j], st)

    n_full = B_WINDOW // TQ
    st = lax.fori_loop(2, jnp.minimum(i, n_full - 1) + 1, win_mid, st)

    def win_edge(d, st):
        j = i - n_full
        return _flash_tile(_dot_nt(q4, bd_ref[4, j]), far_bias(head_b), upper, bd_ref[5, j], st)

    st = lax.fori_loop(0, jnp.where(i >= n_full, 1, 0), win_edge, st)
    o_win = _flash_finish(st)

    sg = jax.nn.sigmoid(gate_ref[0].astype(F32))
    sg_hi = sg.astype(BF16)
    sg_lo = (sg - sg_hi.astype(F32)).astype(BF16)
    for p in range(PAIRS):
        cols = slice(p * LANES, (p + 1) * LANES)
        gts = [_dot(sg_hi, e3_ref[0, br, :, cols]) + _dot(sg_lo, e3_ref[0, br, :, cols]) for br in range(3)]
        o = gts[0] * o_cmp[p * TQ:(p + 1) * TQ] + gts[1] * o_slc[p] + gts[2] * o_win[p]
        ob_ref[0, :, cols] = o.astype(ob_ref.dtype)


def _even_attention(pe, kvc, rel_bias, sinks, t_tiles, cbias, e3, esel, ovt):
    b, s, _ = pe.shape
    n_tiles = s // TQ
    half = PAIRS * LANES

    def seg(k):
        return pl.BlockSpec((1, s, LANES), lambda bi, g, i, k=k: (bi, 0, k))

    in_specs = [
        pl.BlockSpec(memory_space=pltpu.SMEM),
        pl.BlockSpec(memory_space=pltpu.SMEM),
        pl.BlockSpec((1, TQ, half), lambda bi, g, i: (bi, i, COL_QA // half + g)),
        pl.BlockSpec((1, TQ, half), lambda bi, g, i: (bi, i, COL_QB // half + g)),
        pl.BlockSpec((1, TQ, LANES), lambda bi, g, i: (bi, i, SEG_GATE)),
        seg(SEG_KA), seg(SEG_VA), seg(SEG_KS), seg(SEG_VS), seg(SEG_KW), seg(SEG_VW),
        pl.BlockSpec((1, 1, TQ, LANES), lambda bi, g, i: (0, bi, 0, 0)),
        pl.BlockSpec((1, 1, TQ, LANES), lambda bi, g, i: (1, bi, 0, 0)),
        pl.BlockSpec(t_tiles.shape, lambda bi, g, i: (0, 0, 0, 0)),
        pl.BlockSpec((2 * PAIRS, TQ, LANES), lambda bi, g, i: (g, i, 0)),
        pl.BlockSpec((1, 3, LANES, half), lambda bi, g, i: (g, 0, 0, 0)),
        pl.BlockSpec(esel.shape, lambda bi, g, i: (0, 0)),
        pl.BlockSpec(ovt.shape, lambda bi, g, i: (0, 0)),
    ]
    out_spec = pl.BlockSpec((1, TQ, half), lambda bi, g, i: (bi, i, g))
    return pl.pallas_call(
        functools.partial(_even_attn_kernel, n_tiles=n_tiles),
        grid=(b, GROUPS, n_tiles),
        in_specs=in_specs,
        out_specs=[out_spec, out_spec],
        out_shape=[jax.ShapeDtypeStruct((b, s, GROUPS * half), BF16)] * 2,
        scratch_shapes=[pltpu.VMEM((6, n_tiles, 2 * TQ, LANES), BF16),
                        pltpu.VMEM((2, 2 * TQ, LANES), BF16),
                        pltpu.VMEM((n_tiles, TQ, TQ), F32)],
        compiler_params=_params("arbitrary", "arbitrary", "arbitrary"),
    )(rel_bias, sinks, pe, pe, pe, pe, pe, pe, pe, pe, pe, kvc, kvc, t_tiles, cbias, e3, esel, ovt)


def _mla_prep_kernel(x_ref, g_ref, win_ref, qn_ref, kvn_ref, wq_ref, wk_ref, wvt_ref, cos_ref, sin_ref,
                     q_ref, k_ref, vt_ref):
    xn = _rms(x_ref[...], g_ref[...]).astype(BF16)
    proj = _dot(xn, win_ref[...])
    cq = _rms(proj[:, :Q_LORA], qn_ref[...]).astype(BF16)
    ckv = _rms(proj[:, Q_LORA:Q_LORA + KV_LORA], kvn_ref[...]).astype(BF16)
    cos = cos_ref[...]
    sin = sin_ref[...]

    def rope(x):
        return x * cos + pltpu.roll(x, HEAD_DIM, 1) * sin

    kr = rope(proj[:, Q_LORA + KV_LORA:]).astype(BF16)
    scale = (NOPE_DIM + ROPE_DIM) ** -0.5
    for h in range(C_HEADS):
        q = _dot(cq, wq_ref[:, h * 256:(h + 1) * 256])
        q_ref[:, h * 256:h * 256 + LANES] = (q[:, :LANES] * scale).astype(BF16)
        q_ref[:, h * 256 + LANES:(h + 1) * 256] = (rope(q[:, LANES:]) * scale).astype(BF16)
        k_ref[:, h * 256:h * 256 + LANES] = _dot(ckv, wk_ref[:, h * LANES:(h + 1) * LANES]).astype(BF16)
        k_ref[:, h * 256 + LANES:(h + 1) * 256] = kr
        vt_ref[0, h, 0] = _dot_nt(wvt_ref[h], ckv).astype(BF16)


def _mla_prep(x, g, w_in, qn, kvn, wq, wk, wvt, cos, sin, b, s):
    m, d = x.shape
    tm = MLA_TK
    pos_tiles = s // tm
    full = lambda a: pl.BlockSpec(a.shape, lambda i: (0,) * a.ndim)
    return pl.pallas_call(
        _mla_prep_kernel,
        grid=(m // tm,),
        in_specs=[pl.BlockSpec((tm, d), lambda i: (i, 0)),
                  pl.BlockSpec((1, d), lambda i: (0, 0)),
                  full(w_in), pl.BlockSpec((1, Q_LORA), lambda i: (0, 0)),
                  pl.BlockSpec((1, KV_LORA), lambda i: (0, 0)), full(wq), full(wk), full(wvt),
                  pl.BlockSpec((tm, LANES), lambda i: (i % pos_tiles, 0)),
                  pl.BlockSpec((tm, LANES), lambda i: (i % pos_tiles, 0))],
        out_specs=[pl.BlockSpec((tm, C_HEADS * 256), lambda i: (i, 0)),
                   pl.BlockSpec((tm, C_HEADS * 256), lambda i: (i, 0)),
                   pl.BlockSpec((1, C_HEADS, 1, V_DIM, tm), lambda i: (i // pos_tiles, 0, i % pos_tiles, 0, 0))],
        out_shape=[jax.ShapeDtypeStruct((m, C_HEADS * 256), BF16),
                   jax.ShapeDtypeStruct((m, C_HEADS * 256), BF16),
                   jax.ShapeDtypeStruct((b, C_HEADS, pos_tiles, V_DIM, tm), BF16)],
        compiler_params=_params("parallel"),
    )(x, g.reshape(1, d), w_in, qn.reshape(1, -1), kvn.reshape(1, -1), wq, wk, wvt, cos, sin)


def _mla_attn_kernel(q_ref, k_ref, vt_ref, o_ref, *, n_q):
    tq, tk = MLA_TQ, MLA_TK
    r = tq // tk
    ki = lax.broadcasted_iota(jnp.int32, (tk, tq), 0)
    qi = lax.broadcasted_iota(jnp.int32, (tk, tq), 1)

    def q_tile(t, c):
        rows = pl.ds(pl.multiple_of(t * tq, tq), tq)
        q = q_ref[0, rows, :]

        def logits(j):
            return _dot_nt(k_ref[0, pl.ds(pl.multiple_of(j * tk, tk), tk), :], q)

        def update(s, j, st):
            m, l, acc = st
            mn = jnp.maximum(m, jnp.max(s, axis=0, keepdims=True))
            a = jnp.exp(m - mn)
            p = jnp.exp(s - mn)
            l = a * l + jnp.sum(p, axis=0, keepdims=True)
            acc = a * acc + _dot(vt_ref[0, 0, j], p.astype(BF16))
            return mn, l, acc

        s_diag = [jnp.where(ki + d * tk <= qi, logits(t * r + d), NEG) for d in range(r)]
        s_first = logits(0)
        st = (jnp.full((1, tq), NEG, F32), jnp.zeros((1, tq), F32), jnp.zeros((V_DIM, tq), F32))
        for d in range(r):
            st = update(s_diag[d], t * r + d, st)

        def body(j, carry):
            s, st = carry
            s_next = logits(jnp.minimum(j + 1, t * r - 1))
            return s_next, update(s, j, st)

        _, (m, l, acc) = lax.fori_loop(0, t * r, body, (s_first, st))
        o_ref[0, rows, :] = (acc * (1.0 / l)).T.astype(o_ref.dtype)
        return c

    lax.fori_loop(0, n_q, q_tile, 0)


def _mla_attention(q, k, vt):
    b, s, _ = q.shape
    return pl.pallas_call(
        functools.partial(_mla_attn_kernel, n_q=s // MLA_TQ),
        grid=(b, C_HEADS),
        in_specs=[pl.BlockSpec((1, s, 256), lambda bi, h: (bi, 0, h)),
                  pl.BlockSpec((1, s, 256), lambda bi, h: (bi, 0, h)),
                  pl.BlockSpec((1, 1, s // MLA_TK, V_DIM, MLA_TK), lambda bi, h: (bi, h, 0, 0, 0))],
        out_specs=pl.BlockSpec((1, s, LANES), lambda bi, h: (bi, 0, h)),
        out_shape=jax.ShapeDtypeStruct((b, s, C_HEADS * V_DIM), BF16),
        compiler_params=_params("parallel", "arbitrary"),
    )(q, k, vt)


def _even_in_weight(w):
    cuts = np.cumsum([1024, 128, 128, 1024, 128, 128, 128, 128, 128, 128, 48])[:-1]
    qa, ka, va, qb, kc, vc, ks, vs, kw, vw, gate = jnp.split(w, [int(c) for c in cuts], axis=1)
    gate = jnp.pad(gate, ((0, 0), (0, LANES - gate.shape[1])))
    return jnp.concatenate([qa, qb, ka, va, kc, vc, ks, vs, kw, vw, gate], axis=1).astype(BF16)


def _compress_weights(pos, w1, w2):
    half = CMP_BLOCK // 2
    eye = jnp.eye(GROUPS, dtype=w1.dtype)
    w1r = w1.reshape(2, half, HEAD_DIM, CMP_HIDDEN)
    wab = jnp.einsum('rldh,gk->rlgdkh', w1r, eye).reshape(2, half * GROUPS * HEAD_DIM, GROUPS * CMP_HIDDEN)
    posr = jnp.broadcast_to(pos.reshape(2, half, 1, HEAD_DIM), (2, half, GROUPS, HEAD_DIM)).reshape(2, 1, -1)
    w2bd = jnp.einsum('hd,gk->ghkd', w2, eye).reshape(GROUPS * CMP_HIDDEN, GROUPS * HEAD_DIM)
    return posr[0], posr[1], wab[0].astype(BF16), wab[1].astype(BF16), w2bd.astype(BF16)


def _rope_chunk_cols(w):
    z = jnp.zeros((w.shape[0], ROPE_DIM // 2), w.dtype)
    return jnp.concatenate([w[:, :ROPE_DIM // 2], z, w[:, ROPE_DIM // 2:], z], axis=1)


def _mla_weights(w_in, w_q_up, w_kv_up):
    w_in2 = jnp.concatenate([w_in[:, :Q_LORA + KV_LORA], _rope_chunk_cols(w_in[:, Q_LORA + KV_LORA:])], axis=1)
    wq = w_q_up.reshape(Q_LORA, C_HEADS, NOPE_DIM + ROPE_DIM)
    z = jnp.zeros((Q_LORA, C_HEADS, ROPE_DIM // 2), w_q_up.dtype)
    wq2 = jnp.concatenate([wq[..., :NOPE_DIM], wq[..., NOPE_DIM:NOPE_DIM + ROPE_DIM // 2], z,
                           wq[..., NOPE_DIM + ROPE_DIM // 2:], z], axis=-1).reshape(Q_LORA, C_HEADS * 256)
    wkv = w_kv_up.reshape(KV_LORA, C_HEADS, NOPE_DIM + V_DIM)
    wk = wkv[..., :NOPE_DIM].reshape(KV_LORA, C_HEADS * NOPE_DIM)
    wvt = wkv[..., NOPE_DIM:].transpose(1, 2, 0)
    return w_in2.astype(BF16), wq2.astype(BF16), wk.astype(BF16), wvt.astype(BF16)


def _rope_tables(s):
    inv = 1.0 / (ROPE_THETA ** (jnp.arange(0, ROPE_DIM, 2, dtype=F32) / ROPE_DIM))
    ang = jnp.arange(s, dtype=F32)[:, None] * inv[None]
    cos, sin = jnp.cos(ang), jnp.sin(ang)
    z = jnp.zeros_like(cos)
    return jnp.concatenate([cos, z, cos, z], axis=1), jnp.concatenate([-sin, z, sin, z], axis=1)


def _static_tables(s):
    q = np.arange(TQ)[:, None]
    k = np.arange(TQ)[None, :]
    near = np.stack([_bucket_np(q - k), _bucket_np(q - k + TQ)])
    t = np.arange(s)[:, None]
    c = np.arange(LANES)[None, :]
    cmp_map = _bucket_np(t - (c * CMP_STRIDE + CMP_BLOCK - 1))
    ns = s // SLC_BLOCK
    nc = (s - CMP_BLOCK) // CMP_STRIDE + 1
    c_start = np.arange(LANES) * CMP_STRIDE
    s_start = np.arange(ns) * SLC_BLOCK
    ovt = ((c_start[None, :] <= s_start[:, None] + SLC_BLOCK - 1) &
           (c_start[None, :] + CMP_BLOCK - 1 >= s_start[:, None]) &
           (np.arange(LANES)[None, :] < nc)).astype(np.float32)
    esel = (np.arange(s)[None, :] // SLC_BLOCK == np.arange(LANES)[:, None]).astype(np.float32)
    e3 = np.zeros((GROUPS, 3, LANES, PAIRS * LANES), np.float32)
    for g in range(GROUPS):
        for u in range(2 * PAIRS):
            for br in range(3):
                e3[g, br, (8 * g + u) * 3 + br, u * HEAD_DIM:(u + 1) * HEAD_DIM] = 1.0
    return near, cmp_map, ovt, esel, e3


def _even_layer(x, b, s, rel_bias, norm, w_in, sinks, pos_k, pos_v, k_w1, k_w2, v_w1, v_w2, w_out):
    near, cmp_map, ovt, esel, e3 = _static_tables(s)
    t_tiles = _bias_tiles(rel_bias, near, 0, A_HEADS + B_HEADS)
    cbias = _bias_tiles(rel_bias, cmp_map, A_HEADS, B_HEADS)

    pe = _norm_matmul(x, norm, _even_in_weight(w_in), 1024, 640, BF16).reshape(b, s, EVEN_COLS)

    rows = s // (CMP_BLOCK // 2)
    hkv = jnp.stack([pe[:, :, SEG_KC * LANES:(SEG_KC + 1) * LANES],
                     pe[:, :, SEG_VC * LANES:(SEG_VC + 1) * LANES]]).reshape(2, b, rows, -1)
    ck = _compress_weights(pos_k, k_w1, k_w2)
    cv = _compress_weights(pos_v, v_w1, v_w2)
    kvc = _compress(hkv, *[jnp.stack([a, c]) for a, c in zip(ck, cv)])
    assert rows <= TQ
    kvc = jnp.pad(kvc, ((0, 0), (0, 0), (0, TQ - rows), (0, 0)))

    oa, ob = _even_attention(pe, kvc, rel_bias, sinks, t_tiles, cbias,
                             jnp.asarray(e3, BF16), jnp.asarray(esel, BF16), jnp.asarray(ovt, BF16))
    w_out = w_out.astype(BF16)
    n_a = A_HEADS * HEAD_DIM
    return _out_proj(x, [oa.reshape(b * s, -1), ob.reshape(b * s, -1)], [w_out[:n_a], w_out[n_a:]], 512)


def _odd_layer(x, b, s, norm, w_in, q_norm, w_q_up, kv_norm, w_kv_up, w_out):
    w_in2, wq2, wk, wvt = _mla_weights(w_in, w_q_up, w_kv_up)
    cos, sin = _rope_tables(s)
    q, k, vt = _mla_prep(x, norm, w_in2, q_norm, kv_norm, wq2, wk, wvt, cos, sin, b, s)
    o = _mla_attention(q.reshape(b, s, -1), k.reshape(b, s, -1), vt)
    return _out_proj(x, [o.reshape(b * s, -1)], [w_out.astype(BF16)], 512)


def kernel(x, rel_bias, norm_mix_e, w_in_e, sinks, cmp_pos_k, cmp_pos_v, cmp_k_w1, cmp_k_w2, cmp_v_w1, cmp_v_w2, w_out_e, norm_mix_o, w_in_o, q_norm, w_q_up, kv_norm, w_kv_up, w_out_o, norm_mlp, w_up, w_down, norm_final):
    b, s, d = x.shape
    depth = norm_mlp.shape[0]
    h = x.reshape(b * s, d)
    for layer in range(depth):
        i = layer // 2
        if layer % 2 == 0:
            h = _even_layer(h, b, s, rel_bias, norm_mix_e[i], w_in_e[i], sinks[i], cmp_pos_k[i], cmp_pos_v[i],
                            cmp_k_w1[i], cmp_k_w2[i], cmp_v_w1[i], cmp_v_w2[i], w_out_e[i])
        else:
            h = _odd_layer(h, b, s, norm_mix_o[i], w_in_o[i], q_norm[i], w_q_up[i], kv_norm[i], w_kv_up[i],
                           w_out_o[i])
        h = _mlp(h, norm_mlp[layer], w_up[layer].astype(BF16), w_down[layer].astype(BF16), norm_final,
                 512, 512, layer == depth - 1)
    return h.reshape(b, s, d)
```

```python
import functools
import math

import numpy as np
import jax
import jax.numpy as jnp
from jax import lax
from jax.experimental import pallas as pl
from jax.experimental.pallas import tpu as pltpu

F32 = jnp.float32
BF16 = jnp.bfloat16

LANES = 128
VMEM_LIMIT_BYTES = 56 * 1024 * 1024

EPS = 1e-6
NEG = -1e30
BIG = 1e4
HEAD_DIM = 64
TQ = 128
MLA_TQ = 512
MLA_TK = 256
N_BUCKETS = 32
MAX_DISTANCE = 128
A_HEADS = 16
B_HEADS = 16
GROUPS = 2
PAIRS = 4
W4 = PAIRS * TQ
CMP_BLOCK = 32
CMP_STRIDE = 16
CMP_HIDDEN = 256
SLC_BLOCK = 64
SLC_TOPK = 8
A_WINDOW = 128
B_WINDOW = 512
C_HEADS = 16
Q_LORA = 768
KV_LORA = 512
NOPE_DIM = 128
ROPE_DIM = 64
V_DIM = 128
ROPE_THETA = 10000.0

COL_QA, COL_QB = 0, 1024
SEG_KA, SEG_VA, SEG_KC, SEG_VC, SEG_KS, SEG_VS, SEG_KW, SEG_VW, SEG_GATE = range(16, 25)
EVEN_COLS = 25 * LANES
K_A, K_S, K_W = 0, 1, 2


def _params(*sem):
    return pltpu.CompilerParams(dimension_semantics=sem, vmem_limit_bytes=VMEM_LIMIT_BYTES)


def _rms(x, g):
    return x * lax.rsqrt(jnp.mean(x * x, axis=-1, keepdims=True) + EPS) * g


def _dot(a, b):
    return jnp.dot(a, b, preferred_element_type=F32)


def _bucket_np(dist):
    dist = np.maximum(dist, 0)
    max_exact = N_BUCKETS // 2
    d = np.maximum(dist, 1).astype(np.float32)
    large = max_exact + (np.log(d / np.float32(max_exact)) / np.float32(math.log(MAX_DISTANCE / max_exact))
                         * np.float32(N_BUCKETS - max_exact)).astype(np.int32)
    large = np.minimum(large, N_BUCKETS - 1)
    return np.where(dist < max_exact, dist, large).astype(np.int32)


def _bias_kernel(tab_ref, bm_ref, o_ref, *, head0):
    h = pl.program_id(0) + head0
    bm = bm_ref[...]
    acc = jnp.zeros(bm.shape, F32)
    for b in range(N_BUCKETS):
        acc = jnp.where(bm == b, tab_ref[b, h], acc)
    o_ref[0] = acc - tab_ref[N_BUCKETS - 1, h]


def _bias_tiles(rel_bias, bucket_map, head0, n_heads):
    shp = bucket_map.shape
    nd = len(shp)
    return pl.pallas_call(
        functools.partial(_bias_kernel, head0=head0),
        grid=(n_heads,),
        in_specs=[pl.BlockSpec(memory_space=pltpu.SMEM),
                  pl.BlockSpec(shp, lambda h: (0,) * nd)],
        out_specs=pl.BlockSpec((1,) + shp, lambda h: (h,) + (0,) * nd),
        out_shape=jax.ShapeDtypeStruct((n_heads,) + shp, F32),
        compiler_params=_params("arbitrary"),
    )(rel_bias, jnp.asarray(bucket_map))


def _norm_matmul_kernel(x_ref, g_ref, w_ref, o_ref, xn_ref):
    @pl.when(pl.program_id(1) == 0)
    def _():
        xn_ref[...] = _rms(x_ref[...], g_ref[...]).astype(BF16)

    o_ref[...] = _dot(xn_ref[...], w_ref[...]).astype(o_ref.dtype)


def _norm_matmul(x, g, w, tm, tn, out_dtype):
    m, d = x.shape
    n = w.shape[1]
    return pl.pallas_call(
        _norm_matmul_kernel,
        grid=(m // tm, n // tn),
        in_specs=[pl.BlockSpec((tm, d), lambda i, j: (i, 0)),
                  pl.BlockSpec((1, d), lambda i, j: (0, 0)),
                  pl.BlockSpec((d, tn), lambda i, j: (0, j))],
        out_specs=pl.BlockSpec((tm, tn), lambda i, j: (i, j)),
        out_shape=jax.ShapeDtypeStruct((m, n), out_dtype),
        scratch_shapes=[pltpu.VMEM((tm, d), BF16)],
        compiler_params=_params("parallel", "arbitrary"),
    )(x, g.reshape(1, d), w)


def _out_proj_kernel(*refs, n_in):
    x_ref = refs[0]
    o_ref = refs[2 * n_in + 1]
    acc = x_ref[...]
    for t in range(n_in):
        acc = acc + _dot(refs[1 + t][...], refs[1 + n_in + t][...])
    o_ref[...] = acc


def _out_proj(x, acts, ws, tm):
    m, d = x.shape
    n_in = len(acts)
    in_specs = [pl.BlockSpec((tm, d), lambda i: (i, 0))]
    in_specs += [pl.BlockSpec((tm, a.shape[1]), lambda i: (i, 0)) for a in acts]
    in_specs += [pl.BlockSpec(w.shape, lambda i: (0, 0)) for w in ws]
    return pl.pallas_call(
        functools.partial(_out_proj_kernel, n_in=n_in),
        grid=(m // tm,),
        in_specs=in_specs,
        out_specs=pl.BlockSpec((tm, d), lambda i: (i, 0)),
        out_shape=jax.ShapeDtypeStruct((m, d), F32),
        compiler_params=_params("parallel"),
    )(x, *acts, *ws)


def _mlp_kernel(x_ref, g_ref, wu_ref, wd_ref, gf_ref, o_ref, xn_ref, *, final_norm):
    j = pl.program_id(1)

    @pl.when(j == 0)
    def _():
        x = x_ref[...]
        xn_ref[...] = _rms(x, g_ref[...]).astype(BF16)
        o_ref[...] = x

    h = _dot(xn_ref[...], wu_ref[...])
    a = jnp.square(jnp.maximum(h, 0.0)).astype(BF16)
    o_ref[...] += _dot(a, wd_ref[...])

    if final_norm:
        @pl.when(j == pl.num_programs(1) - 1)
        def _():
            o_ref[...] = _rms(o_ref[...], gf_ref[...])


def _mlp(x, g, w_up, w_down, g_final, tm, tf, final_norm):
    m, d = x.shape
    ff = w_up.shape[1]
    return pl.pallas_call(
        functools.partial(_mlp_kernel, final_norm=final_norm),
        grid=(m // tm, ff // tf),
        in_specs=[pl.BlockSpec((tm, d), lambda i, j: (i, 0)),
                  pl.BlockSpec((1, d), lambda i, j: (0, 0)),
                  pl.BlockSpec((d, tf), lambda i, j: (0, j)),
                  pl.BlockSpec((tf, d), lambda i, j: (j, 0)),
                  pl.BlockSpec((1, d), lambda i, j: (0, 0))],
        out_specs=pl.BlockSpec((tm, d), lambda i, j: (i, 0)),
        out_shape=jax.ShapeDtypeStruct((m, d), F32),
        scratch_shapes=[pltpu.VMEM((tm, d), BF16)],
        compiler_params=_params("parallel", "arbitrary"),
    )(x, g.reshape(1, d), w_up, w_down, g_final.reshape(1, d))


def _gelu_tanh(x):
    return 0.5 * x * (1.0 + jnp.tanh(math.sqrt(2.0 / math.pi) * (x + 0.044715 * (x * x * x))))


def _compress_kernel(h_ref, pa_ref, pb_ref, wa_ref, wb_ref, w2_ref, o_ref):
    h = h_ref[0, 0].astype(F32)
    ha = _dot((h + pa_ref[0]).astype(BF16), wa_ref[0])
    hb = _dot((h + pb_ref[0]).astype(BF16), wb_ref[0])
    n = hb.shape[0]
    pre = ha + pltpu.roll(hb, n - 1, 0)
    o_ref[0, 0] = _dot(_gelu_tanh(pre).astype(BF16), w2_ref[0]).astype(o_ref.dtype)


def _compress(hkv, pa, pb, wa, wb, w2):
    _, b, r, w = hkv.shape
    hid2 = wa.shape[2]
    return pl.pallas_call(
        _compress_kernel,
        grid=(2, b),
        in_specs=[pl.BlockSpec((1, 1, r, w), lambda t, i: (t, i, 0, 0)),
                  pl.BlockSpec((1, 1, w), lambda t, i: (t, 0, 0)),
                  pl.BlockSpec((1, 1, w), lambda t, i: (t, 0, 0)),
                  pl.BlockSpec((1, w, hid2), lambda t, i: (t, 0, 0)),
                  pl.BlockSpec((1, w, hid2), lambda t, i: (t, 0, 0)),
                  pl.BlockSpec((1, hid2, LANES), lambda t, i: (t, 0, 0))],
        out_specs=pl.BlockSpec((1, 1, r, LANES), lambda t, i: (t, i, 0, 0)),
        out_shape=jax.ShapeDtypeStruct((2, b, r, LANES), BF16),
        compiler_params=_params("arbitrary", "arbitrary"),
    )(hkv, pa, pb, wa, wb, w2)


def _stack_pairs_t(q_ref):
    cols = [q_ref[0, :, p * LANES:(p + 1) * LANES].astype(F32).T for p in range(PAIRS)]
    return (jnp.concatenate(cols, axis=1) * (HEAD_DIM ** -0.5)).astype(BF16)


def _rows_by_half(a, b):
    return jnp.concatenate([jnp.broadcast_to(a, (HEAD_DIM, W4)), jnp.broadcast_to(b, (HEAD_DIM, W4))], axis=0)


def _flash_update(s, bias, keep, vbd_t, st):
    m, l, acc = st
    nm, nl, ps, al = [], [], [], []
    for hf in range(2):
        sh = s[hf * TQ:(hf + 1) * TQ]
        if bias is not None:
            sh = sh + bias[hf]
        if keep is not None:
            sh = jnp.where(keep, sh, NEG)
        mn = jnp.maximum(m[hf], jnp.max(sh, axis=0, keepdims=True))
        a = jnp.exp(m[hf] - mn)
        pe = jnp.exp(sh - mn)
        nl.append(a * l[hf] + jnp.sum(pe, axis=0, keepdims=True))
        nm.append(mn)
        ps.append(pe.astype(BF16))
        al.append(a)
    pv = _dot(vbd_t, jnp.concatenate(ps, axis=0))
    return tuple(nm), tuple(nl), _rows_by_half(al[0], al[1]) * acc + pv


def _flash_init():
    row = lambda v: jnp.full((1, W4), v, F32)
    return (row(NEG), row(NEG)), (row(0.0), row(0.0)), jnp.zeros((LANES, W4), F32)


def _flash_finish(st, extra=None):
    m, l, acc = st
    inv = [1.0 / (l[hf] if extra is None else l[hf] + extra(hf, m[hf])) for hf in range(2)]
    return acc * _rows_by_half(inv[0], inv[1])


def _even_attn_kernel(tab_ref, sink_ref, qa_ref, qb_ref, gate_ref,
                      ka_ref, va_ref, ks_ref, vs_ref, kw_ref, vw_ref, kc_ref, vc_ref,
                      t_ref, cb_ref, e3_ref, esel_ref, ovt_ref,
                      oa_ref, ob_ref, kbd_ref, vbd_ref, ckbd_ref, cvbd_ref, msk_ref, *, n_tiles):
    g = pl.program_id(1)
    i = pl.program_id(2)

    def block_diag(x):
        r = pltpu.roll(x, HEAD_DIM, 1)
        lo = lax.broadcasted_iota(jnp.int32, x.shape, 1) < HEAD_DIM
        first = g == 0
        return jnp.where(lo, jnp.where(first, x, r), 0.0), jnp.where(lo, 0.0, jnp.where(first, r, x))

    @pl.when(i == 0)
    def _build():
        for t, ref in enumerate((ka_ref, ks_ref, kw_ref)):
            def kbody(j, c, t=t, ref=ref):
                top, bot = block_diag(ref[0, pl.ds(pl.multiple_of(j * TQ, TQ), TQ), :].astype(F32))
                kbd_ref[t, j, 0:TQ, :] = top.astype(BF16)
                kbd_ref[t, j, TQ:2 * TQ, :] = bot.astype(BF16)
                return c
            lax.fori_loop(0, n_tiles, kbody, 0)
        for t, ref in enumerate((va_ref, vs_ref, vw_ref)):
            def vbody(j, c, t=t, ref=ref):
                top, bot = block_diag(ref[0, pl.ds(pl.multiple_of(j * TQ, TQ), TQ), :].astype(F32))
                vbd_ref[t, j, :, 0:TQ] = top.T.astype(BF16)
                vbd_ref[t, j, :, TQ:2 * TQ] = bot.T.astype(BF16)
                return c
            lax.fori_loop(0, n_tiles, vbody, 0)
        top, bot = block_diag(kc_ref[0, 0].astype(F32))
        ckbd_ref[0:TQ, :] = top.astype(BF16)
        ckbd_ref[TQ:2 * TQ, :] = bot.astype(BF16)
        top, bot = block_diag(vc_ref[0, 0].astype(F32))
        cvbd_ref[:, 0:TQ] = top.T.astype(BF16)
        cvbd_ref[:, TQ:2 * TQ] = bot.T.astype(BF16)

    ki = lax.broadcasted_iota(jnp.int32, (TQ, W4), 0)
    qi = lax.broadcasted_iota(jnp.int32, (TQ, W4), 1) & (TQ - 1)
    dk = ki - qi
    prev = jnp.maximum(i - 1, 0)

    def near_bias(head0, delta):
        return [jnp.concatenate([t_ref[head0 + 2 * p + hf, delta] for p in range(PAIRS)], axis=1)
                for hf in range(2)]

    def head_row(fn, head0, hf):
        return jnp.concatenate([jnp.full((1, TQ), fn(head0 + 2 * p + hf), F32) for p in range(PAIRS)], axis=1)

    def write_out(o_ref, o_t):
        for p in range(PAIRS):
            o_ref[0, :, p * LANES:(p + 1) * LANES] = o_t[:, p * TQ:(p + 1) * TQ].T.astype(o_ref.dtype)

    head_a = 8 * g
    q4t = _stack_pairs_t(qa_ref)
    s0 = _dot(kbd_ref[K_A, i], q4t)
    s1 = _dot(kbd_ref[K_A, prev], q4t)
    st = _flash_update(s0, near_bias(head_a, 0), dk <= 0, vbd_ref[K_A, i], _flash_init())
    st = _flash_update(s1, near_bias(head_a, 1), dk > jnp.where(i >= 1, 0, TQ), vbd_ref[K_A, prev], st)
    sink_of = lambda h: sink_ref[h] - tab_ref[N_BUCKETS - 1, h]
    write_out(oa_ref, _flash_finish(st, lambda hf, m: jnp.exp(head_row(sink_of, head_a, hf) - m)))

    head_b = A_HEADS + 8 * g
    q4t = _stack_pairs_t(qb_ref)

    s = _dot(ckbd_ref[...], q4t)
    t_pos = i * TQ + qi
    cvis = (t_pos - CMP_STRIDE * ki - (CMP_BLOCK - 1)) >= 0
    t_row = i * TQ + (lax.broadcasted_iota(jnp.int32, (1, W4), 1) & (TQ - 1))
    anyvis = jnp.where(t_row >= CMP_BLOCK - 1, 1.0, 0.0)
    psum = jnp.zeros((TQ, TQ), F32)
    ps = []
    for hf in range(2):
        cb = jnp.concatenate([cb_ref[2 * p + hf] for p in range(PAIRS)], axis=1)
        sh = jnp.where(cvis, s[hf * TQ:(hf + 1) * TQ] + cb, NEG)
        pe = jnp.exp(sh - jnp.max(sh, axis=0, keepdims=True))
        pc = pe * (anyvis / jnp.sum(pe, axis=0, keepdims=True))
        for p in range(PAIRS):
            psum = psum + pc[:, p * TQ:(p + 1) * TQ]
        ps.append(pc.astype(BF16))
    o_cmp = _dot(cvbd_ref[...], jnp.concatenate(ps, axis=0))

    ph = psum.astype(BF16)
    r1 = psum - ph.astype(F32)
    pm = r1.astype(BF16)
    pl_ = (r1 - pm.astype(F32)).astype(BF16)
    ovt = ovt_ref[...]
    pslc = _dot(ovt, ph) + _dot(ovt, pm) + _dot(ovt, pl_)
    ns = pslc.shape[0]
    nb = lax.broadcasted_iota(jnp.int32, (ns, TQ), 0)
    tq = i * TQ + lax.broadcasted_iota(jnp.int32, (ns, TQ), 1)
    cur = tq // SLC_BLOCK
    forced = jnp.where(nb == 0, 1.0, 0.0) + jnp.where(nb == cur, 1.0, 0.0) + jnp.where(nb == cur - 1, 1.0, 0.0)
    score = jnp.where(nb * SLC_BLOCK > tq, NEG, jnp.where(forced > 0.0, BIG, pslc))
    rank = jnp.zeros((ns, TQ), F32)
    for mth in range(ns):
        sm = score[mth:mth + 1, :]
        tie = jnp.where(nb > mth, 1.0, 0.0)
        rank = rank + jnp.where(sm > score, 1.0, jnp.where(sm == score, tie, 0.0))
    sel = jnp.where(rank < SLC_TOPK, 1.0, 0.0)
    sel = jnp.concatenate([sel, jnp.zeros((LANES - ns, TQ), F32)], axis=0).astype(BF16)
    msk_ref[...] = _dot(esel_ref[...], sel).reshape(n_tiles, TQ, TQ)

    def sel_tile(j):
        return jnp.concatenate([msk_ref[j]] * PAIRS, axis=1)

    s0 = _dot(kbd_ref[K_S, i], q4t)
    s1 = _dot(kbd_ref[K_S, prev], q4t)
    st = _flash_update(s0, near_bias(head_b, 0), jnp.where(dk <= 0, sel_tile(i), 0.0) > 0.5,
                       vbd_ref[K_S, i], _flash_init())
    st = _flash_update(s1, near_bias(head_b, 1), sel_tile(prev) > jnp.where(i >= 1, 0.5, 2.0),
                       vbd_ref[K_S, prev], st)

    def far_logits(d):
        return _dot(kbd_ref[K_S, i - d], q4t)

    def far_body(d, carry):
        s, st = carry
        s_next = far_logits(jnp.minimum(d + 1, i))
        j = i - d
        return s_next, _flash_update(s, None, sel_tile(j) > 0.5, vbd_ref[K_S, j], st)

    _, st = lax.fori_loop(2, i + 1, far_body, (far_logits(jnp.minimum(2, i)), st))
    o_slc = _flash_finish(st)

    n_full = B_WINDOW // TQ
    tiles = [jnp.maximum(i - d, 0) for d in range(n_full + 1)]
    logits = [_dot(kbd_ref[K_W, j], q4t) for j in tiles]
    st = _flash_update(logits[0], near_bias(head_b, 0), dk <= 0, vbd_ref[K_W, tiles[0]], _flash_init())
    for d in range(1, n_full + 1):
        lowest = 0 if d == n_full else -TQ
        keep = dk > jnp.where(i >= d, lowest, TQ)
        st = _flash_update(logits[d], near_bias(head_b, 1) if d == 1 else None, keep, vbd_ref[K_W, tiles[d]], st)
    o_win = _flash_finish(st)

    sg = jax.nn.sigmoid(gate_ref[0].astype(F32))
    sg_hi = sg.astype(BF16)
    sg_lo = (sg - sg_hi.astype(F32)).astype(BF16)
    for p in range(PAIRS):
        cols = slice(p * LANES, (p + 1) * LANES)
        qs = slice(p * TQ, (p + 1) * TQ)
        gts = [_dot(sg_hi, e3_ref[0, br, :, cols]) + _dot(sg_lo, e3_ref[0, br, :, cols]) for br in range(3)]
        o = gts[0] * o_cmp[:, qs].T + gts[1] * o_slc[:, qs].T + gts[2] * o_win[:, qs].T
        ob_ref[0, :, cols] = o.astype(ob_ref.dtype)


def _even_attention(pe, kvc, rel_bias, sinks, t_tiles, cbias, e3, esel, ovt):
    b, s, _ = pe.shape
    n_tiles = s // TQ
    half = PAIRS * LANES

    def seg(k):
        return pl.BlockSpec((1, s, LANES), lambda bi, g, i, k=k: (bi, 0, k))

    in_specs = [
        pl.BlockSpec(memory_space=pltpu.SMEM),
        pl.BlockSpec(memory_space=pltpu.SMEM),
        pl.BlockSpec((1, TQ, half), lambda bi, g, i: (bi, i, COL_QA // half + g)),
        pl.BlockSpec((1, TQ, half), lambda bi, g, i: (bi, i, COL_QB // half + g)),
        pl.BlockSpec((1, TQ, LANES), lambda bi, g, i: (bi, i, SEG_GATE)),
        seg(SEG_KA), seg(SEG_VA), seg(SEG_KS), seg(SEG_VS), seg(SEG_KW), seg(SEG_VW),
        pl.BlockSpec((1, 1, TQ, LANES), lambda bi, g, i: (0, bi, 0, 0)),
        pl.BlockSpec((1, 1, TQ, LANES), lambda bi, g, i: (1, bi, 0, 0)),
        pl.BlockSpec(t_tiles.shape, lambda bi, g, i: (0, 0, 0, 0)),
        pl.BlockSpec((2 * PAIRS, TQ, TQ), lambda bi, g, i: (g, 0, i)),
        pl.BlockSpec((1, 3, LANES, half), lambda bi, g, i: (g, 0, 0, 0)),
        pl.BlockSpec(esel.shape, lambda bi, g, i: (0, 0)),
        pl.BlockSpec(ovt.shape, lambda bi, g, i: (0, 0)),
    ]
    out_spec = pl.BlockSpec((1, TQ, half), lambda bi, g, i: (bi, i, g))
    return pl.pallas_call(
        functools.partial(_even_attn_kernel, n_tiles=n_tiles),
        grid=(b, GROUPS, n_tiles),
        in_specs=in_specs,
        out_specs=[out_spec, out_spec],
        out_shape=[jax.ShapeDtypeStruct((b, s, GROUPS * half), BF16)] * 2,
        scratch_shapes=[pltpu.VMEM((3, n_tiles, 2 * TQ, LANES), BF16),
                        pltpu.VMEM((3, n_tiles, LANES, 2 * TQ), BF16),
                        pltpu.VMEM((2 * TQ, LANES), BF16),
                        pltpu.VMEM((LANES, 2 * TQ), BF16),
                        pltpu.VMEM((n_tiles, TQ, TQ), F32)],
        compiler_params=_params("arbitrary", "arbitrary", "arbitrary"),
    )(rel_bias, sinks, pe, pe, pe, pe, pe, pe, pe, pe, pe, kvc, kvc, t_tiles, cbias, e3, esel, ovt)


def _mla_prep_kernel(x_ref, g_ref, win_ref, qn_ref, kvn_ref, wqt_ref, wk_ref, wvt_ref,
                     cos_ref, sin_ref, cost_ref, sint_ref, qt_ref, k_ref, vt_ref):
    xn = _rms(x_ref[...], g_ref[...]).astype(BF16)
    proj = _dot(xn, win_ref[...])
    cq = _rms(proj[:, :Q_LORA], qn_ref[...])
    ckv = _rms(proj[:, Q_LORA:Q_LORA + KV_LORA], kvn_ref[...])
    cq_t = cq.T.astype(BF16)
    ckv_t = ckv.T.astype(BF16)
    ckv = ckv.astype(BF16)
    kr = proj[:, Q_LORA + KV_LORA:]
    kr = (kr * cos_ref[...] + pltpu.roll(kr, HEAD_DIM, 1) * sin_ref[...]).astype(BF16)
    cos_t = cost_ref[...]
    sin_t = sint_ref[...]
    scale = (NOPE_DIM + ROPE_DIM) ** -0.5
    for h in range(C_HEADS):
        q_t = _dot(wqt_ref[h], cq_t)
        rp = q_t[NOPE_DIM:]
        rp = rp * cos_t + pltpu.roll(rp, HEAD_DIM, 0) * sin_t
        qt_ref[0, h, 0, 0:NOPE_DIM, :] = (q_t[:NOPE_DIM] * scale).astype(BF16)
        qt_ref[0, h, 0, NOPE_DIM:, :] = (rp * scale).astype(BF16)
        k_ref[:, h * 256:h * 256 + LANES] = _dot(ckv, wk_ref[:, h * LANES:(h + 1) * LANES]).astype(BF16)
        k_ref[:, h * 256 + LANES:(h + 1) * 256] = kr
        vt_ref[0, h, 0] = _dot(wvt_ref[h], ckv_t).astype(BF16)


def _mla_prep(x, g, w_in, qn, kvn, wqt, wk, wvt, cos, sin, b, s):
    m, d = x.shape
    tm = MLA_TK
    pos_tiles = s // tm
    full = lambda a: pl.BlockSpec(a.shape, lambda i: (0,) * a.ndim)
    tile_t = lambda rows: pl.BlockSpec((1, C_HEADS, 1, rows, tm), lambda i: (i // pos_tiles, 0, i % pos_tiles, 0, 0))
    return pl.pallas_call(
        _mla_prep_kernel,
        grid=(m // tm,),
        in_specs=[pl.BlockSpec((tm, d), lambda i: (i, 0)),
                  pl.BlockSpec((1, d), lambda i: (0, 0)),
                  full(w_in), pl.BlockSpec((1, Q_LORA), lambda i: (0, 0)),
                  pl.BlockSpec((1, KV_LORA), lambda i: (0, 0)), full(wqt), full(wk), full(wvt),
                  pl.BlockSpec((tm, LANES), lambda i: (i % pos_tiles, 0)),
                  pl.BlockSpec((tm, LANES), lambda i: (i % pos_tiles, 0)),
                  pl.BlockSpec((LANES, tm), lambda i: (0, i % pos_tiles)),
                  pl.BlockSpec((LANES, tm), lambda i: (0, i % pos_tiles))],
        out_specs=[tile_t(256),
                   pl.BlockSpec((tm, C_HEADS * 256), lambda i: (i, 0)),
                   tile_t(V_DIM)],
        out_shape=[jax.ShapeDtypeStruct((b, C_HEADS, pos_tiles, 256, tm), BF16),
                   jax.ShapeDtypeStruct((m, C_HEADS * 256), BF16),
                   jax.ShapeDtypeStruct((b, C_HEADS, pos_tiles, V_DIM, tm), BF16)],
        compiler_params=_params("parallel"),
    )(x, g.reshape(1, d), w_in, qn.reshape(1, -1), kvn.reshape(1, -1), wqt, wk, wvt, cos, sin, cos.T, sin.T)


def _mla_attn_kernel(qt_ref, k_ref, vt_ref, o_ref, *, n_q):
    tq, tk = MLA_TQ, MLA_TK
    r = tq // tk
    ki = lax.broadcasted_iota(jnp.int32, (tk, tq), 0)
    qi = lax.broadcasted_iota(jnp.int32, (tk, tq), 1)

    def q_tile(t, c):
        q_t = jnp.concatenate([qt_ref[0, 0, t * r + d] for d in range(r)], axis=1)

        def logits(j):
            return _dot(k_ref[0, pl.ds(pl.multiple_of(j * tk, tk), tk), :], q_t)

        def update(s, j, st):
            m, l, acc = st
            mn = jnp.maximum(m, jnp.max(s, axis=0, keepdims=True))
            a = jnp.exp(m - mn)
            p = jnp.exp(s - mn)
            l = a * l + jnp.sum(p, axis=0, keepdims=True)
            acc = a * acc + _dot(vt_ref[0, 0, j], p.astype(BF16))
            return mn, l, acc

        s_diag = [jnp.where(ki + d * tk <= qi, logits(t * r + d), NEG) for d in range(r)]
        s_first = logits(0)
        st = (jnp.full((1, tq), NEG, F32), jnp.zeros((1, tq), F32), jnp.zeros((V_DIM, tq), F32))
        for d in range(r):
            st = update(s_diag[d], t * r + d, st)

        def body(j, carry):
            s, st = carry
            s_next = logits(jnp.minimum(j + 1, t * r - 1))
            return s_next, update(s, j, st)

        _, (m, l, acc) = lax.fori_loop(0, t * r, body, (s_first, st))
        o_ref[0, pl.ds(pl.multiple_of(t * tq, tq), tq), :] = (acc * (1.0 / l)).T.astype(o_ref.dtype)
        return c

    lax.fori_loop(0, n_q, q_tile, 0)


def _mla_attention(qt, k, vt):
    b, s, _ = k.shape
    n_k = s // MLA_TK
    return pl.pallas_call(
        functools.partial(_mla_attn_kernel, n_q=s // MLA_TQ),
        grid=(b, C_HEADS),
        in_specs=[pl.BlockSpec((1, 1, n_k, 256, MLA_TK), lambda bi, h: (bi, h, 0, 0, 0)),
                  pl.BlockSpec((1, s, 256), lambda bi, h: (bi, 0, h)),
                  pl.BlockSpec((1, 1, n_k, V_DIM, MLA_TK), lambda bi, h: (bi, h, 0, 0, 0))],
        out_specs=pl.BlockSpec((1, s, LANES), lambda bi, h: (bi, 0, h)),
        out_shape=jax.ShapeDtypeStruct((b, s, C_HEADS * V_DIM), BF16),
        compiler_params=_params("parallel", "arbitrary"),
    )(qt, k, vt)


def _even_in_weight(w):
    cuts = np.cumsum([1024, 128, 128, 1024, 128, 128, 128, 128, 128, 128, 48])[:-1]
    qa, ka, va, qb, kc, vc, ks, vs, kw, vw, gate = jnp.split(w, [int(c) for c in cuts], axis=1)
    gate = jnp.pad(gate, ((0, 0), (0, LANES - gate.shape[1])))
    return jnp.concatenate([qa, qb, ka, va, kc, vc, ks, vs, kw, vw, gate], axis=1).astype(BF16)


def _compress_weights(pos, w1, w2):
    half = CMP_BLOCK // 2
    eye = jnp.eye(GROUPS, dtype=w1.dtype)
    w1r = w1.reshape(2, half, HEAD_DIM, CMP_HIDDEN)
    wab = jnp.einsum('rldh,gk->rlgdkh', w1r, eye).reshape(2, half * GROUPS * HEAD_DIM, GROUPS * CMP_HIDDEN)
    posr = jnp.broadcast_to(pos.reshape(2, half, 1, HEAD_DIM), (2, half, GROUPS, HEAD_DIM)).reshape(2, 1, -1)
    w2bd = jnp.einsum('hd,gk->ghkd', w2, eye).reshape(GROUPS * CMP_HIDDEN, GROUPS * HEAD_DIM)
    return posr[0], posr[1], wab[0].astype(BF16), wab[1].astype(BF16), w2bd.astype(BF16)


def _rope_chunk_cols(w):
    z = jnp.zeros(w.shape[:-1] + (ROPE_DIM // 2,), w.dtype)
    return jnp.concatenate([w[..., :ROPE_DIM // 2], z, w[..., ROPE_DIM // 2:], z], axis=-1)


def _mla_weights(w_in, w_q_up, w_kv_up):
    w_in2 = jnp.concatenate([w_in[:, :Q_LORA + KV_LORA], _rope_chunk_cols(w_in[:, Q_LORA + KV_LORA:])], axis=1)
    wq = w_q_up.reshape(Q_LORA, C_HEADS, NOPE_DIM + ROPE_DIM)
    wq2 = jnp.concatenate([wq[..., :NOPE_DIM], _rope_chunk_cols(wq[..., NOPE_DIM:])], axis=-1)
    wqt = wq2.transpose(1, 2, 0)
    wkv = w_kv_up.reshape(KV_LORA, C_HEADS, NOPE_DIM + V_DIM)
    wk = wkv[..., :NOPE_DIM].reshape(KV_LORA, C_HEADS * NOPE_DIM)
    wvt = wkv[..., NOPE_DIM:].transpose(1, 2, 0)
    return w_in2.astype(BF16), wqt.astype(BF16), wk.astype(BF16), wvt.astype(BF16)


def _rope_tables(s):
    inv = 1.0 / (ROPE_THETA ** (jnp.arange(0, ROPE_DIM, 2, dtype=F32) / ROPE_DIM))
    ang = jnp.arange(s, dtype=F32)[:, None] * inv[None]
    cos, sin = jnp.cos(ang), jnp.sin(ang)
    z = jnp.zeros_like(cos)
    return jnp.concatenate([cos, z, cos, z], axis=1), jnp.concatenate([-sin, z, sin, z], axis=1)


def _static_tables(s):
    k = np.arange(TQ)[:, None]
    q = np.arange(TQ)[None, :]
    near = np.stack([_bucket_np(q - k), _bucket_np(q - k + TQ)])
    c = np.arange(LANES)[:, None]
    t = np.arange(s)[None, :]
    cmp_map = _bucket_np(t - (c * CMP_STRIDE + CMP_BLOCK - 1))
    ns = s // SLC_BLOCK
    nc = (s - CMP_BLOCK) // CMP_STRIDE + 1
    c_start = np.arange(LANES) * CMP_STRIDE
    s_start = np.arange(ns) * SLC_BLOCK
    ovt = ((c_start[None, :] <= s_start[:, None] + SLC_BLOCK - 1) &
           (c_start[None, :] + CMP_BLOCK - 1 >= s_start[:, None]) &
           (np.arange(LANES)[None, :] < nc)).astype(np.float32)
    esel = (np.arange(s)[:, None] // SLC_BLOCK == np.arange(LANES)[None, :]).astype(np.float32)
    e3 = np.zeros((GROUPS, 3, LANES, PAIRS * LANES), np.float32)
    for g in range(GROUPS):
        for u in range(2 * PAIRS):
            for br in range(3):
                e3[g, br, (8 * g + u) * 3 + br, u * HEAD_DIM:(u + 1) * HEAD_DIM] = 1.0
    return near, cmp_map, ovt, esel, e3


def _even_layer(x, b, s, rel_bias, norm, w_in, sinks, pos_k, pos_v, k_w1, k_w2, v_w1, v_w2, w_out):
    near, cmp_map, ovt, esel, e3 = _static_tables(s)
    t_tiles = _bias_tiles(rel_bias, near, 0, A_HEADS + B_HEADS)
    cbias = _bias_tiles(rel_bias, cmp_map, A_HEADS, B_HEADS)

    pe = _norm_matmul(x, norm, _even_in_weight(w_in), 1024, 640, BF16).reshape(b, s, EVEN_COLS)

    rows = s // (CMP_BLOCK // 2)
    hkv = jnp.stack([pe[:, :, SEG_KC * LANES:(SEG_KC + 1) * LANES],
                     pe[:, :, SEG_VC * LANES:(SEG_VC + 1) * LANES]]).reshape(2, b, rows, -1)
    ck = _compress_weights(pos_k, k_w1, k_w2)
    cv = _compress_weights(pos_v, v_w1, v_w2)
    kvc = _compress(hkv, *[jnp.stack([a, c]) for a, c in zip(ck, cv)])
    assert rows <= TQ
    kvc = jnp.pad(kvc, ((0, 0), (0, 0), (0, TQ - rows), (0, 0)))

    oa, ob = _even_attention(pe, kvc, rel_bias, sinks, t_tiles, cbias,
                             jnp.asarray(e3, BF16), jnp.asarray(esel, BF16), jnp.asarray(ovt, BF16))
    w_out = w_out.astype(BF16)
    n_a = A_HEADS * HEAD_DIM
    return _out_proj(x, [oa.reshape(b * s, -1), ob.reshape(b * s, -1)], [w_out[:n_a], w_out[n_a:]], 512)


def _odd_layer(x, b, s, norm, w_in, q_norm, w_q_up, kv_norm, w_kv_up, w_out):
    w_in2, wqt, wk, wvt = _mla_weights(w_in, w_q_up, w_kv_up)
    cos, sin = _rope_tables(s)
    qt, k, vt = _mla_prep(x, norm, w_in2, q_norm, kv_norm, wqt, wk, wvt, cos, sin, b, s)
    o = _mla_attention(qt, k.reshape(b, s, -1), vt)
    return _out_proj(x, [o.reshape(b * s, -1)], [w_out.astype(BF16)], 512)


def kernel(x, rel_bias, norm_mix_e, w_in_e, sinks, cmp_pos_k, cmp_pos_v, cmp_k_w1, cmp_k_w2, cmp_v_w1, cmp_v_w2, w_out_e, norm_mix_o, w_in_o, q_norm, w_q_up, kv_norm, w_kv_up, w_out_o, norm_mlp, w_up, w_down, norm_final):
    b, s, d = x.shape
    depth = norm_mlp.shape[0]
    h = x.reshape(b * s, d)
    for layer in range(depth):
        i = layer // 2
        if layer % 2 == 0:
            h = _even_layer(h, b, s, rel_bias, norm_mix_e[i], w_in_e[i], sinks[i], cmp_pos_k[i], cmp_pos_v[i],
                            cmp_k_w1[i], cmp_k_w2[i], cmp_v_w1[i], cmp_v_w2[i], w_out_e[i])
        else:
            h = _odd_layer(h, b, s, norm_mix_o[i], w_in_o[i], q_norm[i], w_q_up[i], kv_norm[i], w_kv_up[i],
                           w_out_o[i])
        h = _mlp(h, norm_mlp[layer], w_up[layer].astype(BF16), w_down[layer].astype(BF16), norm_final,
                 512, 512, layer == depth - 1)
    return h.reshape(b, s, d)
```

```python
import functools
import math

import numpy as np
import jax
import jax.numpy as jnp
from jax import lax
from jax.experimental import pallas as pl
from jax.experimental.pallas import tpu as pltpu

F32 = jnp.float32
BF16 = jnp.bfloat16

LANES = 128
VMEM_LIMIT_BYTES = 56 * 1024 * 1024

EPS = 1e-6
NEG = -1e30
BIG = 1e4
HEAD_DIM = 64
TQ = 128
MLA_TQ = 512
MLA_TK = 256
MLA_HEADS = 2
N_BUCKETS = 32
MAX_DISTANCE = 128
A_HEADS = 16
B_HEADS = 16
GROUPS = 2
PAIRS = 4
W4 = PAIRS * TQ
CMP_BLOCK = 32
CMP_STRIDE = 16
CMP_HIDDEN = 256
SLC_BLOCK = 64
SLC_TOPK = 8
A_WINDOW = 128
B_WINDOW = 512
C_HEADS = 16
Q_LORA = 768
KV_LORA = 512
NOPE_DIM = 128
ROPE_DIM = 64
V_DIM = 128
VT_ROWS = V_DIM + 16
LOG2E = math.log2(math.e)
ROPE_THETA = 10000.0

COL_QA, COL_QB = 0, 1024
SEG_KA, SEG_VA, SEG_KC, SEG_VC, SEG_KS, SEG_VS, SEG_KW, SEG_VW, SEG_GATE = range(16, 25)
EVEN_COLS = 25 * LANES
K_A, K_S, K_W = 0, 1, 2


def _params(*sem):
    return pltpu.CompilerParams(dimension_semantics=sem, vmem_limit_bytes=VMEM_LIMIT_BYTES)


def _rms(x, g):
    return x * lax.rsqrt(jnp.mean(x * x, axis=-1, keepdims=True) + EPS) * g


def _dot(a, b):
    return jnp.dot(a, b, preferred_element_type=F32)


def _bucket_np(dist):
    dist = np.maximum(dist, 0)
    max_exact = N_BUCKETS // 2
    d = np.maximum(dist, 1).astype(np.float32)
    large = max_exact + (np.log(d / np.float32(max_exact)) / np.float32(math.log(MAX_DISTANCE / max_exact))
                         * np.float32(N_BUCKETS - max_exact)).astype(np.int32)
    large = np.minimum(large, N_BUCKETS - 1)
    return np.where(dist < max_exact, dist, large).astype(np.int32)


def _bias_kernel(tab_ref, bm_ref, o_ref, *, head0):
    h = pl.program_id(0) + head0
    bm = bm_ref[...]
    acc = jnp.zeros(bm.shape, F32)
    for b in range(N_BUCKETS):
        acc = jnp.where(bm == b, tab_ref[b, h], acc)
    o_ref[0] = acc - tab_ref[N_BUCKETS - 1, h]


def _bias_tiles(rel_bias, bucket_map, head0, n_heads):
    shp = bucket_map.shape
    nd = len(shp)
    return pl.pallas_call(
        functools.partial(_bias_kernel, head0=head0),
        grid=(n_heads,),
        in_specs=[pl.BlockSpec(memory_space=pltpu.SMEM),
                  pl.BlockSpec(shp, lambda h: (0,) * nd)],
        out_specs=pl.BlockSpec((1,) + shp, lambda h: (h,) + (0,) * nd),
        out_shape=jax.ShapeDtypeStruct((n_heads,) + shp, F32),
        compiler_params=_params("arbitrary"),
    )(rel_bias, jnp.asarray(bucket_map))


def _norm_matmul_kernel(x_ref, g_ref, w_ref, o_ref, xn_ref):
    @pl.when(pl.program_id(1) == 0)
    def _():
        xn_ref[...] = _rms(x_ref[...], g_ref[...]).astype(BF16)

    o_ref[...] = _dot(xn_ref[...], w_ref[...]).astype(o_ref.dtype)


def _norm_matmul(x, g, w, tm, tn, out_dtype):
    m, d = x.shape
    n = w.shape[1]
    return pl.pallas_call(
        _norm_matmul_kernel,
        grid=(m // tm, n // tn),
        in_specs=[pl.BlockSpec((tm, d), lambda i, j: (i, 0)),
                  pl.BlockSpec((1, d), lambda i, j: (0, 0)),
                  pl.BlockSpec((d, tn), lambda i, j: (0, j))],
        out_specs=pl.BlockSpec((tm, tn), lambda i, j: (i, j)),
        out_shape=jax.ShapeDtypeStruct((m, n), out_dtype),
        scratch_shapes=[pltpu.VMEM((tm, d), BF16)],
        compiler_params=_params("parallel", "arbitrary"),
    )(x, g.reshape(1, d), w)


def _out_proj_kernel(*refs, n_in):
    x_ref = refs[0]
    o_ref = refs[2 * n_in + 1]
    acc = x_ref[...]
    for t in range(n_in):
        acc = acc + _dot(refs[1 + t][...], refs[1 + n_in + t][...])
    o_ref[...] = acc


def _out_proj(x, acts, ws, tm):
    m, d = x.shape
    n_in = len(acts)
    in_specs = [pl.BlockSpec((tm, d), lambda i: (i, 0))]
    in_specs += [pl.BlockSpec((tm, a.shape[1]), lambda i: (i, 0)) for a in acts]
    in_specs += [pl.BlockSpec(w.shape, lambda i: (0, 0)) for w in ws]
    return pl.pallas_call(
        functools.partial(_out_proj_kernel, n_in=n_in),
        grid=(m // tm,),
        in_specs=in_specs,
        out_specs=pl.BlockSpec((tm, d), lambda i: (i, 0)),
        out_shape=jax.ShapeDtypeStruct((m, d), F32),
        compiler_params=_params("parallel"),
    )(x, *acts, *ws)


def _mlp_kernel(x_ref, g_ref, wu_ref, wd_ref, gf_ref, o_ref, xn_ref, *, final_norm):
    j = pl.program_id(1)

    @pl.when(j == 0)
    def _():
        x = x_ref[...]
        xn_ref[...] = _rms(x, g_ref[...]).astype(BF16)
        o_ref[...] = x

    h = _dot(xn_ref[...], wu_ref[...])
    a = jnp.square(jnp.maximum(h, 0.0)).astype(BF16)
    o_ref[...] += _dot(a, wd_ref[...])

    if final_norm:
        @pl.when(j == pl.num_programs(1) - 1)
        def _():
            o_ref[...] = _rms(o_ref[...], gf_ref[...])


def _mlp(x, g, w_up, w_down, g_final, tm, tf, final_norm):
    m, d = x.shape
    ff = w_up.shape[1]
    return pl.pallas_call(
        functools.partial(_mlp_kernel, final_norm=final_norm),
        grid=(m // tm, ff // tf),
        in_specs=[pl.BlockSpec((tm, d), lambda i, j: (i, 0)),
                  pl.BlockSpec((1, d), lambda i, j: (0, 0)),
                  pl.BlockSpec((d, tf), lambda i, j: (0, j)),
                  pl.BlockSpec((tf, d), lambda i, j: (j, 0)),
                  pl.BlockSpec((1, d), lambda i, j: (0, 0))],
        out_specs=pl.BlockSpec((tm, d), lambda i, j: (i, 0)),
        out_shape=jax.ShapeDtypeStruct((m, d), F32),
        scratch_shapes=[pltpu.VMEM((tm, d), BF16)],
        compiler_params=_params("parallel", "arbitrary"),
    )(x, g.reshape(1, d), w_up, w_down, g_final.reshape(1, d))


def _gelu_tanh(x):
    return 0.5 * x * (1.0 + jnp.tanh(math.sqrt(2.0 / math.pi) * (x + 0.044715 * (x * x * x))))


def _compress_kernel(h_ref, pa_ref, pb_ref, wa_ref, wb_ref, w2_ref, o_ref):
    h = h_ref[0, 0].astype(F32)
    ha = _dot((h + pa_ref[0]).astype(BF16), wa_ref[0])
    hb = _dot((h + pb_ref[0]).astype(BF16), wb_ref[0])
    n = hb.shape[0]
    pre = ha + pltpu.roll(hb, n - 1, 0)
    o_ref[0, 0] = _dot(_gelu_tanh(pre).astype(BF16), w2_ref[0]).astype(o_ref.dtype)


def _compress(hkv, pa, pb, wa, wb, w2):
    _, b, r, w = hkv.shape
    hid2 = wa.shape[2]
    return pl.pallas_call(
        _compress_kernel,
        grid=(2, b),
        in_specs=[pl.BlockSpec((1, 1, r, w), lambda t, i: (t, i, 0, 0)),
                  pl.BlockSpec((1, 1, w), lambda t, i: (t, 0, 0)),
                  pl.BlockSpec((1, 1, w), lambda t, i: (t, 0, 0)),
                  pl.BlockSpec((1, w, hid2), lambda t, i: (t, 0, 0)),
                  pl.BlockSpec((1, w, hid2), lambda t, i: (t, 0, 0)),
                  pl.BlockSpec((1, hid2, LANES), lambda t, i: (t, 0, 0))],
        out_specs=pl.BlockSpec((1, 1, r, LANES), lambda t, i: (t, i, 0, 0)),
        out_shape=jax.ShapeDtypeStruct((2, b, r, LANES), BF16),
        compiler_params=_params("arbitrary", "arbitrary"),
    )(hkv, pa, pb, wa, wb, w2)


def _stack_pairs_t(q_ref):
    cols = [q_ref[0, :, p * LANES:(p + 1) * LANES].astype(F32).T for p in range(PAIRS)]
    return (jnp.concatenate(cols, axis=1) * (HEAD_DIM ** -0.5)).astype(BF16)


def _rows_by_half(a, b):
    return jnp.concatenate([jnp.broadcast_to(a, (HEAD_DIM, W4)), jnp.broadcast_to(b, (HEAD_DIM, W4))], axis=0)


def _flash_update(s, bias, keep, vbd_t, st):
    m, l, acc = st
    nm, nl, ps, al = [], [], [], []
    for hf in range(2):
        sh = s[hf * TQ:(hf + 1) * TQ]
        if bias is not None:
            sh = sh + bias[hf]
        if keep is not None:
            sh = jnp.where(keep, sh, NEG)
        mn = jnp.maximum(m[hf], jnp.max(sh, axis=0, keepdims=True))
        a = jnp.exp(m[hf] - mn)
        pe = jnp.exp(sh - mn)
        nl.append(a * l[hf] + jnp.sum(pe, axis=0, keepdims=True))
        nm.append(mn)
        ps.append(pe.astype(BF16))
        al.append(a)
    pv = _dot(vbd_t, jnp.concatenate(ps, axis=0))
    return tuple(nm), tuple(nl), _rows_by_half(al[0], al[1]) * acc + pv


def _flash_init():
    row = lambda v: jnp.full((1, W4), v, F32)
    return (row(NEG), row(NEG)), (row(0.0), row(0.0)), jnp.zeros((LANES, W4), F32)


def _flash_finish(st, extra=None):
    m, l, acc = st
    inv = [1.0 / (l[hf] if extra is None else l[hf] + extra(hf, m[hf])) for hf in range(2)]
    return acc * _rows_by_half(inv[0], inv[1])


def _even_attn_kernel(tab_ref, sink_ref, qa_ref, qb_ref, gate_ref,
                      ka_ref, va_ref, ks_ref, vs_ref, kw_ref, vw_ref, kc_ref, vc_ref,
                      t_ref, cb_ref, e3_ref, esel_ref, ovt_ref,
                      oa_ref, ob_ref, kbd_ref, vbd_ref, ckbd_ref, cvbd_ref, msk_ref, *, n_tiles):
    g = pl.program_id(1)
    i = pl.program_id(2)

    def block_diag(x):
        r = pltpu.roll(x, HEAD_DIM, 1)
        lo = lax.broadcasted_iota(jnp.int32, x.shape, 1) < HEAD_DIM
        first = g == 0
        return jnp.where(lo, jnp.where(first, x, r), 0.0), jnp.where(lo, 0.0, jnp.where(first, r, x))

    @pl.when(i == 0)
    def _build():
        for t, ref in enumerate((ka_ref, ks_ref, kw_ref)):
            def kbody(j, c, t=t, ref=ref):
                top, bot = block_diag(ref[0, pl.ds(pl.multiple_of(j * TQ, TQ), TQ), :].astype(F32))
                kbd_ref[t, j, 0:TQ, :] = top.astype(BF16)
                kbd_ref[t, j, TQ:2 * TQ, :] = bot.astype(BF16)
                return c
            lax.fori_loop(0, n_tiles, kbody, 0)
        for t, ref in enumerate((va_ref, vs_ref, vw_ref)):
            def vbody(j, c, t=t, ref=ref):
                top, bot = block_diag(ref[0, pl.ds(pl.multiple_of(j * TQ, TQ), TQ), :].astype(F32))
                vbd_ref[t, j, :, 0:TQ] = top.T.astype(BF16)
                vbd_ref[t, j, :, TQ:2 * TQ] = bot.T.astype(BF16)
                return c
            lax.fori_loop(0, n_tiles, vbody, 0)
        top, bot = block_diag(kc_ref[0, 0].astype(F32))
        ckbd_ref[0:TQ, :] = top.astype(BF16)
        ckbd_ref[TQ:2 * TQ, :] = bot.astype(BF16)
        top, bot = block_diag(vc_ref[0, 0].astype(F32))
        cvbd_ref[:, 0:TQ] = top.T.astype(BF16)
        cvbd_ref[:, TQ:2 * TQ] = bot.T.astype(BF16)

    ki = lax.broadcasted_iota(jnp.int32, (TQ, W4), 0)
    qi = lax.broadcasted_iota(jnp.int32, (TQ, W4), 1) & (TQ - 1)
    dk = ki - qi
    prev = jnp.maximum(i - 1, 0)

    def near_bias(head0, delta):
        return [jnp.concatenate([t_ref[head0 + 2 * p + hf, delta] for p in range(PAIRS)], axis=1)
                for hf in range(2)]

    def head_row(fn, head0, hf):
        return jnp.concatenate([jnp.full((1, TQ), fn(head0 + 2 * p + hf), F32) for p in range(PAIRS)], axis=1)

    def write_out(o_ref, o_t):
        for p in range(PAIRS):
            o_ref[0, :, p * LANES:(p + 1) * LANES] = o_t[:, p * TQ:(p + 1) * TQ].T.astype(o_ref.dtype)

    head_a = 8 * g
    q4t = _stack_pairs_t(qa_ref)
    s0 = _dot(kbd_ref[K_A, i], q4t)
    s1 = _dot(kbd_ref[K_A, prev], q4t)
    st = _flash_update(s0, near_bias(head_a, 0), dk <= 0, vbd_ref[K_A, i], _flash_init())
    st = _flash_update(s1, near_bias(head_a, 1), dk > jnp.where(i >= 1, 0, TQ), vbd_ref[K_A, prev], st)
    sink_of = lambda h: sink_ref[h] - tab_ref[N_BUCKETS - 1, h]
    write_out(oa_ref, _flash_finish(st, lambda hf, m: jnp.exp(head_row(sink_of, head_a, hf) - m)))

    head_b = A_HEADS + 8 * g
    q4t = _stack_pairs_t(qb_ref)

    s = _dot(ckbd_ref[...], q4t)
    t_pos = i * TQ + qi
    cvis = (t_pos - CMP_STRIDE * ki - (CMP_BLOCK - 1)) >= 0
    t_row = i * TQ + (lax.broadcasted_iota(jnp.int32, (1, W4), 1) & (TQ - 1))
    anyvis = jnp.where(t_row >= CMP_BLOCK - 1, 1.0, 0.0)
    psum = jnp.zeros((TQ, TQ), F32)
    ps = []
    for hf in range(2):
        cb = jnp.concatenate([cb_ref[2 * p + hf] for p in range(PAIRS)], axis=1)
        sh = jnp.where(cvis, s[hf * TQ:(hf + 1) * TQ] + cb, NEG)
        pe = jnp.exp(sh - jnp.max(sh, axis=0, keepdims=True))
        pc = pe * (anyvis / jnp.sum(pe, axis=0, keepdims=True))
        for p in range(PAIRS):
            psum = psum + pc[:, p * TQ:(p + 1) * TQ]
        ps.append(pc.astype(BF16))
    o_cmp = _dot(cvbd_ref[...], jnp.concatenate(ps, axis=0))

    ph = psum.astype(BF16)
    r1 = psum - ph.astype(F32)
    pm = r1.astype(BF16)
    pl_ = (r1 - pm.astype(F32)).astype(BF16)
    ovt = ovt_ref[...]
    pslc = _dot(ovt, ph) + _dot(ovt, pm) + _dot(ovt, pl_)
    ns = pslc.shape[0]
    nb = lax.broadcasted_iota(jnp.int32, (ns, TQ), 0)
    tq = i * TQ + lax.broadcasted_iota(jnp.int32, (ns, TQ), 1)
    cur = tq // SLC_BLOCK
    forced = jnp.where(nb == 0, 1.0, 0.0) + jnp.where(nb == cur, 1.0, 0.0) + jnp.where(nb == cur - 1, 1.0, 0.0)
    score = jnp.where(nb * SLC_BLOCK > tq, NEG, jnp.where(forced > 0.0, BIG, pslc))
    rank = jnp.zeros((ns, TQ), F32)
    for mth in range(ns):
        sm = score[mth:mth + 1, :]
        tie = jnp.where(nb > mth, 1.0, 0.0)
        rank = rank + jnp.where(sm > score, 1.0, jnp.where(sm == score, tie, 0.0))
    sel = jnp.where(rank < SLC_TOPK, 1.0, 0.0)
    sel = jnp.concatenate([sel, jnp.zeros((LANES - ns, TQ), F32)], axis=0).astype(BF16)
    msk_ref[...] = _dot(esel_ref[...], sel).reshape(n_tiles, TQ, TQ)

    def sel_tile(j):
        return jnp.concatenate([msk_ref[j]] * PAIRS, axis=1)

    s0 = _dot(kbd_ref[K_S, i], q4t)
    s1 = _dot(kbd_ref[K_S, prev], q4t)
    st = _flash_update(s0, near_bias(head_b, 0), jnp.where(dk <= 0, sel_tile(i), 0.0) > 0.5,
                       vbd_ref[K_S, i], _flash_init())
    st = _flash_update(s1, near_bias(head_b, 1), sel_tile(prev) > jnp.where(i >= 1, 0.5, 2.0),
                       vbd_ref[K_S, prev], st)

    def far_logits(d):
        return _dot(kbd_ref[K_S, i - d], q4t)

    def far_body(d, carry):
        s, st = carry
        s_next = far_logits(jnp.minimum(d + 1, i))
        j = i - d
        return s_next, _flash_update(s, None, sel_tile(j) > 0.5, vbd_ref[K_S, j], st)

    _, st = lax.fori_loop(2, i + 1, far_body, (far_logits(jnp.minimum(2, i)), st))
    o_slc = _flash_finish(st)

    n_full = B_WINDOW // TQ
    tiles = [jnp.maximum(i - d, 0) for d in range(n_full + 1)]
    logits = [_dot(kbd_ref[K_W, j], q4t) for j in tiles]
    st = _flash_update(logits[0], near_bias(head_b, 0), dk <= 0, vbd_ref[K_W, tiles[0]], _flash_init())
    for d in range(1, n_full + 1):
        lowest = 0 if d == n_full else -TQ
        keep = dk > jnp.where(i >= d, lowest, TQ)
        st = _flash_update(logits[d], near_bias(head_b, 1) if d == 1 else None, keep, vbd_ref[K_W, tiles[d]], st)
    o_win = _flash_finish(st)

    sg = jax.nn.sigmoid(gate_ref[0].astype(F32))
    sg_hi = sg.astype(BF16)
    sg_lo = (sg - sg_hi.astype(F32)).astype(BF16)
    for p in range(PAIRS):
        cols = slice(p * LANES, (p + 1) * LANES)
        qs = slice(p * TQ, (p + 1) * TQ)
        gts = [_dot(sg_hi, e3_ref[0, br, :, cols]) + _dot(sg_lo, e3_ref[0, br, :, cols]) for br in range(3)]
        o = gts[0] * o_cmp[:, qs].T + gts[1] * o_slc[:, qs].T + gts[2] * o_win[:, qs].T
        ob_ref[0, :, cols] = o.astype(ob_ref.dtype)


def _even_attention(pe, kvc, rel_bias, sinks, t_tiles, cbias, e3, esel, ovt):
    b, s, _ = pe.shape
    n_tiles = s // TQ
    half = PAIRS * LANES

    def seg(k):
        return pl.BlockSpec((1, s, LANES), lambda bi, g, i, k=k: (bi, 0, k))

    in_specs = [
        pl.BlockSpec(memory_space=pltpu.SMEM),
        pl.BlockSpec(memory_space=pltpu.SMEM),
        pl.BlockSpec((1, TQ, half), lambda bi, g, i: (bi, i, COL_QA // half + g)),
        pl.BlockSpec((1, TQ, half), lambda bi, g, i: (bi, i, COL_QB // half + g)),
        pl.BlockSpec((1, TQ, LANES), lambda bi, g, i: (bi, i, SEG_GATE)),
        seg(SEG_KA), seg(SEG_VA), seg(SEG_KS), seg(SEG_VS), seg(SEG_KW), seg(SEG_VW),
        pl.BlockSpec((1, 1, TQ, LANES), lambda bi, g, i: (0, bi, 0, 0)),
        pl.BlockSpec((1, 1, TQ, LANES), lambda bi, g, i: (1, bi, 0, 0)),
        pl.BlockSpec(t_tiles.shape, lambda bi, g, i: (0, 0, 0, 0)),
        pl.BlockSpec((2 * PAIRS, TQ, TQ), lambda bi, g, i: (g, 0, i)),
        pl.BlockSpec((1, 3, LANES, half), lambda bi, g, i: (g, 0, 0, 0)),
        pl.BlockSpec(esel.shape, lambda bi, g, i: (0, 0)),
        pl.BlockSpec(ovt.shape, lambda bi, g, i: (0, 0)),
    ]
    out_spec = pl.BlockSpec((1, TQ, half), lambda bi, g, i: (bi, i, g))
    return pl.pallas_call(
        functools.partial(_even_attn_kernel, n_tiles=n_tiles),
        grid=(b, GROUPS, n_tiles),
        in_specs=in_specs,
        out_specs=[out_spec, out_spec],
        out_shape=[jax.ShapeDtypeStruct((b, s, GROUPS * half), BF16)] * 2,
        scratch_shapes=[pltpu.VMEM((3, n_tiles, 2 * TQ, LANES), BF16),
                        pltpu.VMEM((3, n_tiles, LANES, 2 * TQ), BF16),
                        pltpu.VMEM((2 * TQ, LANES), BF16),
                        pltpu.VMEM((LANES, 2 * TQ), BF16),
                        pltpu.VMEM((n_tiles, TQ, TQ), F32)],
        compiler_params=_params("arbitrary", "arbitrary", "arbitrary"),
    )(rel_bias, sinks, pe, pe, pe, pe, pe, pe, pe, pe, pe, kvc, kvc, t_tiles, cbias, e3, esel, ovt)


def _mla_prep_kernel(x_ref, g_ref, win_ref, qn_ref, kvn_ref, wqt_ref, wk_ref, wvt_ref,
                     cos_ref, sin_ref, cost_ref, sint_ref, qt_ref, k_ref, vt_ref):
    xn = _rms(x_ref[...], g_ref[...]).astype(BF16)
    proj = _dot(xn, win_ref[...])
    cq = _rms(proj[:, :Q_LORA], qn_ref[...])
    ckv = _rms(proj[:, Q_LORA:Q_LORA + KV_LORA], kvn_ref[...])
    cq_t = cq.T.astype(BF16)
    ckv_t = ckv.T.astype(BF16)
    ckv = ckv.astype(BF16)
    kr = proj[:, Q_LORA + KV_LORA:]
    kr = (kr * cos_ref[...] + pltpu.roll(kr, HEAD_DIM, 1) * sin_ref[...]).astype(BF16)
    cos_t = cost_ref[...]
    sin_t = sint_ref[...]
    scale = (NOPE_DIM + ROPE_DIM) ** -0.5 * LOG2E
    tm = ckv_t.shape[1]
    ones_rows = jnp.where(lax.broadcasted_iota(jnp.int32, (VT_ROWS - V_DIM, tm), 0) == 0, 1.0, 0.0).astype(BF16)
    for h in range(C_HEADS):
        q_t = _dot(wqt_ref[h], cq_t)
        rp = q_t[NOPE_DIM:]
        rp = rp * cos_t + pltpu.roll(rp, HEAD_DIM, 0) * sin_t
        qt_ref[0, h, 0, 0:NOPE_DIM, :] = (q_t[:NOPE_DIM] * scale).astype(BF16)
        qt_ref[0, h, 0, NOPE_DIM:, :] = (rp * scale).astype(BF16)
        k_ref[:, h * 256:h * 256 + LANES] = _dot(ckv, wk_ref[:, h * LANES:(h + 1) * LANES]).astype(BF16)
        k_ref[:, h * 256 + LANES:(h + 1) * 256] = kr
        vt_ref[0, h, 0, 0:V_DIM, :] = _dot(wvt_ref[h], ckv_t).astype(BF16)
        vt_ref[0, h, 0, V_DIM:, :] = ones_rows


def _mla_prep(x, g, w_in, qn, kvn, wqt, wk, wvt, cos, sin, b, s):
    m, d = x.shape
    tm = MLA_TK
    pos_tiles = s // tm
    full = lambda a: pl.BlockSpec(a.shape, lambda i: (0,) * a.ndim)
    tile_t = lambda rows: pl.BlockSpec((1, C_HEADS, 1, rows, tm), lambda i: (i // pos_tiles, 0, i % pos_tiles, 0, 0))
    return pl.pallas_call(
        _mla_prep_kernel,
        grid=(m // tm,),
        in_specs=[pl.BlockSpec((tm, d), lambda i: (i, 0)),
                  pl.BlockSpec((1, d), lambda i: (0, 0)),
                  full(w_in), pl.BlockSpec((1, Q_LORA), lambda i: (0, 0)),
                  pl.BlockSpec((1, KV_LORA), lambda i: (0, 0)), full(wqt), full(wk), full(wvt),
                  pl.BlockSpec((tm, LANES), lambda i: (i % pos_tiles, 0)),
                  pl.BlockSpec((tm, LANES), lambda i: (i % pos_tiles, 0)),
                  pl.BlockSpec((LANES, tm), lambda i: (0, i % pos_tiles)),
                  pl.BlockSpec((LANES, tm), lambda i: (0, i % pos_tiles))],
        out_specs=[tile_t(256),
                   pl.BlockSpec((tm, C_HEADS * 256), lambda i: (i, 0)),
                   tile_t(VT_ROWS)],
        out_shape=[jax.ShapeDtypeStruct((b, C_HEADS, pos_tiles, 256, tm), BF16),
                   jax.ShapeDtypeStruct((m, C_HEADS * 256), BF16),
                   jax.ShapeDtypeStruct((b, C_HEADS, pos_tiles, VT_ROWS, tm), BF16)],
        compiler_params=_params("parallel"),
    )(x, g.reshape(1, d), w_in, qn.reshape(1, -1), kvn.reshape(1, -1), wqt, wk, wvt, cos, sin, cos.T, sin.T)


def _mla_attn_kernel(qt_ref, k_ref, vt_ref, o_ref, s_scr, acc_scr, m_scr, *, n_q):
    tq = tk = MLA_TQ
    parts = tq // MLA_TK
    ki = lax.broadcasted_iota(jnp.int32, (tk, tq), 0)
    qi = lax.broadcasted_iota(jnp.int32, (tk, tq), 1)
    causal = ki <= qi
    heads = range(MLA_HEADS)

    def q_tile(t, c):
        q_ts = [jnp.concatenate([qt_ref[0, h, parts * t + d] for d in range(parts)], axis=1) for h in heads]

        def logits_to(slot, j):
            rows = pl.ds(pl.multiple_of(j * tk, tk), tk)
            for h in heads:
                s_scr[h, slot] = _dot(k_ref[0, rows, h * 256:(h + 1) * 256], q_ts[h])

        def update_from(slot, j, diagonal):
            for h in heads:
                s = s_scr[h, slot]
                if diagonal:
                    s = jnp.where(causal, s, NEG)
                m = m_scr[h]
                mn = jnp.maximum(m, jnp.max(s, axis=0, keepdims=True))
                p = jnp.exp2(s - mn).astype(BF16)
                m_scr[h] = mn
                v_t = jnp.concatenate([vt_ref[0, h, parts * j + d] for d in range(parts)], axis=1)
                acc_scr[h] = jnp.exp2(m - mn) * acc_scr[h] + _dot(v_t, p)

        m_scr[...] = jnp.full(m_scr.shape, NEG, F32)
        acc_scr[...] = jnp.zeros(acc_scr.shape, F32)
        logits_to(0, t)
        logits_to(1, 0)
        update_from(0, t, True)

        def body(jj, c2):
            logits_to(0, jnp.minimum(2 * jj + 1, t - 1))
            update_from(1, 2 * jj, False)
            logits_to(1, jnp.minimum(2 * jj + 2, t - 1))

            @pl.when(2 * jj + 1 < t)
            def _():
                update_from(0, 2 * jj + 1, False)
            return c2

        lax.fori_loop(0, (t + 1) // 2, body, 0)
        rows = pl.ds(pl.multiple_of(t * tq, tq), tq)
        for h in heads:
            acc = acc_scr[h]
            o = acc[:V_DIM] * (1.0 / acc[V_DIM:V_DIM + 1])
            o_ref[0, rows, h * LANES:(h + 1) * LANES] = o.T.astype(o_ref.dtype)
        return c

    lax.fori_loop(0, n_q, q_tile, 0)


def _mla_attention(qt, k, vt):
    b, s, _ = k.shape
    n_k = s // MLA_TK
    nh = MLA_HEADS
    return pl.pallas_call(
        functools.partial(_mla_attn_kernel, n_q=s // MLA_TQ),
        grid=(b, C_HEADS // nh),
        in_specs=[pl.BlockSpec((1, nh, n_k, 256, MLA_TK), lambda bi, h: (bi, h, 0, 0, 0)),
                  pl.BlockSpec((1, s, 256 * nh), lambda bi, h: (bi, 0, h)),
                  pl.BlockSpec((1, nh, n_k, VT_ROWS, MLA_TK), lambda bi, h: (bi, h, 0, 0, 0))],
        out_specs=pl.BlockSpec((1, s, LANES * nh), lambda bi, h: (bi, 0, h)),
        out_shape=jax.ShapeDtypeStruct((b, s, C_HEADS * V_DIM), BF16),
        scratch_shapes=[pltpu.VMEM((nh, 2, MLA_TQ, MLA_TQ), F32),
                        pltpu.VMEM((nh, VT_ROWS, MLA_TQ), F32),
                        pltpu.VMEM((nh, 1, MLA_TQ), F32)],
        compiler_params=_params("parallel", "arbitrary"),
    )(qt, k, vt)


def _even_in_weight(w):
    cuts = np.cumsum([1024, 128, 128, 1024, 128, 128, 128, 128, 128, 128, 48])[:-1]
    qa, ka, va, qb, kc, vc, ks, vs, kw, vw, gate = jnp.split(w, [int(c) for c in cuts], axis=1)
    gate = jnp.pad(gate, ((0, 0), (0, LANES - gate.shape[1])))
    return jnp.concatenate([qa, qb, ka, va, kc, vc, ks, vs, kw, vw, gate], axis=1).astype(BF16)


def _compress_weights(pos, w1, w2):
    half = CMP_BLOCK // 2
    eye = jnp.eye(GROUPS, dtype=w1.dtype)
    w1r = w1.reshape(2, half, HEAD_DIM, CMP_HIDDEN)
    wab = jnp.einsum('rldh,gk->rlgdkh', w1r, eye).reshape(2, half * GROUPS * HEAD_DIM, GROUPS * CMP_HIDDEN)
    posr = jnp.broadcast_to(pos.reshape(2, half, 1, HEAD_DIM), (2, half, GROUPS, HEAD_DIM)).reshape(2, 1, -1)
    w2bd = jnp.einsum('hd,gk->ghkd', w2, eye).reshape(GROUPS * CMP_HIDDEN, GROUPS * HEAD_DIM)
    return posr[0], posr[1], wab[0].astype(BF16), wab[1].astype(BF16), w2bd.astype(BF16)


def _rope_chunk_cols(w):
    z = jnp.zeros(w.shape[:-1] + (ROPE_DIM // 2,), w.dtype)
    return jnp.concatenate([w[..., :ROPE_DIM // 2], z, w[..., ROPE_DIM // 2:], z], axis=-1)


def _mla_weights(w_in, w_q_up, w_kv_up):
    w_in2 = jnp.concatenate([w_in[:, :Q_LORA + KV_LORA], _rope_chunk_cols(w_in[:, Q_LORA + KV_LORA:])], axis=1)
    wq = w_q_up.reshape(Q_LORA, C_HEADS, NOPE_DIM + ROPE_DIM)
    wq2 = jnp.concatenate([wq[..., :NOPE_DIM], _rope_chunk_cols(wq[..., NOPE_DIM:])], axis=-1)
    wqt = wq2.transpose(1, 2, 0)
    wkv = w_kv_up.reshape(KV_LORA, C_HEADS, NOPE_DIM + V_DIM)
    wk = wkv[..., :NOPE_DIM].reshape(KV_LORA, C_HEADS * NOPE_DIM)
    wvt = wkv[..., NOPE_DIM:].transpose(1, 2, 0)
    return w_in2.astype(BF16), wqt.astype(BF16), wk.astype(BF16), wvt.astype(BF16)


def _rope_tables(s):
    inv = 1.0 / (ROPE_THETA ** (jnp.arange(0, ROPE_DIM, 2, dtype=F32) / ROPE_DIM))
    ang = jnp.arange(s, dtype=F32)[:, None] * inv[None]
    cos, sin = jnp.cos(ang), jnp.sin(ang)
    z = jnp.zeros_like(cos)
    return jnp.concatenate([cos, z, cos, z], axis=1), jnp.concatenate([-sin, z, sin, z], axis=1)


def _static_tables(s):
    k = np.arange(TQ)[:, None]
    q = np.arange(TQ)[None, :]
    near = np.stack([_bucket_np(q - k), _bucket_np(q - k + TQ)])
    c = np.arange(LANES)[:, None]
    t = np.arange(s)[None, :]
    cmp_map = _bucket_np(t - (c * CMP_STRIDE + CMP_BLOCK - 1))
    ns = s // SLC_BLOCK
    nc = (s - CMP_BLOCK) // CMP_STRIDE + 1
    c_start = np.arange(LANES) * CMP_STRIDE
    s_start = np.arange(ns) * SLC_BLOCK
    ovt = ((c_start[None, :] <= s_start[:, None] + SLC_BLOCK - 1) &
           (c_start[None, :] + CMP_BLOCK - 1 >= s_start[:, None]) &
           (np.arange(LANES)[None, :] < nc)).astype(np.float32)
    esel = (np.arange(s)[:, None] // SLC_BLOCK == np.arange(LANES)[None, :]).astype(np.float32)
    e3 = np.zeros((GROUPS, 3, LANES, PAIRS * LANES), np.float32)
    for g in range(GROUPS):
        for u in range(2 * PAIRS):
            for br in range(3):
                e3[g, br, (8 * g + u) * 3 + br, u * HEAD_DIM:(u + 1) * HEAD_DIM] = 1.0
    return near, cmp_map, ovt, esel, e3


def _even_layer(x, b, s, rel_bias, norm, w_in, sinks, pos_k, pos_v, k_w1, k_w2, v_w1, v_w2, w_out):
    near, cmp_map, ovt, esel, e3 = _static_tables(s)
    t_tiles = _bias_tiles(rel_bias, near, 0, A_HEADS + B_HEADS)
    cbias = _bias_tiles(rel_bias, cmp_map, A_HEADS, B_HEADS)

    pe = _norm_matmul(x, norm, _even_in_weight(w_in), 1024, 640, BF16).reshape(b, s, EVEN_COLS)

    rows = s // (CMP_BLOCK // 2)
    hkv = jnp.stack([pe[:, :, SEG_KC * LANES:(SEG_KC + 1) * LANES],
                     pe[:, :, SEG_VC * LANES:(SEG_VC + 1) * LANES]]).reshape(2, b, rows, -1)
    ck = _compress_weights(pos_k, k_w1, k_w2)
    cv = _compress_weights(pos_v, v_w1, v_w2)
    kvc = _compress(hkv, *[jnp.stack([a, c]) for a, c in zip(ck, cv)])
    assert rows <= TQ
    kvc = jnp.pad(kvc, ((0, 0), (0, 0), (0, TQ - rows), (0, 0)))

    oa, ob = _even_attention(pe, kvc, rel_bias, sinks, t_tiles, cbias,
                             jnp.asarray(e3, BF16), jnp.asarray(esel, BF16), jnp.asarray(ovt, BF16))
    w_out = w_out.astype(BF16)
    n_a = A_HEADS * HEAD_DIM
    return _out_proj(x, [oa.reshape(b * s, -1), ob.reshape(b * s, -1)], [w_out[:n_a], w_out[n_a:]], 512)


def _odd_layer(x, b, s, norm, w_in, q_norm, w_q_up, kv_norm, w_kv_up, w_out):
    w_in2, wqt, wk, wvt = _mla_weights(w_in, w_q_up, w_kv_up)
    cos, sin = _rope_tables(s)
    qt, k, vt = _mla_prep(x, norm, w_in2, q_norm, kv_norm, wqt, wk, wvt, cos, sin, b, s)
    o = _mla_attention(qt, k.reshape(b, s, -1), vt)
    return _out_proj(x, [o.reshape(b * s, -1)], [w_out.astype(BF16)], 512)


def kernel(x, rel_bias, norm_mix_e, w_in_e, sinks, cmp_pos_k, cmp_pos_v, cmp_k_w1, cmp_k_w2, cmp_v_w1, cmp_v_w2, w_out_e, norm_mix_o, w_in_o, q_norm, w_q_up, kv_norm, w_kv_up, w_out_o, norm_mlp, w_up, w_down, norm_final):
    b, s, d = x.shape
    depth = norm_mlp.shape[0]
    h = x.reshape(b * s, d)
    for layer in range(depth):
        i = layer // 2
        if layer % 2 == 0:
            h = _even_layer(h, b, s, rel_bias, norm_mix_e[i], w_in_e[i], sinks[i], cmp_pos_k[i], cmp_pos_v[i],
                            cmp_k_w1[i], cmp_k_w2[i], cmp_v_w1[i], cmp_v_w2[i], w_out_e[i])
        else:
            h = _odd_layer(h, b, s, norm_mix_o[i], w_in_o[i], q_norm[i], w_q_up[i], kv_norm[i], w_kv_up[i],
                           w_out_o[i])
        h = _mlp(h, norm_mlp[layer], w_up[layer].astype(BF16), w_down[layer].astype(BF16), norm_final,
                 512, 512, layer == depth - 1)
    return h.reshape(b, s, d)
```

```python
import functools
import math

import numpy as np
import jax
import jax.numpy as jnp
from jax import lax
from jax.experimental import pallas as pl
from jax.experimental.pallas import tpu as pltpu

F32 = jnp.float32
BF16 = jnp.bfloat16

LANES = 128
VMEM_LIMIT_BYTES = 56 * 1024 * 1024

EPS = 1e-6
NEG = -1e30
BIG = 1e4
HEAD_DIM = 64
TQ = 128
MLA_TQ = 512
MLA_TK = 256
MLA_HEADS = 2
N_BUCKETS = 32
MAX_DISTANCE = 128
A_HEADS = 16
B_HEADS = 16
GROUPS = 2
PAIRS = 4
W4 = PAIRS * TQ
CMP_BLOCK = 32
CMP_STRIDE = 16
CMP_HIDDEN = 256
SLC_BLOCK = 64
SLC_TOPK = 8
A_WINDOW = 128
B_WINDOW = 512
C_HEADS = 16
Q_LORA = 768
KV_LORA = 512
NOPE_DIM = 128
ROPE_DIM = 64
V_DIM = 128
VT_ROWS = V_DIM + 16
LOG2E = math.log2(math.e)
ROPE_THETA = 10000.0

COL_QA, COL_QB = 0, 1024
SEG_KA, SEG_VA, SEG_KC, SEG_VC, SEG_KS, SEG_VS, SEG_KW, SEG_VW, SEG_GATE = range(16, 25)
EVEN_COLS = 25 * LANES
K_A, K_S, K_W = 0, 1, 2


def _params(*sem):
    return pltpu.CompilerParams(dimension_semantics=sem, vmem_limit_bytes=VMEM_LIMIT_BYTES)


def _rms(x, g):
    return x * lax.rsqrt(jnp.mean(x * x, axis=-1, keepdims=True) + EPS) * g


def _dot(a, b):
    return jnp.dot(a, b, preferred_element_type=F32)


def _bucket_np(dist):
    dist = np.maximum(dist, 0)
    max_exact = N_BUCKETS // 2
    d = np.maximum(dist, 1).astype(np.float32)
    large = max_exact + (np.log(d / np.float32(max_exact)) / np.float32(math.log(MAX_DISTANCE / max_exact))
                         * np.float32(N_BUCKETS - max_exact)).astype(np.int32)
    large = np.minimum(large, N_BUCKETS - 1)
    return np.where(dist < max_exact, dist, large).astype(np.int32)


def _bias_kernel(tab_ref, bm_ref, o_ref, *, head0):
    h = pl.program_id(0) + head0
    bm = bm_ref[...]
    acc = jnp.zeros(bm.shape, F32)
    for b in range(N_BUCKETS):
        acc = jnp.where(bm == b, tab_ref[b, h], acc)
    o_ref[0] = jnp.where(bm < 0, NEG, (acc - tab_ref[N_BUCKETS - 1, h]) * LOG2E)


def _bias_tiles(rel_bias, bucket_map, head0, n_heads):
    shp = bucket_map.shape
    nd = len(shp)
    return pl.pallas_call(
        functools.partial(_bias_kernel, head0=head0),
        grid=(n_heads,),
        in_specs=[pl.BlockSpec(memory_space=pltpu.SMEM),
                  pl.BlockSpec(shp, lambda h: (0,) * nd)],
        out_specs=pl.BlockSpec((1,) + shp, lambda h: (h,) + (0,) * nd),
        out_shape=jax.ShapeDtypeStruct((n_heads,) + shp, F32),
        compiler_params=_params("arbitrary"),
    )(rel_bias, jnp.asarray(bucket_map))


def _norm_matmul_kernel(x_ref, g_ref, w_ref, o_ref, xn_ref):
    @pl.when(pl.program_id(1) == 0)
    def _():
        xn_ref[...] = _rms(x_ref[...], g_ref[...]).astype(BF16)

    o_ref[...] = _dot(xn_ref[...], w_ref[...]).astype(o_ref.dtype)


def _norm_matmul(x, g, w, tm, tn, out_dtype):
    m, d = x.shape
    n = w.shape[1]
    return pl.pallas_call(
        _norm_matmul_kernel,
        grid=(m // tm, n // tn),
        in_specs=[pl.BlockSpec((tm, d), lambda i, j: (i, 0)),
                  pl.BlockSpec((1, d), lambda i, j: (0, 0)),
                  pl.BlockSpec((d, tn), lambda i, j: (0, j))],
        out_specs=pl.BlockSpec((tm, tn), lambda i, j: (i, j)),
        out_shape=jax.ShapeDtypeStruct((m, n), out_dtype),
        scratch_shapes=[pltpu.VMEM((tm, d), BF16)],
        compiler_params=_params("parallel", "arbitrary"),
    )(x, g.reshape(1, d), w)


def _out_proj_kernel(*refs, n_in):
    x_ref = refs[0]
    o_ref = refs[2 * n_in + 1]
    acc = x_ref[...]
    for t in range(n_in):
        acc = acc + _dot(refs[1 + t][...], refs[1 + n_in + t][...])
    o_ref[...] = acc


def _out_proj(x, acts, ws, tm):
    m, d = x.shape
    n_in = len(acts)
    in_specs = [pl.BlockSpec((tm, d), lambda i: (i, 0))]
    in_specs += [pl.BlockSpec((tm, a.shape[1]), lambda i: (i, 0)) for a in acts]
    in_specs += [pl.BlockSpec(w.shape, lambda i: (0, 0)) for w in ws]
    return pl.pallas_call(
        functools.partial(_out_proj_kernel, n_in=n_in),
        grid=(m // tm,),
        in_specs=in_specs,
        out_specs=pl.BlockSpec((tm, d), lambda i: (i, 0)),
        out_shape=jax.ShapeDtypeStruct((m, d), F32),
        compiler_params=_params("parallel"),
    )(x, *acts, *ws)


def _mlp_kernel(x_ref, g_ref, wu_ref, wd_ref, gf_ref, o_ref, xn_ref, *, final_norm):
    j = pl.program_id(1)

    @pl.when(j == 0)
    def _():
        x = x_ref[...]
        xn_ref[...] = _rms(x, g_ref[...]).astype(BF16)
        o_ref[...] = x

    h = _dot(xn_ref[...], wu_ref[...])
    a = jnp.square(jnp.maximum(h, 0.0)).astype(BF16)
    o_ref[...] += _dot(a, wd_ref[...])

    if final_norm:
        @pl.when(j == pl.num_programs(1) - 1)
        def _():
            o_ref[...] = _rms(o_ref[...], gf_ref[...])


def _mlp(x, g, w_up, w_down, g_final, tm, tf, final_norm):
    m, d = x.shape
    ff = w_up.shape[1]
    return pl.pallas_call(
        functools.partial(_mlp_kernel, final_norm=final_norm),
        grid=(m // tm, ff // tf),
        in_specs=[pl.BlockSpec((tm, d), lambda i, j: (i, 0)),
                  pl.BlockSpec((1, d), lambda i, j: (0, 0)),
                  pl.BlockSpec((d, tf), lambda i, j: (0, j)),
                  pl.BlockSpec((tf, d), lambda i, j: (j, 0)),
                  pl.BlockSpec((1, d), lambda i, j: (0, 0))],
        out_specs=pl.BlockSpec((tm, d), lambda i, j: (i, 0)),
        out_shape=jax.ShapeDtypeStruct((m, d), F32),
        scratch_shapes=[pltpu.VMEM((tm, d), BF16)],
        compiler_params=_params("parallel", "arbitrary"),
    )(x, g.reshape(1, d), w_up, w_down, g_final.reshape(1, d))


def _gelu_tanh(x):
    return 0.5 * x * (1.0 + jnp.tanh(math.sqrt(2.0 / math.pi) * (x + 0.044715 * (x * x * x))))


def _compress_kernel(h_ref, pa_ref, pb_ref, wa_ref, wb_ref, w2_ref, o_ref):
    h = h_ref[0, 0].astype(F32)
    ha = _dot((h + pa_ref[0]).astype(BF16), wa_ref[0])
    hb = _dot((h + pb_ref[0]).astype(BF16), wb_ref[0])
    n = hb.shape[0]
    pre = ha + pltpu.roll(hb, n - 1, 0)
    o_ref[0, 0] = _dot(_gelu_tanh(pre).astype(BF16), w2_ref[0]).astype(o_ref.dtype)


def _compress(hkv, pa, pb, wa, wb, w2):
    _, b, r, w = hkv.shape
    hid2 = wa.shape[2]
    return pl.pallas_call(
        _compress_kernel,
        grid=(2, b),
        in_specs=[pl.BlockSpec((1, 1, r, w), lambda t, i: (t, i, 0, 0)),
                  pl.BlockSpec((1, 1, w), lambda t, i: (t, 0, 0)),
                  pl.BlockSpec((1, 1, w), lambda t, i: (t, 0, 0)),
                  pl.BlockSpec((1, w, hid2), lambda t, i: (t, 0, 0)),
                  pl.BlockSpec((1, w, hid2), lambda t, i: (t, 0, 0)),
                  pl.BlockSpec((1, hid2, LANES), lambda t, i: (t, 0, 0))],
        out_specs=pl.BlockSpec((1, 1, r, LANES), lambda t, i: (t, i, 0, 0)),
        out_shape=jax.ShapeDtypeStruct((2, b, r, LANES), BF16),
        compiler_params=_params("arbitrary", "arbitrary"),
    )(hkv, pa, pb, wa, wb, w2)


SWA, SLC, WIN = 0, 1, 2


def _stack_pairs_t(q_ref, g):
    cols = [q_ref[0, :, (PAIRS * g + p) * LANES:(PAIRS * g + p + 1) * LANES].astype(F32).T for p in range(PAIRS)]
    return (jnp.concatenate(cols, axis=1) * (HEAD_DIM ** -0.5 * LOG2E)).astype(BF16)


def _rows_by_half(a, b, rows):
    parts = [jnp.broadcast_to(a, (HEAD_DIM, W4)), jnp.broadcast_to(b, (HEAD_DIM, W4))]
    if rows > 2 * HEAD_DIM:
        first = lax.broadcasted_iota(jnp.int32, (rows - 2 * HEAD_DIM, W4), 0) == 0
        parts.append(jnp.where(first, a, b))
    return jnp.concatenate(parts, axis=0)


def _lanes4(tile):
    return jnp.concatenate([tile] * PAIRS, axis=1)


def _even_attn_kernel(tab_ref, sink_ref, qa_ref, qb_ref, gate_ref,
                      ka_ref, va_ref, ks_ref, vs_ref, kw_ref, vw_ref, kc_ref, vc_ref,
                      ta_ref, tb_ref, up_ref, cb_ref, e3_ref, esel_ref, ovt_ref,
                      oa_ref, ob_ref,
                      kbd_ref, vbd_ref, ckbd_ref, cvbd_ref, msk_ref, s_scr, acc_scr, m_scr, *, n_tiles):
    i = pl.program_id(1)
    groups = range(GROUPS)

    def block_diag(x, g):
        r = pltpu.roll(x, HEAD_DIM, 1)
        lo = lax.broadcasted_iota(jnp.int32, x.shape, 1) < HEAD_DIM
        own, other = (x, r) if g == 0 else (r, x)
        return jnp.where(lo, own, 0.0), jnp.where(lo, 0.0, other)

    @pl.when(i == 0)
    def _build():
        r16 = lax.broadcasted_iota(jnp.int32, (VT_ROWS - LANES, 2 * TQ), 0)
        c16 = lax.broadcasted_iota(jnp.int32, (VT_ROWS - LANES, 2 * TQ), 1)
        ones_rows = jnp.where((r16 == 0) & (c16 < TQ) | (r16 == 1) & (c16 >= TQ), 1.0, 0.0).astype(BF16)
        for t, ref in enumerate((ka_ref, ks_ref, kw_ref)):
            def kbody(j, c, t=t, ref=ref):
                x = ref[0, pl.ds(pl.multiple_of(j * TQ, TQ), TQ), :].astype(F32)
                for g in groups:
                    top, bot = block_diag(x, g)
                    kbd_ref[g, t, j, 0:TQ, :] = top.astype(BF16)
                    kbd_ref[g, t, j, TQ:2 * TQ, :] = bot.astype(BF16)
                return c
            lax.fori_loop(0, n_tiles, kbody, 0)
        for t, ref in enumerate((va_ref, vs_ref, vw_ref)):
            def vbody(j, c, t=t, ref=ref):
                x = ref[0, pl.ds(pl.multiple_of(j * TQ, TQ), TQ), :].astype(F32)
                for g in groups:
                    top, bot = block_diag(x, g)
                    vbd_ref[g, t, j, 0:LANES, 0:TQ] = top.T.astype(BF16)
                    vbd_ref[g, t, j, 0:LANES, TQ:2 * TQ] = bot.T.astype(BF16)
                    vbd_ref[g, t, j, LANES:, :] = ones_rows
                return c
            lax.fori_loop(0, n_tiles, vbody, 0)
        xk = kc_ref[0, 0].astype(F32)
        xv = vc_ref[0, 0].astype(F32)
        for g in groups:
            top, bot = block_diag(xk, g)
            ckbd_ref[g, 0:TQ, :] = top.astype(BF16)
            ckbd_ref[g, TQ:2 * TQ, :] = bot.astype(BF16)
            top, bot = block_diag(xv, g)
            cvbd_ref[g, :, 0:TQ] = top.T.astype(BF16)
            cvbd_ref[g, :, TQ:2 * TQ] = bot.T.astype(BF16)

    def near_bias(t_ref, g, delta):
        return [jnp.concatenate([t_ref[8 * g + 2 * p + hf, delta] for p in range(PAIRS)], axis=1)
                for hf in range(2)]

    def update(g, st, s, adds, vbd_t):
        ps, al = [], []
        for hf in range(2):
            sh = s[hf * TQ:(hf + 1) * TQ]
            for a in adds:
                sh = sh + a[hf]
            m = m_scr[g, st, hf]
            mn = jnp.maximum(m, jnp.max(sh, axis=0, keepdims=True))
            ps.append(jnp.exp2(sh - mn).astype(BF16))
            al.append(jnp.exp2(m - mn))
            m_scr[g, st, hf] = mn
        acc_scr[g, st] = _rows_by_half(al[0], al[1], VT_ROWS) * acc_scr[g, st] + _dot(vbd_t, jnp.concatenate(ps, axis=0))

    def finish(g, st, extra=None):
        acc = acc_scr[g, st]
        den = [acc[LANES + hf:LANES + hf + 1] for hf in range(2)]
        if extra is not None:
            den = [den[hf] + extra(hf, m_scr[g, st, hf]) for hf in range(2)]
        return acc[:LANES] * _rows_by_half(1.0 / den[0], 1.0 / den[1], LANES)

    def tile_logits(g, slot, j, q4t):
        return _dot(kbd_ref[g, slot, j], q4t)

    m_scr[...] = jnp.full(m_scr.shape, NEG, F32)
    acc_scr[...] = jnp.zeros(acc_scr.shape, F32)
    q4a = [_stack_pairs_t(qa_ref, g) for g in groups]
    q4b = [_stack_pairs_t(qb_ref, g) for g in groups]

    for g in groups:
        update(g, SWA, tile_logits(g, K_A, i, q4a[g]), [near_bias(ta_ref, g, 0)], vbd_ref[g, K_A, i])
        update(g, WIN, tile_logits(g, K_W, i, q4b[g]), [near_bias(tb_ref, g, 0)], vbd_ref[g, K_W, i])

    o_cmp = []
    t_row = i * TQ + (lax.broadcasted_iota(jnp.int32, (1, W4), 1) & (TQ - 1))
    anyvis = jnp.where(t_row >= CMP_BLOCK - 1, 1.0, 0.0)
    for g in groups:
        s = _dot(ckbd_ref[g], q4b[g])
        psum = jnp.zeros((TQ, TQ), F32)
        ps = []
        for hf in range(2):
            cb = jnp.concatenate([cb_ref[8 * g + 2 * p + hf] for p in range(PAIRS)], axis=1)
            sh = s[hf * TQ:(hf + 1) * TQ] + cb
            pe = jnp.exp2(sh - jnp.max(sh, axis=0, keepdims=True))
            pc = pe * (anyvis / jnp.sum(pe, axis=0, keepdims=True))
            for p in range(PAIRS):
                psum = psum + pc[:, p * TQ:(p + 1) * TQ]
            ps.append(pc.astype(BF16))
        o_cmp.append(_dot(cvbd_ref[g], jnp.concatenate(ps, axis=0)))

        ph = psum.astype(BF16)
        r1 = psum - ph.astype(F32)
        pm = r1.astype(BF16)
        pl_ = (r1 - pm.astype(F32)).astype(BF16)
        ovt = ovt_ref[...]
        pslc = _dot(ovt, ph) + _dot(ovt, pm) + _dot(ovt, pl_)
        ns = pslc.shape[0]
        nb = lax.broadcasted_iota(jnp.int32, (ns, TQ), 0)
        tq = i * TQ + lax.broadcasted_iota(jnp.int32, (ns, TQ), 1)
        cur = tq // SLC_BLOCK
        forced = (jnp.where(nb == 0, 1.0, 0.0) + jnp.where(nb == cur, 1.0, 0.0) +
                  jnp.where(nb == cur - 1, 1.0, 0.0))
        score = jnp.where(nb * SLC_BLOCK > tq, NEG, jnp.where(forced > 0.0, BIG, pslc))
        rank = jnp.zeros((ns, TQ), F32)
        for mth in range(ns):
            sm = score[mth:mth + 1, :]
            tie = jnp.where(nb > mth, 1.0, 0.0)
            rank = rank + jnp.where(sm > score, 1.0, jnp.where(sm == score, tie, 0.0))
        sel = jnp.where(rank < SLC_TOPK, 1.0, 0.0)
        sel = jnp.concatenate([sel, jnp.zeros((LANES - ns, TQ), F32)], axis=0).astype(BF16)
        msk_ref[g] = ((_dot(esel_ref[...], sel) - 1.0) * -NEG).reshape(n_tiles, TQ, TQ)

    def sel_add(g, j):
        t4 = _lanes4(msk_ref[g, j])
        return [t4, t4]

    for g in groups:
        update(g, SLC, tile_logits(g, K_S, i, q4b[g]), [near_bias(tb_ref, g, 0), sel_add(g, i)], vbd_ref[g, K_S, i])

    @pl.when(i >= 1)
    def _prev():
        j = i - 1
        for g in groups:
            update(g, SWA, tile_logits(g, K_A, j, q4a[g]), [near_bias(ta_ref, g, 1)], vbd_ref[g, K_A, j])
            update(g, SLC, tile_logits(g, K_S, j, q4b[g]), [near_bias(tb_ref, g, 1), sel_add(g, j)],
                   vbd_ref[g, K_S, j])
            update(g, WIN, tile_logits(g, K_W, j, q4b[g]), [near_bias(tb_ref, g, 1)], vbd_ref[g, K_W, j])

    n_full = B_WINDOW // TQ
    for d in range(2, n_full + 1):
        @pl.when(i >= d)
        def _win(d=d):
            j = i - d
            up4 = _lanes4(up_ref[...])
            for g in groups:
                update(g, WIN, tile_logits(g, K_W, j, q4b[g]), [[up4, up4]] if d == n_full else [], vbd_ref[g, K_W, j])

    def far_logits_to(slot, d):
        for g in groups:
            s_scr[g, slot] = tile_logits(g, K_S, i - jnp.minimum(d, i), q4b[g])

    def far_update_from(slot, d):
        j = i - d
        for g in groups:
            update(g, SLC, s_scr[g, slot], [sel_add(g, j)], vbd_ref[g, K_S, j])

    far_logits_to(0, 2)

    def far_body(jj, c):
        d = 2 + 2 * jj
        far_logits_to(1, d + 1)
        far_update_from(0, d)
        far_logits_to(0, d + 2)

        @pl.when(d + 1 <= i)
        def _():
            far_update_from(1, d + 1)
        return c

    lax.fori_loop(0, i // 2, far_body, 0)

    def to_rows(o_t, p):
        return o_t[:, p * TQ:(p + 1) * TQ].T

    far_of = lambda h: tab_ref[N_BUCKETS - 1, h]
    sg = jax.nn.sigmoid(gate_ref[0].astype(F32))
    sg_hi = sg.astype(BF16)
    sg_lo = (sg - sg_hi.astype(F32)).astype(BF16)
    for g in groups:
        def sink_term(hf, m, g=g):
            row = jnp.concatenate([jnp.full((1, TQ), (sink_ref[8 * g + 2 * p + hf] - far_of(8 * g + 2 * p + hf)) * LOG2E,
                                            F32) for p in range(PAIRS)], axis=1)
            return jnp.exp2(row - m)

        o_swa = finish(g, SWA, sink_term)
        o_slc = finish(g, SLC)
        o_win = finish(g, WIN)
        for p in range(PAIRS):
            cols = slice((PAIRS * g + p) * LANES, (PAIRS * g + p + 1) * LANES)
            oa_ref[0, :, cols] = to_rows(o_swa, p).astype(oa_ref.dtype)
            gts = [_dot(sg_hi, e3_ref[br, :, cols]) + _dot(sg_lo, e3_ref[br, :, cols]) for br in range(3)]
            o = gts[0] * to_rows(o_cmp[g], p) + gts[1] * to_rows(o_slc, p) + gts[2] * to_rows(o_win, p)
            ob_ref[0, :, cols] = o.astype(ob_ref.dtype)


def _even_attention(pe, kvc, rel_bias, sinks, ta, tb, up, cbias, e3, esel, ovt):
    b, s, _ = pe.shape
    n_tiles = s // TQ
    width = GROUPS * PAIRS * LANES

    def seg(k):
        return pl.BlockSpec((1, s, LANES), lambda bi, i, k=k: (bi, 0, k))

    full = lambda a: pl.BlockSpec(a.shape, lambda bi, i: (0,) * a.ndim)
    in_specs = [
        pl.BlockSpec(memory_space=pltpu.SMEM),
        pl.BlockSpec(memory_space=pltpu.SMEM),
        pl.BlockSpec((1, TQ, width), lambda bi, i: (bi, i, COL_QA // width)),
        pl.BlockSpec((1, TQ, width), lambda bi, i: (bi, i, COL_QB // width)),
        pl.BlockSpec((1, TQ, LANES), lambda bi, i: (bi, i, SEG_GATE)),
        seg(SEG_KA), seg(SEG_VA), seg(SEG_KS), seg(SEG_VS), seg(SEG_KW), seg(SEG_VW),
        pl.BlockSpec((1, 1, TQ, LANES), lambda bi, i: (0, bi, 0, 0)),
        pl.BlockSpec((1, 1, TQ, LANES), lambda bi, i: (1, bi, 0, 0)),
        full(ta), full(tb), full(up),
        pl.BlockSpec((B_HEADS, TQ, TQ), lambda bi, i: (0, 0, i)),
        full(e3), full(esel), full(ovt),
    ]
    out_spec = pl.BlockSpec((1, TQ, width), lambda bi, i: (bi, i, 0))
    return pl.pallas_call(
        functools.partial(_even_attn_kernel, n_tiles=n_tiles),
        grid=(b, n_tiles),
        in_specs=in_specs,
        out_specs=[out_spec, out_spec],
        out_shape=[jax.ShapeDtypeStruct((b, s, width), BF16)] * 2,
        scratch_shapes=[pltpu.VMEM((GROUPS, 3, n_tiles, 2 * TQ, LANES), BF16),
                        pltpu.VMEM((GROUPS, 3, n_tiles, VT_ROWS, 2 * TQ), BF16),
                        pltpu.VMEM((GROUPS, 2 * TQ, LANES), BF16),
                        pltpu.VMEM((GROUPS, LANES, 2 * TQ), BF16),
                        pltpu.VMEM((GROUPS, n_tiles, TQ, TQ), F32),
                        pltpu.VMEM((GROUPS, 2, 2 * TQ, W4), F32),
                        pltpu.VMEM((GROUPS, 3, VT_ROWS, W4), F32),
                        pltpu.VMEM((GROUPS, 3, 2, 1, W4), F32)],
        compiler_params=_params("arbitrary", "arbitrary"),
    )(rel_bias, sinks, pe, pe, pe, pe, pe, pe, pe, pe, pe, kvc, kvc, ta, tb, up, cbias, e3, esel, ovt)


def _mla_prep_kernel(x_ref, g_ref, win_ref, qn_ref, kvn_ref, wqt_ref, wk_ref, wvt_ref,
                     cos_ref, sin_ref, cost_ref, sint_ref, qt_ref, k_ref, vt_ref):
    xn = _rms(x_ref[...], g_ref[...]).astype(BF16)
    proj = _dot(xn, win_ref[...])
    cq = _rms(proj[:, :Q_LORA], qn_ref[...])
    ckv = _rms(proj[:, Q_LORA:Q_LORA + KV_LORA], kvn_ref[...])
    cq_t = cq.T.astype(BF16)
    ckv_t = ckv.T.astype(BF16)
    ckv = ckv.astype(BF16)
    kr = proj[:, Q_LORA + KV_LORA:]
    kr = (kr * cos_ref[...] + pltpu.roll(kr, HEAD_DIM, 1) * sin_ref[...]).astype(BF16)
    cos_t = cost_ref[...]
    sin_t = sint_ref[...]
    scale = (NOPE_DIM + ROPE_DIM) ** -0.5 * LOG2E
    tm = ckv_t.shape[1]
    ones_rows = jnp.where(lax.broadcasted_iota(jnp.int32, (VT_ROWS - V_DIM, tm), 0) == 0, 1.0, 0.0).astype(BF16)
    for h in range(C_HEADS):
        q_t = _dot(wqt_ref[h], cq_t)
        rp = q_t[NOPE_DIM:]
        rp = rp * cos_t + pltpu.roll(rp, HEAD_DIM, 0) * sin_t
        qt_ref[0, h, 0, 0:NOPE_DIM, :] = (q_t[:NOPE_DIM] * scale).astype(BF16)
        qt_ref[0, h, 0, NOPE_DIM:, :] = (rp * scale).astype(BF16)
        k_ref[:, h * 256:h * 256 + LANES] = _dot(ckv, wk_ref[:, h * LANES:(h + 1) * LANES]).astype(BF16)
        k_ref[:, h * 256 + LANES:(h + 1) * 256] = kr
        vt_ref[0, h, 0, 0:V_DIM, :] = _dot(wvt_ref[h], ckv_t).astype(BF16)
        vt_ref[0, h, 0, V_DIM:, :] = ones_rows


def _mla_prep(x, g, w_in, qn, kvn, wqt, wk, wvt, cos, sin, b, s):
    m, d = x.shape
    tm = MLA_TK
    pos_tiles = s // tm
    full = lambda a: pl.BlockSpec(a.shape, lambda i: (0,) * a.ndim)
    tile_t = lambda rows: pl.BlockSpec((1, C_HEADS, 1, rows, tm), lambda i: (i // pos_tiles, 0, i % pos_tiles, 0, 0))
    return pl.pallas_call(
        _mla_prep_kernel,
        grid=(m // tm,),
        in_specs=[pl.BlockSpec((tm, d), lambda i: (i, 0)),
                  pl.BlockSpec((1, d), lambda i: (0, 0)),
                  full(w_in), pl.BlockSpec((1, Q_LORA), lambda i: (0, 0)),
                  pl.BlockSpec((1, KV_LORA), lambda i: (0, 0)), full(wqt), full(wk), full(wvt),
                  pl.BlockSpec((tm, LANES), lambda i: (i % pos_tiles, 0)),
                  pl.BlockSpec((tm, LANES), lambda i: (i % pos_tiles, 0)),
                  pl.BlockSpec((LANES, tm), lambda i: (0, i % pos_tiles)),
                  pl.BlockSpec((LANES, tm), lambda i: (0, i % pos_tiles))],
        out_specs=[tile_t(256),
                   pl.BlockSpec((tm, C_HEADS * 256), lambda i: (i, 0)),
                   tile_t(VT_ROWS)],
        out_shape=[jax.ShapeDtypeStruct((b, C_HEADS, pos_tiles, 256, tm), BF16),
                   jax.ShapeDtypeStruct((m, C_HEADS * 256), BF16),
                   jax.ShapeDtypeStruct((b, C_HEADS, pos_tiles, VT_ROWS, tm), BF16)],
        compiler_params=_params("parallel"),
    )(x, g.reshape(1, d), w_in, qn.reshape(1, -1), kvn.reshape(1, -1), wqt, wk, wvt, cos, sin, cos.T, sin.T)


def _mla_attn_kernel(qt_ref, k_ref, vt_ref, o_ref, s_scr, acc_scr, m_scr, *, n_q):
    tq = tk = MLA_TQ
    parts = tq // MLA_TK
    ki = lax.broadcasted_iota(jnp.int32, (tk, tq), 0)
    qi = lax.broadcasted_iota(jnp.int32, (tk, tq), 1)
    causal = ki <= qi
    heads = range(MLA_HEADS)

    def q_tile(t, c):
        q_ts = [jnp.concatenate([qt_ref[0, h, parts * t + d] for d in range(parts)], axis=1) for h in heads]

        def logits_to(slot, j):
            rows = pl.ds(pl.multiple_of(j * tk, tk), tk)
            for h in heads:
                s_scr[h, slot] = _dot(k_ref[0, rows, h * 256:(h + 1) * 256], q_ts[h])

        def update_from(slot, j, diagonal):
            for h in heads:
                s = s_scr[h, slot]
                if diagonal:
                    s = jnp.where(causal, s, NEG)
                m = m_scr[h]
                mn = jnp.maximum(m, jnp.max(s, axis=0, keepdims=True))
                p = jnp.exp2(s - mn).astype(BF16)
                m_scr[h] = mn
                v_t = jnp.concatenate([vt_ref[0, h, parts * j + d] for d in range(parts)], axis=1)
                acc_scr[h] = jnp.exp2(m - mn) * acc_scr[h] + _dot(v_t, p)

        m_scr[...] = jnp.full(m_scr.shape, NEG, F32)
        acc_scr[...] = jnp.zeros(acc_scr.shape, F32)
        logits_to(0, t)
        logits_to(1, 0)
        update_from(0, t, True)

        def body(jj, c2):
            logits_to(0, jnp.minimum(2 * jj + 1, t - 1))
            update_from(1, 2 * jj, False)
            logits_to(1, jnp.minimum(2 * jj + 2, t - 1))

            @pl.when(2 * jj + 1 < t)
            def _():
                update_from(0, 2 * jj + 1, False)
            return c2

        lax.fori_loop(0, (t + 1) // 2, body, 0)
        rows = pl.ds(pl.multiple_of(t * tq, tq), tq)
        for h in heads:
            acc = acc_scr[h]
            o = acc[:V_DIM] * (1.0 / acc[V_DIM:V_DIM + 1])
            o_ref[0, rows, h * LANES:(h + 1) * LANES] = o.T.astype(o_ref.dtype)
        return c

    lax.fori_loop(0, n_q, q_tile, 0)


def _mla_attention(qt, k, vt):
    b, s, _ = k.shape
    n_k = s // MLA_TK
    nh = MLA_HEADS
    return pl.pallas_call(
        functools.partial(_mla_attn_kernel, n_q=s // MLA_TQ),
        grid=(b, C_HEADS // nh),
        in_specs=[pl.BlockSpec((1, nh, n_k, 256, MLA_TK), lambda bi, h: (bi, h, 0, 0, 0)),
                  pl.BlockSpec((1, s, 256 * nh), lambda bi, h: (bi, 0, h)),
                  pl.BlockSpec((1, nh, n_k, VT_ROWS, MLA_TK), lambda bi, h: (bi, h, 0, 0, 0))],
        out_specs=pl.BlockSpec((1, s, LANES * nh), lambda bi, h: (bi, 0, h)),
        out_shape=jax.ShapeDtypeStruct((b, s, C_HEADS * V_DIM), BF16),
        scratch_shapes=[pltpu.VMEM((nh, 2, MLA_TQ, MLA_TQ), F32),
                        pltpu.VMEM((nh, VT_ROWS, MLA_TQ), F32),
                        pltpu.VMEM((nh, 1, MLA_TQ), F32)],
        compiler_params=_params("parallel", "arbitrary"),
    )(qt, k, vt)


def _even_in_weight(w):
    cuts = np.cumsum([1024, 128, 128, 1024, 128, 128, 128, 128, 128, 128, 48])[:-1]
    qa, ka, va, qb, kc, vc, ks, vs, kw, vw, gate = jnp.split(w, [int(c) for c in cuts], axis=1)
    gate = jnp.pad(gate, ((0, 0), (0, LANES - gate.shape[1])))
    return jnp.concatenate([qa, qb, ka, va, kc, vc, ks, vs, kw, vw, gate], axis=1).astype(BF16)


def _compress_weights(pos, w1, w2):
    half = CMP_BLOCK // 2
    eye = jnp.eye(GROUPS, dtype=w1.dtype)
    w1r = w1.reshape(2, half, HEAD_DIM, CMP_HIDDEN)
    wab = jnp.einsum('rldh,gk->rlgdkh', w1r, eye).reshape(2, half * GROUPS * HEAD_DIM, GROUPS * CMP_HIDDEN)
    posr = jnp.broadcast_to(pos.reshape(2, half, 1, HEAD_DIM), (2, half, GROUPS, HEAD_DIM)).reshape(2, 1, -1)
    w2bd = jnp.einsum('hd,gk->ghkd', w2, eye).reshape(GROUPS * CMP_HIDDEN, GROUPS * HEAD_DIM)
    return posr[0], posr[1], wab[0].astype(BF16), wab[1].astype(BF16), w2bd.astype(BF16)


def _rope_chunk_cols(w):
    z = jnp.zeros(w.shape[:-1] + (ROPE_DIM // 2,), w.dtype)
    return jnp.concatenate([w[..., :ROPE_DIM // 2], z, w[..., ROPE_DIM // 2:], z], axis=-1)


def _mla_weights(w_in, w_q_up, w_kv_up):
    w_in2 = jnp.concatenate([w_in[:, :Q_LORA + KV_LORA], _rope_chunk_cols(w_in[:, Q_LORA + KV_LORA:])], axis=1)
    wq = w_q_up.reshape(Q_LORA, C_HEADS, NOPE_DIM + ROPE_DIM)
    wq2 = jnp.concatenate([wq[..., :NOPE_DIM], _rope_chunk_cols(wq[..., NOPE_DIM:])], axis=-1)
    wqt = wq2.transpose(1, 2, 0)
    wkv = w_kv_up.reshape(KV_LORA, C_HEADS, NOPE_DIM + V_DIM)
    wk = wkv[..., :NOPE_DIM].reshape(KV_LORA, C_HEADS * NOPE_DIM)
    wvt = wkv[..., NOPE_DIM:].transpose(1, 2, 0)
    return w_in2.astype(BF16), wqt.astype(BF16), wk.astype(BF16), wvt.astype(BF16)


def _rope_tables(s):
    inv = 1.0 / (ROPE_THETA ** (jnp.arange(0, ROPE_DIM, 2, dtype=F32) / ROPE_DIM))
    ang = jnp.arange(s, dtype=F32)[:, None] * inv[None]
    cos, sin = jnp.cos(ang), jnp.sin(ang)
    z = jnp.zeros_like(cos)
    return jnp.concatenate([cos, z, cos, z], axis=1), jnp.concatenate([-sin, z, sin, z], axis=1)


def _static_tables(s):
    k = np.arange(TQ)[:, None]
    q = np.arange(TQ)[None, :]
    diag = np.where(k <= q, _bucket_np(q - k), -1)
    prev = _bucket_np(q - k + TQ)
    near_swa = np.stack([diag, np.where(k > q, prev, -1)])
    near_nsa = np.stack([diag, prev])
    upper = np.where(k > q, 0.0, NEG).astype(np.float32)
    c = np.arange(LANES)[:, None]
    t = np.arange(s)[None, :]
    cdist = t - (c * CMP_STRIDE + CMP_BLOCK - 1)
    cmp_map = np.where(cdist >= 0, _bucket_np(cdist), -1)
    ns = s // SLC_BLOCK
    nc = (s - CMP_BLOCK) // CMP_STRIDE + 1
    c_start = np.arange(LANES) * CMP_STRIDE
    s_start = np.arange(ns) * SLC_BLOCK
    ovt = ((c_start[None, :] <= s_start[:, None] + SLC_BLOCK - 1) &
           (c_start[None, :] + CMP_BLOCK - 1 >= s_start[:, None]) &
           (np.arange(LANES)[None, :] < nc)).astype(np.float32)
    esel = (np.arange(s)[:, None] // SLC_BLOCK == np.arange(LANES)[None, :]).astype(np.float32)
    e3 = np.zeros((3, LANES, B_HEADS * HEAD_DIM), np.float32)
    for h in range(B_HEADS):
        for br in range(3):
            e3[br, h * 3 + br, h * HEAD_DIM:(h + 1) * HEAD_DIM] = 1.0
    return near_swa, near_nsa, upper, cmp_map, ovt, esel, e3


def _even_layer(x, b, s, rel_bias, norm, w_in, sinks, pos_k, pos_v, k_w1, k_w2, v_w1, v_w2, w_out):
    near_swa, near_nsa, upper, cmp_map, ovt, esel, e3 = _static_tables(s)
    ta = _bias_tiles(rel_bias, near_swa, 0, A_HEADS)
    tb = _bias_tiles(rel_bias, near_nsa, A_HEADS, B_HEADS)
    cbias = _bias_tiles(rel_bias, cmp_map, A_HEADS, B_HEADS)

    pe = _norm_matmul(x, norm, _even_in_weight(w_in), 1024, 640, BF16).reshape(b, s, EVEN_COLS)

    rows = s // (CMP_BLOCK // 2)
    hkv = jnp.stack([pe[:, :, SEG_KC * LANES:(SEG_KC + 1) * LANES],
                     pe[:, :, SEG_VC * LANES:(SEG_VC + 1) * LANES]]).reshape(2, b, rows, -1)
    ck = _compress_weights(pos_k, k_w1, k_w2)
    cv = _compress_weights(pos_v, v_w1, v_w2)
    kvc = _compress(hkv, *[jnp.stack([a, c]) for a, c in zip(ck, cv)])
    assert rows <= TQ
    kvc = jnp.pad(kvc, ((0, 0), (0, 0), (0, TQ - rows), (0, 0)))

    oa, ob = _even_attention(pe, kvc, rel_bias, sinks, ta, tb, jnp.asarray(upper), cbias,
                             jnp.asarray(e3, BF16), jnp.asarray(esel, BF16), jnp.asarray(ovt, BF16))
    w_out = w_out.astype(BF16)
    n_a = A_HEADS * HEAD_DIM
    return _out_proj(x, [oa.reshape(b * s, -1), ob.reshape(b * s, -1)], [w_out[:n_a], w_out[n_a:]], 512)


def _odd_layer(x, b, s, norm, w_in, q_norm, w_q_up, kv_norm, w_kv_up, w_out):
    w_in2, wqt, wk, wvt = _mla_weights(w_in, w_q_up, w_kv_up)
    cos, sin = _rope_tables(s)
    qt, k, vt = _mla_prep(x, norm, w_in2, q_norm, kv_norm, wqt, wk, wvt, cos, sin, b, s)
    o = _mla_attention(qt, k.reshape(b, s, -1), vt)
    return _out_proj(x, [o.reshape(b * s, -1)], [w_out.astype(BF16)], 512)


def kernel(x, rel_bias, norm_mix_e, w_in_e, sinks, cmp_pos_k, cmp_pos_v, cmp_k_w1, cmp_k_w2, cmp_v_w1, cmp_v_w2, w_out_e, norm_mix_o, w_in_o, q_norm, w_q_up, kv_norm, w_kv_up, w_out_o, norm_mlp, w_up, w_down, norm_final):
    b, s, d = x.shape
    depth = norm_mlp.shape[0]
    h = x.reshape(b * s, d)
    for layer in range(depth):
        i = layer // 2
        if layer % 2 == 0:
            h = _even_layer(h, b, s, rel_bias, norm_mix_e[i], w_in_e[i], sinks[i], cmp_pos_k[i], cmp_pos_v[i],
                            cmp_k_w1[i], cmp_k_w2[i], cmp_v_w1[i], cmp_v_w2[i], w_out_e[i])
        else:
            h = _odd_layer(h, b, s, norm_mix_o[i], w_in_o[i], q_norm[i], w_q_up[i], kv_norm[i], w_kv_up[i],
                           w_out_o[i])
        h = _mlp(h, norm_mlp[layer], w_up[layer].astype(BF16), w_down[layer].astype(BF16), norm_final,
                 512, 512, layer == depth - 1)
    return h.reshape(b, s, d)
```

```python
import functools
import math

import numpy as np
import jax
import jax.numpy as jnp
from jax import lax
from jax.experimental import pallas as pl
from jax.experimental.pallas import tpu as pltpu

F32 = jnp.float32
BF16 = jnp.bfloat16

LANES = 128
VMEM_LIMIT_BYTES = 56 * 1024 * 1024

EPS = 1e-6
NEG = -1e30
BIG = 1e4
HEAD_DIM = 64
TQ = 128
MLA_TQ = 512
MLA_TK = 256
MLA_HEADS = 4
N_BUCKETS = 32
MAX_DISTANCE = 128
A_HEADS = 16
B_HEADS = 16
GROUPS = 2
PAIRS = 4
W4 = PAIRS * TQ
CMP_BLOCK = 32
CMP_STRIDE = 16
CMP_HIDDEN = 256
SLC_BLOCK = 64
SLC_TOPK = 8
A_WINDOW = 128
B_WINDOW = 512
C_HEADS = 16
Q_LORA = 768
KV_LORA = 512
NOPE_DIM = 128
ROPE_DIM = 64
V_DIM = 128
VT_ROWS = V_DIM + 16
LOG2E = math.log2(math.e)
ROPE_THETA = 10000.0

COL_QA, COL_QB = 0, 1024
SEG_KA, SEG_VA, SEG_KC, SEG_VC, SEG_KS, SEG_VS, SEG_KW, SEG_VW, SEG_GATE = range(16, 25)
EVEN_COLS = 25 * LANES
K_A, K_S, K_W = 0, 1, 2


def _params(*sem):
    return pltpu.CompilerParams(dimension_semantics=sem, vmem_limit_bytes=VMEM_LIMIT_BYTES)


def _rms(x, g):
    return x * lax.rsqrt(jnp.mean(x * x, axis=-1, keepdims=True) + EPS) * g


def _dot(a, b):
    return jnp.dot(a, b, preferred_element_type=F32)


def _bucket_np(dist):
    dist = np.maximum(dist, 0)
    max_exact = N_BUCKETS // 2
    d = np.maximum(dist, 1).astype(np.float32)
    large = max_exact + (np.log(d / np.float32(max_exact)) / np.float32(math.log(MAX_DISTANCE / max_exact))
                         * np.float32(N_BUCKETS - max_exact)).astype(np.int32)
    large = np.minimum(large, N_BUCKETS - 1)
    return np.where(dist < max_exact, dist, large).astype(np.int32)


def _bias_kernel(tab_ref, bm_ref, o_ref, *, head0):
    h = pl.program_id(0) + head0
    bm = bm_ref[...]
    acc = jnp.zeros(bm.shape, F32)
    for b in range(N_BUCKETS):
        acc = jnp.where(bm == b, tab_ref[b, h], acc)
    o_ref[0] = jnp.where(bm < 0, NEG, (acc - tab_ref[N_BUCKETS - 1, h]) * LOG2E)


def _bias_tiles(rel_bias, bucket_map, head0, n_heads):
    shp = bucket_map.shape
    nd = len(shp)
    return pl.pallas_call(
        functools.partial(_bias_kernel, head0=head0),
        grid=(n_heads,),
        in_specs=[pl.BlockSpec(memory_space=pltpu.SMEM),
                  pl.BlockSpec(shp, lambda h: (0,) * nd)],
        out_specs=pl.BlockSpec((1,) + shp, lambda h: (h,) + (0,) * nd),
        out_shape=jax.ShapeDtypeStruct((n_heads,) + shp, F32),
        compiler_params=_params("arbitrary"),
    )(rel_bias, jnp.asarray(bucket_map))


def _norm_matmul_kernel(x_ref, g_ref, w_ref, o_ref, xn_ref):
    @pl.when(pl.program_id(1) == 0)
    def _():
        xn_ref[...] = _rms(x_ref[...], g_ref[...]).astype(BF16)

    o_ref[...] = _dot(xn_ref[...], w_ref[...]).astype(o_ref.dtype)


def _norm_matmul(x, g, w, tm, tn, out_dtype):
    m, d = x.shape
    n = w.shape[1]
    return pl.pallas_call(
        _norm_matmul_kernel,
        grid=(m // tm, n // tn),
        in_specs=[pl.BlockSpec((tm, d), lambda i, j: (i, 0)),
                  pl.BlockSpec((1, d), lambda i, j: (0, 0)),
                  pl.BlockSpec((d, tn), lambda i, j: (0, j))],
        out_specs=pl.BlockSpec((tm, tn), lambda i, j: (i, j)),
        out_shape=jax.ShapeDtypeStruct((m, n), out_dtype),
        scratch_shapes=[pltpu.VMEM((tm, d), BF16)],
        compiler_params=_params("parallel", "arbitrary"),
    )(x, g.reshape(1, d), w)


def _out_proj_kernel(*refs, n_in):
    x_ref = refs[0]
    o_ref = refs[2 * n_in + 1]
    acc = x_ref[...]
    for t in range(n_in):
        acc = acc + _dot(refs[1 + t][...], refs[1 + n_in + t][...])
    o_ref[...] = acc


def _out_proj(x, acts, ws, tm):
    m, d = x.shape
    n_in = len(acts)
    in_specs = [pl.BlockSpec((tm, d), lambda i: (i, 0))]
    in_specs += [pl.BlockSpec((tm, a.shape[1]), lambda i: (i, 0)) for a in acts]
    in_specs += [pl.BlockSpec(w.shape, lambda i: (0, 0)) for w in ws]
    return pl.pallas_call(
        functools.partial(_out_proj_kernel, n_in=n_in),
        grid=(m // tm,),
        in_specs=in_specs,
        out_specs=pl.BlockSpec((tm, d), lambda i: (i, 0)),
        out_shape=jax.ShapeDtypeStruct((m, d), F32),
        compiler_params=_params("parallel"),
    )(x, *acts, *ws)


def _mlp_kernel(x_ref, g_ref, wu_ref, wd_ref, gf_ref, o_ref, xn_ref, *, final_norm):
    j = pl.program_id(1)

    @pl.when(j == 0)
    def _():
        x = x_ref[...]
        xn_ref[...] = _rms(x, g_ref[...]).astype(BF16)
        o_ref[...] = x

    h = _dot(xn_ref[...], wu_ref[0].astype(BF16))
    a = jnp.square(jnp.maximum(h, 0.0)).astype(BF16)
    o_ref[...] += _dot(a, wd_ref[0].astype(BF16))

    if final_norm:
        @pl.when(j == pl.num_programs(1) - 1)
        def _():
            o_ref[...] = _rms(o_ref[...], gf_ref[...])


def _mlp(x, g, w_up, w_down, layer, g_final, tm, tf, final_norm):
    m, d = x.shape
    ff = w_up.shape[2]
    return pl.pallas_call(
        functools.partial(_mlp_kernel, final_norm=final_norm),
        grid=(m // tm, ff // tf),
        in_specs=[pl.BlockSpec((tm, d), lambda i, j: (i, 0), pipeline_mode=pl.Buffered(1)),
                  pl.BlockSpec((1, d), lambda i, j: (0, 0)),
                  pl.BlockSpec((1, d, tf), lambda i, j: (layer, 0, j)),
                  pl.BlockSpec((1, tf, d), lambda i, j: (layer, j, 0)),
                  pl.BlockSpec((1, d), lambda i, j: (0, 0))],
        out_specs=pl.BlockSpec((tm, d), lambda i, j: (i, 0)),
        out_shape=jax.ShapeDtypeStruct((m, d), F32),
        scratch_shapes=[pltpu.VMEM((tm, d), BF16)],
        compiler_params=_params("parallel", "arbitrary"),
    )(x, g.reshape(1, d), w_up, w_down, g_final.reshape(1, d))


def _gelu_tanh(x):
    return 0.5 * x * (1.0 + jnp.tanh(math.sqrt(2.0 / math.pi) * (x + 0.044715 * (x * x * x))))


def _compress_kernel(h_ref, pa_ref, pb_ref, wa_ref, wb_ref, w2_ref, o_ref):
    h = h_ref[0, 0].astype(F32)
    ha = _dot((h + pa_ref[0]).astype(BF16), wa_ref[0])
    hb = _dot((h + pb_ref[0]).astype(BF16), wb_ref[0])
    n = hb.shape[0]
    pre = ha + pltpu.roll(hb, n - 1, 0)
    o_ref[0, 0] = _dot(_gelu_tanh(pre).astype(BF16), w2_ref[0]).astype(o_ref.dtype)


def _compress(hkv, pa, pb, wa, wb, w2):
    _, b, r, w = hkv.shape
    hid2 = wa.shape[2]
    return pl.pallas_call(
        _compress_kernel,
        grid=(2, b),
        in_specs=[pl.BlockSpec((1, 1, r, w), lambda t, i: (t, i, 0, 0)),
                  pl.BlockSpec((1, 1, w), lambda t, i: (t, 0, 0)),
                  pl.BlockSpec((1, 1, w), lambda t, i: (t, 0, 0)),
                  pl.BlockSpec((1, w, hid2), lambda t, i: (t, 0, 0)),
                  pl.BlockSpec((1, w, hid2), lambda t, i: (t, 0, 0)),
                  pl.BlockSpec((1, hid2, LANES), lambda t, i: (t, 0, 0))],
        out_specs=pl.BlockSpec((1, 1, r, LANES), lambda t, i: (t, i, 0, 0)),
        out_shape=jax.ShapeDtypeStruct((2, b, r, LANES), BF16),
        compiler_params=_params("arbitrary", "arbitrary"),
    )(hkv, pa, pb, wa, wb, w2)


SWA, SLC, WIN = 0, 1, 2


def _stack_pairs_t(q_ref, g):
    cols = [q_ref[0, :, (PAIRS * g + p) * LANES:(PAIRS * g + p + 1) * LANES].astype(F32).T for p in range(PAIRS)]
    return (jnp.concatenate(cols, axis=1) * (HEAD_DIM ** -0.5 * LOG2E)).astype(BF16)


def _rows_by_half(a, b, rows):
    parts = [jnp.broadcast_to(a, (HEAD_DIM, W4)), jnp.broadcast_to(b, (HEAD_DIM, W4))]
    if rows > 2 * HEAD_DIM:
        first = lax.broadcasted_iota(jnp.int32, (rows - 2 * HEAD_DIM, W4), 0) == 0
        parts.append(jnp.where(first, a, b))
    return jnp.concatenate(parts, axis=0)


def _lanes4(tile):
    return jnp.concatenate([tile] * PAIRS, axis=1)


def _even_attn_kernel(tab_ref, sink_ref, qa_ref, qb_ref, gate_ref,
                      ka_ref, va_ref, ks_ref, vs_ref, kw_ref, vw_ref, kc_ref, vc_ref,
                      ta_ref, tb_ref, up_ref, cb_ref, e3_ref, esel_ref, ovt_ref,
                      oa_ref, ob_ref,
                      kbd_ref, vbd_ref, ckbd_ref, cvbd_ref, msk_ref, s_scr, acc_scr, m_scr, *, n_tiles):
    i = pl.program_id(1)
    groups = range(GROUPS)

    def block_diag(x, g):
        r = pltpu.roll(x, HEAD_DIM, 1)
        lo = lax.broadcasted_iota(jnp.int32, x.shape, 1) < HEAD_DIM
        own, other = (x, r) if g == 0 else (r, x)
        return jnp.where(lo, own, 0.0), jnp.where(lo, 0.0, other)

    @pl.when(i == 0)
    def _build():
        r16 = lax.broadcasted_iota(jnp.int32, (VT_ROWS - LANES, 2 * TQ), 0)
        c16 = lax.broadcasted_iota(jnp.int32, (VT_ROWS - LANES, 2 * TQ), 1)
        ones_rows = jnp.where((r16 == 0) & (c16 < TQ) | (r16 == 1) & (c16 >= TQ), 1.0, 0.0).astype(BF16)
        for t, ref in enumerate((ka_ref, ks_ref, kw_ref)):
            def kbody(j, c, t=t, ref=ref):
                x = ref[0, pl.ds(pl.multiple_of(j * TQ, TQ), TQ), :].astype(F32)
                for g in groups:
                    top, bot = block_diag(x, g)
                    kbd_ref[g, t, j, 0:TQ, :] = top.astype(BF16)
                    kbd_ref[g, t, j, TQ:2 * TQ, :] = bot.astype(BF16)
                return c
            lax.fori_loop(0, n_tiles, kbody, 0)
        for t, ref in enumerate((va_ref, vs_ref, vw_ref)):
            def vbody(j, c, t=t, ref=ref):
                x = ref[0, pl.ds(pl.multiple_of(j * TQ, TQ), TQ), :].astype(F32)
                for g in groups:
                    top, bot = block_diag(x, g)
                    vbd_ref[g, t, j, 0:LANES, 0:TQ] = top.T.astype(BF16)
                    vbd_ref[g, t, j, 0:LANES, TQ:2 * TQ] = bot.T.astype(BF16)
                    vbd_ref[g, t, j, LANES:, :] = ones_rows
                return c
            lax.fori_loop(0, n_tiles, vbody, 0)
        xk = kc_ref[0, 0].astype(F32)
        xv = vc_ref[0, 0].astype(F32)
        for g in groups:
            top, bot = block_diag(xk, g)
            ckbd_ref[g, 0:TQ, :] = top.astype(BF16)
            ckbd_ref[g, TQ:2 * TQ, :] = bot.astype(BF16)
            top, bot = block_diag(xv, g)
            cvbd_ref[g, :, 0:TQ] = top.T.astype(BF16)
            cvbd_ref[g, :, TQ:2 * TQ] = bot.T.astype(BF16)

    def near_bias(t_ref, g, delta):
        return [jnp.concatenate([t_ref[8 * g + 2 * p + hf, delta] for p in range(PAIRS)], axis=1)
                for hf in range(2)]

    def update(g, st, s, adds, vbd_t):
        ps, al = [], []
        for hf in range(2):
            sh = s[hf * TQ:(hf + 1) * TQ]
            for a in adds:
                sh = sh + a[hf]
            m = m_scr[g, st, hf]
            mn = jnp.maximum(m, jnp.max(sh, axis=0, keepdims=True))
            ps.append(jnp.exp2(sh - mn).astype(BF16))
            al.append(jnp.exp2(m - mn))
            m_scr[g, st, hf] = mn
        acc_scr[g, st] = _rows_by_half(al[0], al[1], VT_ROWS) * acc_scr[g, st] + _dot(vbd_t, jnp.concatenate(ps, axis=0))

    def finish(g, st, extra=None):
        acc = acc_scr[g, st]
        den = [acc[LANES + hf:LANES + hf + 1] for hf in range(2)]
        if extra is not None:
            den = [den[hf] + extra(hf, m_scr[g, st, hf]) for hf in range(2)]
        return acc[:LANES] * _rows_by_half(1.0 / den[0], 1.0 / den[1], LANES)

    def tile_logits(g, slot, j, q4t):
        return _dot(kbd_ref[g, slot, j], q4t)

    m_scr[...] = jnp.full(m_scr.shape, NEG, F32)
    acc_scr[...] = jnp.zeros(acc_scr.shape, F32)
    q4a = [_stack_pairs_t(qa_ref, g) for g in groups]
    q4b = [_stack_pairs_t(qb_ref, g) for g in groups]

    for g in groups:
        update(g, SWA, tile_logits(g, K_A, i, q4a[g]), [near_bias(ta_ref, g, 0)], vbd_ref[g, K_A, i])
        update(g, WIN, tile_logits(g, K_W, i, q4b[g]), [near_bias(tb_ref, g, 0)], vbd_ref[g, K_W, i])

    o_cmp = []
    t_row = i * TQ + (lax.broadcasted_iota(jnp.int32, (1, W4), 1) & (TQ - 1))
    anyvis = jnp.where(t_row >= CMP_BLOCK - 1, 1.0, 0.0)
    for g in groups:
        s = _dot(ckbd_ref[g], q4b[g])
        psum = jnp.zeros((TQ, TQ), F32)
        ps = []
        for hf in range(2):
            cb = jnp.concatenate([cb_ref[8 * g + 2 * p + hf] for p in range(PAIRS)], axis=1)
            sh = s[hf * TQ:(hf + 1) * TQ] + cb
            pe = jnp.exp2(sh - jnp.max(sh, axis=0, keepdims=True))
            pc = pe * (anyvis / jnp.sum(pe, axis=0, keepdims=True))
            for p in range(PAIRS):
                psum = psum + pc[:, p * TQ:(p + 1) * TQ]
            ps.append(pc.astype(BF16))
        o_cmp.append(_dot(cvbd_ref[g], jnp.concatenate(ps, axis=0)))

        ph = psum.astype(BF16)
        r1 = psum - ph.astype(F32)
        pm = r1.astype(BF16)
        pl_ = (r1 - pm.astype(F32)).astype(BF16)
        ovt = ovt_ref[...]
        pslc = _dot(ovt, ph) + _dot(ovt, pm) + _dot(ovt, pl_)
        ns = pslc.shape[0]
        nb = lax.broadcasted_iota(jnp.int32, (ns, TQ), 0)
        tq = i * TQ + lax.broadcasted_iota(jnp.int32, (ns, TQ), 1)
        cur = tq // SLC_BLOCK
        forced = (jnp.where(nb == 0, 1.0, 0.0) + jnp.where(nb == cur, 1.0, 0.0) +
                  jnp.where(nb == cur - 1, 1.0, 0.0))
        score = jnp.where(nb * SLC_BLOCK > tq, NEG, jnp.where(forced > 0.0, BIG, pslc))
        rank = jnp.zeros((ns, TQ), F32)
        for mth in range(ns):
            sm = score[mth:mth + 1, :]
            tie = jnp.where(nb > mth, 1.0, 0.0)
            rank = rank + jnp.where(sm > score, 1.0, jnp.where(sm == score, tie, 0.0))
        sel = jnp.where(rank < SLC_TOPK, 1.0, 0.0)
        sel = jnp.concatenate([sel, jnp.zeros((LANES - ns, TQ), F32)], axis=0).astype(BF16)
        msk_ref[g] = ((_dot(esel_ref[...], sel) - 1.0) * -NEG).reshape(n_tiles, TQ, TQ)

    def sel_add(g, j):
        t4 = _lanes4(msk_ref[g, j])
        return [t4, t4]

    n_full = B_WINDOW // TQ

    def near_tiles(n_back):
        for g in groups:
            update(g, SLC, tile_logits(g, K_S, i, q4b[g]), [near_bias(tb_ref, g, 0), sel_add(g, i)], vbd_ref[g, K_S, i])
        if n_back >= 1:
            j = i - 1
            for g in groups:
                update(g, SWA, tile_logits(g, K_A, j, q4a[g]), [near_bias(ta_ref, g, 1)], vbd_ref[g, K_A, j])
                update(g, SLC, tile_logits(g, K_S, j, q4b[g]), [near_bias(tb_ref, g, 1), sel_add(g, j)],
                       vbd_ref[g, K_S, j])
                update(g, WIN, tile_logits(g, K_W, j, q4b[g]), [near_bias(tb_ref, g, 1)], vbd_ref[g, K_W, j])
        for d in range(2, n_back + 1):
            j = i - d
            edge = []
            if d == n_full:
                up4 = _lanes4(up_ref[...])
                edge = [[up4, up4]]
            for g in groups:
                update(g, WIN, tile_logits(g, K_W, j, q4b[g]), edge, vbd_ref[g, K_W, j])

    for c in range(n_full):
        pl.when(i == c)(functools.partial(near_tiles, c))
    pl.when(i >= n_full)(functools.partial(near_tiles, n_full))

    def far_logits_to(slot, d):
        for g in groups:
            s_scr[g, slot] = tile_logits(g, K_S, i - jnp.minimum(d, i), q4b[g])

    def far_update_from(slot, d):
        j = i - d
        for g in groups:
            update(g, SLC, s_scr[g, slot], [sel_add(g, j)], vbd_ref[g, K_S, j])

    far_logits_to(0, 2)

    def far_body(jj, c):
        d = 2 + 2 * jj
        far_logits_to(1, d + 1)
        far_update_from(0, d)
        far_logits_to(0, d + 2)

        @pl.when(d + 1 <= i)
        def _():
            far_update_from(1, d + 1)
        return c

    lax.fori_loop(0, i // 2, far_body, 0)

    def to_rows(o_t, p):
        return o_t[:, p * TQ:(p + 1) * TQ].T

    far_of = lambda h: tab_ref[N_BUCKETS - 1, h]
    sg = jax.nn.sigmoid(gate_ref[0].astype(F32))
    sg_hi = sg.astype(BF16)
    sg_lo = (sg - sg_hi.astype(F32)).astype(BF16)
    for g in groups:
        def sink_term(hf, m, g=g):
            row = jnp.concatenate([jnp.full((1, TQ), (sink_ref[8 * g + 2 * p + hf] - far_of(8 * g + 2 * p + hf)) * LOG2E,
                                            F32) for p in range(PAIRS)], axis=1)
            return jnp.exp2(row - m)

        o_swa = finish(g, SWA, sink_term)
        o_slc = finish(g, SLC)
        o_win = finish(g, WIN)
        for p in range(PAIRS):
            cols = slice((PAIRS * g + p) * LANES, (PAIRS * g + p + 1) * LANES)
            oa_ref[0, :, cols] = to_rows(o_swa, p).astype(oa_ref.dtype)
            gts = [_dot(sg_hi, e3_ref[br, :, cols]) + _dot(sg_lo, e3_ref[br, :, cols]) for br in range(3)]
            o = gts[0] * to_rows(o_cmp[g], p) + gts[1] * to_rows(o_slc, p) + gts[2] * to_rows(o_win, p)
            ob_ref[0, :, cols] = o.astype(ob_ref.dtype)


def _even_attention(pe, kvc, rel_bias, sinks, ta, tb, up, cbias, e3, esel, ovt):
    b, s, _ = pe.shape
    n_tiles = s // TQ
    width = GROUPS * PAIRS * LANES

    def seg(k):
        return pl.BlockSpec((1, s, LANES), lambda bi, i, k=k: (bi, 0, k))

    full = lambda a: pl.BlockSpec(a.shape, lambda bi, i: (0,) * a.ndim)
    in_specs = [
        pl.BlockSpec(memory_space=pltpu.SMEM),
        pl.BlockSpec(memory_space=pltpu.SMEM),
        pl.BlockSpec((1, TQ, width), lambda bi, i: (bi, i, COL_QA // width)),
        pl.BlockSpec((1, TQ, width), lambda bi, i: (bi, i, COL_QB // width)),
        pl.BlockSpec((1, TQ, LANES), lambda bi, i: (bi, i, SEG_GATE)),
        seg(SEG_KA), seg(SEG_VA), seg(SEG_KS), seg(SEG_VS), seg(SEG_KW), seg(SEG_VW),
        pl.BlockSpec((1, 1, TQ, LANES), lambda bi, i: (0, bi, 0, 0)),
        pl.BlockSpec((1, 1, TQ, LANES), lambda bi, i: (1, bi, 0, 0)),
        full(ta), full(tb), full(up),
        pl.BlockSpec((B_HEADS, TQ, TQ), lambda bi, i: (0, 0, i)),
        full(e3), full(esel), full(ovt),
    ]
    out_spec = pl.BlockSpec((1, TQ, width), lambda bi, i: (bi, i, 0))
    return pl.pallas_call(
        functools.partial(_even_attn_kernel, n_tiles=n_tiles),
        grid=(b, n_tiles),
        in_specs=in_specs,
        out_specs=[out_spec, out_spec],
        out_shape=[jax.ShapeDtypeStruct((b, s, width), BF16)] * 2,
        scratch_shapes=[pltpu.VMEM((GROUPS, 3, n_tiles, 2 * TQ, LANES), BF16),
                        pltpu.VMEM((GROUPS, 3, n_tiles, VT_ROWS, 2 * TQ), BF16),
                        pltpu.VMEM((GROUPS, 2 * TQ, LANES), BF16),
                        pltpu.VMEM((GROUPS, LANES, 2 * TQ), BF16),
                        pltpu.VMEM((GROUPS, n_tiles, TQ, TQ), F32),
                        pltpu.VMEM((GROUPS, 2, 2 * TQ, W4), F32),
                        pltpu.VMEM((GROUPS, 3, VT_ROWS, W4), F32),
                        pltpu.VMEM((GROUPS, 3, 2, 1, W4), F32)],
        compiler_params=_params("arbitrary", "arbitrary"),
    )(rel_bias, sinks, pe, pe, pe, pe, pe, pe, pe, pe, pe, kvc, kvc, ta, tb, up, cbias, e3, esel, ovt)


def _mla_prep_kernel(x_ref, g_ref, win_ref, qn_ref, kvn_ref, wqt_ref, wk_ref, wvt_ref,
                     cos_ref, sin_ref, cost_ref, sint_ref, qt_ref, k_ref, vt_ref):
    xn = _rms(x_ref[...], g_ref[...]).astype(BF16)
    proj = _dot(xn, win_ref[...])
    cq = _rms(proj[:, :Q_LORA], qn_ref[...])
    ckv = _rms(proj[:, Q_LORA:Q_LORA + KV_LORA], kvn_ref[...])
    cq_t = cq.T.astype(BF16)
    ckv_t = ckv.T.astype(BF16)
    ckv = ckv.astype(BF16)
    kr = proj[:, Q_LORA + KV_LORA:]
    kr = (kr * cos_ref[...] + pltpu.roll(kr, HEAD_DIM, 1) * sin_ref[...]).astype(BF16)
    cos_t = cost_ref[...]
    sin_t = sint_ref[...]
    scale = (NOPE_DIM + ROPE_DIM) ** -0.5 * LOG2E
    tm = ckv_t.shape[1]
    ones_rows = jnp.where(lax.broadcasted_iota(jnp.int32, (VT_ROWS - V_DIM, tm), 0) == 0, 1.0, 0.0).astype(BF16)
    for h in range(C_HEADS):
        q_t = _dot(wqt_ref[h], cq_t)
        rp = q_t[NOPE_DIM:]
        rp = rp * cos_t + pltpu.roll(rp, HEAD_DIM, 0) * sin_t
        qt_ref[0, h, 0, 0:NOPE_DIM, :] = (q_t[:NOPE_DIM] * scale).astype(BF16)
        qt_ref[0, h, 0, NOPE_DIM:, :] = (rp * scale).astype(BF16)
        if h % 2 == 0:
            k2 = _dot(ckv, wk_ref[:, h * LANES:(h + 2) * LANES]).astype(BF16)
        k_ref[:, h * 256:h * 256 + LANES] = k2[:, (h % 2) * LANES:(h % 2 + 1) * LANES]
        k_ref[:, h * 256 + LANES:(h + 1) * 256] = kr
        vt_ref[0, h, 0, 0:V_DIM, :] = _dot(wvt_ref[h], ckv_t).astype(BF16)
        vt_ref[0, h, 0, V_DIM:, :] = ones_rows


def _mla_prep(x, g, w_in, qn, kvn, wqt, wk, wvt, cos, sin, b, s):
    m, d = x.shape
    tm = MLA_TK
    pos_tiles = s // tm
    full = lambda a: pl.BlockSpec(a.shape, lambda i: (0,) * a.ndim)
    tile_t = lambda rows: pl.BlockSpec((1, C_HEADS, 1, rows, tm), lambda i: (i // pos_tiles, 0, i % pos_tiles, 0, 0))
    return pl.pallas_call(
        _mla_prep_kernel,
        grid=(m // tm,),
        in_specs=[pl.BlockSpec((tm, d), lambda i: (i, 0)),
                  pl.BlockSpec((1, d), lambda i: (0, 0)),
                  full(w_in), pl.BlockSpec((1, Q_LORA), lambda i: (0, 0)),
                  pl.BlockSpec((1, KV_LORA), lambda i: (0, 0)), full(wqt), full(wk), full(wvt),
                  pl.BlockSpec((tm, LANES), lambda i: (i % pos_tiles, 0)),
                  pl.BlockSpec((tm, LANES), lambda i: (i % pos_tiles, 0)),
                  pl.BlockSpec((LANES, tm), lambda i: (0, i % pos_tiles)),
                  pl.BlockSpec((LANES, tm), lambda i: (0, i % pos_tiles))],
        out_specs=[tile_t(256),
                   pl.BlockSpec((tm, C_HEADS * 256), lambda i: (i, 0)),
                   tile_t(VT_ROWS)],
        out_shape=[jax.ShapeDtypeStruct((b, C_HEADS, pos_tiles, 256, tm), BF16),
                   jax.ShapeDtypeStruct((m, C_HEADS * 256), BF16),
                   jax.ShapeDtypeStruct((b, C_HEADS, pos_tiles, VT_ROWS, tm), BF16)],
        compiler_params=_params("parallel"),
    )(x, g.reshape(1, d), w_in, qn.reshape(1, -1), kvn.reshape(1, -1), wqt, wk, wvt, cos, sin, cos.T, sin.T)


def _mla_attn_kernel(qt_ref, k_ref, vt_ref, o_ref, s_scr, acc_scr, m_scr, *, n_q):
    tq = tk = MLA_TQ
    parts = tq // MLA_TK
    ki = lax.broadcasted_iota(jnp.int32, (tk, tq), 0)
    qi = lax.broadcasted_iota(jnp.int32, (tk, tq), 1)
    causal = ki <= qi
    heads = range(MLA_HEADS)

    def q_tile(t, c):
        q_ts = [jnp.concatenate([qt_ref[0, h, parts * t + d] for d in range(parts)], axis=1) for h in heads]

        def logits_to(slot, j):
            rows = pl.ds(pl.multiple_of(j * tk, tk), tk)
            for h in heads:
                s_scr[h, slot] = _dot(k_ref[0, rows, h * 256:(h + 1) * 256], q_ts[h])

        def update_from(slot, j, diagonal):
            for h in heads:
                s = s_scr[h, slot]
                if diagonal:
                    s = jnp.where(causal, s, NEG)
                m = m_scr[h]
                mn = jnp.maximum(m, jnp.max(s, axis=0, keepdims=True))
                p = jnp.exp2(s - mn).astype(BF16)
                m_scr[h] = mn
                v_t = jnp.concatenate([vt_ref[0, h, parts * j + d] for d in range(parts)], axis=1)
                acc_scr[h] = jnp.exp2(m - mn) * acc_scr[h] + _dot(v_t, p)

        m_scr[...] = jnp.full(m_scr.shape, NEG, F32)
        acc_scr[...] = jnp.zeros(acc_scr.shape, F32)
        logits_to(0, t)
        logits_to(1, 0)
        update_from(0, t, True)

        def body(jj, c2):
            logits_to(0, jnp.minimum(2 * jj + 1, t - 1))
            update_from(1, 2 * jj, False)
            logits_to(1, jnp.minimum(2 * jj + 2, t - 1))

            @pl.when(2 * jj + 1 < t)
            def _():
                update_from(0, 2 * jj + 1, False)
            return c2

        lax.fori_loop(0, (t + 1) // 2, body, 0)
        rows = pl.ds(pl.multiple_of(t * tq, tq), tq)
        for h in heads:
            acc = acc_scr[h]
            o = acc[:V_DIM] * (1.0 / acc[V_DIM:V_DIM + 1])
            o_ref[0, rows, h * LANES:(h + 1) * LANES] = o.T.astype(o_ref.dtype)
        return c

    lax.fori_loop(0, n_q, q_tile, 0)


def _mla_attention(qt, k, vt):
    b, s, _ = k.shape
    n_k = s // MLA_TK
    nh = MLA_HEADS
    return pl.pallas_call(
        functools.partial(_mla_attn_kernel, n_q=s // MLA_TQ),
        grid=(b, C_HEADS // nh),
        in_specs=[pl.BlockSpec((1, nh, n_k, 256, MLA_TK), lambda bi, h: (bi, h, 0, 0, 0)),
                  pl.BlockSpec((1, s, 256 * nh), lambda bi, h: (bi, 0, h)),
                  pl.BlockSpec((1, nh, n_k, VT_ROWS, MLA_TK), lambda bi, h: (bi, h, 0, 0, 0))],
        out_specs=pl.BlockSpec((1, s, LANES * nh), lambda bi, h: (bi, 0, h)),
        out_shape=jax.ShapeDtypeStruct((b, s, C_HEADS * V_DIM), BF16),
        scratch_shapes=[pltpu.VMEM((nh, 2, MLA_TQ, MLA_TQ), F32),
                        pltpu.VMEM((nh, VT_ROWS, MLA_TQ), F32),
                        pltpu.VMEM((nh, 1, MLA_TQ), F32)],
        compiler_params=_params("parallel", "arbitrary"),
    )(qt, k, vt)


def _even_in_weight(w):
    cuts = np.cumsum([1024, 128, 128, 1024, 128, 128, 128, 128, 128, 128, 48])[:-1]
    qa, ka, va, qb, kc, vc, ks, vs, kw, vw, gate = jnp.split(w, [int(c) for c in cuts], axis=1)
    gate = jnp.pad(gate, ((0, 0), (0, LANES - gate.shape[1])))
    return jnp.concatenate([qa, qb, ka, va, kc, vc, ks, vs, kw, vw, gate], axis=1).astype(BF16)


def _compress_weights(pos, w1, w2):
    half = CMP_BLOCK // 2
    eye = jnp.eye(GROUPS, dtype=w1.dtype)
    w1r = w1.reshape(2, half, HEAD_DIM, CMP_HIDDEN)
    wab = jnp.einsum('rldh,gk->rlgdkh', w1r, eye).reshape(2, half * GROUPS * HEAD_DIM, GROUPS * CMP_HIDDEN)
    posr = jnp.broadcast_to(pos.reshape(2, half, 1, HEAD_DIM), (2, half, GROUPS, HEAD_DIM)).reshape(2, 1, -1)
    w2bd = jnp.einsum('hd,gk->ghkd', w2, eye).reshape(GROUPS * CMP_HIDDEN, GROUPS * HEAD_DIM)
    return posr[0], posr[1], wab[0].astype(BF16), wab[1].astype(BF16), w2bd.astype(BF16)


def _rope_chunk_cols(w):
    z = jnp.zeros(w.shape[:-1] + (ROPE_DIM // 2,), w.dtype)
    return jnp.concatenate([w[..., :ROPE_DIM // 2], z, w[..., ROPE_DIM // 2:], z], axis=-1)


def _mla_weights(w_in, w_q_up, w_kv_up):
    w_in2 = jnp.concatenate([w_in[:, :Q_LORA + KV_LORA], _rope_chunk_cols(w_in[:, Q_LORA + KV_LORA:])], axis=1)
    wq = w_q_up.reshape(Q_LORA, C_HEADS, NOPE_DIM + ROPE_DIM)
    wq2 = jnp.concatenate([wq[..., :NOPE_DIM], _rope_chunk_cols(wq[..., NOPE_DIM:])], axis=-1)
    wqt = wq2.transpose(1, 2, 0)
    wkv = w_kv_up.reshape(KV_LORA, C_HEADS, NOPE_DIM + V_DIM)
    wk = wkv[..., :NOPE_DIM].reshape(KV_LORA, C_HEADS * NOPE_DIM)
    wvt = wkv[..., NOPE_DIM:].transpose(1, 2, 0)
    return w_in2.astype(BF16), wqt.astype(BF16), wk.astype(BF16), wvt.astype(BF16)


def _rope_tables(s):
    inv = 1.0 / (ROPE_THETA ** (jnp.arange(0, ROPE_DIM, 2, dtype=F32) / ROPE_DIM))
    ang = jnp.arange(s, dtype=F32)[:, None] * inv[None]
    cos, sin = jnp.cos(ang), jnp.sin(ang)
    z = jnp.zeros_like(cos)
    return jnp.concatenate([cos, z, cos, z], axis=1), jnp.concatenate([-sin, z, sin, z], axis=1)


def _static_tables(s):
    k = np.arange(TQ)[:, None]
    q = np.arange(TQ)[None, :]
    diag = np.where(k <= q, _bucket_np(q - k), -1)
    prev = _bucket_np(q - k + TQ)
    near_swa = np.stack([diag, np.where(k > q, prev, -1)])
    near_nsa = np.stack([diag, prev])
    upper = np.where(k > q, 0.0, NEG).astype(np.float32)
    c = np.arange(LANES)[:, None]
    t = np.arange(s)[None, :]
    cdist = t - (c * CMP_STRIDE + CMP_BLOCK - 1)
    cmp_map = np.where(cdist >= 0, _bucket_np(cdist), -1)
    ns = s // SLC_BLOCK
    nc = (s - CMP_BLOCK) // CMP_STRIDE + 1
    c_start = np.arange(LANES) * CMP_STRIDE
    s_start = np.arange(ns) * SLC_BLOCK
    ovt = ((c_start[None, :] <= s_start[:, None] + SLC_BLOCK - 1) &
           (c_start[None, :] + CMP_BLOCK - 1 >= s_start[:, None]) &
           (np.arange(LANES)[None, :] < nc)).astype(np.float32)
    esel = (np.arange(s)[:, None] // SLC_BLOCK == np.arange(LANES)[None, :]).astype(np.float32)
    e3 = np.zeros((3, LANES, B_HEADS * HEAD_DIM), np.float32)
    for h in range(B_HEADS):
        for br in range(3):
            e3[br, h * 3 + br, h * HEAD_DIM:(h + 1) * HEAD_DIM] = 1.0
    return near_swa, near_nsa, upper, cmp_map, ovt, esel, e3


def _even_layer(x, b, s, rel_bias, norm, w_in, sinks, pos_k, pos_v, k_w1, k_w2, v_w1, v_w2, w_out):
    near_swa, near_nsa, upper, cmp_map, ovt, esel, e3 = _static_tables(s)
    ta = _bias_tiles(rel_bias, near_swa, 0, A_HEADS)
    tb = _bias_tiles(rel_bias, near_nsa, A_HEADS, B_HEADS)
    cbias = _bias_tiles(rel_bias, cmp_map, A_HEADS, B_HEADS)

    pe = _norm_matmul(x, norm, _even_in_weight(w_in), 512, EVEN_COLS, BF16).reshape(b, s, EVEN_COLS)

    rows = s // (CMP_BLOCK // 2)
    hkv = jnp.stack([pe[:, :, SEG_KC * LANES:(SEG_KC + 1) * LANES],
                     pe[:, :, SEG_VC * LANES:(SEG_VC + 1) * LANES]]).reshape(2, b, rows, -1)
    ck = _compress_weights(pos_k, k_w1, k_w2)
    cv = _compress_weights(pos_v, v_w1, v_w2)
    kvc = _compress(hkv, *[jnp.stack([a, c]) for a, c in zip(ck, cv)])
    assert rows <= TQ
    kvc = jnp.pad(kvc, ((0, 0), (0, 0), (0, TQ - rows), (0, 0)))

    oa, ob = _even_attention(pe, kvc, rel_bias, sinks, ta, tb, jnp.asarray(upper), cbias,
                             jnp.asarray(e3, BF16), jnp.asarray(esel, BF16), jnp.asarray(ovt, BF16))
    w_out = w_out.astype(BF16)
    n_a = A_HEADS * HEAD_DIM
    return _out_proj(x, [oa.reshape(b * s, -1), ob.reshape(b * s, -1)], [w_out[:n_a], w_out[n_a:]], 512)


def _odd_layer(x, b, s, norm, w_in, q_norm, w_q_up, kv_norm, w_kv_up, w_out):
    w_in2, wqt, wk, wvt = _mla_weights(w_in, w_q_up, w_kv_up)
    cos, sin = _rope_tables(s)
    qt, k, vt = _mla_prep(x, norm, w_in2, q_norm, kv_norm, wqt, wk, wvt, cos, sin, b, s)
    o = _mla_attention(qt, k.reshape(b, s, -1), vt)
    return _out_proj(x, [o.reshape(b * s, -1)], [w_out.astype(BF16)], 512)


def kernel(x, rel_bias, norm_mix_e, w_in_e, sinks, cmp_pos_k, cmp_pos_v, cmp_k_w1, cmp_k_w2, cmp_v_w1, cmp_v_w2, w_out_e, norm_mix_o, w_in_o, q_norm, w_q_up, kv_norm, w_kv_up, w_out_o, norm_mlp, w_up, w_down, norm_final):
    b, s, d = x.shape
    depth = norm_mlp.shape[0]
    h = x.reshape(b * s, d)
    for layer in range(depth):
        i = layer // 2
        if layer % 2 == 0:
            h = _even_layer(h, b, s, rel_bias, norm_mix_e[i], w_in_e[i], sinks[i], cmp_pos_k[i], cmp_pos_v[i],
                            cmp_k_w1[i], cmp_k_w2[i], cmp_v_w1[i], cmp_v_w2[i], w_out_e[i])
        else:
            h = _odd_layer(h, b, s, norm_mix_o[i], w_in_o[i], q_norm[i], w_q_up[i], kv_norm[i], w_kv_up[i],
                           w_out_o[i])
        h = _mlp(h, norm_mlp[layer], w_up, w_down, layer, norm_final, 1024, 512, layer == depth - 1)
    return h.reshape(b, s, d)
```

```python
import functools
import math

import numpy as np
import jax
import jax.numpy as jnp
from jax import lax
from jax.experimental import pallas as pl
from jax.experimental.pallas import tpu as pltpu

F32 = jnp.float32
BF16 = jnp.bfloat16

LANES = 128
VMEM_LIMIT_BYTES = 56 * 1024 * 1024

EPS = 1e-6
NEG = -1e30
BIG = 1e4
HEAD_DIM = 64
TQ = 128
MLA_TQ = 512
MLA_TK = 256
MLA_HEADS = 4
N_BUCKETS = 32
MAX_DISTANCE = 128
A_HEADS = 16
B_HEADS = 16
GROUPS = 2
PAIRS = 4
W4 = PAIRS * TQ
CMP_BLOCK = 32
CMP_STRIDE = 16
CMP_HIDDEN = 256
SLC_BLOCK = 64
SLC_TOPK = 8
A_WINDOW = 128
B_WINDOW = 512
C_HEADS = 16
Q_LORA = 768
KV_LORA = 512
NOPE_DIM = 128
ROPE_DIM = 64
V_DIM = 128
VT_ROWS = V_DIM + 16
LOG2E = math.log2(math.e)
ROPE_THETA = 10000.0

COL_QA, COL_QB = 0, 1024
SEG_KA, SEG_VA, SEG_KC, SEG_VC, SEG_KS, SEG_VS, SEG_KW, SEG_VW, SEG_GATE = range(16, 25)
EVEN_COLS = 25 * LANES
K_A, K_S, K_W = 0, 1, 2


def _params(*sem):
    return pltpu.CompilerParams(dimension_semantics=sem, vmem_limit_bytes=VMEM_LIMIT_BYTES)


def _rms(x, g):
    return x * lax.rsqrt(jnp.mean(x * x, axis=-1, keepdims=True) + EPS) * g


def _dot(a, b):
    return jnp.dot(a, b, preferred_element_type=F32)


def _bucket_np(dist):
    dist = np.maximum(dist, 0)
    max_exact = N_BUCKETS // 2
    d = np.maximum(dist, 1).astype(np.float32)
    large = max_exact + (np.log(d / np.float32(max_exact)) / np.float32(math.log(MAX_DISTANCE / max_exact))
                         * np.float32(N_BUCKETS - max_exact)).astype(np.int32)
    large = np.minimum(large, N_BUCKETS - 1)
    return np.where(dist < max_exact, dist, large).astype(np.int32)


def _bias_kernel(tab_ref, bm_ref, o_ref, *, head0):
    h = pl.program_id(0) + head0
    bm = bm_ref[...]
    acc = jnp.zeros(bm.shape, F32)
    for b in range(N_BUCKETS):
        acc = jnp.where(bm == b, tab_ref[b, h], acc)
    o_ref[0] = jnp.where(bm < 0, NEG, (acc - tab_ref[N_BUCKETS - 1, h]) * LOG2E)


def _bias_tiles(rel_bias, bucket_map, head0, n_heads):
    shp = bucket_map.shape
    nd = len(shp)
    return pl.pallas_call(
        functools.partial(_bias_kernel, head0=head0),
        grid=(n_heads,),
        in_specs=[pl.BlockSpec(memory_space=pltpu.SMEM),
                  pl.BlockSpec(shp, lambda h: (0,) * nd)],
        out_specs=pl.BlockSpec((1,) + shp, lambda h: (h,) + (0,) * nd),
        out_shape=jax.ShapeDtypeStruct((n_heads,) + shp, F32),
        compiler_params=_params("arbitrary"),
    )(rel_bias, jnp.asarray(bucket_map))


def _norm_matmul_kernel(x_ref, g_ref, w_ref, o_ref, xn_ref):
    @pl.when(pl.program_id(1) == 0)
    def _():
        xn_ref[...] = _rms(x_ref[...], g_ref[...]).astype(BF16)

    o_ref[...] = _dot(xn_ref[...], w_ref[...]).astype(o_ref.dtype)


def _norm_matmul(x, g, w, tm, tn, out_dtype):
    m, d = x.shape
    n = w.shape[1]
    return pl.pallas_call(
        _norm_matmul_kernel,
        grid=(m // tm, n // tn),
        in_specs=[pl.BlockSpec((tm, d), lambda i, j: (i, 0)),
                  pl.BlockSpec((1, d), lambda i, j: (0, 0)),
                  pl.BlockSpec((d, tn), lambda i, j: (0, j))],
        out_specs=pl.BlockSpec((tm, tn), lambda i, j: (i, j)),
        out_shape=jax.ShapeDtypeStruct((m, n), out_dtype),
        scratch_shapes=[pltpu.VMEM((tm, d), BF16)],
        compiler_params=_params("parallel", "arbitrary"),
    )(x, g.reshape(1, d), w)


def _out_proj_kernel(*refs, n_in):
    x_ref = refs[0]
    o_ref = refs[2 * n_in + 1]
    acc = x_ref[...]
    for t in range(n_in):
        acc = acc + _dot(refs[1 + t][...], refs[1 + n_in + t][...])
    o_ref[...] = acc


def _out_proj(x, acts, ws, tm):
    m, d = x.shape
    n_in = len(acts)
    in_specs = [pl.BlockSpec((tm, d), lambda i: (i, 0))]
    in_specs += [pl.BlockSpec((tm, a.shape[1]), lambda i: (i, 0)) for a in acts]
    in_specs += [pl.BlockSpec(w.shape, lambda i: (0, 0)) for w in ws]
    return pl.pallas_call(
        functools.partial(_out_proj_kernel, n_in=n_in),
        grid=(m // tm,),
        in_specs=in_specs,
        out_specs=pl.BlockSpec((tm, d), lambda i: (i, 0)),
        out_shape=jax.ShapeDtypeStruct((m, d), F32),
        compiler_params=_params("parallel"),
    )(x, *acts, *ws)


def _mlp_kernel(x_ref, g_ref, wu_ref, wd_ref, gf_ref, o_ref, xn_ref, *, final_norm):
    j = pl.program_id(1)

    @pl.when(j == 0)
    def _():
        x = x_ref[...]
        xn_ref[...] = _rms(x, g_ref[...]).astype(BF16)
        o_ref[...] = x

    h = _dot(xn_ref[...], wu_ref[0].astype(BF16))
    a = jnp.square(jnp.maximum(h, 0.0)).astype(BF16)
    o_ref[...] += _dot(a, wd_ref[0].astype(BF16))

    if final_norm:
        @pl.when(j == pl.num_programs(1) - 1)
        def _():
            o_ref[...] = _rms(o_ref[...], gf_ref[...])


def _mlp(x, g, w_up, w_down, layer, g_final, tm, tf, final_norm):
    m, d = x.shape
    ff = w_up.shape[2]
    return pl.pallas_call(
        functools.partial(_mlp_kernel, final_norm=final_norm),
        grid=(m // tm, ff // tf),
        in_specs=[pl.BlockSpec((tm, d), lambda i, j: (i, 0), pipeline_mode=pl.Buffered(1)),
                  pl.BlockSpec((1, d), lambda i, j: (0, 0)),
                  pl.BlockSpec((1, d, tf), lambda i, j: (layer, 0, j)),
                  pl.BlockSpec((1, tf, d), lambda i, j: (layer, j, 0)),
                  pl.BlockSpec((1, d), lambda i, j: (0, 0))],
        out_specs=pl.BlockSpec((tm, d), lambda i, j: (i, 0)),
        out_shape=jax.ShapeDtypeStruct((m, d), F32),
        scratch_shapes=[pltpu.VMEM((tm, d), BF16)],
        compiler_params=_params("parallel", "arbitrary"),
    )(x, g.reshape(1, d), w_up, w_down, g_final.reshape(1, d))


def _gelu_tanh(x):
    return 0.5 * x * (1.0 + jnp.tanh(math.sqrt(2.0 / math.pi) * (x + 0.044715 * (x * x * x))))


def _compress_kernel(h_ref, pa_ref, pb_ref, wa_ref, wb_ref, w2_ref, o_ref):
    h = h_ref[0, 0].astype(F32)
    ha = _dot((h + pa_ref[0]).astype(BF16), wa_ref[0])
    hb = _dot((h + pb_ref[0]).astype(BF16), wb_ref[0])
    n = hb.shape[0]
    pre = ha + pltpu.roll(hb, n - 1, 0)
    o_ref[0, 0] = _dot(_gelu_tanh(pre).astype(BF16), w2_ref[0]).astype(o_ref.dtype)


def _compress(hkv, pa, pb, wa, wb, w2):
    _, b, r, w = hkv.shape
    hid2 = wa.shape[2]
    return pl.pallas_call(
        _compress_kernel,
        grid=(2, b),
        in_specs=[pl.BlockSpec((1, 1, r, w), lambda t, i: (t, i, 0, 0)),
                  pl.BlockSpec((1, 1, w), lambda t, i: (t, 0, 0)),
                  pl.BlockSpec((1, 1, w), lambda t, i: (t, 0, 0)),
                  pl.BlockSpec((1, w, hid2), lambda t, i: (t, 0, 0)),
                  pl.BlockSpec((1, w, hid2), lambda t, i: (t, 0, 0)),
                  pl.BlockSpec((1, hid2, LANES), lambda t, i: (t, 0, 0))],
        out_specs=pl.BlockSpec((1, 1, r, LANES), lambda t, i: (t, i, 0, 0)),
        out_shape=jax.ShapeDtypeStruct((2, b, r, LANES), BF16),
        compiler_params=_params("arbitrary", "arbitrary"),
    )(hkv, pa, pb, wa, wb, w2)


SWA, SLC, WIN = 0, 1, 2


def _stack_pairs_t(q_ref, g):
    cols = [q_ref[0, :, (PAIRS * g + p) * LANES:(PAIRS * g + p + 1) * LANES].astype(F32).T for p in range(PAIRS)]
    return (jnp.concatenate(cols, axis=1) * (HEAD_DIM ** -0.5 * LOG2E)).astype(BF16)


def _rows_by_half(a, b, rows):
    parts = [jnp.broadcast_to(a, (HEAD_DIM, W4)), jnp.broadcast_to(b, (HEAD_DIM, W4))]
    if rows > 2 * HEAD_DIM:
        first = lax.broadcasted_iota(jnp.int32, (rows - 2 * HEAD_DIM, W4), 0) == 0
        parts.append(jnp.where(first, a, b))
    return jnp.concatenate(parts, axis=0)


def _even_attn_kernel(tab_ref, sink_ref, qa_ref, qb_ref, gate_ref,
                      ka_ref, va_ref, ks_ref, vs_ref, kw_ref, vw_ref, kc_ref, vc_ref,
                      ta_ref, tb_ref, up_ref, cb_ref, e3_ref, esel_ref, ovt_ref,
                      oa_ref, ob_ref,
                      kbd_ref, vbd_ref, ckbd_ref, cvbd_ref, msk_ref, acc_scr, m_scr, *, n_tiles):
    i = pl.program_id(1)
    groups = range(GROUPS)

    def block_diag(x, g):
        r = pltpu.roll(x, HEAD_DIM, 1)
        lo = lax.broadcasted_iota(jnp.int32, x.shape, 1) < HEAD_DIM
        own, other = (x, r) if g == 0 else (r, x)
        return jnp.where(lo, own, 0.0), jnp.where(lo, 0.0, other)

    @pl.when(i == 0)
    def _build():
        r16 = lax.broadcasted_iota(jnp.int32, (VT_ROWS - LANES, 2 * TQ), 0)
        c16 = lax.broadcasted_iota(jnp.int32, (VT_ROWS - LANES, 2 * TQ), 1)
        ones_rows = jnp.where((r16 == 0) & (c16 < TQ) | (r16 == 1) & (c16 >= TQ), 1.0, 0.0).astype(BF16)
        for t, ref in enumerate((ka_ref, ks_ref, kw_ref)):
            def kbody(j, c, t=t, ref=ref):
                x = ref[0, pl.ds(pl.multiple_of(j * TQ, TQ), TQ), :].astype(F32)
                for g in groups:
                    top, bot = block_diag(x, g)
                    kbd_ref[g, t, j, 0:TQ, :] = top.astype(BF16)
                    kbd_ref[g, t, j, TQ:2 * TQ, :] = bot.astype(BF16)
                return c
            lax.fori_loop(0, n_tiles, kbody, 0)
        for t, ref in enumerate((va_ref, vs_ref, vw_ref)):
            def vbody(j, c, t=t, ref=ref):
                x = ref[0, pl.ds(pl.multiple_of(j * TQ, TQ), TQ), :].astype(F32)
                for g in groups:
                    top, bot = block_diag(x, g)
                    vbd_ref[g, t, j, 0:LANES, 0:TQ] = top.T.astype(BF16)
                    vbd_ref[g, t, j, 0:LANES, TQ:2 * TQ] = bot.T.astype(BF16)
                    vbd_ref[g, t, j, LANES:, :] = ones_rows
                return c
            lax.fori_loop(0, n_tiles, vbody, 0)
        xk = kc_ref[0, 0].astype(F32)
        xv = vc_ref[0, 0].astype(F32)
        for g in groups:
            top, bot = block_diag(xk, g)
            ckbd_ref[g, 0:TQ, :] = top.astype(BF16)
            ckbd_ref[g, TQ:2 * TQ, :] = bot.astype(BF16)
            top, bot = block_diag(xv, g)
            cvbd_ref[g, :, 0:TQ] = top.T.astype(BF16)
            cvbd_ref[g, :, TQ:2 * TQ] = bot.T.astype(BF16)

    def near_bias(t_ref, g, delta):
        return lambda hf, p: t_ref[8 * g + 2 * p + hf, delta]

    def update(g, st, s, adds, vbd_t):
        m_old = [m_scr[g, st, hf] for hf in range(2)]
        p_rows, m_new = [], []
        for hf in range(2):
            strips, mns = [], []
            for p in range(PAIRS):
                sh = s[hf * TQ:(hf + 1) * TQ, p * TQ:(p + 1) * TQ]
                for a in adds:
                    sh = sh + a(hf, p)
                mn = jnp.maximum(m_old[hf][:, p * TQ:(p + 1) * TQ], jnp.max(sh, axis=0, keepdims=True))
                strips.append(jnp.exp2(sh - mn).astype(BF16))
                mns.append(mn)
            p_rows.append(jnp.concatenate(strips, axis=1))
            m_new.append(jnp.concatenate(mns, axis=1))
            m_scr[g, st, hf] = m_new[hf]
        alpha = _rows_by_half(jnp.exp2(m_old[0] - m_new[0]), jnp.exp2(m_old[1] - m_new[1]), VT_ROWS)
        acc_scr[g, st] = alpha * acc_scr[g, st] + _dot(vbd_t, jnp.concatenate(p_rows, axis=0))

    def run_pipelined(tasks):
        s = tasks[0][0]()
        for n, (_, apply_fn) in enumerate(tasks):
            s_next = tasks[n + 1][0]() if n + 1 < len(tasks) else None
            apply_fn(s)
            s = s_next

    def tile_task(g, st, slot, j, q4t, adds):
        return (lambda: _dot(kbd_ref[g, slot, j], q4t),
                lambda s: update(g, st, s, adds, vbd_ref[g, slot, j]))

    def finish(g, st, extra=None):
        acc = acc_scr[g, st]
        den = [acc[LANES + hf:LANES + hf + 1] for hf in range(2)]
        if extra is not None:
            den = [den[hf] + extra(hf, m_scr[g, st, hf]) for hf in range(2)]
        return acc[:LANES] * _rows_by_half(1.0 / den[0], 1.0 / den[1], LANES)

    m_scr[...] = jnp.full(m_scr.shape, NEG, F32)
    acc_scr[...] = jnp.zeros(acc_scr.shape, F32)
    q4a = [_stack_pairs_t(qa_ref, g) for g in groups]
    q4b = [_stack_pairs_t(qb_ref, g) for g in groups]

    run_pipelined([tile_task(g, SWA, K_A, i, q4a[g], [near_bias(ta_ref, g, 0)]) for g in groups] +
                  [tile_task(g, WIN, K_W, i, q4b[g], [near_bias(tb_ref, g, 0)]) for g in groups])

    o_cmp = []
    t_row = i * TQ + (lax.broadcasted_iota(jnp.int32, (1, W4), 1) & (TQ - 1))
    anyvis = jnp.where(t_row >= CMP_BLOCK - 1, 1.0, 0.0)
    for g in groups:
        s = _dot(ckbd_ref[g], q4b[g])
        psum = jnp.zeros((TQ, TQ), F32)
        ps = []
        for hf in range(2):
            cb = jnp.concatenate([cb_ref[8 * g + 2 * p + hf] for p in range(PAIRS)], axis=1)
            sh = s[hf * TQ:(hf + 1) * TQ] + cb
            pe = jnp.exp2(sh - jnp.max(sh, axis=0, keepdims=True))
            pc = pe * (anyvis / jnp.sum(pe, axis=0, keepdims=True))
            for p in range(PAIRS):
                psum = psum + pc[:, p * TQ:(p + 1) * TQ]
            ps.append(pc.astype(BF16))
        o_cmp.append(_dot(cvbd_ref[g], jnp.concatenate(ps, axis=0)))

        ph = psum.astype(BF16)
        r1 = psum - ph.astype(F32)
        pm = r1.astype(BF16)
        pl_ = (r1 - pm.astype(F32)).astype(BF16)
        ovt = ovt_ref[...]
        pslc = _dot(ovt, ph) + _dot(ovt, pm) + _dot(ovt, pl_)
        ns = pslc.shape[0]
        nb = lax.broadcasted_iota(jnp.int32, (ns, TQ), 0)
        tq = i * TQ + lax.broadcasted_iota(jnp.int32, (ns, TQ), 1)
        cur = tq // SLC_BLOCK
        forced = (jnp.where(nb == 0, 1.0, 0.0) + jnp.where(nb == cur, 1.0, 0.0) +
                  jnp.where(nb == cur - 1, 1.0, 0.0))
        score = jnp.where(nb * SLC_BLOCK > tq, NEG, jnp.where(forced > 0.0, BIG, pslc))
        rank = jnp.zeros((ns, TQ), F32)
        for mth in range(ns):
            sm = score[mth:mth + 1, :]
            tie = jnp.where(nb > mth, 1.0, 0.0)
            rank = rank + jnp.where(sm > score, 1.0, jnp.where(sm == score, tie, 0.0))
        sel = jnp.where(rank < SLC_TOPK, 1.0, 0.0)
        sel = jnp.concatenate([sel, jnp.zeros((LANES - ns, TQ), F32)], axis=0).astype(BF16)
        msk_ref[g] = ((_dot(esel_ref[...], sel) - 1.0) * -NEG).reshape(n_tiles, TQ, TQ)

    def sel_add(g, j):
        return lambda hf, p: msk_ref[g, j]

    n_full = B_WINDOW // TQ

    def near_tiles(n_back):
        tasks = [tile_task(g, SLC, K_S, i, q4b[g], [near_bias(tb_ref, g, 0), sel_add(g, i)]) for g in groups]
        if n_back >= 1:
            j = i - 1
            tasks += [tile_task(g, SWA, K_A, j, q4a[g], [near_bias(ta_ref, g, 1)]) for g in groups]
            tasks += [tile_task(g, WIN, K_W, j, q4b[g], [near_bias(tb_ref, g, 1)]) for g in groups]
            tasks += [tile_task(g, SLC, K_S, j, q4b[g], [near_bias(tb_ref, g, 1), sel_add(g, j)]) for g in groups]
        for d in range(2, n_back + 1):
            edge = [lambda hf, p: up_ref[...]] if d == n_full else []
            tasks += [tile_task(g, WIN, K_W, i - d, q4b[g], edge) for g in groups]
        run_pipelined(tasks)

    for c in range(n_full):
        pl.when(i == c)(functools.partial(near_tiles, c))
    pl.when(i >= n_full)(functools.partial(near_tiles, n_full))

    def far_tasks(d):
        return [tile_task(g, SLC, K_S, i - d, q4b[g], [sel_add(g, i - d)]) for g in groups]

    def far_body(jj, c):
        d = 2 + 2 * jj
        run_pipelined(far_tasks(d) + far_tasks(d + 1))
        return c

    n_far = jnp.maximum(i - 1, 0)
    lax.fori_loop(0, n_far // 2, far_body, 0)
    pl.when(n_far % 2 == 1)(lambda: run_pipelined(far_tasks(i)))

    def to_rows(o_t, p):
        return o_t[:, p * TQ:(p + 1) * TQ].T

    far_of = lambda h: tab_ref[N_BUCKETS - 1, h]
    sg = jax.nn.sigmoid(gate_ref[0].astype(F32))
    sg_hi = sg.astype(BF16)
    sg_lo = (sg - sg_hi.astype(F32)).astype(BF16)
    for g in groups:
        def sink_term(hf, m, g=g):
            row = jnp.concatenate([jnp.full((1, TQ), (sink_ref[8 * g + 2 * p + hf] - far_of(8 * g + 2 * p + hf)) * LOG2E,
                                            F32) for p in range(PAIRS)], axis=1)
            return jnp.exp2(row - m)

        o_swa = finish(g, SWA, sink_term)
        o_slc = finish(g, SLC)
        o_win = finish(g, WIN)
        for p in range(PAIRS):
            cols = slice((PAIRS * g + p) * LANES, (PAIRS * g + p + 1) * LANES)
            oa_ref[0, :, cols] = to_rows(o_swa, p).astype(oa_ref.dtype)
            gts = [_dot(sg_hi, e3_ref[br, :, cols]) + _dot(sg_lo, e3_ref[br, :, cols]) for br in range(3)]
            o = gts[0] * to_rows(o_cmp[g], p) + gts[1] * to_rows(o_slc, p) + gts[2] * to_rows(o_win, p)
            ob_ref[0, :, cols] = o.astype(ob_ref.dtype)


def _even_attention(pe, kvc, rel_bias, sinks, ta, tb, up, cbias, e3, esel, ovt):
    b, s, _ = pe.shape
    n_tiles = s // TQ
    width = GROUPS * PAIRS * LANES

    def seg(k):
        return pl.BlockSpec((1, s, LANES), lambda bi, i, k=k: (bi, 0, k))

    full = lambda a: pl.BlockSpec(a.shape, lambda bi, i: (0,) * a.ndim)
    in_specs = [
        pl.BlockSpec(memory_space=pltpu.SMEM),
        pl.BlockSpec(memory_space=pltpu.SMEM),
        pl.BlockSpec((1, TQ, width), lambda bi, i: (bi, i, COL_QA // width)),
        pl.BlockSpec((1, TQ, width), lambda bi, i: (bi, i, COL_QB // width)),
        pl.BlockSpec((1, TQ, LANES), lambda bi, i: (bi, i, SEG_GATE)),
        seg(SEG_KA), seg(SEG_VA), seg(SEG_KS), seg(SEG_VS), seg(SEG_KW), seg(SEG_VW),
        pl.BlockSpec((1, 1, TQ, LANES), lambda bi, i: (0, bi, 0, 0)),
        pl.BlockSpec((1, 1, TQ, LANES), lambda bi, i: (1, bi, 0, 0)),
        full(ta), full(tb), full(up),
        pl.BlockSpec((B_HEADS, TQ, TQ), lambda bi, i: (0, 0, i)),
        full(e3), full(esel), full(ovt),
    ]
    out_spec = pl.BlockSpec((1, TQ, width), lambda bi, i: (bi, i, 0))
    return pl.pallas_call(
        functools.partial(_even_attn_kernel, n_tiles=n_tiles),
        grid=(b, n_tiles),
        in_specs=in_specs,
        out_specs=[out_spec, out_spec],
        out_shape=[jax.ShapeDtypeStruct((b, s, width), BF16)] * 2,
        scratch_shapes=[pltpu.VMEM((GROUPS, 3, n_tiles, 2 * TQ, LANES), BF16),
                        pltpu.VMEM((GROUPS, 3, n_tiles, VT_ROWS, 2 * TQ), BF16),
                        pltpu.VMEM((GROUPS, 2 * TQ, LANES), BF16),
                        pltpu.VMEM((GROUPS, LANES, 2 * TQ), BF16),
                        pltpu.VMEM((GROUPS, n_tiles, TQ, TQ), F32),
                        pltpu.VMEM((GROUPS, 3, VT_ROWS, W4), F32),
                        pltpu.VMEM((GROUPS, 3, 2, 1, W4), F32)],
        compiler_params=_params("arbitrary", "arbitrary"),
    )(rel_bias, sinks, pe, pe, pe, pe, pe, pe, pe, pe, pe, kvc, kvc, ta, tb, up, cbias, e3, esel, ovt)


def _mla_prep_kernel(x_ref, g_ref, win_ref, qn_ref, kvn_ref, wqt_ref, wk_ref, wvt_ref,
                     cos_ref, sin_ref, cost_ref, sint_ref, qt_ref, k_ref, vt_ref):
    xn = _rms(x_ref[...], g_ref[...]).astype(BF16)
    proj = _dot(xn, win_ref[...])
    cq = _rms(proj[:, :Q_LORA], qn_ref[...])
    ckv = _rms(proj[:, Q_LORA:Q_LORA + KV_LORA], kvn_ref[...])
    cq_t = cq.T.astype(BF16)
    ckv_t = ckv.T.astype(BF16)
    ckv = ckv.astype(BF16)
    kr = proj[:, Q_LORA + KV_LORA:]
    kr = (kr * cos_ref[...] + pltpu.roll(kr, HEAD_DIM, 1) * sin_ref[...]).astype(BF16)
    cos_t = cost_ref[...]
    sin_t = sint_ref[...]
    scale = (NOPE_DIM + ROPE_DIM) ** -0.5 * LOG2E
    tm = ckv_t.shape[1]
    ones_rows = jnp.where(lax.broadcasted_iota(jnp.int32, (VT_ROWS - V_DIM, tm), 0) == 0, 1.0, 0.0).astype(BF16)
    for h in range(C_HEADS):
        q_t = _dot(wqt_ref[h], cq_t)
        rp = q_t[NOPE_DIM:]
        rp = rp * cos_t + pltpu.roll(rp, HEAD_DIM, 0) * sin_t
        qt_ref[0, h, 0, 0:NOPE_DIM, :] = (q_t[:NOPE_DIM] * scale).astype(BF16)
        qt_ref[0, h, 0, NOPE_DIM:, :] = (rp * scale).astype(BF16)
        if h % 2 == 0:
            k2 = _dot(ckv, wk_ref[:, h * LANES:(h + 2) * LANES]).astype(BF16)
        k_ref[:, h * 256:h * 256 + LANES] = k2[:, (h % 2) * LANES:(h % 2 + 1) * LANES]
        k_ref[:, h * 256 + LANES:(h + 1) * 256] = kr
        vt_ref[0, h, 0, 0:V_DIM, :] = _dot(wvt_ref[h], ckv_t).astype(BF16)
        vt_ref[0, h, 0, V_DIM:, :] = ones_rows


def _mla_prep(x, g, w_in, qn, kvn, wqt, wk, wvt, cos, sin, b, s):
    m, d = x.shape
    tm = MLA_TK
    pos_tiles = s // tm
    full = lambda a: pl.BlockSpec(a.shape, lambda i: (0,) * a.ndim)
    tile_t = lambda rows: pl.BlockSpec((1, C_HEADS, 1, rows, tm), lambda i: (i // pos_tiles, 0, i % pos_tiles, 0, 0))
    return pl.pallas_call(
        _mla_prep_kernel,
        grid=(m // tm,),
        in_specs=[pl.BlockSpec((tm, d), lambda i: (i, 0)),
                  pl.BlockSpec((1, d), lambda i: (0, 0)),
                  full(w_in), pl.BlockSpec((1, Q_LORA), lambda i: (0, 0)),
                  pl.BlockSpec((1, KV_LORA), lambda i: (0, 0)), full(wqt), full(wk), full(wvt),
                  pl.BlockSpec((tm, LANES), lambda i: (i % pos_tiles, 0)),
                  pl.BlockSpec((tm, LANES), lambda i: (i % pos_tiles, 0)),
                  pl.BlockSpec((LANES, tm), lambda i: (0, i % pos_tiles)),
                  pl.BlockSpec((LANES, tm), lambda i: (0, i % pos_tiles))],
        out_specs=[tile_t(256),
                   pl.BlockSpec((tm, C_HEADS * 256), lambda i: (i, 0)),
                   tile_t(VT_ROWS)],
        out_shape=[jax.ShapeDtypeStruct((b, C_HEADS, pos_tiles, 256, tm), BF16),
                   jax.ShapeDtypeStruct((m, C_HEADS * 256), BF16),
                   jax.ShapeDtypeStruct((b, C_HEADS, pos_tiles, VT_ROWS, tm), BF16)],
        compiler_params=_params("parallel"),
    )(x, g.reshape(1, d), w_in, qn.reshape(1, -1), kvn.reshape(1, -1), wqt, wk, wvt, cos, sin, cos.T, sin.T)


def _mla_attn_kernel(qt_ref, k_ref, vt_ref, o_ref, s_scr, acc_scr, m_scr, *, n_q):
    tq = tk = MLA_TQ
    parts = tq // MLA_TK
    ki = lax.broadcasted_iota(jnp.int32, (tk, tq), 0)
    qi = lax.broadcasted_iota(jnp.int32, (tk, tq), 1)
    causal = ki <= qi
    heads = range(MLA_HEADS)

    def q_tile(t, c):
        q_ts = [jnp.concatenate([qt_ref[0, h, parts * t + d] for d in range(parts)], axis=1) for h in heads]

        def logits_to(slot, j):
            rows = pl.ds(pl.multiple_of(j * tk, tk), tk)
            for h in heads:
                s_scr[h, slot] = _dot(k_ref[0, rows, h * 256:(h + 1) * 256], q_ts[h])

        def update_from(slot, j, diagonal):
            for h in heads:
                s = s_scr[h, slot]
                if diagonal:
                    s = jnp.where(causal, s, NEG)
                m = m_scr[h]
                mn = jnp.maximum(m, jnp.max(s, axis=0, keepdims=True))
                p = jnp.exp2(s - mn).astype(BF16)
                m_scr[h] = mn
                v_t = jnp.concatenate([vt_ref[0, h, parts * j + d] for d in range(parts)], axis=1)
                acc_scr[h] = jnp.exp2(m - mn) * acc_scr[h] + _dot(v_t, p)

        m_scr[...] = jnp.full(m_scr.shape, NEG, F32)
        acc_scr[...] = jnp.zeros(acc_scr.shape, F32)
        logits_to(0, t)
        logits_to(1, 0)
        update_from(0, t, True)

        def body(jj, c2):
            logits_to(0, jnp.minimum(2 * jj + 1, t - 1))
            update_from(1, 2 * jj, False)
            logits_to(1, jnp.minimum(2 * jj + 2, t - 1))

            @pl.when(2 * jj + 1 < t)
            def _():
                update_from(0, 2 * jj + 1, False)
            return c2

        lax.fori_loop(0, (t + 1) // 2, body, 0)
        rows = pl.ds(pl.multiple_of(t * tq, tq), tq)
        for h in heads:
            acc = acc_scr[h]
            o = acc[:V_DIM] * (1.0 / acc[V_DIM:V_DIM + 1])
            o_ref[0, rows, h * LANES:(h + 1) * LANES] = o.T.astype(o_ref.dtype)
        return c

    lax.fori_loop(0, n_q, q_tile, 0)


def _mla_attention(qt, k, vt):
    b, s, _ = k.shape
    n_k = s // MLA_TK
    nh = MLA_HEADS
    return pl.pallas_call(
        functools.partial(_mla_attn_kernel, n_q=s // MLA_TQ),
        grid=(b, C_HEADS // nh),
        in_specs=[pl.BlockSpec((1, nh, n_k, 256, MLA_TK), lambda bi, h: (bi, h, 0, 0, 0)),
                  pl.BlockSpec((1, s, 256 * nh), lambda bi, h: (bi, 0, h)),
                  pl.BlockSpec((1, nh, n_k, VT_ROWS, MLA_TK), lambda bi, h: (bi, h, 0, 0, 0))],
        out_specs=pl.BlockSpec((1, s, LANES * nh), lambda bi, h: (bi, 0, h)),
        out_shape=jax.ShapeDtypeStruct((b, s, C_HEADS * V_DIM), BF16),
        scratch_shapes=[pltpu.VMEM((nh, 2, MLA_TQ, MLA_TQ), F32),
                        pltpu.VMEM((nh, VT_ROWS, MLA_TQ), F32),
                        pltpu.VMEM((nh, 1, MLA_TQ), F32)],
        compiler_params=_params("parallel", "arbitrary"),
    )(qt, k, vt)


def _even_in_weight(w):
    cuts = np.cumsum([1024, 128, 128, 1024, 128, 128, 128, 128, 128, 128, 48])[:-1]
    qa, ka, va, qb, kc, vc, ks, vs, kw, vw, gate = jnp.split(w, [int(c) for c in cuts], axis=1)
    gate = jnp.pad(gate, ((0, 0), (0, LANES - gate.shape[1])))
    return jnp.concatenate([qa, qb, ka, va, kc, vc, ks, vs, kw, vw, gate], axis=1).astype(BF16)


def _compress_weights(pos, w1, w2):
    half = CMP_BLOCK // 2
    eye = jnp.eye(GROUPS, dtype=w1.dtype)
    w1r = w1.reshape(2, half, HEAD_DIM, CMP_HIDDEN)
    wab = jnp.einsum('rldh,gk->rlgdkh', w1r, eye).reshape(2, half * GROUPS * HEAD_DIM, GROUPS * CMP_HIDDEN)
    posr = jnp.broadcast_to(pos.reshape(2, half, 1, HEAD_DIM), (2, half, GROUPS, HEAD_DIM)).reshape(2, 1, -1)
    w2bd = jnp.einsum('hd,gk->ghkd', w2, eye).reshape(GROUPS * CMP_HIDDEN, GROUPS * HEAD_DIM)
    return posr[0], posr[1], wab[0].astype(BF16), wab[1].astype(BF16), w2bd.astype(BF16)


def _rope_chunk_cols(w):
    z = jnp.zeros(w.shape[:-1] + (ROPE_DIM // 2,), w.dtype)
    return jnp.concatenate([w[..., :ROPE_DIM // 2], z, w[..., ROPE_DIM // 2:], z], axis=-1)


def _mla_weights(w_in, w_q_up, w_kv_up):
    w_in2 = jnp.concatenate([w_in[:, :Q_LORA + KV_LORA], _rope_chunk_cols(w_in[:, Q_LORA + KV_LORA:])], axis=1)
    wq = w_q_up.reshape(Q_LORA, C_HEADS, NOPE_DIM + ROPE_DIM)
    wq2 = jnp.concatenate([wq[..., :NOPE_DIM], _rope_chunk_cols(wq[..., NOPE_DIM:])], axis=-1)
    wqt = wq2.transpose(1, 2, 0)
    wkv = w_kv_up.reshape(KV_LORA, C_HEADS, NOPE_DIM + V_DIM)
    wk = wkv[..., :NOPE_DIM].reshape(KV_LORA, C_HEADS * NOPE_DIM)
    wvt = wkv[..., NOPE_DIM:].transpose(1, 2, 0)
    return w_in2.astype(BF16), wqt.astype(BF16), wk.astype(BF16), wvt.astype(BF16)


def _rope_tables(s):
    inv = 1.0 / (ROPE_THETA ** (jnp.arange(0, ROPE_DIM, 2, dtype=F32) / ROPE_DIM))
    ang = jnp.arange(s, dtype=F32)[:, None] * inv[None]
    cos, sin = jnp.cos(ang), jnp.sin(ang)
    z = jnp.zeros_like(cos)
    return jnp.concatenate([cos, z, cos, z], axis=1), jnp.concatenate([-sin, z, sin, z], axis=1)


def _static_tables(s):
    k = np.arange(TQ)[:, None]
    q = np.arange(TQ)[None, :]
    diag = np.where(k <= q, _bucket_np(q - k), -1)
    prev = _bucket_np(q - k + TQ)
    near_swa = np.stack([diag, np.where(k > q, prev, -1)])
    near_nsa = np.stack([diag, prev])
    upper = np.where(k > q, 0.0, NEG).astype(np.float32)
    c = np.arange(LANES)[:, None]
    t = np.arange(s)[None, :]
    cdist = t - (c * CMP_STRIDE + CMP_BLOCK - 1)
    cmp_map = np.where(cdist >= 0, _bucket_np(cdist), -1)
    ns = s // SLC_BLOCK
    nc = (s - CMP_BLOCK) // CMP_STRIDE + 1
    c_start = np.arange(LANES) * CMP_STRIDE
    s_start = np.arange(ns) * SLC_BLOCK
    ovt = ((c_start[None, :] <= s_start[:, None] + SLC_BLOCK - 1) &
           (c_start[None, :] + CMP_BLOCK - 1 >= s_start[:, None]) &
           (np.arange(LANES)[None, :] < nc)).astype(np.float32)
    esel = (np.arange(s)[:, None] // SLC_BLOCK == np.arange(LANES)[None, :]).astype(np.float32)
    e3 = np.zeros((3, LANES, B_HEADS * HEAD_DIM), np.float32)
    for h in range(B_HEADS):
        for br in range(3):
            e3[br, h * 3 + br, h * HEAD_DIM:(h + 1) * HEAD_DIM] = 1.0
    return near_swa, near_nsa, upper, cmp_map, ovt, esel, e3


def _even_layer(x, b, s, rel_bias, norm, w_in, sinks, pos_k, pos_v, k_w1, k_w2, v_w1, v_w2, w_out):
    near_swa, near_nsa, upper, cmp_map, ovt, esel, e3 = _static_tables(s)
    ta = _bias_tiles(rel_bias, near_swa, 0, A_HEADS)
    tb = _bias_tiles(rel_bias, near_nsa, A_HEADS, B_HEADS)
    cbias = _bias_tiles(rel_bias, cmp_map, A_HEADS, B_HEADS)

    pe = _norm_matmul(x, norm, _even_in_weight(w_in), 512, EVEN_COLS, BF16).reshape(b, s, EVEN_COLS)

    rows = s // (CMP_BLOCK // 2)
    hkv = jnp.stack([pe[:, :, SEG_KC * LANES:(SEG_KC + 1) * LANES],
                     pe[:, :, SEG_VC * LANES:(SEG_VC + 1) * LANES]]).reshape(2, b, rows, -1)
    ck = _compress_weights(pos_k, k_w1, k_w2)
    cv = _compress_weights(pos_v, v_w1, v_w2)
    kvc = _compress(hkv, *[jnp.stack([a, c]) for a, c in zip(ck, cv)])
    assert rows <= TQ
    kvc = jnp.pad(kvc, ((0, 0), (0, 0), (0, TQ - rows), (0, 0)))

    oa, ob = _even_attention(pe, kvc, rel_bias, sinks, ta, tb, jnp.asarray(upper), cbias,
                             jnp.asarray(e3, BF16), jnp.asarray(esel, BF16), jnp.asarray(ovt, BF16))
    w_out = w_out.astype(BF16)
    n_a = A_HEADS * HEAD_DIM
    return _out_proj(x, [oa.reshape(b * s, -1), ob.reshape(b * s, -1)], [w_out[:n_a], w_out[n_a:]], 512)


def _odd_layer(x, b, s, norm, w_in, q_norm, w_q_up, kv_norm, w_kv_up, w_out):
    w_in2, wqt, wk, wvt = _mla_weights(w_in, w_q_up, w_kv_up)
    cos, sin = _rope_tables(s)
    qt, k, vt = _mla_prep(x, norm, w_in2, q_norm, kv_norm, wqt, wk, wvt, cos, sin, b, s)
    o = _mla_attention(qt, k.reshape(b, s, -1), vt)
    return _out_proj(x, [o.reshape(b * s, -1)], [w_out.astype(BF16)], 512)


def kernel(x, rel_bias, norm_mix_e, w_in_e, sinks, cmp_pos_k, cmp_pos_v, cmp_k_w1, cmp_k_w2, cmp_v_w1, cmp_v_w2, w_out_e, norm_mix_o, w_in_o, q_norm, w_q_up, kv_norm, w_kv_up, w_out_o, norm_mlp, w_up, w_down, norm_final):
    b, s, d = x.shape
    depth = norm_mlp.shape[0]
    h = x.reshape(b * s, d)
    for layer in range(depth):
        i = layer // 2
        if layer % 2 == 0:
            h = _even_layer(h, b, s, rel_bias, norm_mix_e[i], w_in_e[i], sinks[i], cmp_pos_k[i], cmp_pos_v[i],
                            cmp_k_w1[i], cmp_k_w2[i], cmp_v_w1[i], cmp_v_w2[i], w_out_e[i])
        else:
            h = _odd_layer(h, b, s, norm_mix_o[i], w_in_o[i], q_norm[i], w_q_up[i], kv_norm[i], w_kv_up[i],
                           w_out_o[i])
        h = _mlp(h, norm_mlp[layer], w_up, w_down, layer, norm_final, 1024, 512, layer == depth - 1)
    return h.reshape(b, s, d)
```

```python
import functools
import math

import numpy as np
import jax
import jax.numpy as jnp
from jax import lax
from jax.experimental import pallas as pl
from jax.experimental.pallas import tpu as pltpu

F32 = jnp.float32
BF16 = jnp.bfloat16

LANES = 128
VMEM_LIMIT_BYTES = 56 * 1024 * 1024

EPS = 1e-6
NEG = -1e30
BIG = 1e4
HEAD_DIM = 64
TQ = 128
MLA_TQ = 512
MLA_TK = 256
MLA_HEADS = 4
N_BUCKETS = 32
MAX_DISTANCE = 128
A_HEADS = 16
B_HEADS = 16
GROUPS = 2
PAIRS = 4
W4 = PAIRS * TQ
CMP_BLOCK = 32
CMP_STRIDE = 16
CMP_HIDDEN = 256
SLC_BLOCK = 64
SLC_TOPK = 8
A_WINDOW = 128
B_WINDOW = 512
C_HEADS = 16
Q_LORA = 768
KV_LORA = 512
NOPE_DIM = 128
ROPE_DIM = 64
V_DIM = 128
VT_ROWS = V_DIM + 16
LOG2E = math.log2(math.e)
ROPE_THETA = 10000.0

COL_QA, COL_QB = 0, 1024
SEG_KA, SEG_VA, SEG_KC, SEG_VC, SEG_KS, SEG_VS, SEG_KW, SEG_VW, SEG_GATE = range(16, 25)
EVEN_COLS = 25 * LANES
K_A, K_S, K_W = 0, 1, 2


def _params(*sem):
    return pltpu.CompilerParams(dimension_semantics=sem, vmem_limit_bytes=VMEM_LIMIT_BYTES)


def _rms(x, g):
    return x * lax.rsqrt(jnp.mean(x * x, axis=-1, keepdims=True) + EPS) * g


def _dot(a, b):
    return jnp.dot(a, b, preferred_element_type=F32)


def _run_pipelined(tasks):
    s = tasks[0][0]()
    for n, (_, apply_fn) in enumerate(tasks):
        s_next = tasks[n + 1][0]() if n + 1 < len(tasks) else None
        apply_fn(s)
        s = s_next


def _bucket_np(dist):
    dist = np.maximum(dist, 0)
    max_exact = N_BUCKETS // 2
    d = np.maximum(dist, 1).astype(np.float32)
    large = max_exact + (np.log(d / np.float32(max_exact)) / np.float32(math.log(MAX_DISTANCE / max_exact))
                         * np.float32(N_BUCKETS - max_exact)).astype(np.int32)
    large = np.minimum(large, N_BUCKETS - 1)
    return np.where(dist < max_exact, dist, large).astype(np.int32)


def _bias_kernel(tab_ref, bm_ref, o_ref, *, head0):
    h = pl.program_id(0) + head0
    bm = bm_ref[...]
    acc = jnp.zeros(bm.shape, F32)
    for b in range(N_BUCKETS):
        acc = jnp.where(bm == b, tab_ref[b, h], acc)
    o_ref[0] = jnp.where(bm < 0, NEG, (acc - tab_ref[N_BUCKETS - 1, h]) * LOG2E)


def _bias_tiles(rel_bias, bucket_map, head0, n_heads):
    shp = bucket_map.shape
    nd = len(shp)
    return pl.pallas_call(
        functools.partial(_bias_kernel, head0=head0),
        grid=(n_heads,),
        in_specs=[pl.BlockSpec(memory_space=pltpu.SMEM),
                  pl.BlockSpec(shp, lambda h: (0,) * nd)],
        out_specs=pl.BlockSpec((1,) + shp, lambda h: (h,) + (0,) * nd),
        out_shape=jax.ShapeDtypeStruct((n_heads,) + shp, F32),
        compiler_params=_params("arbitrary"),
    )(rel_bias, jnp.asarray(bucket_map))


def _norm_matmul_kernel(x_ref, g_ref, w_ref, o_ref, xn_ref):
    @pl.when(pl.program_id(1) == 0)
    def _():
        xn_ref[...] = _rms(x_ref[...], g_ref[...]).astype(BF16)

    o_ref[...] = _dot(xn_ref[...], w_ref[...]).astype(o_ref.dtype)


def _norm_matmul(x, g, w, tm, tn, out_dtype):
    m, d = x.shape
    n = w.shape[1]
    return pl.pallas_call(
        _norm_matmul_kernel,
        grid=(m // tm, n // tn),
        in_specs=[pl.BlockSpec((tm, d), lambda i, j: (i, 0)),
                  pl.BlockSpec((1, d), lambda i, j: (0, 0)),
                  pl.BlockSpec((d, tn), lambda i, j: (0, j))],
        out_specs=pl.BlockSpec((tm, tn), lambda i, j: (i, j)),
        out_shape=jax.ShapeDtypeStruct((m, n), out_dtype),
        scratch_shapes=[pltpu.VMEM((tm, d), BF16)],
        compiler_params=_params("parallel", "arbitrary"),
    )(x, g.reshape(1, d), w)


def _out_proj_kernel(*refs, n_in):
    x_ref = refs[0]
    o_ref = refs[2 * n_in + 1]
    acc = x_ref[...]
    for t in range(n_in):
        acc = acc + _dot(refs[1 + t][...], refs[1 + n_in + t][...])
    o_ref[...] = acc


def _out_proj(x, acts, ws, tm):
    m, d = x.shape
    n_in = len(acts)
    in_specs = [pl.BlockSpec((tm, d), lambda i: (i, 0))]
    in_specs += [pl.BlockSpec((tm, a.shape[1]), lambda i: (i, 0)) for a in acts]
    in_specs += [pl.BlockSpec(w.shape, lambda i: (0, 0)) for w in ws]
    return pl.pallas_call(
        functools.partial(_out_proj_kernel, n_in=n_in),
        grid=(m // tm,),
        in_specs=in_specs,
        out_specs=pl.BlockSpec((tm, d), lambda i: (i, 0)),
        out_shape=jax.ShapeDtypeStruct((m, d), F32),
        compiler_params=_params("parallel"),
    )(x, *acts, *ws)


def _mlp_kernel(x_ref, g_ref, wu_ref, wd_ref, gf_ref, o_ref, xn_ref, *, final_norm):
    j = pl.program_id(1)

    @pl.when(j == 0)
    def _():
        x = x_ref[...]
        xn_ref[...] = _rms(x, g_ref[...]).astype(BF16)
        o_ref[...] = x

    h = _dot(xn_ref[...], wu_ref[0].astype(BF16))
    a = jnp.square(jnp.maximum(h, 0.0)).astype(BF16)
    o_ref[...] += _dot(a, wd_ref[0].astype(BF16))

    if final_norm:
        @pl.when(j == pl.num_programs(1) - 1)
        def _():
            o_ref[...] = _rms(o_ref[...], gf_ref[...])


def _mlp(x, g, w_up, w_down, layer, g_final, tm, tf, final_norm):
    m, d = x.shape
    ff = w_up.shape[2]
    return pl.pallas_call(
        functools.partial(_mlp_kernel, final_norm=final_norm),
        grid=(m // tm, ff // tf),
        in_specs=[pl.BlockSpec((tm, d), lambda i, j: (i, 0), pipeline_mode=pl.Buffered(1)),
                  pl.BlockSpec((1, d), lambda i, j: (0, 0)),
                  pl.BlockSpec((1, d, tf), lambda i, j: (layer, 0, j)),
                  pl.BlockSpec((1, tf, d), lambda i, j: (layer, j, 0)),
                  pl.BlockSpec((1, d), lambda i, j: (0, 0))],
        out_specs=pl.BlockSpec((tm, d), lambda i, j: (i, 0)),
        out_shape=jax.ShapeDtypeStruct((m, d), F32),
        scratch_shapes=[pltpu.VMEM((tm, d), BF16)],
        compiler_params=_params("parallel", "arbitrary"),
    )(x, g.reshape(1, d), w_up, w_down, g_final.reshape(1, d))


def _gelu_tanh(x):
    return 0.5 * x * (1.0 + jnp.tanh(math.sqrt(2.0 / math.pi) * (x + 0.044715 * (x * x * x))))


def _compress_kernel(h_ref, pa_ref, pb_ref, wa_ref, wb_ref, w2_ref, o_ref):
    h = h_ref[0, 0].astype(F32)
    ha = _dot((h + pa_ref[0]).astype(BF16), wa_ref[0])
    hb = _dot((h + pb_ref[0]).astype(BF16), wb_ref[0])
    n = hb.shape[0]
    pre = ha + pltpu.roll(hb, n - 1, 0)
    o_ref[0, 0] = _dot(_gelu_tanh(pre).astype(BF16), w2_ref[0]).astype(o_ref.dtype)


def _compress(hkv, pa, pb, wa, wb, w2):
    _, b, r, w = hkv.shape
    hid2 = wa.shape[2]
    return pl.pallas_call(
        _compress_kernel,
        grid=(2, b),
        in_specs=[pl.BlockSpec((1, 1, r, w), lambda t, i: (t, i, 0, 0)),
                  pl.BlockSpec((1, 1, w), lambda t, i: (t, 0, 0)),
                  pl.BlockSpec((1, 1, w), lambda t, i: (t, 0, 0)),
                  pl.BlockSpec((1, w, hid2), lambda t, i: (t, 0, 0)),
                  pl.BlockSpec((1, w, hid2), lambda t, i: (t, 0, 0)),
                  pl.BlockSpec((1, hid2, LANES), lambda t, i: (t, 0, 0))],
        out_specs=pl.BlockSpec((1, 1, r, LANES), lambda t, i: (t, i, 0, 0)),
        out_shape=jax.ShapeDtypeStruct((2, b, r, LANES), BF16),
        compiler_params=_params("arbitrary", "arbitrary"),
    )(hkv, pa, pb, wa, wb, w2)


SWA, SLC, WIN = 0, 1, 2


def _stack_pairs_t(q_ref, g):
    cols = [q_ref[0, :, (PAIRS * g + p) * LANES:(PAIRS * g + p + 1) * LANES].astype(F32).T for p in range(PAIRS)]
    return (jnp.concatenate(cols, axis=1) * (HEAD_DIM ** -0.5 * LOG2E)).astype(BF16)


def _rows_by_half(a, b, rows):
    parts = [jnp.broadcast_to(a, (HEAD_DIM, W4)), jnp.broadcast_to(b, (HEAD_DIM, W4))]
    if rows > 2 * HEAD_DIM:
        first = lax.broadcasted_iota(jnp.int32, (rows - 2 * HEAD_DIM, W4), 0) == 0
        parts.append(jnp.where(first, a, b))
    return jnp.concatenate(parts, axis=0)


def _even_attn_kernel(tab_ref, sink_ref, qa_ref, qb_ref, gate_ref,
                      ka_ref, va_ref, ks_ref, vs_ref, kw_ref, vw_ref, kc_ref, vc_ref,
                      ta_ref, tb_ref, up_ref, cb_ref, e3_ref, ovt_ref,
                      oa_ref, ob_ref,
                      kbd_ref, vbd_ref, ckbd_ref, cvbd_ref, msk_ref, acc_scr, m_scr, *, n_tiles):
    i = pl.program_id(1)
    groups = range(GROUPS)

    def block_diag(x, g):
        r = pltpu.roll(x, HEAD_DIM, 1)
        lo = lax.broadcasted_iota(jnp.int32, x.shape, 1) < HEAD_DIM
        own, other = (x, r) if g == 0 else (r, x)
        return jnp.where(lo, own, 0.0), jnp.where(lo, 0.0, other)

    @pl.when(i == 0)
    def _build():
        r16 = lax.broadcasted_iota(jnp.int32, (VT_ROWS - LANES, 2 * TQ), 0)
        c16 = lax.broadcasted_iota(jnp.int32, (VT_ROWS - LANES, 2 * TQ), 1)
        ones_rows = jnp.where((r16 == 0) & (c16 < TQ) | (r16 == 1) & (c16 >= TQ), 1.0, 0.0).astype(BF16)
        for t, ref in enumerate((ka_ref, ks_ref, kw_ref)):
            def kbody(j, c, t=t, ref=ref):
                x = ref[0, pl.ds(pl.multiple_of(j * TQ, TQ), TQ), :].astype(F32)
                for g in groups:
                    top, bot = block_diag(x, g)
                    kbd_ref[g, t, j, 0:TQ, :] = top.astype(BF16)
                    kbd_ref[g, t, j, TQ:2 * TQ, :] = bot.astype(BF16)
                return c
            lax.fori_loop(0, n_tiles, kbody, 0)
        for t, ref in enumerate((va_ref, vs_ref, vw_ref)):
            def vbody(j, c, t=t, ref=ref):
                x = ref[0, pl.ds(pl.multiple_of(j * TQ, TQ), TQ), :].astype(F32)
                for g in groups:
                    top, bot = block_diag(x, g)
                    vbd_ref[g, t, j, 0:LANES, 0:TQ] = top.T.astype(BF16)
                    vbd_ref[g, t, j, 0:LANES, TQ:2 * TQ] = bot.T.astype(BF16)
                    vbd_ref[g, t, j, LANES:, :] = ones_rows
                return c
            lax.fori_loop(0, n_tiles, vbody, 0)
        xk = kc_ref[0, 0].astype(F32)
        xv = vc_ref[0, 0].astype(F32)
        for g in groups:
            top, bot = block_diag(xk, g)
            ckbd_ref[g, 0:TQ, :] = top.astype(BF16)
            ckbd_ref[g, TQ:2 * TQ, :] = bot.astype(BF16)
            top, bot = block_diag(xv, g)
            cvbd_ref[g, :, 0:TQ] = top.T.astype(BF16)
            cvbd_ref[g, :, TQ:2 * TQ] = bot.T.astype(BF16)

    def near_bias(t_ref, g, delta):
        return lambda hf, p: t_ref[8 * g + 2 * p + hf, delta]

    def update(g, st, s, adds, vbd_t, first):
        m_old = [None if first else m_scr[g, st, hf] for hf in range(2)]
        p_rows, m_new = [], []
        for hf in range(2):
            strips, mns = [], []
            for p in range(PAIRS):
                sh = s[hf * TQ:(hf + 1) * TQ, p * TQ:(p + 1) * TQ]
                for a in adds:
                    sh = sh + a(hf, p)
                mn = jnp.max(sh, axis=0, keepdims=True)
                if not first:
                    mn = jnp.maximum(m_old[hf][:, p * TQ:(p + 1) * TQ], mn)
                strips.append(jnp.exp2(sh - mn).astype(BF16))
                mns.append(mn)
            p_rows.append(jnp.concatenate(strips, axis=1))
            m_new.append(jnp.concatenate(mns, axis=1))
            m_scr[g, st, hf] = m_new[hf]
        pv = _dot(vbd_t, jnp.concatenate(p_rows, axis=0))
        if first:
            acc_scr[g, st] = pv
        else:
            alpha = _rows_by_half(jnp.exp2(m_old[0] - m_new[0]), jnp.exp2(m_old[1] - m_new[1]), VT_ROWS)
            acc_scr[g, st] = alpha * acc_scr[g, st] + pv

    run_pipelined = _run_pipelined

    def tile_task(g, st, slot, j, q4t, adds, first=False):
        return (lambda: _dot(kbd_ref[g, slot, j], q4t),
                lambda s: update(g, st, s, adds, vbd_ref[g, slot, j], first))

    def finish(g, st, extra=None):
        acc = acc_scr[g, st]
        den = [acc[LANES + hf:LANES + hf + 1] for hf in range(2)]
        if extra is not None:
            den = [den[hf] + extra(hf, m_scr[g, st, hf]) for hf in range(2)]
        return acc[:LANES] * _rows_by_half(1.0 / den[0], 1.0 / den[1], LANES)

    q4a = [_stack_pairs_t(qa_ref, g) for g in groups]
    q4b = [_stack_pairs_t(qb_ref, g) for g in groups]

    run_pipelined([tile_task(g, SWA, K_A, i, q4a[g], [near_bias(ta_ref, g, 0)], first=True) for g in groups] +
                  [tile_task(g, WIN, K_W, i, q4b[g], [near_bias(tb_ref, g, 0)], first=True) for g in groups])

    o_cmp = []
    t_row = i * TQ + (lax.broadcasted_iota(jnp.int32, (1, W4), 1) & (TQ - 1))
    anyvis = jnp.where(t_row >= CMP_BLOCK - 1, 1.0, 0.0)
    for g in groups:
        s = _dot(ckbd_ref[g], q4b[g])
        psum = jnp.zeros((TQ, TQ), F32)
        ps = []
        for hf in range(2):
            cb = jnp.concatenate([cb_ref[8 * g + 2 * p + hf] for p in range(PAIRS)], axis=1)
            sh = s[hf * TQ:(hf + 1) * TQ] + cb
            pe = jnp.exp2(sh - jnp.max(sh, axis=0, keepdims=True))
            pc = pe * (anyvis / jnp.sum(pe, axis=0, keepdims=True))
            for p in range(PAIRS):
                psum = psum + pc[:, p * TQ:(p + 1) * TQ]
            ps.append(pc.astype(BF16))
        o_cmp.append(_dot(cvbd_ref[g], jnp.concatenate(ps, axis=0)))

        ph = psum.astype(BF16)
        r1 = psum - ph.astype(F32)
        pm = r1.astype(BF16)
        pl_ = (r1 - pm.astype(F32)).astype(BF16)
        ovt = ovt_ref[...]
        pslc = _dot(ovt, ph) + _dot(ovt, pm) + _dot(ovt, pl_)
        ns = pslc.shape[0]
        nb = lax.broadcasted_iota(jnp.int32, (ns, TQ), 0)
        tq = i * TQ + lax.broadcasted_iota(jnp.int32, (ns, TQ), 1)
        cur = tq // SLC_BLOCK
        forced = (jnp.where(nb == 0, 1.0, 0.0) + jnp.where(nb == cur, 1.0, 0.0) +
                  jnp.where(nb == cur - 1, 1.0, 0.0))
        score = jnp.where(nb * SLC_BLOCK > tq, NEG, jnp.where(forced > 0.0, BIG, pslc))
        rank = jnp.zeros((ns, TQ), F32)
        for mth in range(ns):
            sm = score[mth:mth + 1, :]
            tie = jnp.where(nb > mth, 1.0, 0.0)
            rank = rank + jnp.where(sm > score, 1.0, jnp.where(sm == score, tie, 0.0))
        msk_ref[g] = jnp.where(rank < SLC_TOPK, 0.0, NEG)

    def sel_add(g, j):
        cache = []

        def tile(hf, p):
            if not cache:
                rows = [jnp.broadcast_to(msk_ref[g, pl.ds(2 * j + e, 1), :], (SLC_BLOCK, TQ)) for e in range(2)]
                cache.append(jnp.concatenate(rows, axis=0))
            return cache[0]
        return tile

    n_full = B_WINDOW // TQ

    def near_tiles(n_back):
        tasks = [tile_task(g, SLC, K_S, i, q4b[g], [near_bias(tb_ref, g, 0), sel_add(g, i)], first=True)
                 for g in groups]
        if n_back >= 1:
            j = i - 1
            tasks += [tile_task(g, SWA, K_A, j, q4a[g], [near_bias(ta_ref, g, 1)]) for g in groups]
            tasks += [tile_task(g, WIN, K_W, j, q4b[g], [near_bias(tb_ref, g, 1)]) for g in groups]
            tasks += [tile_task(g, SLC, K_S, j, q4b[g], [near_bias(tb_ref, g, 1), sel_add(g, j)]) for g in groups]
        for d in range(2, n_back + 1):
            edge = [lambda hf, p: up_ref[...]] if d == n_full else []
            tasks += [tile_task(g, WIN, K_W, i - d, q4b[g], edge) for g in groups]
        run_pipelined(tasks)

    for c in range(n_full):
        pl.when(i == c)(functools.partial(near_tiles, c))
    pl.when(i >= n_full)(functools.partial(near_tiles, n_full))

    def far_tasks(d):
        return [tile_task(g, SLC, K_S, i - d, q4b[g], [sel_add(g, i - d)]) for g in groups]

    def far_body(jj, c):
        d = 2 + 2 * jj
        run_pipelined(far_tasks(d) + far_tasks(d + 1))
        return c

    n_far = jnp.maximum(i - 1, 0)
    lax.fori_loop(0, n_far // 2, far_body, 0)
    pl.when(n_far % 2 == 1)(lambda: run_pipelined(far_tasks(i)))

    def to_rows(o_t, p):
        return o_t[:, p * TQ:(p + 1) * TQ].T

    far_of = lambda h: tab_ref[N_BUCKETS - 1, h]
    sg = jax.nn.sigmoid(gate_ref[0].astype(F32)).T
    sg_hi = sg.astype(BF16)
    sg_lo = (sg - sg_hi.astype(F32)).astype(BF16)
    gates_t = _dot(e3_ref[...], sg_hi) + _dot(e3_ref[...], sg_lo)
    for g in groups:
        def sink_term(hf, m, g=g):
            row = jnp.concatenate([jnp.full((1, TQ), (sink_ref[8 * g + 2 * p + hf] - far_of(8 * g + 2 * p + hf)) * LOG2E,
                                            F32) for p in range(PAIRS)], axis=1)
            return jnp.exp2(row - m)

        o_swa = finish(g, SWA, sink_term)
        o_slc = finish(g, SLC)
        o_win = finish(g, WIN)
        for p in range(PAIRS):
            cols = slice((PAIRS * g + p) * LANES, (PAIRS * g + p + 1) * LANES)
            oa_ref[0, :, cols] = to_rows(o_swa, p).astype(oa_ref.dtype)
            qs = slice(p * TQ, (p + 1) * TQ)
            gate = lambda br: gates_t[br * B_HEADS * HEAD_DIM + cols.start:br * B_HEADS * HEAD_DIM + cols.stop]
            o_t = gate(0) * o_cmp[g][:, qs] + gate(1) * o_slc[:, qs] + gate(2) * o_win[:, qs]
            ob_ref[0, :, cols] = o_t.T.astype(ob_ref.dtype)


def _even_attention(pe, kvc, rel_bias, sinks, ta, tb, up, cbias, e3, ovt):
    b, s, _ = pe.shape
    n_tiles = s // TQ
    width = GROUPS * PAIRS * LANES

    def seg(k):
        return pl.BlockSpec((1, s, LANES), lambda bi, i, k=k: (bi, 0, k))

    full = lambda a: pl.BlockSpec(a.shape, lambda bi, i: (0,) * a.ndim)
    in_specs = [
        pl.BlockSpec(memory_space=pltpu.SMEM),
        pl.BlockSpec(memory_space=pltpu.SMEM),
        pl.BlockSpec((1, TQ, width), lambda bi, i: (bi, i, COL_QA // width)),
        pl.BlockSpec((1, TQ, width), lambda bi, i: (bi, i, COL_QB // width)),
        pl.BlockSpec((1, TQ, LANES), lambda bi, i: (bi, i, SEG_GATE)),
        seg(SEG_KA), seg(SEG_VA), seg(SEG_KS), seg(SEG_VS), seg(SEG_KW), seg(SEG_VW),
        pl.BlockSpec((1, 1, TQ, LANES), lambda bi, i: (0, bi, 0, 0)),
        pl.BlockSpec((1, 1, TQ, LANES), lambda bi, i: (1, bi, 0, 0)),
        full(ta), full(tb), full(up),
        pl.BlockSpec((B_HEADS, TQ, TQ), lambda bi, i: (0, 0, i)),
        full(e3), full(ovt),
    ]
    out_spec = pl.BlockSpec((1, TQ, width), lambda bi, i: (bi, i, 0))
    return pl.pallas_call(
        functools.partial(_even_attn_kernel, n_tiles=n_tiles),
        grid=(b, n_tiles),
        in_specs=in_specs,
        out_specs=[out_spec, out_spec],
        out_shape=[jax.ShapeDtypeStruct((b, s, width), BF16)] * 2,
        scratch_shapes=[pltpu.VMEM((GROUPS, 3, n_tiles, 2 * TQ, LANES), BF16),
                        pltpu.VMEM((GROUPS, 3, n_tiles, VT_ROWS, 2 * TQ), BF16),
                        pltpu.VMEM((GROUPS, 2 * TQ, LANES), BF16),
                        pltpu.VMEM((GROUPS, LANES, 2 * TQ), BF16),
                        pltpu.VMEM((GROUPS, s // SLC_BLOCK, TQ), F32),
                        pltpu.VMEM((GROUPS, 3, VT_ROWS, W4), F32),
                        pltpu.VMEM((GROUPS, 3, 2, 1, W4), F32)],
        compiler_params=_params("arbitrary", "arbitrary"),
    )(rel_bias, sinks, pe, pe, pe, pe, pe, pe, pe, pe, pe, kvc, kvc, ta, tb, up, cbias, e3, ovt)


def _mla_prep_kernel(x_ref, g_ref, win_ref, qn_ref, kvn_ref, wqt_ref, wk_ref, wvt_ref,
                     cos_ref, sin_ref, cost_ref, sint_ref, qt_ref, k_ref, vt_ref):
    xn = _rms(x_ref[...], g_ref[...]).astype(BF16)
    proj = _dot(xn, win_ref[...])
    cq = _rms(proj[:, :Q_LORA], qn_ref[...])
    ckv = _rms(proj[:, Q_LORA:Q_LORA + KV_LORA], kvn_ref[...])
    cq_t = cq.T.astype(BF16)
    ckv_t = ckv.T.astype(BF16)
    ckv = ckv.astype(BF16)
    kr = proj[:, Q_LORA + KV_LORA:]
    kr = (kr * cos_ref[...] + pltpu.roll(kr, HEAD_DIM, 1) * sin_ref[...]).astype(BF16)
    cos_t = cost_ref[...]
    sin_t = sint_ref[...]
    scale = (NOPE_DIM + ROPE_DIM) ** -0.5 * LOG2E
    tm = ckv_t.shape[1]
    ones_rows = jnp.where(lax.broadcasted_iota(jnp.int32, (VT_ROWS - V_DIM, tm), 0) == 0, 1.0, 0.0).astype(BF16)
    for h in range(C_HEADS):
        q_t = _dot(wqt_ref[h], cq_t)
        rp = q_t[NOPE_DIM:]
        rp = rp * cos_t + pltpu.roll(rp, HEAD_DIM, 0) * sin_t
        qt_ref[0, h, 0, 0:NOPE_DIM, :] = (q_t[:NOPE_DIM] * scale).astype(BF16)
        qt_ref[0, h, 0, NOPE_DIM:, :] = (rp * scale).astype(BF16)
        if h % 2 == 0:
            k2 = _dot(ckv, wk_ref[:, h * LANES:(h + 2) * LANES]).astype(BF16)
        k_ref[:, h * 256:h * 256 + LANES] = k2[:, (h % 2) * LANES:(h % 2 + 1) * LANES]
        k_ref[:, h * 256 + LANES:(h + 1) * 256] = kr
        vt_ref[0, h, 0, 0:V_DIM, :] = _dot(wvt_ref[h], ckv_t).astype(BF16)
        vt_ref[0, h, 0, V_DIM:, :] = ones_rows


def _mla_prep(x, g, w_in, qn, kvn, wqt, wk, wvt, cos, sin, b, s):
    m, d = x.shape
    tm = MLA_TK
    pos_tiles = s // tm
    full = lambda a: pl.BlockSpec(a.shape, lambda i: (0,) * a.ndim)
    tile_t = lambda rows: pl.BlockSpec((1, C_HEADS, 1, rows, tm), lambda i: (i // pos_tiles, 0, i % pos_tiles, 0, 0))
    return pl.pallas_call(
        _mla_prep_kernel,
        grid=(m // tm,),
        in_specs=[pl.BlockSpec((tm, d), lambda i: (i, 0)),
                  pl.BlockSpec((1, d), lambda i: (0, 0)),
                  full(w_in), pl.BlockSpec((1, Q_LORA), lambda i: (0, 0)),
                  pl.BlockSpec((1, KV_LORA), lambda i: (0, 0)), full(wqt), full(wk), full(wvt),
                  pl.BlockSpec((tm, LANES), lambda i: (i % pos_tiles, 0)),
                  pl.BlockSpec((tm, LANES), lambda i: (i % pos_tiles, 0)),
                  pl.BlockSpec((LANES, tm), lambda i: (0, i % pos_tiles)),
                  pl.BlockSpec((LANES, tm), lambda i: (0, i % pos_tiles))],
        out_specs=[tile_t(256),
                   pl.BlockSpec((tm, C_HEADS * 256), lambda i: (i, 0)),
                   tile_t(VT_ROWS)],
        out_shape=[jax.ShapeDtypeStruct((b, C_HEADS, pos_tiles, 256, tm), BF16),
                   jax.ShapeDtypeStruct((m, C_HEADS * 256), BF16),
                   jax.ShapeDtypeStruct((b, C_HEADS, pos_tiles, VT_ROWS, tm), BF16)],
        compiler_params=_params("parallel"),
    )(x, g.reshape(1, d), w_in, qn.reshape(1, -1), kvn.reshape(1, -1), wqt, wk, wvt, cos, sin, cos.T, sin.T)


def _mla_attn_kernel(qt_ref, k_ref, vt_ref, o_ref, acc_scr, m_scr, *, n_q):
    tq, tk = MLA_TQ, MLA_TK
    r = tq // tk
    ki = lax.broadcasted_iota(jnp.int32, (tk, LANES), 0)
    qi = lax.broadcasted_iota(jnp.int32, (tk, LANES), 1)
    heads = range(MLA_HEADS)

    def q_tile(t, c):
        q_ts = [jnp.concatenate([qt_ref[0, h, r * t + d] for d in range(r)], axis=1) for h in heads]

        def update(h, j, s, diag):
            m_old = m_scr[h]
            ps, mns = [], []
            for p in range(tq // LANES):
                sh = s[:, p * LANES:(p + 1) * LANES]
                if diag is not None:
                    sh = jnp.where(ki + diag * tk <= qi + p * LANES, sh, NEG)
                mn = jnp.maximum(m_old[:, p * LANES:(p + 1) * LANES], jnp.max(sh, axis=0, keepdims=True))
                ps.append(jnp.exp2(sh - mn).astype(BF16))
                mns.append(mn)
            m_new = jnp.concatenate(mns, axis=1)
            m_scr[h] = m_new
            acc_scr[h] = jnp.exp2(m_old - m_new) * acc_scr[h] + _dot(vt_ref[0, h, j], jnp.concatenate(ps, axis=1))

        def task(h, j, diag=None):
            rows = pl.ds(pl.multiple_of(j * tk, tk), tk)
            return (lambda: _dot(k_ref[0, rows, h * 256:(h + 1) * 256], q_ts[h]),
                    lambda s: update(h, j, s, diag))

        m_scr[...] = jnp.full(m_scr.shape, NEG, F32)
        acc_scr[...] = jnp.zeros(acc_scr.shape, F32)
        _run_pipelined([task(h, r * t + d, d) for d in range(r) for h in heads])

        def body(jj, c2):
            _run_pipelined([task(h, 2 * jj + e) for e in range(2) for h in heads])
            return c2

        lax.fori_loop(0, t * r // 2, body, 0)
        rows = pl.ds(pl.multiple_of(t * tq, tq), tq)
        for h in heads:
            acc = acc_scr[h]
            o = acc[:V_DIM] * (1.0 / acc[V_DIM:V_DIM + 1])
            o_ref[0, rows, h * LANES:(h + 1) * LANES] = o.T.astype(o_ref.dtype)
        return c

    lax.fori_loop(0, n_q, q_tile, 0)


def _mla_attention(qt, k, vt):
    b, s, _ = k.shape
    n_k = s // MLA_TK
    nh = MLA_HEADS
    return pl.pallas_call(
        functools.partial(_mla_attn_kernel, n_q=s // MLA_TQ),
        grid=(b, C_HEADS // nh),
        in_specs=[pl.BlockSpec((1, nh, n_k, 256, MLA_TK), lambda bi, h: (bi, h, 0, 0, 0)),
                  pl.BlockSpec((1, s, 256 * nh), lambda bi, h: (bi, 0, h)),
                  pl.BlockSpec((1, nh, n_k, VT_ROWS, MLA_TK), lambda bi, h: (bi, h, 0, 0, 0))],
        out_specs=pl.BlockSpec((1, s, LANES * nh), lambda bi, h: (bi, 0, h)),
        out_shape=jax.ShapeDtypeStruct((b, s, C_HEADS * V_DIM), BF16),
        scratch_shapes=[pltpu.VMEM((nh, VT_ROWS, MLA_TQ), F32),
                        pltpu.VMEM((nh, 1, MLA_TQ), F32)],
        compiler_params=_params("parallel", "arbitrary"),
    )(qt, k, vt)


def _even_in_weight(w):
    cuts = np.cumsum([1024, 128, 128, 1024, 128, 128, 128, 128, 128, 128, 48])[:-1]
    qa, ka, va, qb, kc, vc, ks, vs, kw, vw, gate = jnp.split(w, [int(c) for c in cuts], axis=1)
    gate = jnp.pad(gate, ((0, 0), (0, LANES - gate.shape[1])))
    return jnp.concatenate([qa, qb, ka, va, kc, vc, ks, vs, kw, vw, gate], axis=1).astype(BF16)


def _compress_weights(pos, w1, w2):
    half = CMP_BLOCK // 2
    eye = jnp.eye(GROUPS, dtype=w1.dtype)
    w1r = w1.reshape(2, half, HEAD_DIM, CMP_HIDDEN)
    wab = jnp.einsum('rldh,gk->rlgdkh', w1r, eye).reshape(2, half * GROUPS * HEAD_DIM, GROUPS * CMP_HIDDEN)
    posr = jnp.broadcast_to(pos.reshape(2, half, 1, HEAD_DIM), (2, half, GROUPS, HEAD_DIM)).reshape(2, 1, -1)
    w2bd = jnp.einsum('hd,gk->ghkd', w2, eye).reshape(GROUPS * CMP_HIDDEN, GROUPS * HEAD_DIM)
    return posr[0], posr[1], wab[0].astype(BF16), wab[1].astype(BF16), w2bd.astype(BF16)


def _rope_chunk_cols(w):
    z = jnp.zeros(w.shape[:-1] + (ROPE_DIM // 2,), w.dtype)
    return jnp.concatenate([w[..., :ROPE_DIM // 2], z, w[..., ROPE_DIM // 2:], z], axis=-1)


def _mla_weights(w_in, w_q_up, w_kv_up):
    w_in2 = jnp.concatenate([w_in[:, :Q_LORA + KV_LORA], _rope_chunk_cols(w_in[:, Q_LORA + KV_LORA:])], axis=1)
    wq = w_q_up.reshape(Q_LORA, C_HEADS, NOPE_DIM + ROPE_DIM)
    wq2 = jnp.concatenate([wq[..., :NOPE_DIM], _rope_chunk_cols(wq[..., NOPE_DIM:])], axis=-1)
    wqt = wq2.transpose(1, 2, 0)
    wkv = w_kv_up.reshape(KV_LORA, C_HEADS, NOPE_DIM + V_DIM)
    wk = wkv[..., :NOPE_DIM].reshape(KV_LORA, C_HEADS * NOPE_DIM)
    wvt = wkv[..., NOPE_DIM:].transpose(1, 2, 0)
    return w_in2.astype(BF16), wqt.astype(BF16), wk.astype(BF16), wvt.astype(BF16)


def _rope_tables(s):
    inv = 1.0 / (ROPE_THETA ** (jnp.arange(0, ROPE_DIM, 2, dtype=F32) / ROPE_DIM))
    ang = jnp.arange(s, dtype=F32)[:, None] * inv[None]
    cos, sin = jnp.cos(ang), jnp.sin(ang)
    z = jnp.zeros_like(cos)
    return jnp.concatenate([cos, z, cos, z], axis=1), jnp.concatenate([-sin, z, sin, z], axis=1)


def _static_tables(s):
    k = np.arange(TQ)[:, None]
    q = np.arange(TQ)[None, :]
    diag = np.where(k <= q, _bucket_np(q - k), -1)
    prev = _bucket_np(q - k + TQ)
    near_swa = np.stack([diag, np.where(k > q, prev, -1)])
    near_nsa = np.stack([diag, prev])
    upper = np.where(k > q, 0.0, NEG).astype(np.float32)
    c = np.arange(LANES)[:, None]
    t = np.arange(s)[None, :]
    cdist = t - (c * CMP_STRIDE + CMP_BLOCK - 1)
    cmp_map = np.where(cdist >= 0, _bucket_np(cdist), -1)
    ns = s // SLC_BLOCK
    nc = (s - CMP_BLOCK) // CMP_STRIDE + 1
    c_start = np.arange(LANES) * CMP_STRIDE
    s_start = np.arange(ns) * SLC_BLOCK
    ovt = ((c_start[None, :] <= s_start[:, None] + SLC_BLOCK - 1) &
           (c_start[None, :] + CMP_BLOCK - 1 >= s_start[:, None]) &
           (np.arange(LANES)[None, :] < nc)).astype(np.float32)
    e3 = np.zeros((3, B_HEADS * HEAD_DIM, LANES), np.float32)
    for h in range(B_HEADS):
        for br in range(3):
            e3[br, h * HEAD_DIM:(h + 1) * HEAD_DIM, h * 3 + br] = 1.0
    e3 = e3.reshape(3 * B_HEADS * HEAD_DIM, LANES)
    return near_swa, near_nsa, upper, cmp_map, ovt, e3


def _even_layer(x, b, s, rel_bias, norm, w_in, sinks, pos_k, pos_v, k_w1, k_w2, v_w1, v_w2, w_out):
    near_swa, near_nsa, upper, cmp_map, ovt, e3 = _static_tables(s)
    ta = _bias_tiles(rel_bias, near_swa, 0, A_HEADS)
    tb = _bias_tiles(rel_bias, near_nsa, A_HEADS, B_HEADS)
    cbias = _bias_tiles(rel_bias, cmp_map, A_HEADS, B_HEADS)

    pe = _norm_matmul(x, norm, _even_in_weight(w_in), 512, EVEN_COLS, BF16).reshape(b, s, EVEN_COLS)

    rows = s // (CMP_BLOCK // 2)
    hkv = jnp.stack([pe[:, :, SEG_KC * LANES:(SEG_KC + 1) * LANES],
                     pe[:, :, SEG_VC * LANES:(SEG_VC + 1) * LANES]]).reshape(2, b, rows, -1)
    ck = _compress_weights(pos_k, k_w1, k_w2)
    cv = _compress_weights(pos_v, v_w1, v_w2)
    kvc = _compress(hkv, *[jnp.stack([a, c]) for a, c in zip(ck, cv)])
    assert rows <= TQ
    kvc = jnp.pad(kvc, ((0, 0), (0, 0), (0, TQ - rows), (0, 0)))

    oa, ob = _even_attention(pe, kvc, rel_bias, sinks, ta, tb, jnp.asarray(upper), cbias,
                             jnp.asarray(e3, BF16), jnp.asarray(ovt, BF16))
    w_out = w_out.astype(BF16)
    n_a = A_HEADS * HEAD_DIM
    return _out_proj(x, [oa.reshape(b * s, -1), ob.reshape(b * s, -1)], [w_out[:n_a], w_out[n_a:]], 512)


def _odd_layer(x, b, s, norm, w_in, q_norm, w_q_up, kv_norm, w_kv_up, w_out):
    w_in2, wqt, wk, wvt = _mla_weights(w_in, w_q_up, w_kv_up)
    cos, sin = _rope_tables(s)
    qt, k, vt = _mla_prep(x, norm, w_in2, q_norm, kv_norm, wqt, wk, wvt, cos, sin, b, s)
    o = _mla_attention(qt, k.reshape(b, s, -1), vt)
    return _out_proj(x, [o.reshape(b * s, -1)], [w_out.astype(BF16)], 512)


def kernel(x, rel_bias, norm_mix_e, w_in_e, sinks, cmp_pos_k, cmp_pos_v, cmp_k_w1, cmp_k_w2, cmp_v_w1, cmp_v_w2, w_out_e, norm_mix_o, w_in_o, q_norm, w_q_up, kv_norm, w_kv_up, w_out_o, norm_mlp, w_up, w_down, norm_final):
    b, s, d = x.shape
    depth = norm_mlp.shape[0]
    h = x.reshape(b * s, d)
    for layer in range(depth):
        i = layer // 2
        if layer % 2 == 0:
            h = _even_layer(h, b, s, rel_bias, norm_mix_e[i], w_in_e[i], sinks[i], cmp_pos_k[i], cmp_pos_v[i],
                            cmp_k_w1[i], cmp_k_w2[i], cmp_v_w1[i], cmp_v_w2[i], w_out_e[i])
        else:
            h = _odd_layer(h, b, s, norm_mix_o[i], w_in_o[i], q_norm[i], w_q_up[i], kv_norm[i], w_kv_up[i],
                           w_out_o[i])
        h = _mlp(h, norm_mlp[layer], w_up, w_down, layer, norm_final, 1024, 512, layer == depth - 1)
    return h.reshape(b, s, d)
```

```python
import functools
import math

import numpy as np
import jax
import jax.numpy as jnp
from jax import lax
from jax.experimental import pallas as pl
from jax.experimental.pallas import tpu as pltpu

F32 = jnp.float32
BF16 = jnp.bfloat16

LANES = 128
VMEM_LIMIT_BYTES = 56 * 1024 * 1024

EPS = 1e-6
NEG = -1e30
BIG = 1e4
HEAD_DIM = 64
TQ = 128
MLA_TQ = 512
MLA_TK = 512
MLA_HEADS = 4
N_BUCKETS = 32
MAX_DISTANCE = 128
A_HEADS = 16
B_HEADS = 16
GROUPS = 2
PAIRS = 4
W4 = PAIRS * TQ
CMP_BLOCK = 32
CMP_STRIDE = 16
CMP_HIDDEN = 256
SLC_BLOCK = 64
SLC_TOPK = 8
A_WINDOW = 128
B_WINDOW = 512
C_HEADS = 16
Q_LORA = 768
KV_LORA = 512
NOPE_DIM = 128
ROPE_DIM = 64
V_DIM = 128
VT_ROWS = V_DIM + 16
LOG2E = math.log2(math.e)
ROPE_THETA = 10000.0

COL_QA, COL_QB = 0, 1024
SEG_KA, SEG_VA, SEG_KC, SEG_VC, SEG_KS, SEG_VS, SEG_KW, SEG_VW, SEG_GATE = range(16, 25)
EVEN_COLS = 25 * LANES
K_A, K_S, K_W = 0, 1, 2


def _params(*sem):
    return pltpu.CompilerParams(dimension_semantics=sem, vmem_limit_bytes=VMEM_LIMIT_BYTES)


def _rms(x, g):
    return x * lax.rsqrt(jnp.mean(x * x, axis=-1, keepdims=True) + EPS) * g


def _dot(a, b):
    return jnp.dot(a, b, preferred_element_type=F32)


def _run_pipelined(tasks):
    s = tasks[0][0]()
    for n, (_, apply_fn) in enumerate(tasks):
        s_next = tasks[n + 1][0]() if n + 1 < len(tasks) else None
        apply_fn(s)
        s = s_next


def _bucket_np(dist):
    dist = np.maximum(dist, 0)
    max_exact = N_BUCKETS // 2
    d = np.maximum(dist, 1).astype(np.float32)
    large = max_exact + (np.log(d / np.float32(max_exact)) / np.float32(math.log(MAX_DISTANCE / max_exact))
                         * np.float32(N_BUCKETS - max_exact)).astype(np.int32)
    large = np.minimum(large, N_BUCKETS - 1)
    return np.where(dist < max_exact, dist, large).astype(np.int32)


def _bias_kernel(tab_ref, bm_ref, o_ref, *, head0):
    h = pl.program_id(0) + head0
    bm = bm_ref[...]
    acc = jnp.zeros(bm.shape, F32)
    for b in range(N_BUCKETS):
        acc = jnp.where(bm == b, tab_ref[b, h], acc)
    o_ref[0] = jnp.where(bm < 0, NEG, (acc - tab_ref[N_BUCKETS - 1, h]) * LOG2E)


def _bias_tiles(rel_bias, bucket_map, head0, n_heads):
    shp = bucket_map.shape
    nd = len(shp)
    return pl.pallas_call(
        functools.partial(_bias_kernel, head0=head0),
        grid=(n_heads,),
        in_specs=[pl.BlockSpec(memory_space=pltpu.SMEM),
                  pl.BlockSpec(shp, lambda h: (0,) * nd)],
        out_specs=pl.BlockSpec((1,) + shp, lambda h: (h,) + (0,) * nd),
        out_shape=jax.ShapeDtypeStruct((n_heads,) + shp, F32),
        compiler_params=_params("arbitrary"),
    )(rel_bias, jnp.asarray(bucket_map))


def _norm_matmul_kernel(x_ref, g_ref, w_ref, o_ref, xn_ref):
    @pl.when(pl.program_id(1) == 0)
    def _():
        xn_ref[...] = _rms(x_ref[...], g_ref[...]).astype(BF16)

    o_ref[...] = _dot(xn_ref[...], w_ref[...]).astype(o_ref.dtype)


def _norm_matmul(x, g, w, tm, tn, out_dtype):
    m, d = x.shape
    n = w.shape[1]
    return pl.pallas_call(
        _norm_matmul_kernel,
        grid=(m // tm, n // tn),
        in_specs=[pl.BlockSpec((tm, d), lambda i, j: (i, 0)),
                  pl.BlockSpec((1, d), lambda i, j: (0, 0)),
                  pl.BlockSpec((d, tn), lambda i, j: (0, j),
                               pipeline_mode=pl.Buffered(1) if tn == n else None)],
        out_specs=pl.BlockSpec((tm, tn), lambda i, j: (i, j)),
        out_shape=jax.ShapeDtypeStruct((m, n), out_dtype),
        scratch_shapes=[pltpu.VMEM((tm, d), BF16)],
        compiler_params=_params("parallel", "arbitrary"),
    )(x, g.reshape(1, d), w)


def _out_proj_kernel(*refs, n_in):
    x_ref = refs[0]
    o_ref = refs[2 * n_in + 1]
    acc = x_ref[...]
    for t in range(n_in):
        acc = acc + _dot(refs[1 + t][...], refs[1 + n_in + t][...])
    o_ref[...] = acc


def _out_proj(x, acts, ws, tm):
    m, d = x.shape
    n_in = len(acts)
    in_specs = [pl.BlockSpec((tm, d), lambda i: (i, 0))]
    in_specs += [pl.BlockSpec((tm, a.shape[1]), lambda i: (i, 0)) for a in acts]
    in_specs += [pl.BlockSpec(w.shape, lambda i: (0, 0), pipeline_mode=pl.Buffered(1)) for w in ws]
    return pl.pallas_call(
        functools.partial(_out_proj_kernel, n_in=n_in),
        grid=(m // tm,),
        in_specs=in_specs,
        out_specs=pl.BlockSpec((tm, d), lambda i: (i, 0)),
        out_shape=jax.ShapeDtypeStruct((m, d), F32),
        compiler_params=_params("parallel"),
    )(x, *acts, *ws)


def _mlp_kernel(x_ref, g_ref, wu_ref, wd_ref, gf_ref, o_ref, xn_ref, *, final_norm):
    j = pl.program_id(1)

    @pl.when(j == 0)
    def _():
        x = x_ref[...]
        xn_ref[...] = _rms(x, g_ref[...]).astype(BF16)
        o_ref[...] = x

    h = _dot(xn_ref[...], wu_ref[0].astype(BF16))
    a = jnp.square(jnp.maximum(h, 0.0)).astype(BF16)
    o_ref[...] += _dot(a, wd_ref[0].astype(BF16))

    if final_norm:
        @pl.when(j == pl.num_programs(1) - 1)
        def _():
            o_ref[...] = _rms(o_ref[...], gf_ref[...])


def _mlp(x, g, w_up, w_down, layer, g_final, tm, tf, final_norm):
    m, d = x.shape
    ff = w_up.shape[2]
    return pl.pallas_call(
        functools.partial(_mlp_kernel, final_norm=final_norm),
        grid=(m // tm, ff // tf),
        in_specs=[pl.BlockSpec((tm, d), lambda i, j: (i, 0), pipeline_mode=pl.Buffered(1)),
                  pl.BlockSpec((1, d), lambda i, j: (0, 0)),
                  pl.BlockSpec((1, d, tf), lambda i, j: (layer, 0, j)),
                  pl.BlockSpec((1, tf, d), lambda i, j: (layer, j, 0)),
                  pl.BlockSpec((1, d), lambda i, j: (0, 0))],
        out_specs=pl.BlockSpec((tm, d), lambda i, j: (i, 0)),
        out_shape=jax.ShapeDtypeStruct((m, d), F32),
        scratch_shapes=[pltpu.VMEM((tm, d), BF16)],
        compiler_params=_params("parallel", "arbitrary"),
    )(x, g.reshape(1, d), w_up, w_down, g_final.reshape(1, d))


def _gelu_tanh(x):
    return 0.5 * x * (1.0 + jnp.tanh(math.sqrt(2.0 / math.pi) * (x + 0.044715 * (x * x * x))))


def _compress_kernel(h_ref, pa_ref, pb_ref, wa_ref, wb_ref, w2_ref, o_ref):
    h = h_ref[0, 0].astype(F32)
    ha = _dot((h + pa_ref[0]).astype(BF16), wa_ref[0])
    hb = _dot((h + pb_ref[0]).astype(BF16), wb_ref[0])
    n = hb.shape[0]
    pre = ha + pltpu.roll(hb, n - 1, 0)
    o_ref[0, 0] = _dot(_gelu_tanh(pre).astype(BF16), w2_ref[0]).astype(o_ref.dtype)


def _compress(hkv, pa, pb, wa, wb, w2):
    _, b, r, w = hkv.shape
    hid2 = wa.shape[2]
    return pl.pallas_call(
        _compress_kernel,
        grid=(2, b),
        in_specs=[pl.BlockSpec((1, 1, r, w), lambda t, i: (t, i, 0, 0)),
                  pl.BlockSpec((1, 1, w), lambda t, i: (t, 0, 0)),
                  pl.BlockSpec((1, 1, w), lambda t, i: (t, 0, 0)),
                  pl.BlockSpec((1, w, hid2), lambda t, i: (t, 0, 0)),
                  pl.BlockSpec((1, w, hid2), lambda t, i: (t, 0, 0)),
                  pl.BlockSpec((1, hid2, LANES), lambda t, i: (t, 0, 0))],
        out_specs=pl.BlockSpec((1, 1, r, LANES), lambda t, i: (t, i, 0, 0)),
        out_shape=jax.ShapeDtypeStruct((2, b, r, LANES), BF16),
        compiler_params=_params("arbitrary", "arbitrary"),
    )(hkv, pa, pb, wa, wb, w2)


SWA, SLC, WIN = 0, 1, 2


def _stack_pairs_t(q_ref, g):
    cols = [q_ref[0, :, (PAIRS * g + p) * LANES:(PAIRS * g + p + 1) * LANES].astype(F32).T for p in range(PAIRS)]
    return (jnp.concatenate(cols, axis=1) * (HEAD_DIM ** -0.5 * LOG2E)).astype(BF16)


def _rows_by_half(a, b, rows):
    parts = [jnp.broadcast_to(a, (HEAD_DIM, W4)), jnp.broadcast_to(b, (HEAD_DIM, W4))]
    if rows > 2 * HEAD_DIM:
        first = lax.broadcasted_iota(jnp.int32, (rows - 2 * HEAD_DIM, W4), 0) == 0
        parts.append(jnp.where(first, a, b))
    return jnp.concatenate(parts, axis=0)


def _even_attn_kernel(tab_ref, sink_ref, qa_ref, qb_ref, gate_ref,
                      ka_ref, va_ref, ks_ref, vs_ref, kw_ref, vw_ref, kc_ref, vc_ref,
                      ta_ref, tb_ref, up_ref, cb_ref, e3_ref, ovt_ref,
                      oa_ref, ob_ref,
                      kbd_ref, vbd_ref, ckbd_ref, cvbd_ref, msk_ref, acc_scr, m_scr, *, n_tiles):
    i = pl.program_id(1)
    groups = range(GROUPS)

    def block_diag(x, g):
        r = pltpu.roll(x, HEAD_DIM, 1)
        lo = lax.broadcasted_iota(jnp.int32, x.shape, 1) < HEAD_DIM
        own, other = (x, r) if g == 0 else (r, x)
        return jnp.where(lo, own, 0.0), jnp.where(lo, 0.0, other)

    @pl.when(i == 0)
    def _build():
        r16 = lax.broadcasted_iota(jnp.int32, (VT_ROWS - LANES, 2 * TQ), 0)
        c16 = lax.broadcasted_iota(jnp.int32, (VT_ROWS - LANES, 2 * TQ), 1)
        ones_rows = jnp.where((r16 == 0) & (c16 < TQ) | (r16 == 1) & (c16 >= TQ), 1.0, 0.0).astype(BF16)
        for t, ref in enumerate((ka_ref, ks_ref, kw_ref)):
            def kbody(j, c, t=t, ref=ref):
                x = ref[0, pl.ds(pl.multiple_of(j * TQ, TQ), TQ), :].astype(F32)
                for g in groups:
                    top, bot = block_diag(x, g)
                    kbd_ref[g, t, j, 0:TQ, :] = top.astype(BF16)
                    kbd_ref[g, t, j, TQ:2 * TQ, :] = bot.astype(BF16)
                return c
            lax.fori_loop(0, n_tiles, kbody, 0)
        for t, ref in enumerate((va_ref, vs_ref, vw_ref)):
            def vbody(j, c, t=t, ref=ref):
                x = ref[0, pl.ds(pl.multiple_of(j * TQ, TQ), TQ), :].astype(F32)
                for g in groups:
                    top, bot = block_diag(x, g)
                    vbd_ref[g, t, j, 0:LANES, 0:TQ] = top.T.astype(BF16)
                    vbd_ref[g, t, j, 0:LANES, TQ:2 * TQ] = bot.T.astype(BF16)
                    vbd_ref[g, t, j, LANES:, :] = ones_rows
                return c
            lax.fori_loop(0, n_tiles, vbody, 0)
        xk = kc_ref[0, 0].astype(F32)
        xv = vc_ref[0, 0].astype(F32)
        for g in groups:
            top, bot = block_diag(xk, g)
            ckbd_ref[g, 0:TQ, :] = top.astype(BF16)
            ckbd_ref[g, TQ:2 * TQ, :] = bot.astype(BF16)
            top, bot = block_diag(xv, g)
            cvbd_ref[g, :, 0:TQ] = top.T.astype(BF16)
            cvbd_ref[g, :, TQ:2 * TQ] = bot.T.astype(BF16)

    def near_bias(t_ref, g, delta):
        return lambda hf, p: t_ref[8 * g + 2 * p + hf, delta]

    def update(g, st, s, adds, vbd_t, first):
        m_old = [None if first else m_scr[g, st, hf] for hf in range(2)]
        p_rows, m_new = [], []
        for hf in range(2):
            strips, mns = [], []
            for p in range(PAIRS):
                sh = s[hf * TQ:(hf + 1) * TQ, p * TQ:(p + 1) * TQ]
                for a in adds:
                    sh = sh + a(hf, p)
                mn = jnp.max(sh, axis=0, keepdims=True)
                if not first:
                    mn = jnp.maximum(m_old[hf][:, p * TQ:(p + 1) * TQ], mn)
                strips.append(jnp.exp2(sh - mn).astype(BF16))
                mns.append(mn)
            p_rows.append(jnp.concatenate(strips, axis=1))
            m_new.append(jnp.concatenate(mns, axis=1))
            m_scr[g, st, hf] = m_new[hf]
        pv = _dot(vbd_t, jnp.concatenate(p_rows, axis=0))
        if first:
            acc_scr[g, st] = pv
        else:
            alpha = _rows_by_half(jnp.exp2(m_old[0] - m_new[0]), jnp.exp2(m_old[1] - m_new[1]), VT_ROWS)
            acc_scr[g, st] = alpha * acc_scr[g, st] + pv

    run_pipelined = _run_pipelined

    def tile_task(g, st, slot, j, q4t, adds, first=False):
        return (lambda: _dot(kbd_ref[g, slot, j], q4t),
                lambda s: update(g, st, s, adds, vbd_ref[g, slot, j], first))

    def finish(g, st, extra=None):
        acc = acc_scr[g, st]
        den = [acc[LANES + hf:LANES + hf + 1] for hf in range(2)]
        if extra is not None:
            den = [den[hf] + extra(hf, m_scr[g, st, hf]) for hf in range(2)]
        return acc[:LANES] * _rows_by_half(1.0 / den[0], 1.0 / den[1], LANES)

    q4a = [_stack_pairs_t(qa_ref, g) for g in groups]
    q4b = [_stack_pairs_t(qb_ref, g) for g in groups]

    run_pipelined([tile_task(g, SWA, K_A, i, q4a[g], [near_bias(ta_ref, g, 0)], first=True) for g in groups] +
                  [tile_task(g, WIN, K_W, i, q4b[g], [near_bias(tb_ref, g, 0)], first=True) for g in groups])

    o_cmp = []
    t_row = i * TQ + (lax.broadcasted_iota(jnp.int32, (1, W4), 1) & (TQ - 1))
    anyvis = jnp.where(t_row >= CMP_BLOCK - 1, 1.0, 0.0)
    for g in groups:
        s = _dot(ckbd_ref[g], q4b[g])
        psum = jnp.zeros((TQ, TQ), F32)
        ps = []
        for hf in range(2):
            cb = jnp.concatenate([cb_ref[8 * g + 2 * p + hf] for p in range(PAIRS)], axis=1)
            sh = s[hf * TQ:(hf + 1) * TQ] + cb
            pe = jnp.exp2(sh - jnp.max(sh, axis=0, keepdims=True))
            pc = pe * (anyvis / jnp.sum(pe, axis=0, keepdims=True))
            for p in range(PAIRS):
                psum = psum + pc[:, p * TQ:(p + 1) * TQ]
            ps.append(pc.astype(BF16))
        o_cmp.append(_dot(cvbd_ref[g], jnp.concatenate(ps, axis=0)))

        ph = psum.astype(BF16)
        r1 = psum - ph.astype(F32)
        pm = r1.astype(BF16)
        pl_ = (r1 - pm.astype(F32)).astype(BF16)
        ovt = ovt_ref[...]
        pslc = _dot(ovt, ph) + _dot(ovt, pm) + _dot(ovt, pl_)
        ns = pslc.shape[0]
        nb = lax.broadcasted_iota(jnp.int32, (ns, TQ), 0)
        tq = i * TQ + lax.broadcasted_iota(jnp.int32, (ns, TQ), 1)
        cur = tq // SLC_BLOCK
        forced = (jnp.where(nb == 0, 1.0, 0.0) + jnp.where(nb == cur, 1.0, 0.0) +
                  jnp.where(nb == cur - 1, 1.0, 0.0))
        score = jnp.where(nb * SLC_BLOCK > tq, NEG, jnp.where(forced > 0.0, BIG, pslc))
        rank = jnp.zeros((ns, TQ), F32)
        for mth in range(ns):
            sm = score[mth:mth + 1, :]
            tie = jnp.where(nb > mth, 1.0, 0.0)
            rank = rank + jnp.where(sm > score, 1.0, jnp.where(sm == score, tie, 0.0))
        msk_ref[g] = jnp.where(rank < SLC_TOPK, 0.0, NEG)

    def sel_add(g, j):
        cache = []

        def tile(hf, p):
            if not cache:
                rows = [jnp.broadcast_to(msk_ref[g, pl.ds(2 * j + e, 1), :], (SLC_BLOCK, TQ)) for e in range(2)]
                cache.append(jnp.concatenate(rows, axis=0))
            return cache[0]
        return tile

    n_full = B_WINDOW // TQ

    def near_tiles(n_back):
        tasks = [tile_task(g, SLC, K_S, i, q4b[g], [near_bias(tb_ref, g, 0), sel_add(g, i)], first=True)
                 for g in groups]
        if n_back >= 1:
            j = i - 1
            tasks += [tile_task(g, SWA, K_A, j, q4a[g], [near_bias(ta_ref, g, 1)]) for g in groups]
            tasks += [tile_task(g, WIN, K_W, j, q4b[g], [near_bias(tb_ref, g, 1)]) for g in groups]
            tasks += [tile_task(g, SLC, K_S, j, q4b[g], [near_bias(tb_ref, g, 1), sel_add(g, j)]) for g in groups]
        for d in range(2, n_back + 1):
            edge = [lambda hf, p: up_ref[...]] if d == n_full else []
            tasks += [tile_task(g, WIN, K_W, i - d, q4b[g], edge) for g in groups]
        run_pipelined(tasks)

    for c in range(n_full):
        pl.when(i == c)(functools.partial(near_tiles, c))
    pl.when(i >= n_full)(functools.partial(near_tiles, n_full))

    def far_tasks(d):
        return [tile_task(g, SLC, K_S, i - d, q4b[g], [sel_add(g, i - d)]) for g in groups]

    def far_body(jj, c):
        d = 2 + 2 * jj
        run_pipelined(far_tasks(d) + far_tasks(d + 1))
        return c

    n_far = jnp.maximum(i - 1, 0)
    lax.fori_loop(0, n_far // 2, far_body, 0)
    pl.when(n_far % 2 == 1)(lambda: run_pipelined(far_tasks(i)))

    def to_rows(o_t, p):
        return o_t[:, p * TQ:(p + 1) * TQ].T

    far_of = lambda h: tab_ref[N_BUCKETS - 1, h]
    sg = jax.nn.sigmoid(gate_ref[0].astype(F32)).T
    sg_hi = sg.astype(BF16)
    sg_lo = (sg - sg_hi.astype(F32)).astype(BF16)
    gates_t = _dot(e3_ref[...], sg_hi) + _dot(e3_ref[...], sg_lo)
    for g in groups:
        def sink_term(hf, m, g=g):
            row = jnp.concatenate([jnp.full((1, TQ), (sink_ref[8 * g + 2 * p + hf] - far_of(8 * g + 2 * p + hf)) * LOG2E,
                                            F32) for p in range(PAIRS)], axis=1)
            return jnp.exp2(row - m)

        o_swa = finish(g, SWA, sink_term)
        o_slc = finish(g, SLC)
        o_win = finish(g, WIN)
        for p in range(PAIRS):
            cols = slice((PAIRS * g + p) * LANES, (PAIRS * g + p + 1) * LANES)
            oa_ref[0, :, cols] = to_rows(o_swa, p).astype(oa_ref.dtype)
            qs = slice(p * TQ, (p + 1) * TQ)
            gate = lambda br: gates_t[br * B_HEADS * HEAD_DIM + cols.start:br * B_HEADS * HEAD_DIM + cols.stop]
            o_t = gate(0) * o_cmp[g][:, qs] + gate(1) * o_slc[:, qs] + gate(2) * o_win[:, qs]
            ob_ref[0, :, cols] = o_t.T.astype(ob_ref.dtype)


def _even_attention(pe, kvc, rel_bias, sinks, ta, tb, up, cbias, e3, ovt):
    b, s, _ = pe.shape
    n_tiles = s // TQ
    width = GROUPS * PAIRS * LANES

    def seg(k):
        return pl.BlockSpec((1, s, LANES), lambda bi, i, k=k: (bi, 0, k))

    full = lambda a: pl.BlockSpec(a.shape, lambda bi, i: (0,) * a.ndim)
    in_specs = [
        pl.BlockSpec(memory_space=pltpu.SMEM),
        pl.BlockSpec(memory_space=pltpu.SMEM),
        pl.BlockSpec((1, TQ, width), lambda bi, i: (bi, i, COL_QA // width)),
        pl.BlockSpec((1, TQ, width), lambda bi, i: (bi, i, COL_QB // width)),
        pl.BlockSpec((1, TQ, LANES), lambda bi, i: (bi, i, SEG_GATE)),
        seg(SEG_KA), seg(SEG_VA), seg(SEG_KS), seg(SEG_VS), seg(SEG_KW), seg(SEG_VW),
        pl.BlockSpec((1, 1, TQ, LANES), lambda bi, i: (0, bi, 0, 0)),
        pl.BlockSpec((1, 1, TQ, LANES), lambda bi, i: (1, bi, 0, 0)),
        full(ta), full(tb), full(up),
        pl.BlockSpec((B_HEADS, TQ, TQ), lambda bi, i: (0, 0, i)),
        full(e3), full(ovt),
    ]
    out_spec = pl.BlockSpec((1, TQ, width), lambda bi, i: (bi, i, 0))
    return pl.pallas_call(
        functools.partial(_even_attn_kernel, n_tiles=n_tiles),
        grid=(b, n_tiles),
        in_specs=in_specs,
        out_specs=[out_spec, out_spec],
        out_shape=[jax.ShapeDtypeStruct((b, s, width), BF16)] * 2,
        scratch_shapes=[pltpu.VMEM((GROUPS, 3, n_tiles, 2 * TQ, LANES), BF16),
                        pltpu.VMEM((GROUPS, 3, n_tiles, VT_ROWS, 2 * TQ), BF16),
                        pltpu.VMEM((GROUPS, 2 * TQ, LANES), BF16),
                        pltpu.VMEM((GROUPS, LANES, 2 * TQ), BF16),
                        pltpu.VMEM((GROUPS, s // SLC_BLOCK, TQ), F32),
                        pltpu.VMEM((GROUPS, 3, VT_ROWS, W4), F32),
                        pltpu.VMEM((GROUPS, 3, 2, 1, W4), F32)],
        compiler_params=_params("arbitrary", "arbitrary"),
    )(rel_bias, sinks, pe, pe, pe, pe, pe, pe, pe, pe, pe, kvc, kvc, ta, tb, up, cbias, e3, ovt)


def _mla_prep_kernel(x_ref, g_ref, win_ref, qn_ref, kvn_ref, wqt_ref, wk_ref, wvt_ref,
                     cos_ref, sin_ref, cost_ref, sint_ref, qt_ref, k_ref, vt_ref):
    xn = _rms(x_ref[...], g_ref[...]).astype(BF16)
    proj = _dot(xn, win_ref[...])
    cq = _rms(proj[:, :Q_LORA], qn_ref[...])
    ckv = _rms(proj[:, Q_LORA:Q_LORA + KV_LORA], kvn_ref[...])
    cq_t = cq.T.astype(BF16)
    ckv_t = ckv.T.astype(BF16)
    ckv = ckv.astype(BF16)
    kr = proj[:, Q_LORA + KV_LORA:]
    kr = (kr * cos_ref[...] + pltpu.roll(kr, HEAD_DIM, 1) * sin_ref[...]).astype(BF16)
    cos_t = cost_ref[...]
    sin_t = sint_ref[...]
    scale = (NOPE_DIM + ROPE_DIM) ** -0.5 * LOG2E
    tm = ckv_t.shape[1]
    ones_rows = jnp.where(lax.broadcasted_iota(jnp.int32, (VT_ROWS - V_DIM, tm), 0) == 0, 1.0, 0.0).astype(BF16)
    for h in range(C_HEADS):
        q_t = _dot(wqt_ref[h], cq_t)
        rp = q_t[NOPE_DIM:]
        rp = rp * cos_t + pltpu.roll(rp, HEAD_DIM, 0) * sin_t
        qt_ref[0, h, 0, 0:NOPE_DIM, :] = (q_t[:NOPE_DIM] * scale).astype(BF16)
        qt_ref[0, h, 0, NOPE_DIM:, :] = (rp * scale).astype(BF16)
        if h % 2 == 0:
            k2 = _dot(ckv, wk_ref[:, h * LANES:(h + 2) * LANES]).astype(BF16)
        k_ref[:, h * 256:h * 256 + LANES] = k2[:, (h % 2) * LANES:(h % 2 + 1) * LANES]
        k_ref[:, h * 256 + LANES:(h + 1) * 256] = kr
        vt_ref[0, h, 0, 0:V_DIM, :] = _dot(wvt_ref[h], ckv_t).astype(BF16)
        vt_ref[0, h, 0, V_DIM:, :] = ones_rows


def _mla_prep(x, g, w_in, qn, kvn, wqt, wk, wvt, cos, sin, b, s):
    m, d = x.shape
    tm = MLA_TK
    pos_tiles = s // tm
    full = lambda a: pl.BlockSpec(a.shape, lambda i: (0,) * a.ndim, pipeline_mode=pl.Buffered(1))
    tile_t = lambda rows: pl.BlockSpec((1, C_HEADS, 1, rows, tm), lambda i: (i // pos_tiles, 0, i % pos_tiles, 0, 0))
    return pl.pallas_call(
        _mla_prep_kernel,
        grid=(m // tm,),
        in_specs=[pl.BlockSpec((tm, d), lambda i: (i, 0)),
                  pl.BlockSpec((1, d), lambda i: (0, 0)),
                  full(w_in), pl.BlockSpec((1, Q_LORA), lambda i: (0, 0)),
                  pl.BlockSpec((1, KV_LORA), lambda i: (0, 0)), full(wqt), full(wk), full(wvt),
                  pl.BlockSpec((tm, LANES), lambda i: (i % pos_tiles, 0)),
                  pl.BlockSpec((tm, LANES), lambda i: (i % pos_tiles, 0)),
                  pl.BlockSpec((LANES, tm), lambda i: (0, i % pos_tiles)),
                  pl.BlockSpec((LANES, tm), lambda i: (0, i % pos_tiles))],
        out_specs=[tile_t(256),
                   pl.BlockSpec((tm, C_HEADS * 256), lambda i: (i, 0)),
                   tile_t(VT_ROWS)],
        out_shape=[jax.ShapeDtypeStruct((b, C_HEADS, pos_tiles, 256, tm), BF16),
                   jax.ShapeDtypeStruct((m, C_HEADS * 256), BF16),
                   jax.ShapeDtypeStruct((b, C_HEADS, pos_tiles, VT_ROWS, tm), BF16)],
        compiler_params=_params("parallel"),
    )(x, g.reshape(1, d), w_in, qn.reshape(1, -1), kvn.reshape(1, -1), wqt, wk, wvt, cos, sin, cos.T, sin.T)


def _mla_attn_kernel(qt_ref, k_ref, vt_ref, o_ref, s_scr, acc_scr, m_scr, *, n_q):
    tq = tk = MLA_TQ
    parts = tq // MLA_TK
    ki = lax.broadcasted_iota(jnp.int32, (tk, tq), 0)
    qi = lax.broadcasted_iota(jnp.int32, (tk, tq), 1)
    causal = ki <= qi
    heads = range(MLA_HEADS)

    def q_tile(t, c):
        q_ts = [jnp.concatenate([qt_ref[0, h, parts * t + d] for d in range(parts)], axis=1) for h in heads]

        def logits_to(slot, j):
            rows = pl.ds(pl.multiple_of(j * tk, tk), tk)
            for h in heads:
                s_scr[h, slot] = _dot(k_ref[0, rows, h * 256:(h + 1) * 256], q_ts[h])

        def update_from(slot, j, diagonal):
            for h in heads:
                s = s_scr[h, slot]
                if diagonal:
                    s = jnp.where(causal, s, NEG)
                m = m_scr[h]
                mn = jnp.maximum(m, jnp.max(s, axis=0, keepdims=True))
                p = jnp.exp2(s - mn).astype(BF16)
                m_scr[h] = mn
                v_t = jnp.concatenate([vt_ref[0, h, parts * j + d] for d in range(parts)], axis=1)
                acc_scr[h] = jnp.exp2(m - mn) * acc_scr[h] + _dot(v_t, p)

        m_scr[...] = jnp.full(m_scr.shape, NEG, F32)
        acc_scr[...] = jnp.zeros(acc_scr.shape, F32)
        logits_to(0, t)
        logits_to(1, 0)
        update_from(0, t, True)

        def body(jj, c2):
            logits_to(0, jnp.minimum(2 * jj + 1, t - 1))
            update_from(1, 2 * jj, False)
            logits_to(1, jnp.minimum(2 * jj + 2, t - 1))

            @pl.when(2 * jj + 1 < t)
            def _():
                update_from(0, 2 * jj + 1, False)
            return c2

        lax.fori_loop(0, (t + 1) // 2, body, 0)
        rows = pl.ds(pl.multiple_of(t * tq, tq), tq)
        for h in heads:
            acc = acc_scr[h]
            o = acc[:V_DIM] * (1.0 / acc[V_DIM:V_DIM + 1])
            o_ref[0, rows, h * LANES:(h + 1) * LANES] = o.T.astype(o_ref.dtype)
        return c

    lax.fori_loop(0, n_q, q_tile, 0)


def _mla_attention(qt, k, vt):
    b, s, _ = k.shape
    n_k = s // MLA_TK
    nh = MLA_HEADS
    return pl.pallas_call(
        functools.partial(_mla_attn_kernel, n_q=s // MLA_TQ),
        grid=(b, C_HEADS // nh),
        in_specs=[pl.BlockSpec((1, nh, n_k, 256, MLA_TK), lambda bi, h: (bi, h, 0, 0, 0)),
                  pl.BlockSpec((1, s, 256 * nh), lambda bi, h: (bi, 0, h)),
                  pl.BlockSpec((1, nh, n_k, VT_ROWS, MLA_TK), lambda bi, h: (bi, h, 0, 0, 0))],
        out_specs=pl.BlockSpec((1, s, LANES * nh), lambda bi, h: (bi, 0, h)),
        out_shape=jax.ShapeDtypeStruct((b, s, C_HEADS * V_DIM), BF16),
        scratch_shapes=[pltpu.VMEM((nh, 2, MLA_TQ, MLA_TQ), F32),
                        pltpu.VMEM((nh, VT_ROWS, MLA_TQ), F32),
                        pltpu.VMEM((nh, 1, MLA_TQ), F32)],
        compiler_params=_params("parallel", "arbitrary"),
    )(qt, k, vt)


def _even_in_weight(w):
    cuts = np.cumsum([1024, 128, 128, 1024, 128, 128, 128, 128, 128, 128, 48])[:-1]
    qa, ka, va, qb, kc, vc, ks, vs, kw, vw, gate = jnp.split(w, [int(c) for c in cuts], axis=1)
    gate = jnp.pad(gate, ((0, 0), (0, LANES - gate.shape[1])))
    return jnp.concatenate([qa, qb, ka, va, kc, vc, ks, vs, kw, vw, gate], axis=1).astype(BF16)


def _compress_weights(pos, w1, w2):
    half = CMP_BLOCK // 2
    eye = jnp.eye(GROUPS, dtype=w1.dtype)
    w1r = w1.reshape(2, half, HEAD_DIM, CMP_HIDDEN)
    wab = jnp.einsum('rldh,gk->rlgdkh', w1r, eye).reshape(2, half * GROUPS * HEAD_DIM, GROUPS * CMP_HIDDEN)
    posr = jnp.broadcast_to(pos.reshape(2, half, 1, HEAD_DIM), (2, half, GROUPS, HEAD_DIM)).reshape(2, 1, -1)
    w2bd = jnp.einsum('hd,gk->ghkd', w2, eye).reshape(GROUPS * CMP_HIDDEN, GROUPS * HEAD_DIM)
    return posr[0], posr[1], wab[0].astype(BF16), wab[1].astype(BF16), w2bd.astype(BF16)


def _rope_chunk_cols(w):
    z = jnp.zeros(w.shape[:-1] + (ROPE_DIM // 2,), w.dtype)
    return jnp.concatenate([w[..., :ROPE_DIM // 2], z, w[..., ROPE_DIM // 2:], z], axis=-1)


def _mla_weights(w_in, w_q_up, w_kv_up):
    w_in2 = jnp.concatenate([w_in[:, :Q_LORA + KV_LORA], _rope_chunk_cols(w_in[:, Q_LORA + KV_LORA:])], axis=1)
    wq = w_q_up.reshape(Q_LORA, C_HEADS, NOPE_DIM + ROPE_DIM)
    wq2 = jnp.concatenate([wq[..., :NOPE_DIM], _rope_chunk_cols(wq[..., NOPE_DIM:])], axis=-1)
    wqt = wq2.transpose(1, 2, 0)
    wkv = w_kv_up.reshape(KV_LORA, C_HEADS, NOPE_DIM + V_DIM)
    wk = wkv[..., :NOPE_DIM].reshape(KV_LORA, C_HEADS * NOPE_DIM)
    wvt = wkv[..., NOPE_DIM:].transpose(1, 2, 0)
    return w_in2.astype(BF16), wqt.astype(BF16), wk.astype(BF16), wvt.astype(BF16)


def _rope_tables(s):
    inv = 1.0 / (ROPE_THETA ** (jnp.arange(0, ROPE_DIM, 2, dtype=F32) / ROPE_DIM))
    ang = jnp.arange(s, dtype=F32)[:, None] * inv[None]
    cos, sin = jnp.cos(ang), jnp.sin(ang)
    z = jnp.zeros_like(cos)
    return jnp.concatenate([cos, z, cos, z], axis=1), jnp.concatenate([-sin, z, sin, z], axis=1)


def _static_tables(s):
    k = np.arange(TQ)[:, None]
    q = np.arange(TQ)[None, :]
    diag = np.where(k <= q, _bucket_np(q - k), -1)
    prev = _bucket_np(q - k + TQ)
    near_swa = np.stack([diag, np.where(k > q, prev, -1)])
    near_nsa = np.stack([diag, prev])
    upper = np.where(k > q, 0.0, NEG).astype(np.float32)
    c = np.arange(LANES)[:, None]
    t = np.arange(s)[None, :]
    cdist = t - (c * CMP_STRIDE + CMP_BLOCK - 1)
    cmp_map = np.where(cdist >= 0, _bucket_np(cdist), -1)
    ns = s // SLC_BLOCK
    nc = (s - CMP_BLOCK) // CMP_STRIDE + 1
    c_start = np.arange(LANES) * CMP_STRIDE
    s_start = np.arange(ns) * SLC_BLOCK
    ovt = ((c_start[None, :] <= s_start[:, None] + SLC_BLOCK - 1) &
           (c_start[None, :] + CMP_BLOCK - 1 >= s_start[:, None]) &
           (np.arange(LANES)[None, :] < nc)).astype(np.float32)
    e3 = np.zeros((3, B_HEADS * HEAD_DIM, LANES), np.float32)
    for h in range(B_HEADS):
        for br in range(3):
            e3[br, h * HEAD_DIM:(h + 1) * HEAD_DIM, h * 3 + br] = 1.0
    e3 = e3.reshape(3 * B_HEADS * HEAD_DIM, LANES)
    return near_swa, near_nsa, upper, cmp_map, ovt, e3


def _even_layer(x, b, s, rel_bias, norm, w_in, sinks, pos_k, pos_v, k_w1, k_w2, v_w1, v_w2, w_out):
    near_swa, near_nsa, upper, cmp_map, ovt, e3 = _static_tables(s)
    ta = _bias_tiles(rel_bias, near_swa, 0, A_HEADS)
    tb = _bias_tiles(rel_bias, near_nsa, A_HEADS, B_HEADS)
    cbias = _bias_tiles(rel_bias, cmp_map, A_HEADS, B_HEADS)

    pe = _norm_matmul(x, norm, _even_in_weight(w_in), 1024, EVEN_COLS, BF16).reshape(b, s, EVEN_COLS)

    rows = s // (CMP_BLOCK // 2)
    hkv = jnp.stack([pe[:, :, SEG_KC * LANES:(SEG_KC + 1) * LANES],
                     pe[:, :, SEG_VC * LANES:(SEG_VC + 1) * LANES]]).reshape(2, b, rows, -1)
    ck = _compress_weights(pos_k, k_w1, k_w2)
    cv = _compress_weights(pos_v, v_w1, v_w2)
    kvc = _compress(hkv, *[jnp.stack([a, c]) for a, c in zip(ck, cv)])
    assert rows <= TQ
    kvc = jnp.pad(kvc, ((0, 0), (0, 0), (0, TQ - rows), (0, 0)))

    oa, ob = _even_attention(pe, kvc, rel_bias, sinks, ta, tb, jnp.asarray(upper), cbias,
                             jnp.asarray(e3, BF16), jnp.asarray(ovt, BF16))
    w_out = w_out.astype(BF16)
    n_a = A_HEADS * HEAD_DIM
    return _out_proj(x, [oa.reshape(b * s, -1), ob.reshape(b * s, -1)], [w_out[:n_a], w_out[n_a:]], 512)


def _odd_layer(x, b, s, norm, w_in, q_norm, w_q_up, kv_norm, w_kv_up, w_out):
    w_in2, wqt, wk, wvt = _mla_weights(w_in, w_q_up, w_kv_up)
    cos, sin = _rope_tables(s)
    qt, k, vt = _mla_prep(x, norm, w_in2, q_norm, kv_norm, wqt, wk, wvt, cos, sin, b, s)
    o = _mla_attention(qt, k.reshape(b, s, -1), vt)
    return _out_proj(x, [o.reshape(b * s, -1)], [w_out.astype(BF16)], 512)


def kernel(x, rel_bias, norm_mix_e, w_in_e, sinks, cmp_pos_k, cmp_pos_v, cmp_k_w1, cmp_k_w2, cmp_v_w1, cmp_v_w2, w_out_e, norm_mix_o, w_in_o, q_norm, w_q_up, kv_norm, w_kv_up, w_out_o, norm_mlp, w_up, w_down, norm_final):
    b, s, d = x.shape
    depth = norm_mlp.shape[0]
    h = x.reshape(b * s, d)
    for layer in range(depth):
        i = layer // 2
        if layer % 2 == 0:
            h = _even_layer(h, b, s, rel_bias, norm_mix_e[i], w_in_e[i], sinks[i], cmp_pos_k[i], cmp_pos_v[i],
                            cmp_k_w1[i], cmp_k_w2[i], cmp_v_w1[i], cmp_v_w2[i], w_out_e[i])
        else:
            h = _odd_layer(h, b, s, norm_mix_o[i], w_in_o[i], q_norm[i], w_q_up[i], kv_norm[i], w_kv_up[i],
                           w_out_o[i])
        h = _mlp(h, norm_mlp[layer], w_up, w_down, layer, norm_final, 1024, 512, layer == depth - 1)
    return h.reshape(b, s, d)
```

```python
import functools
import math

import numpy as np
import jax
import jax.numpy as jnp
from jax import lax
from jax.experimental import pallas as pl
from jax.experimental.pallas import tpu as pltpu

F32 = jnp.float32
BF16 = jnp.bfloat16

LANES = 128
VMEM_LIMIT_BYTES = 56 * 1024 * 1024

EPS = 1e-6
NEG = -1e30
BIG = 1e4
HEAD_DIM = 64
TQ = 128
MLA_TQ = 512
MLA_TK = 512
MLA_HEADS = 4
IN_PROJ_TM = 1024
OUT_PROJ_TM = 512
MLP_TM, MLP_TF = 1024, 512
N_BUCKETS = 32
MAX_DISTANCE = 128
A_HEADS = 16
B_HEADS = 16
GROUPS = 2
PAIRS = 4
W4 = PAIRS * TQ
CMP_BLOCK = 32
CMP_STRIDE = 16
CMP_HIDDEN = 256
SLC_BLOCK = 64
SLC_TOPK = 8
A_WINDOW = 128
B_WINDOW = 512
C_HEADS = 16
Q_LORA = 768
KV_LORA = 512
NOPE_DIM = 128
ROPE_DIM = 64
V_DIM = 128
MLA_QK = NOPE_DIM + LANES
VT_ROWS = V_DIM + 16
LOG2E = math.log2(math.e)
ROPE_THETA = 10000.0

COL_QA, COL_QB = 0, 1024
SEG_KA, SEG_VA, SEG_KC, SEG_VC, SEG_KS, SEG_VS, SEG_KW, SEG_VW, SEG_GATE = range(16, 25)
EVEN_COLS = 25 * LANES
K_A, K_S, K_W = 0, 1, 2


def _params(*sem):
    return pltpu.CompilerParams(dimension_semantics=sem, vmem_limit_bytes=VMEM_LIMIT_BYTES)


def _rms(x, g):
    return x * lax.rsqrt(jnp.mean(x * x, axis=-1, keepdims=True) + EPS) * g


def _dot(a, b):
    return jnp.dot(a, b, preferred_element_type=F32)


def _run_pipelined(tasks):
    s = tasks[0][0]()
    for n, (_, apply_fn) in enumerate(tasks):
        s_next = tasks[n + 1][0]() if n + 1 < len(tasks) else None
        apply_fn(s)
        s = s_next


def _bucket_np(dist):
    dist = np.maximum(dist, 0)
    max_exact = N_BUCKETS // 2
    d = np.maximum(dist, 1).astype(np.float32)
    large = max_exact + (np.log(d / np.float32(max_exact)) / np.float32(math.log(MAX_DISTANCE / max_exact))
                         * np.float32(N_BUCKETS - max_exact)).astype(np.int32)
    large = np.minimum(large, N_BUCKETS - 1)
    return np.where(dist < max_exact, dist, large).astype(np.int32)


def _bias_kernel(tab_ref, bm_ref, o_ref, *, head0):
    h = pl.program_id(0) + head0
    bm = bm_ref[...]
    acc = jnp.zeros(bm.shape, F32)
    for b in range(N_BUCKETS):
        acc = jnp.where(bm == b, tab_ref[b, h], acc)
    o_ref[0] = jnp.where(bm < 0, NEG, (acc - tab_ref[N_BUCKETS - 1, h]) * LOG2E)


def _bias_tiles(rel_bias, bucket_map, head0, n_heads):
    shp = bucket_map.shape
    nd = len(shp)
    return pl.pallas_call(
        functools.partial(_bias_kernel, head0=head0),
        grid=(n_heads,),
        in_specs=[pl.BlockSpec(memory_space=pltpu.SMEM),
                  pl.BlockSpec(shp, lambda h: (0,) * nd)],
        out_specs=pl.BlockSpec((1,) + shp, lambda h: (h,) + (0,) * nd),
        out_shape=jax.ShapeDtypeStruct((n_heads,) + shp, F32),
        compiler_params=_params("arbitrary"),
    )(rel_bias, jnp.asarray(bucket_map))


def _norm_matmul_kernel(x_ref, g_ref, w_ref, o_ref, xn_ref):
    @pl.when(pl.program_id(1) == 0)
    def _():
        xn_ref[...] = _rms(x_ref[...], g_ref[...]).astype(BF16)

    o_ref[...] = _dot(xn_ref[...], w_ref[...]).astype(o_ref.dtype)


def _norm_matmul(x, g, w, tm, tn, out_dtype):
    m, d = x.shape
    n = w.shape[1]
    return pl.pallas_call(
        _norm_matmul_kernel,
        grid=(m // tm, n // tn),
        in_specs=[pl.BlockSpec((tm, d), lambda i, j: (i, 0)),
                  pl.BlockSpec((1, d), lambda i, j: (0, 0)),
                  pl.BlockSpec((d, tn), lambda i, j: (0, j),
                               pipeline_mode=pl.Buffered(1) if tn == n else None)],
        out_specs=pl.BlockSpec((tm, tn), lambda i, j: (i, j)),
        out_shape=jax.ShapeDtypeStruct((m, n), out_dtype),
        scratch_shapes=[pltpu.VMEM((tm, d), BF16)],
        compiler_params=_params("parallel", "arbitrary"),
    )(x, g.reshape(1, d), w)


def _out_proj_kernel(*refs, n_in):
    x_ref = refs[0]
    o_ref = refs[2 * n_in + 1]
    acc = x_ref[...]
    for t in range(n_in):
        acc = acc + _dot(refs[1 + t][...], refs[1 + n_in + t][...])
    o_ref[...] = acc


def _out_proj(x, acts, ws, tm):
    m, d = x.shape
    n_in = len(acts)
    in_specs = [pl.BlockSpec((tm, d), lambda i: (i, 0))]
    in_specs += [pl.BlockSpec((tm, a.shape[1]), lambda i: (i, 0)) for a in acts]
    in_specs += [pl.BlockSpec(w.shape, lambda i: (0, 0), pipeline_mode=pl.Buffered(1)) for w in ws]
    return pl.pallas_call(
        functools.partial(_out_proj_kernel, n_in=n_in),
        grid=(m // tm,),
        in_specs=in_specs,
        out_specs=pl.BlockSpec((tm, d), lambda i: (i, 0)),
        out_shape=jax.ShapeDtypeStruct((m, d), F32),
        compiler_params=_params("parallel"),
    )(x, *acts, *ws)


def _mlp_kernel(x_ref, g_ref, wu_ref, wd_ref, gf_ref, o_ref, xn_ref, *, final_norm):
    j = pl.program_id(1)

    @pl.when(j == 0)
    def _():
        x = x_ref[...]
        xn_ref[...] = _rms(x, g_ref[...]).astype(BF16)
        o_ref[...] = x

    h = _dot(xn_ref[...], wu_ref[0].astype(BF16))
    a = jnp.square(jnp.maximum(h, 0.0)).astype(BF16)
    o_ref[...] += _dot(a, wd_ref[0].astype(BF16))

    if final_norm:
        @pl.when(j == pl.num_programs(1) - 1)
        def _():
            o_ref[...] = _rms(o_ref[...], gf_ref[...])


def _mlp(x, g, w_up, w_down, layer, g_final, tm, tf, final_norm):
    m, d = x.shape
    ff = w_up.shape[2]
    return pl.pallas_call(
        functools.partial(_mlp_kernel, final_norm=final_norm),
        grid=(m // tm, ff // tf),
        in_specs=[pl.BlockSpec((tm, d), lambda i, j: (i, 0), pipeline_mode=pl.Buffered(1)),
                  pl.BlockSpec((1, d), lambda i, j: (0, 0)),
                  pl.BlockSpec((1, d, tf), lambda i, j: (layer, 0, j)),
                  pl.BlockSpec((1, tf, d), lambda i, j: (layer, j, 0)),
                  pl.BlockSpec((1, d), lambda i, j: (0, 0))],
        out_specs=pl.BlockSpec((tm, d), lambda i, j: (i, 0)),
        out_shape=jax.ShapeDtypeStruct((m, d), F32),
        scratch_shapes=[pltpu.VMEM((tm, d), BF16)],
        compiler_params=_params("parallel", "arbitrary"),
    )(x, g.reshape(1, d), w_up, w_down, g_final.reshape(1, d))


def _gelu_tanh(x):
    return 0.5 * x * (1.0 + jnp.tanh(math.sqrt(2.0 / math.pi) * (x + 0.044715 * (x * x * x))))


def _compress_kernel(h_ref, pa_ref, pb_ref, wa_ref, wb_ref, w2_ref, o_ref):
    h = h_ref[0, 0].astype(F32)
    ha = _dot((h + pa_ref[0]).astype(BF16), wa_ref[0])
    hb = _dot((h + pb_ref[0]).astype(BF16), wb_ref[0])
    n = hb.shape[0]
    pre = ha + pltpu.roll(hb, n - 1, 0)
    o_ref[0, 0] = _dot(_gelu_tanh(pre).astype(BF16), w2_ref[0]).astype(o_ref.dtype)


def _compress(hkv, pa, pb, wa, wb, w2):
    _, b, r, w = hkv.shape
    hid2 = wa.shape[2]
    return pl.pallas_call(
        _compress_kernel,
        grid=(2, b),
        in_specs=[pl.BlockSpec((1, 1, r, w), lambda t, i: (t, i, 0, 0)),
                  pl.BlockSpec((1, 1, w), lambda t, i: (t, 0, 0)),
                  pl.BlockSpec((1, 1, w), lambda t, i: (t, 0, 0)),
                  pl.BlockSpec((1, w, hid2), lambda t, i: (t, 0, 0)),
                  pl.BlockSpec((1, w, hid2), lambda t, i: (t, 0, 0)),
                  pl.BlockSpec((1, hid2, LANES), lambda t, i: (t, 0, 0))],
        out_specs=pl.BlockSpec((1, 1, r, LANES), lambda t, i: (t, i, 0, 0)),
        out_shape=jax.ShapeDtypeStruct((2, b, r, LANES), BF16),
        compiler_params=_params("arbitrary", "arbitrary"),
    )(hkv, pa, pb, wa, wb, w2)


SWA, SLC, WIN = 0, 1, 2


def _stack_pairs_t(q_ref, g):
    cols = [q_ref[0, :, (PAIRS * g + p) * LANES:(PAIRS * g + p + 1) * LANES].astype(F32).T for p in range(PAIRS)]
    return (jnp.concatenate(cols, axis=1) * (HEAD_DIM ** -0.5 * LOG2E)).astype(BF16)


def _rows_by_half(a, b, rows):
    parts = [jnp.broadcast_to(a, (HEAD_DIM, W4)), jnp.broadcast_to(b, (HEAD_DIM, W4))]
    if rows > 2 * HEAD_DIM:
        first = lax.broadcasted_iota(jnp.int32, (rows - 2 * HEAD_DIM, W4), 0) == 0
        parts.append(jnp.where(first, a, b))
    return jnp.concatenate(parts, axis=0)


def _even_attn_kernel(tab_ref, sink_ref, qa_ref, qb_ref, gate_ref,
                      ka_ref, va_ref, ks_ref, vs_ref, kw_ref, vw_ref, kc_ref, vc_ref,
                      ta_ref, tb_ref, up_ref, cb_ref, e3_ref, ovt_ref,
                      oa_ref, ob_ref,
                      kbd_ref, vbd_ref, ckbd_ref, cvbd_ref, msk_ref, acc_scr, m_scr, *, n_tiles):
    i = pl.program_id(1)
    groups = range(GROUPS)

    def block_diag(x, g):
        r = pltpu.roll(x, HEAD_DIM, 1)
        lo = lax.broadcasted_iota(jnp.int32, x.shape, 1) < HEAD_DIM
        own, other = (x, r) if g == 0 else (r, x)
        return jnp.where(lo, own, 0.0), jnp.where(lo, 0.0, other)

    @pl.when(i == 0)
    def _build():
        r16 = lax.broadcasted_iota(jnp.int32, (VT_ROWS - LANES, 2 * TQ), 0)
        c16 = lax.broadcasted_iota(jnp.int32, (VT_ROWS - LANES, 2 * TQ), 1)
        ones_rows = jnp.where((r16 == 0) & (c16 < TQ) | (r16 == 1) & (c16 >= TQ), 1.0, 0.0).astype(BF16)
        for t, ref in enumerate((ka_ref, ks_ref, kw_ref)):
            def kbody(j, c, t=t, ref=ref):
                x = ref[0, pl.ds(pl.multiple_of(j * TQ, TQ), TQ), :].astype(F32)
                for g in groups:
                    top, bot = block_diag(x, g)
                    kbd_ref[g, t, j, 0:TQ, :] = top.astype(BF16)
                    kbd_ref[g, t, j, TQ:2 * TQ, :] = bot.astype(BF16)
                return c
            lax.fori_loop(0, n_tiles, kbody, 0)
        for t, ref in enumerate((va_ref, vs_ref, vw_ref)):
            def vbody(j, c, t=t, ref=ref):
                x = ref[0, pl.ds(pl.multiple_of(j * TQ, TQ), TQ), :].astype(F32)
                for g in groups:
                    top, bot = block_diag(x, g)
                    vbd_ref[g, t, j, 0:LANES, 0:TQ] = top.T.astype(BF16)
                    vbd_ref[g, t, j, 0:LANES, TQ:2 * TQ] = bot.T.astype(BF16)
                    vbd_ref[g, t, j, LANES:, :] = ones_rows
                return c
            lax.fori_loop(0, n_tiles, vbody, 0)
        xk = kc_ref[0, 0].astype(F32)
        xv = vc_ref[0, 0].astype(F32)
        for g in groups:
            top, bot = block_diag(xk, g)
            ckbd_ref[g, 0:TQ, :] = top.astype(BF16)
            ckbd_ref[g, TQ:2 * TQ, :] = bot.astype(BF16)
            top, bot = block_diag(xv, g)
            cvbd_ref[g, :, 0:TQ] = top.T.astype(BF16)
            cvbd_ref[g, :, TQ:2 * TQ] = bot.T.astype(BF16)

    def near_bias(t_ref, g, delta):
        return lambda hf, p: t_ref[8 * g + 2 * p + hf, delta]

    def update(g, st, s, adds, vbd_t, first):
        m_old = [None if first else m_scr[g, st, hf] for hf in range(2)]
        p_rows, m_new = [], []
        for hf in range(2):
            strips, mns = [], []
            for p in range(PAIRS):
                sh = s[hf * TQ:(hf + 1) * TQ, p * TQ:(p + 1) * TQ]
                for a in adds:
                    sh = sh + a(hf, p)
                mn = jnp.max(sh, axis=0, keepdims=True)
                if not first:
                    mn = jnp.maximum(m_old[hf][:, p * TQ:(p + 1) * TQ], mn)
                strips.append(jnp.exp2(sh - mn).astype(BF16))
                mns.append(mn)
            p_rows.append(jnp.concatenate(strips, axis=1))
            m_new.append(jnp.concatenate(mns, axis=1))
            m_scr[g, st, hf] = m_new[hf]
        pv = _dot(vbd_t, jnp.concatenate(p_rows, axis=0))
        if first:
            acc_scr[g, st] = pv
        else:
            alpha = _rows_by_half(jnp.exp2(m_old[0] - m_new[0]), jnp.exp2(m_old[1] - m_new[1]), VT_ROWS)
            acc_scr[g, st] = alpha * acc_scr[g, st] + pv

    run_pipelined = _run_pipelined

    def tile_task(g, st, slot, j, q4t, adds, first=False):
        return (lambda: _dot(kbd_ref[g, slot, j], q4t),
                lambda s: update(g, st, s, adds, vbd_ref[g, slot, j], first))

    def finish(g, st, extra=None):
        acc = acc_scr[g, st]
        den = [acc[LANES + hf:LANES + hf + 1] for hf in range(2)]
        if extra is not None:
            den = [den[hf] + extra(hf, m_scr[g, st, hf]) for hf in range(2)]
        return acc[:LANES] * _rows_by_half(1.0 / den[0], 1.0 / den[1], LANES)

    q4a = [_stack_pairs_t(qa_ref, g) for g in groups]
    q4b = [_stack_pairs_t(qb_ref, g) for g in groups]

    run_pipelined([tile_task(g, SWA, K_A, i, q4a[g], [near_bias(ta_ref, g, 0)], first=True) for g in groups] +
                  [tile_task(g, WIN, K_W, i, q4b[g], [near_bias(tb_ref, g, 0)], first=True) for g in groups])

    t_q = i * TQ + lax.broadcasted_iota(jnp.int32, (1, TQ), 1)
    anyvis = jnp.where(t_q >= CMP_BLOCK - 1, 1.0, 0.0)
    s_cmp = [_dot(ckbd_ref[g], q4b[g]) for g in groups]
    psum = [jnp.zeros((TQ, TQ), F32) for _ in groups]
    p_rows = [[] for _ in groups]
    for hf in range(2):
        strips = [[] for _ in groups]
        for p in range(PAIRS):
            for g in groups:
                sh = s_cmp[g][hf * TQ:(hf + 1) * TQ, p * TQ:(p + 1) * TQ] + cb_ref[8 * g + 2 * p + hf]
                pe = jnp.exp2(sh - jnp.max(sh, axis=0, keepdims=True))
                pc = pe * (anyvis / jnp.sum(pe, axis=0, keepdims=True))
                psum[g] = psum[g] + pc
                strips[g].append(pc.astype(BF16))
        for g in groups:
            p_rows[g].append(jnp.concatenate(strips[g], axis=1))
    o_cmp = [_dot(cvbd_ref[g], jnp.concatenate(p_rows[g], axis=0)) for g in groups]

    psum2 = jnp.concatenate(psum, axis=1)
    ph = psum2.astype(BF16)
    r1 = psum2 - ph.astype(F32)
    pm = r1.astype(BF16)
    pl_ = (r1 - pm.astype(F32)).astype(BF16)
    ovt = ovt_ref[...]
    pslc = _dot(ovt, ph) + _dot(ovt, pm) + _dot(ovt, pl_)
    ns = pslc.shape[0]
    nb = lax.broadcasted_iota(jnp.int32, (ns, GROUPS * TQ), 0)
    tq = i * TQ + (lax.broadcasted_iota(jnp.int32, (ns, GROUPS * TQ), 1) & (TQ - 1))
    cur = tq // SLC_BLOCK
    forced = (jnp.where(nb == 0, 1.0, 0.0) + jnp.where(nb == cur, 1.0, 0.0) +
              jnp.where(nb == cur - 1, 1.0, 0.0))
    score = jnp.where(nb * SLC_BLOCK > tq, NEG, jnp.where(forced > 0.0, BIG, pslc))
    rank = jnp.zeros((ns, GROUPS * TQ), F32)
    for mth in range(ns):
        sm = score[mth:mth + 1, :]
        tie = jnp.where(nb > mth, 1.0, 0.0)
        rank = rank + jnp.where(sm > score, 1.0, jnp.where(sm == score, tie, 0.0))
    sel_rows = jnp.where(rank < SLC_TOPK, 0.0, NEG)
    for g in groups:
        msk_ref[g] = sel_rows[:, g * TQ:(g + 1) * TQ]

    def sel_add(g, j):
        cache = []

        def tile(hf, p):
            if not cache:
                rows = [jnp.broadcast_to(msk_ref[g, pl.ds(2 * j + e, 1), :], (SLC_BLOCK, TQ)) for e in range(2)]
                cache.append(jnp.concatenate(rows, axis=0))
            return cache[0]
        return tile

    n_full = B_WINDOW // TQ

    def near_tiles(n_back):
        tasks = [tile_task(g, SLC, K_S, i, q4b[g], [near_bias(tb_ref, g, 0), sel_add(g, i)], first=True)
                 for g in groups]
        if n_back >= 1:
            j = i - 1
            tasks += [tile_task(g, SWA, K_A, j, q4a[g], [near_bias(ta_ref, g, 1)]) for g in groups]
            tasks += [tile_task(g, WIN, K_W, j, q4b[g], [near_bias(tb_ref, g, 1)]) for g in groups]
            tasks += [tile_task(g, SLC, K_S, j, q4b[g], [near_bias(tb_ref, g, 1), sel_add(g, j)]) for g in groups]
        for d in range(2, n_back + 1):
            edge = [lambda hf, p: up_ref[...]] if d == n_full else []
            tasks += [tile_task(g, WIN, K_W, i - d, q4b[g], edge) for g in groups]
        run_pipelined(tasks)

    for c in range(n_full):
        pl.when(i == c)(functools.partial(near_tiles, c))
    pl.when(i >= n_full)(functools.partial(near_tiles, n_full))

    def far_tasks(d):
        return [tile_task(g, SLC, K_S, i - d, q4b[g], [sel_add(g, i - d)]) for g in groups]

    def far_body(jj, c):
        d = 2 + 2 * jj
        run_pipelined(far_tasks(d) + far_tasks(d + 1))
        return c

    n_far = jnp.maximum(i - 1, 0)
    lax.fori_loop(0, n_far // 2, far_body, 0)
    pl.when(n_far % 2 == 1)(lambda: run_pipelined(far_tasks(i)))

    def to_rows(o_t, p):
        return o_t[:, p * TQ:(p + 1) * TQ].T

    far_of = lambda h: tab_ref[N_BUCKETS - 1, h]
    sg = jax.nn.sigmoid(gate_ref[0].astype(F32)).T
    sg_hi = sg.astype(BF16)
    sg_lo = (sg - sg_hi.astype(F32)).astype(BF16)
    gates_t = _dot(e3_ref[...], sg_hi) + _dot(e3_ref[...], sg_lo)
    for g in groups:
        def sink_term(hf, m, g=g):
            row = jnp.concatenate([jnp.full((1, TQ), (sink_ref[8 * g + 2 * p + hf] - far_of(8 * g + 2 * p + hf)) * LOG2E,
                                            F32) for p in range(PAIRS)], axis=1)
            return jnp.exp2(row - m)

        o_swa = finish(g, SWA, sink_term)
        o_slc = finish(g, SLC)
        o_win = finish(g, WIN)
        for p in range(PAIRS):
            cols = slice((PAIRS * g + p) * LANES, (PAIRS * g + p + 1) * LANES)
            oa_ref[0, :, cols] = to_rows(o_swa, p).astype(oa_ref.dtype)
            qs = slice(p * TQ, (p + 1) * TQ)
            gate = lambda br: gates_t[br * B_HEADS * HEAD_DIM + cols.start:br * B_HEADS * HEAD_DIM + cols.stop]
            o_t = gate(0) * o_cmp[g][:, qs] + gate(1) * o_slc[:, qs] + gate(2) * o_win[:, qs]
            ob_ref[0, :, cols] = o_t.T.astype(ob_ref.dtype)


def _even_attention(pe, kvc, rel_bias, sinks, ta, tb, up, cbias, e3, ovt):
    b, s, _ = pe.shape
    n_tiles = s // TQ
    width = GROUPS * PAIRS * LANES

    def seg(k):
        return pl.BlockSpec((1, s, LANES), lambda bi, i, k=k: (bi, 0, k))

    full = lambda a: pl.BlockSpec(a.shape, lambda bi, i: (0,) * a.ndim)
    in_specs = [
        pl.BlockSpec(memory_space=pltpu.SMEM),
        pl.BlockSpec(memory_space=pltpu.SMEM),
        pl.BlockSpec((1, TQ, width), lambda bi, i: (bi, i, COL_QA // width)),
        pl.BlockSpec((1, TQ, width), lambda bi, i: (bi, i, COL_QB // width)),
        pl.BlockSpec((1, TQ, LANES), lambda bi, i: (bi, i, SEG_GATE)),
        seg(SEG_KA), seg(SEG_VA), seg(SEG_KS), seg(SEG_VS), seg(SEG_KW), seg(SEG_VW),
        pl.BlockSpec((1, 1, TQ, LANES), lambda bi, i: (0, bi, 0, 0)),
        pl.BlockSpec((1, 1, TQ, LANES), lambda bi, i: (1, bi, 0, 0)),
        full(ta), full(tb), full(up),
        pl.BlockSpec((B_HEADS, TQ, TQ), lambda bi, i: (0, 0, i)),
        full(e3), full(ovt),
    ]
    out_spec = pl.BlockSpec((1, TQ, width), lambda bi, i: (bi, i, 0))
    return pl.pallas_call(
        functools.partial(_even_attn_kernel, n_tiles=n_tiles),
        grid=(b, n_tiles),
        in_specs=in_specs,
        out_specs=[out_spec, out_spec],
        out_shape=[jax.ShapeDtypeStruct((b, s, width), BF16)] * 2,
        scratch_shapes=[pltpu.VMEM((GROUPS, 3, n_tiles, 2 * TQ, LANES), BF16),
                        pltpu.VMEM((GROUPS, 3, n_tiles, VT_ROWS, 2 * TQ), BF16),
                        pltpu.VMEM((GROUPS, 2 * TQ, LANES), BF16),
                        pltpu.VMEM((GROUPS, LANES, 2 * TQ), BF16),
                        pltpu.VMEM((GROUPS, s // SLC_BLOCK, TQ), F32),
                        pltpu.VMEM((GROUPS, 3, VT_ROWS, W4), F32),
                        pltpu.VMEM((GROUPS, 3, 2, 1, W4), F32)],
        compiler_params=_params("arbitrary", "arbitrary"),
    )(rel_bias, sinks, pe, pe, pe, pe, pe, pe, pe, pe, pe, kvc, kvc, ta, tb, up, cbias, e3, ovt)


def _mla_prep_kernel(x_ref, g_ref, win_ref, qn_ref, kvn_ref, wqt_ref, wk_ref, wvt_ref,
                     cos_ref, sin_ref, cost_ref, sint_ref, qt_ref, k_ref, vt_ref):
    xn = _rms(x_ref[...], g_ref[...]).astype(BF16)
    proj = _dot(xn, win_ref[...])
    cq = _rms(proj[:, :Q_LORA], qn_ref[...])
    ckv = _rms(proj[:, Q_LORA:Q_LORA + KV_LORA], kvn_ref[...])
    cq_t = cq.T.astype(BF16)
    ckv_t = ckv.T.astype(BF16)
    ckv = ckv.astype(BF16)
    kr = proj[:, Q_LORA + KV_LORA:]
    kr = (kr * cos_ref[...] + pltpu.roll(kr, HEAD_DIM, 1) * sin_ref[...]).astype(BF16)
    cos_t = cost_ref[...]
    sin_t = sint_ref[...]
    scale = (NOPE_DIM + ROPE_DIM) ** -0.5 * LOG2E
    tm = ckv_t.shape[1]
    ones_rows = jnp.where(lax.broadcasted_iota(jnp.int32, (VT_ROWS - V_DIM, tm), 0) == 0, 1.0, 0.0).astype(BF16)
    for h in range(C_HEADS):
        q_t = _dot(wqt_ref[h], cq_t)
        rp = q_t[NOPE_DIM:]
        rp = rp * cos_t + pltpu.roll(rp, HEAD_DIM, 0) * sin_t
        qt_ref[0, h, 0, 0:NOPE_DIM, :] = (q_t[:NOPE_DIM] * scale).astype(BF16)
        qt_ref[0, h, 0, NOPE_DIM:, :] = (rp * scale).astype(BF16)
        if h % 2 == 0:
            k2 = _dot(ckv, wk_ref[:, h * LANES:(h + 2) * LANES]).astype(BF16)
        k_ref[:, h * MLA_QK:h * MLA_QK + NOPE_DIM] = k2[:, (h % 2) * LANES:(h % 2 + 1) * LANES]
        k_ref[:, h * MLA_QK + NOPE_DIM:(h + 1) * MLA_QK] = kr
        vt_ref[0, h, 0, 0:V_DIM, :] = _dot(wvt_ref[h], ckv_t).astype(BF16)
        vt_ref[0, h, 0, V_DIM:, :] = ones_rows


def _mla_prep(x, g, w_in, qn, kvn, wqt, wk, wvt, cos, sin, b, s):
    m, d = x.shape
    tm = MLA_TK
    pos_tiles = s // tm
    full = lambda a: pl.BlockSpec(a.shape, lambda i: (0,) * a.ndim, pipeline_mode=pl.Buffered(1))
    tile_t = lambda rows: pl.BlockSpec((1, C_HEADS, 1, rows, tm), lambda i: (i // pos_tiles, 0, i % pos_tiles, 0, 0))
    return pl.pallas_call(
        _mla_prep_kernel,
        grid=(m // tm,),
        in_specs=[pl.BlockSpec((tm, d), lambda i: (i, 0)),
                  pl.BlockSpec((1, d), lambda i: (0, 0)),
                  full(w_in), pl.BlockSpec((1, Q_LORA), lambda i: (0, 0)),
                  pl.BlockSpec((1, KV_LORA), lambda i: (0, 0)), full(wqt), full(wk), full(wvt),
                  pl.BlockSpec((tm, LANES), lambda i: (i % pos_tiles, 0)),
                  pl.BlockSpec((tm, LANES), lambda i: (i % pos_tiles, 0)),
                  pl.BlockSpec((LANES, tm), lambda i: (0, i % pos_tiles)),
                  pl.BlockSpec((LANES, tm), lambda i: (0, i % pos_tiles))],
        out_specs=[tile_t(MLA_QK),
                   pl.BlockSpec((tm, C_HEADS * MLA_QK), lambda i: (i, 0)),
                   tile_t(VT_ROWS)],
        out_shape=[jax.ShapeDtypeStruct((b, C_HEADS, pos_tiles, MLA_QK, tm), BF16),
                   jax.ShapeDtypeStruct((m, C_HEADS * MLA_QK), BF16),
                   jax.ShapeDtypeStruct((b, C_HEADS, pos_tiles, VT_ROWS, tm), BF16)],
        compiler_params=_params("parallel"),
    )(x, g.reshape(1, d), w_in, qn.reshape(1, -1), kvn.reshape(1, -1), wqt, wk, wvt, cos, sin, cos.T, sin.T)


def _mla_attn_kernel(qt_ref, k_ref, vt_ref, o_ref, s_scr, acc_scr, m_scr, *, n_q):
    tq = tk = MLA_TQ
    parts = tq // MLA_TK
    ki = lax.broadcasted_iota(jnp.int32, (tk, tq), 0)
    qi = lax.broadcasted_iota(jnp.int32, (tk, tq), 1)
    causal = ki <= qi
    heads = range(MLA_HEADS)

    def q_tile(t, c):
        q_ts = [jnp.concatenate([qt_ref[0, h, parts * t + d] for d in range(parts)], axis=1) for h in heads]

        def logits_to(slot, j):
            rows = pl.ds(pl.multiple_of(j * tk, tk), tk)
            for h in heads:
                s_scr[h, slot] = _dot(k_ref[0, rows, h * MLA_QK:(h + 1) * MLA_QK], q_ts[h])

        def update_from(slot, j, diagonal):
            for h in heads:
                s = s_scr[h, slot]
                if diagonal:
                    s = jnp.where(causal, s, NEG)
                m = m_scr[h]
                mn = jnp.maximum(m, jnp.max(s, axis=0, keepdims=True))
                p = jnp.exp2(s - mn).astype(BF16)
                m_scr[h] = mn
                v_t = jnp.concatenate([vt_ref[0, h, parts * j + d] for d in range(parts)], axis=1)
                acc_scr[h] = jnp.exp2(m - mn) * acc_scr[h] + _dot(v_t, p)

        m_scr[...] = jnp.full(m_scr.shape, NEG, F32)
        acc_scr[...] = jnp.zeros(acc_scr.shape, F32)
        logits_to(0, t)
        logits_to(1, 0)
        update_from(0, t, True)

        def body(jj, c2):
            logits_to(0, jnp.minimum(2 * jj + 1, t - 1))
            update_from(1, 2 * jj, False)
            logits_to(1, jnp.minimum(2 * jj + 2, t - 1))

            @pl.when(2 * jj + 1 < t)
            def _():
                update_from(0, 2 * jj + 1, False)
            return c2

        lax.fori_loop(0, (t + 1) // 2, body, 0)
        rows = pl.ds(pl.multiple_of(t * tq, tq), tq)
        for h in heads:
            acc = acc_scr[h]
            o = acc[:V_DIM] * (1.0 / acc[V_DIM:V_DIM + 1])
            o_ref[0, rows, h * LANES:(h + 1) * LANES] = o.T.astype(o_ref.dtype)
        return c

    lax.fori_loop(0, n_q, q_tile, 0)


def _mla_attention(qt, k, vt):
    b, s, _ = k.shape
    n_k = s // MLA_TK
    nh = MLA_HEADS
    return pl.pallas_call(
        functools.partial(_mla_attn_kernel, n_q=s // MLA_TQ),
        grid=(b, C_HEADS // nh),
        in_specs=[pl.BlockSpec((1, nh, n_k, MLA_QK, MLA_TK), lambda bi, h: (bi, h, 0, 0, 0)),
                  pl.BlockSpec((1, s, MLA_QK * nh), lambda bi, h: (bi, 0, h)),
                  pl.BlockSpec((1, nh, n_k, VT_ROWS, MLA_TK), lambda bi, h: (bi, h, 0, 0, 0))],
        out_specs=pl.BlockSpec((1, s, LANES * nh), lambda bi, h: (bi, 0, h)),
        out_shape=jax.ShapeDtypeStruct((b, s, C_HEADS * V_DIM), BF16),
        scratch_shapes=[pltpu.VMEM((nh, 2, MLA_TQ, MLA_TQ), F32),
                        pltpu.VMEM((nh, VT_ROWS, MLA_TQ), F32),
                        pltpu.VMEM((nh, 1, MLA_TQ), F32)],
        compiler_params=_params("parallel", "arbitrary"),
    )(qt, k, vt)


def _even_in_weight(w):
    cuts = np.cumsum([1024, 128, 128, 1024, 128, 128, 128, 128, 128, 128, 48])[:-1]
    qa, ka, va, qb, kc, vc, ks, vs, kw, vw, gate = jnp.split(w, [int(c) for c in cuts], axis=1)
    gate = jnp.pad(gate, ((0, 0), (0, LANES - gate.shape[1])))
    return jnp.concatenate([qa, qb, ka, va, kc, vc, ks, vs, kw, vw, gate], axis=1).astype(BF16)


def _compress_weights(pos, w1, w2):
    half = CMP_BLOCK // 2
    eye = jnp.eye(GROUPS, dtype=w1.dtype)
    w1r = w1.reshape(2, half, HEAD_DIM, CMP_HIDDEN)
    wab = jnp.einsum('rldh,gk->rlgdkh', w1r, eye).reshape(2, half * GROUPS * HEAD_DIM, GROUPS * CMP_HIDDEN)
    posr = jnp.broadcast_to(pos.reshape(2, half, 1, HEAD_DIM), (2, half, GROUPS, HEAD_DIM)).reshape(2, 1, -1)
    w2bd = jnp.einsum('hd,gk->ghkd', w2, eye).reshape(GROUPS * CMP_HIDDEN, GROUPS * HEAD_DIM)
    return posr[0], posr[1], wab[0].astype(BF16), wab[1].astype(BF16), w2bd.astype(BF16)


def _rope_chunk_cols(w):
    z = jnp.zeros(w.shape[:-1] + (ROPE_DIM // 2,), w.dtype)
    return jnp.concatenate([w[..., :ROPE_DIM // 2], z, w[..., ROPE_DIM // 2:], z], axis=-1)


def _mla_weights(w_in, w_q_up, w_kv_up):
    w_in2 = jnp.concatenate([w_in[:, :Q_LORA + KV_LORA], _rope_chunk_cols(w_in[:, Q_LORA + KV_LORA:])], axis=1)
    wq = w_q_up.reshape(Q_LORA, C_HEADS, NOPE_DIM + ROPE_DIM)
    wq2 = jnp.concatenate([wq[..., :NOPE_DIM], _rope_chunk_cols(wq[..., NOPE_DIM:])], axis=-1)
    wqt = wq2.transpose(1, 2, 0)
    wkv = w_kv_up.reshape(KV_LORA, C_HEADS, NOPE_DIM + V_DIM)
    wk = wkv[..., :NOPE_DIM].reshape(KV_LORA, C_HEADS * NOPE_DIM)
    wvt = wkv[..., NOPE_DIM:].transpose(1, 2, 0)
    return w_in2.astype(BF16), wqt.astype(BF16), wk.astype(BF16), wvt.astype(BF16)


def _rope_tables(s):
    inv = 1.0 / (ROPE_THETA ** (jnp.arange(0, ROPE_DIM, 2, dtype=F32) / ROPE_DIM))
    ang = jnp.arange(s, dtype=F32)[:, None] * inv[None]
    cos, sin = jnp.cos(ang), jnp.sin(ang)
    z = jnp.zeros_like(cos)
    return jnp.concatenate([cos, z, cos, z], axis=1), jnp.concatenate([-sin, z, sin, z], axis=1)


def _static_tables(s):
    k = np.arange(TQ)[:, None]
    q = np.arange(TQ)[None, :]
    diag = np.where(k <= q, _bucket_np(q - k), -1)
    prev = _bucket_np(q - k + TQ)
    near_swa = np.stack([diag, np.where(k > q, prev, -1)])
    near_nsa = np.stack([diag, prev])
    upper = np.where(k > q, 0.0, NEG).astype(np.float32)
    c = np.arange(LANES)[:, None]
    t = np.arange(s)[None, :]
    cdist = t - (c * CMP_STRIDE + CMP_BLOCK - 1)
    cmp_map = np.where(cdist >= 0, _bucket_np(cdist), -1)
    ns = s // SLC_BLOCK
    nc = (s - CMP_BLOCK) // CMP_STRIDE + 1
    c_start = np.arange(LANES) * CMP_STRIDE
    s_start = np.arange(ns) * SLC_BLOCK
    ovt = ((c_start[None, :] <= s_start[:, None] + SLC_BLOCK - 1) &
           (c_start[None, :] + CMP_BLOCK - 1 >= s_start[:, None]) &
           (np.arange(LANES)[None, :] < nc)).astype(np.float32)
    e3 = np.zeros((3, B_HEADS * HEAD_DIM, LANES), np.float32)
    for h in range(B_HEADS):
        for br in range(3):
            e3[br, h * HEAD_DIM:(h + 1) * HEAD_DIM, h * 3 + br] = 1.0
    e3 = e3.reshape(3 * B_HEADS * HEAD_DIM, LANES)
    return near_swa, near_nsa, upper, cmp_map, ovt, e3


def _even_layer(x, b, s, rel_bias, norm, w_in, sinks, pos_k, pos_v, k_w1, k_w2, v_w1, v_w2, w_out):
    near_swa, near_nsa, upper, cmp_map, ovt, e3 = _static_tables(s)
    ta = _bias_tiles(rel_bias, near_swa, 0, A_HEADS)
    tb = _bias_tiles(rel_bias, near_nsa, A_HEADS, B_HEADS)
    cbias = _bias_tiles(rel_bias, cmp_map, A_HEADS, B_HEADS)

    pe = _norm_matmul(x, norm, _even_in_weight(w_in), IN_PROJ_TM, EVEN_COLS, BF16).reshape(b, s, EVEN_COLS)

    rows = s // (CMP_BLOCK // 2)
    hkv = jnp.stack([pe[:, :, SEG_KC * LANES:(SEG_KC + 1) * LANES],
                     pe[:, :, SEG_VC * LANES:(SEG_VC + 1) * LANES]]).reshape(2, b, rows, -1)
    ck = _compress_weights(pos_k, k_w1, k_w2)
    cv = _compress_weights(pos_v, v_w1, v_w2)
    kvc = _compress(hkv, *[jnp.stack([a, c]) for a, c in zip(ck, cv)])
    assert rows <= TQ
    kvc = jnp.pad(kvc, ((0, 0), (0, 0), (0, TQ - rows), (0, 0)))

    oa, ob = _even_attention(pe, kvc, rel_bias, sinks, ta, tb, jnp.asarray(upper), cbias,
                             jnp.asarray(e3, BF16), jnp.asarray(ovt, BF16))
    w_out = w_out.astype(BF16)
    n_a = A_HEADS * HEAD_DIM
    return _out_proj(x, [oa.reshape(b * s, -1), ob.reshape(b * s, -1)], [w_out[:n_a], w_out[n_a:]], OUT_PROJ_TM)


def _odd_layer(x, b, s, norm, w_in, q_norm, w_q_up, kv_norm, w_kv_up, w_out):
    w_in2, wqt, wk, wvt = _mla_weights(w_in, w_q_up, w_kv_up)
    cos, sin = _rope_tables(s)
    qt, k, vt = _mla_prep(x, norm, w_in2, q_norm, kv_norm, wqt, wk, wvt, cos, sin, b, s)
    o = _mla_attention(qt, k.reshape(b, s, -1), vt)
    return _out_proj(x, [o.reshape(b * s, -1)], [w_out.astype(BF16)], OUT_PROJ_TM)


def kernel(x, rel_bias, norm_mix_e, w_in_e, sinks, cmp_pos_k, cmp_pos_v, cmp_k_w1, cmp_k_w2, cmp_v_w1, cmp_v_w2, w_out_e, norm_mix_o, w_in_o, q_norm, w_q_up, kv_norm, w_kv_up, w_out_o, norm_mlp, w_up, w_down, norm_final):
    b, s, d = x.shape
    depth = norm_mlp.shape[0]
    assert s % MLA_TQ == 0 and s // (CMP_BLOCK // 2) <= TQ and s // SLC_BLOCK <= LANES
    assert (b * s) % MLP_TM == 0 and (b * s) % IN_PROJ_TM == 0 and w_up.shape[2] % MLP_TF == 0
    h = x.reshape(b * s, d)
    for layer in range(depth):
        i = layer // 2
        if layer % 2 == 0:
            h = _even_layer(h, b, s, rel_bias, norm_mix_e[i], w_in_e[i], sinks[i], cmp_pos_k[i], cmp_pos_v[i],
                            cmp_k_w1[i], cmp_k_w2[i], cmp_v_w1[i], cmp_v_w2[i], w_out_e[i])
        else:
            h = _odd_layer(h, b, s, norm_mix_o[i], w_in_o[i], q_norm[i], w_q_up[i], kv_norm[i], w_kv_up[i],
                           w_out_o[i])
        h = _mlp(h, norm_mlp[layer], w_up, w_down, layer, norm_final, MLP_TM, MLP_TF, layer == depth - 1)
    return h.reshape(b, s, d)
```

```python
import functools
import math

import numpy as np
import jax
import jax.numpy as jnp
from jax import lax
from jax.experimental import pallas as pl
from jax.experimental.pallas import tpu as pltpu

F32 = jnp.float32
BF16 = jnp.bfloat16

LANES = 128
VMEM_LIMIT_BYTES = 56 * 1024 * 1024

EPS = 1e-6
NEG = -1e30
BIG = 1e4
HEAD_DIM = 64
TQ = 128
MLA_TQ = 512
MLA_TK = 512
MLA_HEADS = 4
IN_PROJ_TM = 1024
OUT_PROJ_TM = 512
MLP_TM, MLP_TF = 1024, 512
N_BUCKETS = 32
MAX_DISTANCE = 128
A_HEADS = 16
B_HEADS = 16
GROUPS = 2
PAIRS = 4
W4 = PAIRS * TQ
CMP_BLOCK = 32
CMP_STRIDE = 16
CMP_HIDDEN = 256
SLC_BLOCK = 64
SLC_TOPK = 8
A_WINDOW = 128
B_WINDOW = 512
C_HEADS = 16
Q_LORA = 768
KV_LORA = 512
NOPE_DIM = 128
ROPE_DIM = 64
V_DIM = 128
MLA_QK = NOPE_DIM + LANES
VT_ROWS = V_DIM + 16
LOG2E = math.log2(math.e)
ROPE_THETA = 10000.0

COL_QA, COL_QB = 0, 1024
SEG_KA, SEG_VA, SEG_KC, SEG_VC, SEG_KS, SEG_VS, SEG_KW, SEG_VW, SEG_GATE = range(16, 25)
EVEN_COLS = 25 * LANES
K_A, K_S, K_W = 0, 1, 2


def _params(*sem):
    return pltpu.CompilerParams(dimension_semantics=sem, vmem_limit_bytes=VMEM_LIMIT_BYTES)


def _rms(x, g):
    return x * lax.rsqrt(jnp.mean(x * x, axis=-1, keepdims=True) + EPS) * g


def _dot(a, b):
    return jnp.dot(a, b, preferred_element_type=F32)


def _run_pipelined(tasks):
    s = tasks[0][0]()
    for n, (_, apply_fn) in enumerate(tasks):
        s_next = tasks[n + 1][0]() if n + 1 < len(tasks) else None
        apply_fn(s)
        s = s_next


def _bucket_np(dist):
    dist = np.maximum(dist, 0)
    max_exact = N_BUCKETS // 2
    d = np.maximum(dist, 1).astype(np.float32)
    large = max_exact + (np.log(d / np.float32(max_exact)) / np.float32(math.log(MAX_DISTANCE / max_exact))
                         * np.float32(N_BUCKETS - max_exact)).astype(np.int32)
    large = np.minimum(large, N_BUCKETS - 1)
    return np.where(dist < max_exact, dist, large).astype(np.int32)


def _bias_kernel(tab_ref, bm_ref, o_ref, *, head0):
    h = pl.program_id(0) + head0
    bm = bm_ref[...]
    acc = jnp.zeros(bm.shape, F32)
    for b in range(N_BUCKETS):
        acc = jnp.where(bm == b, tab_ref[b, h], acc)
    o_ref[0] = jnp.where(bm < 0, NEG, (acc - tab_ref[N_BUCKETS - 1, h]) * LOG2E)


def _cmp_bias_kernel(tab_ref, bm_ref, o_ref, *, head0, bands):
    h = pl.program_id(0) + head0
    far = tab_ref[N_BUCKETS - 1, h]
    rows = lax.broadcasted_iota(jnp.int32, (bm_ref.shape[0], TQ), 0)
    for i, (start, size) in enumerate(bands):
        cols = slice(i * TQ, (i + 1) * TQ)
        o_ref[0, :, cols] = jnp.where(rows < start, 0.0, NEG)
        bm = bm_ref[start:start + size, cols]
        acc = jnp.zeros(bm.shape, F32)
        for b in range(N_BUCKETS):
            acc = jnp.where(bm == b, tab_ref[b, h], acc)
        o_ref[0, start:start + size, cols] = jnp.where(bm < 0, NEG, (acc - far) * LOG2E)


def _cmp_bias(rel_bias, cmp_map, head0, n_heads):
    n_blocks, s = cmp_map.shape
    bands = []
    for i in range(s // TQ):
        tile = cmp_map[:, i * TQ:(i + 1) * TQ]
        varying = np.nonzero(((tile >= 0) & (tile < N_BUCKETS - 1)).any(axis=1))[0]
        lo = int(varying.min()) // 8 * 8 if varying.size else 0
        hi = -(-(int(varying.max()) + 1) // 8) * 8 if varying.size else 8
        assert (tile[:lo] == N_BUCKETS - 1).all() and (tile[hi:] < 0).all()
        bands.append((lo, hi - lo))
    return pl.pallas_call(
        functools.partial(_cmp_bias_kernel, head0=head0, bands=tuple(bands)),
        grid=(n_heads,),
        in_specs=[pl.BlockSpec(memory_space=pltpu.SMEM),
                  pl.BlockSpec(cmp_map.shape, lambda h: (0, 0))],
        out_specs=pl.BlockSpec((1, n_blocks, s), lambda h: (h, 0, 0)),
        out_shape=jax.ShapeDtypeStruct((n_heads, n_blocks, s), F32),
        compiler_params=_params("arbitrary"),
    )(rel_bias, jnp.asarray(cmp_map))


def _bias_tiles(rel_bias, bucket_map, head0, n_heads):
    shp = bucket_map.shape
    nd = len(shp)
    return pl.pallas_call(
        functools.partial(_bias_kernel, head0=head0),
        grid=(n_heads,),
        in_specs=[pl.BlockSpec(memory_space=pltpu.SMEM),
                  pl.BlockSpec(shp, lambda h: (0,) * nd)],
        out_specs=pl.BlockSpec((1,) + shp, lambda h: (h,) + (0,) * nd),
        out_shape=jax.ShapeDtypeStruct((n_heads,) + shp, F32),
        compiler_params=_params("arbitrary"),
    )(rel_bias, jnp.asarray(bucket_map))


def _norm_matmul_kernel(x_ref, g_ref, w_ref, o_ref, xn_ref):
    @pl.when(pl.program_id(1) == 0)
    def _():
        xn_ref[...] = _rms(x_ref[...], g_ref[...]).astype(BF16)

    o_ref[...] = _dot(xn_ref[...], w_ref[...]).astype(o_ref.dtype)


def _norm_matmul(x, g, w, tm, tn, out_dtype):
    m, d = x.shape
    n = w.shape[1]
    return pl.pallas_call(
        _norm_matmul_kernel,
        grid=(m // tm, n // tn),
        in_specs=[pl.BlockSpec((tm, d), lambda i, j: (i, 0)),
                  pl.BlockSpec((1, d), lambda i, j: (0, 0)),
                  pl.BlockSpec((d, tn), lambda i, j: (0, j),
                               pipeline_mode=pl.Buffered(1) if tn == n else None)],
        out_specs=pl.BlockSpec((tm, tn), lambda i, j: (i, j)),
        out_shape=jax.ShapeDtypeStruct((m, n), out_dtype),
        scratch_shapes=[pltpu.VMEM((tm, d), BF16)],
        compiler_params=_params("parallel", "arbitrary"),
    )(x, g.reshape(1, d), w)


def _out_proj_kernel(*refs, n_in):
    x_ref = refs[0]
    o_ref = refs[2 * n_in + 1]
    acc = x_ref[...]
    for t in range(n_in):
        acc = acc + _dot(refs[1 + t][...], refs[1 + n_in + t][...])
    o_ref[...] = acc


def _out_proj(x, acts, ws, tm):
    m, d = x.shape
    n_in = len(acts)
    in_specs = [pl.BlockSpec((tm, d), lambda i: (i, 0))]
    in_specs += [pl.BlockSpec((tm, a.shape[1]), lambda i: (i, 0)) for a in acts]
    in_specs += [pl.BlockSpec(w.shape, lambda i: (0, 0), pipeline_mode=pl.Buffered(1)) for w in ws]
    return pl.pallas_call(
        functools.partial(_out_proj_kernel, n_in=n_in),
        grid=(m // tm,),
        in_specs=in_specs,
        out_specs=pl.BlockSpec((tm, d), lambda i: (i, 0)),
        out_shape=jax.ShapeDtypeStruct((m, d), F32),
        compiler_params=_params("parallel"),
    )(x, *acts, *ws)


def _mlp_kernel(x_ref, g_ref, wu_ref, wd_ref, gf_ref, o_ref, xn_ref, *, final_norm):
    j = pl.program_id(1)

    @pl.when(j == 0)
    def _():
        x = x_ref[...]
        xn_ref[...] = _rms(x, g_ref[...]).astype(BF16)
        o_ref[...] = x

    h = _dot(xn_ref[...], wu_ref[0].astype(BF16))
    a = jnp.square(jnp.maximum(h, 0.0)).astype(BF16)
    o_ref[...] += _dot(a, wd_ref[0].astype(BF16))

    if final_norm:
        @pl.when(j == pl.num_programs(1) - 1)
        def _():
            o_ref[...] = _rms(o_ref[...], gf_ref[...])


def _mlp(x, g, w_up, w_down, layer, g_final, tm, tf, final_norm):
    m, d = x.shape
    ff = w_up.shape[2]
    return pl.pallas_call(
        functools.partial(_mlp_kernel, final_norm=final_norm),
        grid=(m // tm, ff // tf),
        in_specs=[pl.BlockSpec((tm, d), lambda i, j: (i, 0), pipeline_mode=pl.Buffered(1)),
                  pl.BlockSpec((1, d), lambda i, j: (0, 0)),
                  pl.BlockSpec((1, d, tf), lambda i, j: (layer, 0, j)),
                  pl.BlockSpec((1, tf, d), lambda i, j: (layer, j, 0)),
                  pl.BlockSpec((1, d), lambda i, j: (0, 0))],
        out_specs=pl.BlockSpec((tm, d), lambda i, j: (i, 0)),
        out_shape=jax.ShapeDtypeStruct((m, d), F32),
        scratch_shapes=[pltpu.VMEM((tm, d), BF16)],
        compiler_params=_params("parallel", "arbitrary"),
    )(x, g.reshape(1, d), w_up, w_down, g_final.reshape(1, d))


def _gelu_tanh(x):
    return 0.5 * x * (1.0 + jnp.tanh(math.sqrt(2.0 / math.pi) * (x + 0.044715 * (x * x * x))))


def _compress_kernel(h_ref, pa_ref, pb_ref, wa_ref, wb_ref, w2_ref, o_ref):
    h = h_ref[0, 0].astype(F32)
    ha = _dot((h + pa_ref[0]).astype(BF16), wa_ref[0])
    hb = _dot((h + pb_ref[0]).astype(BF16), wb_ref[0])
    n = hb.shape[0]
    pre = ha + pltpu.roll(hb, n - 1, 0)
    o_ref[0, 0] = _dot(_gelu_tanh(pre).astype(BF16), w2_ref[0]).astype(o_ref.dtype)


def _compress(hkv, pa, pb, wa, wb, w2):
    _, b, r, w = hkv.shape
    hid2 = wa.shape[2]
    return pl.pallas_call(
        _compress_kernel,
        grid=(2, b),
        in_specs=[pl.BlockSpec((1, 1, r, w), lambda t, i: (t, i, 0, 0)),
                  pl.BlockSpec((1, 1, w), lambda t, i: (t, 0, 0)),
                  pl.BlockSpec((1, 1, w), lambda t, i: (t, 0, 0)),
                  pl.BlockSpec((1, w, hid2), lambda t, i: (t, 0, 0)),
                  pl.BlockSpec((1, w, hid2), lambda t, i: (t, 0, 0)),
                  pl.BlockSpec((1, hid2, LANES), lambda t, i: (t, 0, 0))],
        out_specs=pl.BlockSpec((1, 1, r, LANES), lambda t, i: (t, i, 0, 0)),
        out_shape=jax.ShapeDtypeStruct((2, b, r, LANES), BF16),
        compiler_params=_params("arbitrary", "arbitrary"),
    )(hkv, pa, pb, wa, wb, w2)


SWA, SLC, WIN = 0, 1, 2


def _stack_pairs_t(q_ref, g):
    cols = [q_ref[0, :, (PAIRS * g + p) * LANES:(PAIRS * g + p + 1) * LANES].astype(F32).T for p in range(PAIRS)]
    return (jnp.concatenate(cols, axis=1) * (HEAD_DIM ** -0.5 * LOG2E)).astype(BF16)


def _rows_by_half(a, b, rows):
    parts = [jnp.broadcast_to(a, (HEAD_DIM, W4)), jnp.broadcast_to(b, (HEAD_DIM, W4))]
    if rows > 2 * HEAD_DIM:
        first = lax.broadcasted_iota(jnp.int32, (rows - 2 * HEAD_DIM, W4), 0) == 0
        parts.append(jnp.where(first, a, b))
    return jnp.concatenate(parts, axis=0)


def _even_attn_kernel(tab_ref, sink_ref, qa_ref, qb_ref, gate_ref,
                      ka_ref, va_ref, ks_ref, vs_ref, kw_ref, vw_ref, kc_ref, vc_ref,
                      ta_ref, tb_ref, up_ref, cb_ref, e3_ref, ovt_ref,
                      oa_ref, ob_ref,
                      kbd_ref, vbd_ref, ckbd_ref, cvbd_ref, msk_ref, acc_scr, m_scr, *, n_tiles):
    i = pl.program_id(1)
    groups = range(GROUPS)

    def block_diag(x, g):
        r = pltpu.roll(x, HEAD_DIM, 1)
        lo = lax.broadcasted_iota(jnp.int32, x.shape, 1) < HEAD_DIM
        own, other = (x, r) if g == 0 else (r, x)
        return jnp.where(lo, own, 0.0), jnp.where(lo, 0.0, other)

    @pl.when(i == 0)
    def _build():
        r16 = lax.broadcasted_iota(jnp.int32, (VT_ROWS - LANES, 2 * TQ), 0)
        c16 = lax.broadcasted_iota(jnp.int32, (VT_ROWS - LANES, 2 * TQ), 1)
        ones_rows = jnp.where((r16 == 0) & (c16 < TQ) | (r16 == 1) & (c16 >= TQ), 1.0, 0.0).astype(BF16)
        for t, ref in enumerate((ka_ref, ks_ref, kw_ref)):
            def kbody(j, c, t=t, ref=ref):
                x = ref[0, pl.ds(pl.multiple_of(j * TQ, TQ), TQ), :].astype(F32)
                for g in groups:
                    top, bot = block_diag(x, g)
                    kbd_ref[g, t, j, 0:TQ, :] = top.astype(BF16)
                    kbd_ref[g, t, j, TQ:2 * TQ, :] = bot.astype(BF16)
                return c
            lax.fori_loop(0, n_tiles, kbody, 0)
        for t, ref in enumerate((va_ref, vs_ref, vw_ref)):
            def vbody(j, c, t=t, ref=ref):
                x = ref[0, pl.ds(pl.multiple_of(j * TQ, TQ), TQ), :].astype(F32)
                for g in groups:
                    top, bot = block_diag(x, g)
                    vbd_ref[g, t, j, 0:LANES, 0:TQ] = top.T.astype(BF16)
                    vbd_ref[g, t, j, 0:LANES, TQ:2 * TQ] = bot.T.astype(BF16)
                    vbd_ref[g, t, j, LANES:, :] = ones_rows
                return c
            lax.fori_loop(0, n_tiles, vbody, 0)
        xk = kc_ref[0, 0].astype(F32)
        xv = vc_ref[0, 0].astype(F32)
        for g in groups:
            top, bot = block_diag(xk, g)
            ckbd_ref[g, 0:TQ, :] = top.astype(BF16)
            ckbd_ref[g, TQ:2 * TQ, :] = bot.astype(BF16)
            top, bot = block_diag(xv, g)
            cvbd_ref[g, :, 0:TQ] = top.T.astype(BF16)
            cvbd_ref[g, :, TQ:2 * TQ] = bot.T.astype(BF16)

    def near_bias(t_ref, g, delta):
        return lambda hf, p: t_ref[8 * g + 2 * p + hf, delta]

    def update(g, st, s, adds, vbd_t, first):
        m_old = [None if first else m_scr[g, st, hf] for hf in range(2)]
        p_rows, m_new = [], []
        for hf in range(2):
            strips, mns = [], []
            for p in range(PAIRS):
                sh = s[hf * TQ:(hf + 1) * TQ, p * TQ:(p + 1) * TQ]
                for a in adds:
                    sh = sh + a(hf, p)
                mn = jnp.max(sh, axis=0, keepdims=True)
                if not first:
                    mn = jnp.maximum(m_old[hf][:, p * TQ:(p + 1) * TQ], mn)
                strips.append(jnp.exp2(sh - mn).astype(BF16))
                mns.append(mn)
            p_rows.append(jnp.concatenate(strips, axis=1))
            m_new.append(jnp.concatenate(mns, axis=1))
            m_scr[g, st, hf] = m_new[hf]
        pv = _dot(vbd_t, jnp.concatenate(p_rows, axis=0))
        if first:
            acc_scr[g, st] = pv
        else:
            alpha = _rows_by_half(jnp.exp2(m_old[0] - m_new[0]), jnp.exp2(m_old[1] - m_new[1]), VT_ROWS)
            acc_scr[g, st] = alpha * acc_scr[g, st] + pv

    run_pipelined = _run_pipelined

    def tile_task(g, st, slot, j, q4t, adds, first=False):
        return (lambda: _dot(kbd_ref[g, slot, j], q4t),
                lambda s: update(g, st, s, adds, vbd_ref[g, slot, j], first))

    def finish(g, st, extra=None):
        acc = acc_scr[g, st]
        den = [acc[LANES + hf:LANES + hf + 1] for hf in range(2)]
        if extra is not None:
            den = [den[hf] + extra(hf, m_scr[g, st, hf]) for hf in range(2)]
        return acc[:LANES] * _rows_by_half(1.0 / den[0], 1.0 / den[1], LANES)

    q4a = [_stack_pairs_t(qa_ref, g) for g in groups]
    q4b = [_stack_pairs_t(qb_ref, g) for g in groups]

    run_pipelined([tile_task(g, SWA, K_A, i, q4a[g], [near_bias(ta_ref, g, 0)], first=True) for g in groups] +
                  [tile_task(g, WIN, K_W, i, q4b[g], [near_bias(tb_ref, g, 0)], first=True) for g in groups])

    t_q = i * TQ + lax.broadcasted_iota(jnp.int32, (1, TQ), 1)
    anyvis = jnp.where(t_q >= CMP_BLOCK - 1, 1.0, 0.0)
    s_cmp = [_dot(ckbd_ref[g], q4b[g]) for g in groups]
    psum = [jnp.zeros((TQ, TQ), F32) for _ in groups]
    p_rows = [[] for _ in groups]
    for hf in range(2):
        strips = [[] for _ in groups]
        for p in range(PAIRS):
            for g in groups:
                sh = s_cmp[g][hf * TQ:(hf + 1) * TQ, p * TQ:(p + 1) * TQ] + cb_ref[8 * g + 2 * p + hf]
                pe = jnp.exp2(sh - jnp.max(sh, axis=0, keepdims=True))
                pc = pe * (anyvis / jnp.sum(pe, axis=0, keepdims=True))
                psum[g] = psum[g] + pc
                strips[g].append(pc.astype(BF16))
        for g in groups:
            p_rows[g].append(jnp.concatenate(strips[g], axis=1))
    o_cmp = [_dot(cvbd_ref[g], jnp.concatenate(p_rows[g], axis=0)) for g in groups]

    psum2 = jnp.concatenate(psum, axis=1)
    ph = psum2.astype(BF16)
    r1 = psum2 - ph.astype(F32)
    pm = r1.astype(BF16)
    pl_ = (r1 - pm.astype(F32)).astype(BF16)
    ovt = ovt_ref[...]
    pslc = _dot(ovt, ph) + _dot(ovt, pm) + _dot(ovt, pl_)
    ns = pslc.shape[0]
    nb = lax.broadcasted_iota(jnp.int32, (ns, GROUPS * TQ), 0)
    tq = i * TQ + (lax.broadcasted_iota(jnp.int32, (ns, GROUPS * TQ), 1) & (TQ - 1))
    cur = tq // SLC_BLOCK
    forced = (jnp.where(nb == 0, 1.0, 0.0) + jnp.where(nb == cur, 1.0, 0.0) +
              jnp.where(nb == cur - 1, 1.0, 0.0))
    score = jnp.where(nb * SLC_BLOCK > tq, NEG, jnp.where(forced > 0.0, BIG, pslc))
    rank = jnp.zeros((ns, GROUPS * TQ), F32)
    for mth in range(ns):
        sm = score[mth:mth + 1, :]
        tie = jnp.where(nb > mth, 1.0, 0.0)
        rank = rank + jnp.where(sm > score, 1.0, jnp.where(sm == score, tie, 0.0))
    sel_rows = jnp.where(rank < SLC_TOPK, 0.0, NEG)
    for g in groups:
        msk_ref[g] = sel_rows[:, g * TQ:(g + 1) * TQ]

    def sel_add(g, j):
        cache = []

        def tile(hf, p):
            if not cache:
                rows = [jnp.broadcast_to(msk_ref[g, pl.ds(2 * j + e, 1), :], (SLC_BLOCK, TQ)) for e in range(2)]
                cache.append(jnp.concatenate(rows, axis=0))
            return cache[0]
        return tile

    n_full = B_WINDOW // TQ

    def near_tiles(n_back):
        tasks = [tile_task(g, SLC, K_S, i, q4b[g], [near_bias(tb_ref, g, 0), sel_add(g, i)], first=True)
                 for g in groups]
        if n_back >= 1:
            j = i - 1
            tasks += [tile_task(g, SWA, K_A, j, q4a[g], [near_bias(ta_ref, g, 1)]) for g in groups]
            tasks += [tile_task(g, WIN, K_W, j, q4b[g], [near_bias(tb_ref, g, 1)]) for g in groups]
            tasks += [tile_task(g, SLC, K_S, j, q4b[g], [near_bias(tb_ref, g, 1), sel_add(g, j)]) for g in groups]
        for d in range(2, n_back + 1):
            edge = [lambda hf, p: up_ref[...]] if d == n_full else []
            tasks += [tile_task(g, WIN, K_W, i - d, q4b[g], edge) for g in groups]
        run_pipelined(tasks)

    for c in range(n_full):
        pl.when(i == c)(functools.partial(near_tiles, c))
    pl.when(i >= n_full)(functools.partial(near_tiles, n_full))

    def far_tasks(d):
        return [tile_task(g, SLC, K_S, i - d, q4b[g], [sel_add(g, i - d)]) for g in groups]

    def far_body(jj, c):
        d = 2 + 2 * jj
        run_pipelined(far_tasks(d) + far_tasks(d + 1))
        return c

    n_far = jnp.maximum(i - 1, 0)
    lax.fori_loop(0, n_far // 2, far_body, 0)
    pl.when(n_far % 2 == 1)(lambda: run_pipelined(far_tasks(i)))

    def to_rows(o_t, p):
        return o_t[:, p * TQ:(p + 1) * TQ].T

    far_of = lambda h: tab_ref[N_BUCKETS - 1, h]
    sg = jax.nn.sigmoid(gate_ref[0].astype(F32)).T
    sg_hi = sg.astype(BF16)
    sg_lo = (sg - sg_hi.astype(F32)).astype(BF16)
    gates_t = _dot(e3_ref[...], sg_hi) + _dot(e3_ref[...], sg_lo)
    for g in groups:
        def sink_term(hf, m, g=g):
            row = jnp.concatenate([jnp.full((1, TQ), (sink_ref[8 * g + 2 * p + hf] - far_of(8 * g + 2 * p + hf)) * LOG2E,
                                            F32) for p in range(PAIRS)], axis=1)
            return jnp.exp2(row - m)

        o_swa = finish(g, SWA, sink_term)
        o_slc = finish(g, SLC)
        o_win = finish(g, WIN)
        for p in range(PAIRS):
            cols = slice((PAIRS * g + p) * LANES, (PAIRS * g + p + 1) * LANES)
            oa_ref[0, :, cols] = to_rows(o_swa, p).astype(oa_ref.dtype)
            qs = slice(p * TQ, (p + 1) * TQ)
            gate = lambda br: gates_t[br * B_HEADS * HEAD_DIM + cols.start:br * B_HEADS * HEAD_DIM + cols.stop]
            o_t = gate(0) * o_cmp[g][:, qs] + gate(1) * o_slc[:, qs] + gate(2) * o_win[:, qs]
            ob_ref[0, :, cols] = o_t.T.astype(ob_ref.dtype)


def _even_attention(pe, kvc, rel_bias, sinks, ta, tb, up, cbias, e3, ovt):
    b, s, _ = pe.shape
    n_tiles = s // TQ
    width = GROUPS * PAIRS * LANES

    def seg(k):
        return pl.BlockSpec((1, s, LANES), lambda bi, i, k=k: (bi, 0, k))

    full = lambda a: pl.BlockSpec(a.shape, lambda bi, i: (0,) * a.ndim)
    in_specs = [
        pl.BlockSpec(memory_space=pltpu.SMEM),
        pl.BlockSpec(memory_space=pltpu.SMEM),
        pl.BlockSpec((1, TQ, width), lambda bi, i: (bi, i, COL_QA // width)),
        pl.BlockSpec((1, TQ, width), lambda bi, i: (bi, i, COL_QB // width)),
        pl.BlockSpec((1, TQ, LANES), lambda bi, i: (bi, i, SEG_GATE)),
        seg(SEG_KA), seg(SEG_VA), seg(SEG_KS), seg(SEG_VS), seg(SEG_KW), seg(SEG_VW),
        pl.BlockSpec((1, 1, TQ, LANES), lambda bi, i: (0, bi, 0, 0)),
        pl.BlockSpec((1, 1, TQ, LANES), lambda bi, i: (1, bi, 0, 0)),
        full(ta), full(tb), full(up),
        pl.BlockSpec((B_HEADS, TQ, TQ), lambda bi, i: (0, 0, i)),
        full(e3), full(ovt),
    ]
    out_spec = pl.BlockSpec((1, TQ, width), lambda bi, i: (bi, i, 0))
    return pl.pallas_call(
        functools.partial(_even_attn_kernel, n_tiles=n_tiles),
        grid=(b, n_tiles),
        in_specs=in_specs,
        out_specs=[out_spec, out_spec],
        out_shape=[jax.ShapeDtypeStruct((b, s, width), BF16)] * 2,
        scratch_shapes=[pltpu.VMEM((GROUPS, 3, n_tiles, 2 * TQ, LANES), BF16),
                        pltpu.VMEM((GROUPS, 3, n_tiles, VT_ROWS, 2 * TQ), BF16),
                        pltpu.VMEM((GROUPS, 2 * TQ, LANES), BF16),
                        pltpu.VMEM((GROUPS, LANES, 2 * TQ), BF16),
                        pltpu.VMEM((GROUPS, s // SLC_BLOCK, TQ), F32),
                        pltpu.VMEM((GROUPS, 3, VT_ROWS, W4), F32),
                        pltpu.VMEM((GROUPS, 3, 2, 1, W4), F32)],
        compiler_params=_params("arbitrary", "arbitrary"),
    )(rel_bias, sinks, pe, pe, pe, pe, pe, pe, pe, pe, pe, kvc, kvc, ta, tb, up, cbias, e3, ovt)


def _mla_prep_kernel(x_ref, g_ref, win_ref, qn_ref, kvn_ref, wqt_ref, wk_ref, wvt_ref,
                     cos_ref, sin_ref, cost_ref, sint_ref, qt_ref, k_ref, vt_ref):
    xn = _rms(x_ref[...], g_ref[...]).astype(BF16)
    proj = _dot(xn, win_ref[...])
    cq = _rms(proj[:, :Q_LORA], qn_ref[...])
    ckv = _rms(proj[:, Q_LORA:Q_LORA + KV_LORA], kvn_ref[...])
    cq_t = cq.T.astype(BF16)
    ckv_t = ckv.T.astype(BF16)
    ckv = ckv.astype(BF16)
    kr = proj[:, Q_LORA + KV_LORA:]
    kr = (kr * cos_ref[...] + pltpu.roll(kr, HEAD_DIM, 1) * sin_ref[...]).astype(BF16)
    cos_t = cost_ref[...]
    sin_t = sint_ref[...]
    scale = (NOPE_DIM + ROPE_DIM) ** -0.5 * LOG2E
    tm = ckv_t.shape[1]
    ones_rows = jnp.where(lax.broadcasted_iota(jnp.int32, (VT_ROWS - V_DIM, tm), 0) == 0, 1.0, 0.0).astype(BF16)
    for h in range(C_HEADS):
        q_t = _dot(wqt_ref[h], cq_t)
        rp = q_t[NOPE_DIM:]
        rp = rp * cos_t + pltpu.roll(rp, HEAD_DIM, 0) * sin_t
        qt_ref[0, h, 0, 0:NOPE_DIM, :] = (q_t[:NOPE_DIM] * scale).astype(BF16)
        qt_ref[0, h, 0, NOPE_DIM:, :] = (rp * scale).astype(BF16)
        if h % 2 == 0:
            k2 = _dot(ckv, wk_ref[:, h * LANES:(h + 2) * LANES]).astype(BF16)
        k_ref[:, h * MLA_QK:h * MLA_QK + NOPE_DIM] = k2[:, (h % 2) * LANES:(h % 2 + 1) * LANES]
        k_ref[:, h * MLA_QK + NOPE_DIM:(h + 1) * MLA_QK] = kr
        vt_ref[0, h, 0, 0:V_DIM, :] = _dot(wvt_ref[h], ckv_t).astype(BF16)
        vt_ref[0, h, 0, V_DIM:, :] = ones_rows


def _mla_prep(x, g, w_in, qn, kvn, wqt, wk, wvt, cos, sin, b, s):
    m, d = x.shape
    tm = MLA_TK
    pos_tiles = s // tm
    full = lambda a: pl.BlockSpec(a.shape, lambda i: (0,) * a.ndim, pipeline_mode=pl.Buffered(1))
    tile_t = lambda rows: pl.BlockSpec((1, C_HEADS, 1, rows, tm), lambda i: (i // pos_tiles, 0, i % pos_tiles, 0, 0))
    return pl.pallas_call(
        _mla_prep_kernel,
        grid=(m // tm,),
        in_specs=[pl.BlockSpec((tm, d), lambda i: (i, 0)),
                  pl.BlockSpec((1, d), lambda i: (0, 0)),
                  full(w_in), pl.BlockSpec((1, Q_LORA), lambda i: (0, 0)),
                  pl.BlockSpec((1, KV_LORA), lambda i: (0, 0)), full(wqt), full(wk), full(wvt),
                  pl.BlockSpec((tm, LANES), lambda i: (i % pos_tiles, 0)),
                  pl.BlockSpec((tm, LANES), lambda i: (i % pos_tiles, 0)),
                  pl.BlockSpec((LANES, tm), lambda i: (0, i % pos_tiles)),
                  pl.BlockSpec((LANES, tm), lambda i: (0, i % pos_tiles))],
        out_specs=[tile_t(MLA_QK),
                   pl.BlockSpec((tm, C_HEADS * MLA_QK), lambda i: (i, 0)),
                   tile_t(VT_ROWS)],
        out_shape=[jax.ShapeDtypeStruct((b, C_HEADS, pos_tiles, MLA_QK, tm), BF16),
                   jax.ShapeDtypeStruct((m, C_HEADS * MLA_QK), BF16),
                   jax.ShapeDtypeStruct((b, C_HEADS, pos_tiles, VT_ROWS, tm), BF16)],
        compiler_params=_params("parallel"),
    )(x, g.reshape(1, d), w_in, qn.reshape(1, -1), kvn.reshape(1, -1), wqt, wk, wvt, cos, sin, cos.T, sin.T)


def _mla_attn_kernel(qt_ref, k_ref, vt_ref, o_ref, s_scr, acc_scr, m_scr, *, n_q):
    tq = tk = MLA_TQ
    parts = tq // MLA_TK
    ki = lax.broadcasted_iota(jnp.int32, (tk, tq), 0)
    qi = lax.broadcasted_iota(jnp.int32, (tk, tq), 1)
    causal = ki <= qi
    heads = range(MLA_HEADS)

    def q_tile(t, c):
        q_ts = [jnp.concatenate([qt_ref[0, h, parts * t + d] for d in range(parts)], axis=1) for h in heads]

        def logits_to(slot, j):
            rows = pl.ds(pl.multiple_of(j * tk, tk), tk)
            for h in heads:
                s_scr[h, slot] = _dot(k_ref[0, rows, h * MLA_QK:(h + 1) * MLA_QK], q_ts[h])

        def update_from(slot, j, diagonal):
            for h in heads:
                s = s_scr[h, slot]
                if diagonal:
                    s = jnp.where(causal, s, NEG)
                m = m_scr[h]
                mn = jnp.maximum(m, jnp.max(s, axis=0, keepdims=True))
                p = jnp.exp2(s - mn).astype(BF16)
                m_scr[h] = mn
                v_t = jnp.concatenate([vt_ref[0, h, parts * j + d] for d in range(parts)], axis=1)
                acc_scr[h] = jnp.exp2(m - mn) * acc_scr[h] + _dot(v_t, p)

        m_scr[...] = jnp.full(m_scr.shape, NEG, F32)
        acc_scr[...] = jnp.zeros(acc_scr.shape, F32)
        logits_to(0, t)
        logits_to(1, 0)
        update_from(0, t, True)

        def body(jj, c2):
            logits_to(0, jnp.minimum(2 * jj + 1, t - 1))
            update_from(1, 2 * jj, False)
            logits_to(1, jnp.minimum(2 * jj + 2, t - 1))

            @pl.when(2 * jj + 1 < t)
            def _():
                update_from(0, 2 * jj + 1, False)
            return c2

        lax.fori_loop(0, (t + 1) // 2, body, 0)
        rows = pl.ds(pl.multiple_of(t * tq, tq), tq)
        for h in heads:
            acc = acc_scr[h]
            o = acc[:V_DIM] * (1.0 / acc[V_DIM:V_DIM + 1])
            o_ref[0, rows, h * LANES:(h + 1) * LANES] = o.T.astype(o_ref.dtype)
        return c

    lax.fori_loop(0, n_q, q_tile, 0)


def _mla_attention(qt, k, vt):
    b, s, _ = k.shape
    n_k = s // MLA_TK
    nh = MLA_HEADS
    return pl.pallas_call(
        functools.partial(_mla_attn_kernel, n_q=s // MLA_TQ),
        grid=(b, C_HEADS // nh),
        in_specs=[pl.BlockSpec((1, nh, n_k, MLA_QK, MLA_TK), lambda bi, h: (bi, h, 0, 0, 0)),
                  pl.BlockSpec((1, s, MLA_QK * nh), lambda bi, h: (bi, 0, h)),
                  pl.BlockSpec((1, nh, n_k, VT_ROWS, MLA_TK), lambda bi, h: (bi, h, 0, 0, 0))],
        out_specs=pl.BlockSpec((1, s, LANES * nh), lambda bi, h: (bi, 0, h)),
        out_shape=jax.ShapeDtypeStruct((b, s, C_HEADS * V_DIM), BF16),
        scratch_shapes=[pltpu.VMEM((nh, 2, MLA_TQ, MLA_TQ), F32),
                        pltpu.VMEM((nh, VT_ROWS, MLA_TQ), F32),
                        pltpu.VMEM((nh, 1, MLA_TQ), F32)],
        compiler_params=_params("parallel", "arbitrary"),
    )(qt, k, vt)


def _even_in_weight(w):
    cuts = np.cumsum([1024, 128, 128, 1024, 128, 128, 128, 128, 128, 128, 48])[:-1]
    qa, ka, va, qb, kc, vc, ks, vs, kw, vw, gate = jnp.split(w.astype(BF16), [int(c) for c in cuts], axis=1)
    gate = jnp.pad(gate, ((0, 0), (0, LANES - gate.shape[1])))
    return jnp.concatenate([qa, qb, ka, va, kc, vc, ks, vs, kw, vw, gate], axis=1)


def _compress_weights(pos, w1, w2):
    half = CMP_BLOCK // 2
    eye = jnp.eye(GROUPS, dtype=BF16)
    w1r = w1.astype(BF16).reshape(2, half, HEAD_DIM, CMP_HIDDEN)
    wab = jnp.einsum('rldh,gk->rlgdkh', w1r, eye).reshape(2, half * GROUPS * HEAD_DIM, GROUPS * CMP_HIDDEN)
    posr = jnp.broadcast_to(pos.reshape(2, half, 1, HEAD_DIM), (2, half, GROUPS, HEAD_DIM)).reshape(2, 1, -1)
    w2bd = jnp.einsum('hd,gk->ghkd', w2.astype(BF16), eye).reshape(GROUPS * CMP_HIDDEN, GROUPS * HEAD_DIM)
    return posr[0], posr[1], wab[0], wab[1], w2bd


def _rope_chunk_cols(w):
    z = jnp.zeros(w.shape[:-1] + (ROPE_DIM // 2,), w.dtype)
    return jnp.concatenate([w[..., :ROPE_DIM // 2], z, w[..., ROPE_DIM // 2:], z], axis=-1)


def _mla_weights(w_in, w_q_up, w_kv_up):
    w_in, w_q_up, w_kv_up = w_in.astype(BF16), w_q_up.astype(BF16), w_kv_up.astype(BF16)
    w_in2 = jnp.concatenate([w_in[:, :Q_LORA + KV_LORA], _rope_chunk_cols(w_in[:, Q_LORA + KV_LORA:])], axis=1)
    wq = w_q_up.reshape(Q_LORA, C_HEADS, NOPE_DIM + ROPE_DIM)
    wq2 = jnp.concatenate([wq[..., :NOPE_DIM], _rope_chunk_cols(wq[..., NOPE_DIM:])], axis=-1)
    wqt = wq2.transpose(1, 2, 0)
    wkv = w_kv_up.reshape(KV_LORA, C_HEADS, NOPE_DIM + V_DIM)
    wk = wkv[..., :NOPE_DIM].reshape(KV_LORA, C_HEADS * NOPE_DIM)
    wvt = wkv[..., NOPE_DIM:].transpose(1, 2, 0)
    return w_in2, wqt, wk, wvt


def _rope_tables(s):
    inv = 1.0 / (ROPE_THETA ** (jnp.arange(0, ROPE_DIM, 2, dtype=F32) / ROPE_DIM))
    ang = jnp.arange(s, dtype=F32)[:, None] * inv[None]
    cos, sin = jnp.cos(ang), jnp.sin(ang)
    z = jnp.zeros_like(cos)
    return jnp.concatenate([cos, z, cos, z], axis=1), jnp.concatenate([-sin, z, sin, z], axis=1)


def _static_tables(s):
    k = np.arange(TQ)[:, None]
    q = np.arange(TQ)[None, :]
    diag = np.where(k <= q, _bucket_np(q - k), -1)
    prev = _bucket_np(q - k + TQ)
    near_swa = np.stack([diag, np.where(k > q, prev, -1)])
    near_nsa = np.stack([diag, prev])
    upper = np.where(k > q, 0.0, NEG).astype(np.float32)
    c = np.arange(LANES)[:, None]
    t = np.arange(s)[None, :]
    cdist = t - (c * CMP_STRIDE + CMP_BLOCK - 1)
    cmp_map = np.where(cdist >= 0, _bucket_np(cdist), -1)
    ns = s // SLC_BLOCK
    nc = (s - CMP_BLOCK) // CMP_STRIDE + 1
    c_start = np.arange(LANES) * CMP_STRIDE
    s_start = np.arange(ns) * SLC_BLOCK
    ovt = ((c_start[None, :] <= s_start[:, None] + SLC_BLOCK - 1) &
           (c_start[None, :] + CMP_BLOCK - 1 >= s_start[:, None]) &
           (np.arange(LANES)[None, :] < nc)).astype(np.float32)
    e3 = np.zeros((3, B_HEADS * HEAD_DIM, LANES), np.float32)
    for h in range(B_HEADS):
        for br in range(3):
            e3[br, h * HEAD_DIM:(h + 1) * HEAD_DIM, h * 3 + br] = 1.0
    e3 = e3.reshape(3 * B_HEADS * HEAD_DIM, LANES)
    return near_swa, near_nsa, upper, cmp_map, ovt, e3


def _even_layer(x, b, s, rel_bias, norm, w_in, sinks, pos_k, pos_v, k_w1, k_w2, v_w1, v_w2, w_out):
    near_swa, near_nsa, upper, cmp_map, ovt, e3 = _static_tables(s)
    ta = _bias_tiles(rel_bias, near_swa, 0, A_HEADS)
    tb = _bias_tiles(rel_bias, near_nsa, A_HEADS, B_HEADS)
    cbias = _cmp_bias(rel_bias, cmp_map, A_HEADS, B_HEADS)

    pe = _norm_matmul(x, norm, _even_in_weight(w_in), IN_PROJ_TM, EVEN_COLS, BF16).reshape(b, s, EVEN_COLS)

    rows = s // (CMP_BLOCK // 2)
    hkv = jnp.stack([pe[:, :, SEG_KC * LANES:(SEG_KC + 1) * LANES],
                     pe[:, :, SEG_VC * LANES:(SEG_VC + 1) * LANES]]).reshape(2, b, rows, -1)
    ck = _compress_weights(pos_k, k_w1, k_w2)
    cv = _compress_weights(pos_v, v_w1, v_w2)
    kvc = _compress(hkv, *[jnp.stack([a, c]) for a, c in zip(ck, cv)])
    assert rows <= TQ
    kvc = jnp.pad(kvc, ((0, 0), (0, 0), (0, TQ - rows), (0, 0)))

    oa, ob = _even_attention(pe, kvc, rel_bias, sinks, ta, tb, jnp.asarray(upper), cbias,
                             jnp.asarray(e3, BF16), jnp.asarray(ovt, BF16))
    w_out = w_out.astype(BF16)
    n_a = A_HEADS * HEAD_DIM
    return _out_proj(x, [oa.reshape(b * s, -1), ob.reshape(b * s, -1)], [w_out[:n_a], w_out[n_a:]], OUT_PROJ_TM)


def _odd_layer(x, b, s, norm, w_in, q_norm, w_q_up, kv_norm, w_kv_up, w_out):
    w_in2, wqt, wk, wvt = _mla_weights(w_in, w_q_up, w_kv_up)
    cos, sin = _rope_tables(s)
    qt, k, vt = _mla_prep(x, norm, w_in2, q_norm, kv_norm, wqt, wk, wvt, cos, sin, b, s)
    o = _mla_attention(qt, k.reshape(b, s, -1), vt)
    return _out_proj(x, [o.reshape(b * s, -1)], [w_out.astype(BF16)], OUT_PROJ_TM)


def kernel(x, rel_bias, norm_mix_e, w_in_e, sinks, cmp_pos_k, cmp_pos_v, cmp_k_w1, cmp_k_w2, cmp_v_w1, cmp_v_w2, w_out_e, norm_mix_o, w_in_o, q_norm, w_q_up, kv_norm, w_kv_up, w_out_o, norm_mlp, w_up, w_down, norm_final):
    b, s, d = x.shape
    depth = norm_mlp.shape[0]
    assert s % MLA_TQ == 0 and s // (CMP_BLOCK // 2) <= TQ and s // SLC_BLOCK <= LANES
    assert (b * s) % MLP_TM == 0 and (b * s) % IN_PROJ_TM == 0 and w_up.shape[2] % MLP_TF == 0
    h = x.reshape(b * s, d)
    for layer in range(depth):
        i = layer // 2
        if layer % 2 == 0:
            h = _even_layer(h, b, s, rel_bias, norm_mix_e[i], w_in_e[i], sinks[i], cmp_pos_k[i], cmp_pos_v[i],
                            cmp_k_w1[i], cmp_k_w2[i], cmp_v_w1[i], cmp_v_w2[i], w_out_e[i])
        else:
            h = _odd_layer(h, b, s, norm_mix_o[i], w_in_o[i], q_norm[i], w_q_up[i], kv_norm[i], w_kv_up[i],
                           w_out_o[i])
        h = _mlp(h, norm_mlp[layer], w_up, w_down, layer, norm_final, MLP_TM, MLP_TF, layer == depth - 1)
    return h.reshape(b, s, d)
```

```python
import functools
import math

import numpy as np
import jax
import jax.numpy as jnp
from jax import lax
from jax.experimental import pallas as pl
from jax.experimental.pallas import tpu as pltpu

F32 = jnp.float32
BF16 = jnp.bfloat16

LANES = 128
VMEM_LIMIT_BYTES = 56 * 1024 * 1024

EPS = 1e-6
NEG = -1e30
BIG = 1e4
HEAD_DIM = 64
TQ = 128
MLA_TQ = 512
MLA_TK = 512
MLA_HEADS = 4
IN_PROJ_TM = 1024
OUT_PROJ_TM = 512
MLP_TM, MLP_TF = 1024, 512
N_BUCKETS = 32
MAX_DISTANCE = 128
A_HEADS = 16
B_HEADS = 16
GROUPS = 2
PAIRS = 4
W4 = PAIRS * TQ
CMP_BLOCK = 32
CMP_STRIDE = 16
CMP_HIDDEN = 256
SLC_BLOCK = 64
SLC_TOPK = 8
A_WINDOW = 128
B_WINDOW = 512
C_HEADS = 16
Q_LORA = 768
KV_LORA = 512
NOPE_DIM = 128
ROPE_DIM = 64
V_DIM = 128
MLA_QK = NOPE_DIM + LANES
VT_ROWS = V_DIM + 16
LOG2E = math.log2(math.e)
ROPE_THETA = 10000.0

COL_QA, COL_QB = 0, 1024
SEG_KA, SEG_VA, SEG_KC, SEG_VC, SEG_KS, SEG_VS, SEG_KW, SEG_VW, SEG_GATE = range(16, 25)
EVEN_COLS = 25 * LANES
K_A, K_S, K_W = 0, 1, 2


def _params(*sem):
    return pltpu.CompilerParams(dimension_semantics=sem, vmem_limit_bytes=VMEM_LIMIT_BYTES)


def _rms(x, g):
    return x * lax.rsqrt(jnp.mean(x * x, axis=-1, keepdims=True) + EPS) * g


def _dot(a, b):
    return jnp.dot(a, b, preferred_element_type=F32)


def _run_pipelined(tasks):
    s = tasks[0][0]()
    for n, (_, apply_fn) in enumerate(tasks):
        s_next = tasks[n + 1][0]() if n + 1 < len(tasks) else None
        apply_fn(s)
        s = s_next


def _bucket_np(dist):
    dist = np.maximum(dist, 0)
    max_exact = N_BUCKETS // 2
    d = np.maximum(dist, 1).astype(np.float32)
    large = max_exact + (np.log(d / np.float32(max_exact)) / np.float32(math.log(MAX_DISTANCE / max_exact))
                         * np.float32(N_BUCKETS - max_exact)).astype(np.int32)
    large = np.minimum(large, N_BUCKETS - 1)
    return np.where(dist < max_exact, dist, large).astype(np.int32)


def _bias_kernel(tab_ref, bm_ref, o_ref, *, head0):
    h = pl.program_id(0) + head0
    bm = bm_ref[...]
    acc = jnp.zeros(bm.shape, F32)
    for b in range(N_BUCKETS):
        acc = jnp.where(bm == b, tab_ref[b, h], acc)
    o_ref[0] = jnp.where(bm < 0, NEG, (acc - tab_ref[N_BUCKETS - 1, h]) * LOG2E)


def _cmp_bias_kernel(tab_ref, bm_ref, o_ref, *, head0, bands):
    h = pl.program_id(0) + head0
    far = tab_ref[N_BUCKETS - 1, h]
    rows = lax.broadcasted_iota(jnp.int32, (bm_ref.shape[0], TQ), 0)
    for i, (start, size) in enumerate(bands):
        cols = slice(i * TQ, (i + 1) * TQ)
        o_ref[0, :, cols] = jnp.where(rows < start, 0.0, NEG)
        bm = bm_ref[start:start + size, cols]
        acc = jnp.zeros(bm.shape, F32)
        for b in range(N_BUCKETS):
            acc = jnp.where(bm == b, tab_ref[b, h], acc)
        o_ref[0, start:start + size, cols] = jnp.where(bm < 0, NEG, (acc - far) * LOG2E)


def _cmp_bias(rel_bias, cmp_map, head0, n_heads):
    n_blocks, s = cmp_map.shape
    bands = []
    for i in range(s // TQ):
        tile = cmp_map[:, i * TQ:(i + 1) * TQ]
        varying = np.nonzero(((tile >= 0) & (tile < N_BUCKETS - 1)).any(axis=1))[0]
        lo = int(varying.min()) // 8 * 8 if varying.size else 0
        hi = -(-(int(varying.max()) + 1) // 8) * 8 if varying.size else 8
        assert (tile[:lo] == N_BUCKETS - 1).all() and (tile[hi:] < 0).all()
        bands.append((lo, hi - lo))
    return pl.pallas_call(
        functools.partial(_cmp_bias_kernel, head0=head0, bands=tuple(bands)),
        grid=(n_heads,),
        in_specs=[pl.BlockSpec(memory_space=pltpu.SMEM),
                  pl.BlockSpec(cmp_map.shape, lambda h: (0, 0))],
        out_specs=pl.BlockSpec((1, n_blocks, s), lambda h: (h, 0, 0)),
        out_shape=jax.ShapeDtypeStruct((n_heads, n_blocks, s), F32),
        compiler_params=_params("arbitrary"),
    )(rel_bias, jnp.asarray(cmp_map))


def _bias_tiles(rel_bias, bucket_map, head0, n_heads):
    shp = bucket_map.shape
    nd = len(shp)
    return pl.pallas_call(
        functools.partial(_bias_kernel, head0=head0),
        grid=(n_heads,),
        in_specs=[pl.BlockSpec(memory_space=pltpu.SMEM),
                  pl.BlockSpec(shp, lambda h: (0,) * nd)],
        out_specs=pl.BlockSpec((1,) + shp, lambda h: (h,) + (0,) * nd),
        out_shape=jax.ShapeDtypeStruct((n_heads,) + shp, F32),
        compiler_params=_params("arbitrary"),
    )(rel_bias, jnp.asarray(bucket_map))


def _norm_matmul_kernel(x_ref, g_ref, w_ref, o_ref, xn_ref):
    @pl.when(pl.program_id(1) == 0)
    def _():
        xn_ref[...] = _rms(x_ref[...], g_ref[...]).astype(BF16)

    o_ref[...] = _dot(xn_ref[...], w_ref[...]).astype(o_ref.dtype)


def _norm_matmul(x, g, w, tm, tn, out_dtype):
    m, d = x.shape
    n = w.shape[1]
    return pl.pallas_call(
        _norm_matmul_kernel,
        grid=(m // tm, n // tn),
        in_specs=[pl.BlockSpec((tm, d), lambda i, j: (i, 0)),
                  pl.BlockSpec((1, d), lambda i, j: (0, 0)),
                  pl.BlockSpec((d, tn), lambda i, j: (0, j),
                               pipeline_mode=pl.Buffered(1) if tn == n else None)],
        out_specs=pl.BlockSpec((tm, tn), lambda i, j: (i, j)),
        out_shape=jax.ShapeDtypeStruct((m, n), out_dtype),
        scratch_shapes=[pltpu.VMEM((tm, d), BF16)],
        compiler_params=_params("parallel", "arbitrary"),
    )(x, g.reshape(1, d), w)


def _out_proj_kernel(*refs, n_in):
    x_ref = refs[0]
    o_ref = refs[2 * n_in + 1]
    acc = x_ref[...]
    for t in range(n_in):
        acc = acc + _dot(refs[1 + t][...], refs[1 + n_in + t][...])
    o_ref[...] = acc


def _out_proj(x, acts, ws, tm):
    m, d = x.shape
    n_in = len(acts)
    in_specs = [pl.BlockSpec((tm, d), lambda i: (i, 0))]
    in_specs += [pl.BlockSpec((tm, a.shape[1]), lambda i: (i, 0)) for a in acts]
    in_specs += [pl.BlockSpec(w.shape, lambda i: (0, 0), pipeline_mode=pl.Buffered(1)) for w in ws]
    return pl.pallas_call(
        functools.partial(_out_proj_kernel, n_in=n_in),
        grid=(m // tm,),
        in_specs=in_specs,
        out_specs=pl.BlockSpec((tm, d), lambda i: (i, 0)),
        out_shape=jax.ShapeDtypeStruct((m, d), F32),
        compiler_params=_params("parallel"),
    )(x, *acts, *ws)


def _mlp_kernel(x_ref, g_ref, wu_ref, wd_ref, gf_ref, o_ref, xn_ref, *, final_norm):
    j = pl.program_id(1)

    @pl.when(j == 0)
    def _():
        x = x_ref[...]
        xn_ref[...] = _rms(x, g_ref[...]).astype(BF16)
        o_ref[...] = x

    h = _dot(xn_ref[...], wu_ref[0].astype(BF16))
    a = jnp.square(jnp.maximum(h, 0.0)).astype(BF16)
    o_ref[...] += _dot(a, wd_ref[0].astype(BF16))

    if final_norm:
        @pl.when(j == pl.num_programs(1) - 1)
        def _():
            o_ref[...] = _rms(o_ref[...], gf_ref[...])


def _mlp(x, g, w_up, w_down, layer, g_final, tm, tf, final_norm):
    m, d = x.shape
    ff = w_up.shape[2]
    return pl.pallas_call(
        functools.partial(_mlp_kernel, final_norm=final_norm),
        grid=(m // tm, ff // tf),
        in_specs=[pl.BlockSpec((tm, d), lambda i, j: (i, 0)),
                  pl.BlockSpec((1, d), lambda i, j: (0, 0)),
                  pl.BlockSpec((1, d, tf), lambda i, j: (layer, 0, j)),
                  pl.BlockSpec((1, tf, d), lambda i, j: (layer, j, 0)),
                  pl.BlockSpec((1, d), lambda i, j: (0, 0))],
        out_specs=pl.BlockSpec((tm, d), lambda i, j: (i, 0)),
        out_shape=jax.ShapeDtypeStruct((m, d), F32),
        scratch_shapes=[pltpu.VMEM((tm, d), BF16)],
        compiler_params=_params("parallel", "arbitrary"),
    )(x, g.reshape(1, d), w_up, w_down, g_final.reshape(1, d))


def _gelu_tanh(x):
    return 0.5 * x * (1.0 + jnp.tanh(math.sqrt(2.0 / math.pi) * (x + 0.044715 * (x * x * x))))


def _compress_kernel(h_ref, pa_ref, pb_ref, wa_ref, wb_ref, w2_ref, o_ref):
    h = h_ref[0, 0].astype(F32)
    ha = _dot((h + pa_ref[0]).astype(BF16), wa_ref[0])
    hb = _dot((h + pb_ref[0]).astype(BF16), wb_ref[0])
    n = hb.shape[0]
    pre = ha + pltpu.roll(hb, n - 1, 0)
    o_ref[0, 0] = _dot(_gelu_tanh(pre).astype(BF16), w2_ref[0]).astype(o_ref.dtype)


def _compress(hkv, pa, pb, wa, wb, w2):
    _, b, r, w = hkv.shape
    hid2 = wa.shape[2]
    return pl.pallas_call(
        _compress_kernel,
        grid=(2, b),
        in_specs=[pl.BlockSpec((1, 1, r, w), lambda t, i: (t, i, 0, 0)),
                  pl.BlockSpec((1, 1, w), lambda t, i: (t, 0, 0)),
                  pl.BlockSpec((1, 1, w), lambda t, i: (t, 0, 0)),
                  pl.BlockSpec((1, w, hid2), lambda t, i: (t, 0, 0)),
                  pl.BlockSpec((1, w, hid2), lambda t, i: (t, 0, 0)),
                  pl.BlockSpec((1, hid2, LANES), lambda t, i: (t, 0, 0))],
        out_specs=pl.BlockSpec((1, 1, r, LANES), lambda t, i: (t, i, 0, 0)),
        out_shape=jax.ShapeDtypeStruct((2, b, r, LANES), BF16),
        compiler_params=_params("arbitrary", "arbitrary"),
    )(hkv, pa, pb, wa, wb, w2)


SWA, SLC, WIN = 0, 1, 2


def _stack_pairs_t(q_ref, g):
    cols = [q_ref[0, :, (PAIRS * g + p) * LANES:(PAIRS * g + p + 1) * LANES].astype(F32).T for p in range(PAIRS)]
    return (jnp.concatenate(cols, axis=1) * (HEAD_DIM ** -0.5 * LOG2E)).astype(BF16)


def _rows_by_half(a, b, rows):
    parts = [jnp.broadcast_to(a, (HEAD_DIM, W4)), jnp.broadcast_to(b, (HEAD_DIM, W4))]
    if rows > 2 * HEAD_DIM:
        first = lax.broadcasted_iota(jnp.int32, (rows - 2 * HEAD_DIM, W4), 0) == 0
        parts.append(jnp.where(first, a, b))
    return jnp.concatenate(parts, axis=0)


def _even_attn_kernel(tab_ref, sink_ref, qa_ref, qb_ref, gate_ref,
                      ka_ref, va_ref, ks_ref, vs_ref, kw_ref, vw_ref, kc_ref, vc_ref,
                      ta_ref, tb_ref, up_ref, cb_ref, e3_ref, ovt_ref,
                      oa_ref, ob_ref,
                      kbd_ref, vbd_ref, ckbd_ref, cvbd_ref, msk_ref, acc_scr, m_scr, *, n_tiles):
    i = pl.program_id(1)
    groups = range(GROUPS)

    def block_diag(x, g):
        r = pltpu.roll(x, HEAD_DIM, 1)
        lo = lax.broadcasted_iota(jnp.int32, x.shape, 1) < HEAD_DIM
        own, other = (x, r) if g == 0 else (r, x)
        return jnp.where(lo, own, 0.0), jnp.where(lo, 0.0, other)

    @pl.when(i == 0)
    def _build():
        r16 = lax.broadcasted_iota(jnp.int32, (VT_ROWS - LANES, 2 * TQ), 0)
        c16 = lax.broadcasted_iota(jnp.int32, (VT_ROWS - LANES, 2 * TQ), 1)
        ones_rows = jnp.where((r16 == 0) & (c16 < TQ) | (r16 == 1) & (c16 >= TQ), 1.0, 0.0).astype(BF16)
        for t, ref in enumerate((ka_ref, ks_ref, kw_ref)):
            def kbody(j, c, t=t, ref=ref):
                x = ref[0, pl.ds(pl.multiple_of(j * TQ, TQ), TQ), :].astype(F32)
                for g in groups:
                    top, bot = block_diag(x, g)
                    kbd_ref[g, t, j, 0:TQ, :] = top.astype(BF16)
                    kbd_ref[g, t, j, TQ:2 * TQ, :] = bot.astype(BF16)
                return c
            lax.fori_loop(0, n_tiles, kbody, 0)
        for t, ref in enumerate((va_ref, vs_ref, vw_ref)):
            def vbody(j, c, t=t, ref=ref):
                x = ref[0, pl.ds(pl.multiple_of(j * TQ, TQ), TQ), :].astype(F32)
                for g in groups:
                    top, bot = block_diag(x, g)
                    vbd_ref[g, t, j, 0:LANES, 0:TQ] = top.T.astype(BF16)
                    vbd_ref[g, t, j, 0:LANES, TQ:2 * TQ] = bot.T.astype(BF16)
                    vbd_ref[g, t, j, LANES:, :] = ones_rows
                return c
            lax.fori_loop(0, n_tiles, vbody, 0)
        xk = kc_ref[0, 0].astype(F32)
        xv = vc_ref[0, 0].astype(F32)
        for g in groups:
            top, bot = block_diag(xk, g)
            ckbd_ref[g, 0:TQ, :] = top.astype(BF16)
            ckbd_ref[g, TQ:2 * TQ, :] = bot.astype(BF16)
            top, bot = block_diag(xv, g)
            cvbd_ref[g, :, 0:TQ] = top.T.astype(BF16)
            cvbd_ref[g, :, TQ:2 * TQ] = bot.T.astype(BF16)

    def near_bias(t_ref, g, delta):
        return lambda hf, p: t_ref[8 * g + 2 * p + hf, delta]

    def update(g, st, s, adds, vbd_t, first):
        m_old = [None if first else m_scr[g, st, hf] for hf in range(2)]
        p_rows, m_new = [], []
        for hf in range(2):
            strips, mns = [], []
            for p in range(PAIRS):
                sh = s[hf * TQ:(hf + 1) * TQ, p * TQ:(p + 1) * TQ]
                for a in adds:
                    sh = sh + a(hf, p)
                mn = jnp.max(sh, axis=0, keepdims=True)
                if not first:
                    mn = jnp.maximum(m_old[hf][:, p * TQ:(p + 1) * TQ], mn)
                strips.append(jnp.exp2(sh - mn).astype(BF16))
                mns.append(mn)
            p_rows.append(jnp.concatenate(strips, axis=1))
            m_new.append(jnp.concatenate(mns, axis=1))
            m_scr[g, st, hf] = m_new[hf]
        pv = _dot(vbd_t, jnp.concatenate(p_rows, axis=0))
        if first:
            acc_scr[g, st] = pv
        else:
            alpha = _rows_by_half(jnp.exp2(m_old[0] - m_new[0]), jnp.exp2(m_old[1] - m_new[1]), VT_ROWS)
            acc_scr[g, st] = alpha * acc_scr[g, st] + pv

    run_pipelined = _run_pipelined

    def tile_task(g, st, slot, j, q4t, adds, first=False):
        return (lambda: _dot(kbd_ref[g, slot, j], q4t),
                lambda s: update(g, st, s, adds, vbd_ref[g, slot, j], first))

    def finish(g, st, extra=None):
        acc = acc_scr[g, st]
        den = [acc[LANES + hf:LANES + hf + 1] for hf in range(2)]
        if extra is not None:
            den = [den[hf] + extra(hf, m_scr[g, st, hf]) for hf in range(2)]
        return acc[:LANES] * _rows_by_half(1.0 / den[0], 1.0 / den[1], LANES)

    q4a = [_stack_pairs_t(qa_ref, g) for g in groups]
    q4b = [_stack_pairs_t(qb_ref, g) for g in groups]

    run_pipelined([tile_task(g, SWA, K_A, i, q4a[g], [near_bias(ta_ref, g, 0)], first=True) for g in groups] +
                  [tile_task(g, WIN, K_W, i, q4b[g], [near_bias(tb_ref, g, 0)], first=True) for g in groups])

    t_q = i * TQ + lax.broadcasted_iota(jnp.int32, (1, TQ), 1)
    anyvis = jnp.where(t_q >= CMP_BLOCK - 1, 1.0, 0.0)
    s_cmp = [_dot(ckbd_ref[g], q4b[g]) for g in groups]
    psum = [jnp.zeros((TQ, TQ), F32) for _ in groups]
    p_rows = [[] for _ in groups]
    for hf in range(2):
        strips = [[] for _ in groups]
        for p in range(PAIRS):
            for g in groups:
                sh = s_cmp[g][hf * TQ:(hf + 1) * TQ, p * TQ:(p + 1) * TQ] + cb_ref[8 * g + 2 * p + hf]
                pe = jnp.exp2(sh - jnp.max(sh, axis=0, keepdims=True))
                pc = pe * (anyvis / jnp.sum(pe, axis=0, keepdims=True))
                psum[g] = psum[g] + pc
                strips[g].append(pc.astype(BF16))
        for g in groups:
            p_rows[g].append(jnp.concatenate(strips[g], axis=1))
    o_cmp = [_dot(cvbd_ref[g], jnp.concatenate(p_rows[g], axis=0)) for g in groups]

    psum2 = jnp.concatenate(psum, axis=1)
    ph = psum2.astype(BF16)
    r1 = psum2 - ph.astype(F32)
    pm = r1.astype(BF16)
    pl_ = (r1 - pm.astype(F32)).astype(BF16)
    ovt = ovt_ref[...]
    pslc = _dot(ovt, ph) + _dot(ovt, pm) + _dot(ovt, pl_)
    ns = pslc.shape[0]
    nb = lax.broadcasted_iota(jnp.int32, (ns, GROUPS * TQ), 0)
    tq = i * TQ + (lax.broadcasted_iota(jnp.int32, (ns, GROUPS * TQ), 1) & (TQ - 1))
    cur = tq // SLC_BLOCK
    forced = (jnp.where(nb == 0, 1.0, 0.0) + jnp.where(nb == cur, 1.0, 0.0) +
              jnp.where(nb == cur - 1, 1.0, 0.0))
    score = jnp.where(nb * SLC_BLOCK > tq, NEG, jnp.where(forced > 0.0, BIG, pslc))
    rank = jnp.zeros((ns, GROUPS * TQ), F32)
    for mth in range(ns):
        sm = score[mth:mth + 1, :]
        tie = jnp.where(nb > mth, 1.0, 0.0)
        rank = rank + jnp.where(sm > score, 1.0, jnp.where(sm == score, tie, 0.0))
    sel_rows = jnp.where(rank < SLC_TOPK, 0.0, NEG)
    for g in groups:
        msk_ref[g] = sel_rows[:, g * TQ:(g + 1) * TQ]

    def sel_add(g, j):
        cache = []

        def tile(hf, p):
            if not cache:
                rows = [jnp.broadcast_to(msk_ref[g, pl.ds(2 * j + e, 1), :], (SLC_BLOCK, TQ)) for e in range(2)]
                cache.append(jnp.concatenate(rows, axis=0))
            return cache[0]
        return tile

    n_full = B_WINDOW // TQ

    def near_tiles(n_back):
        tasks = [tile_task(g, SLC, K_S, i, q4b[g], [near_bias(tb_ref, g, 0), sel_add(g, i)], first=True)
                 for g in groups]
        if n_back >= 1:
            j = i - 1
            tasks += [tile_task(g, SWA, K_A, j, q4a[g], [near_bias(ta_ref, g, 1)]) for g in groups]
            tasks += [tile_task(g, WIN, K_W, j, q4b[g], [near_bias(tb_ref, g, 1)]) for g in groups]
            tasks += [tile_task(g, SLC, K_S, j, q4b[g], [near_bias(tb_ref, g, 1), sel_add(g, j)]) for g in groups]
        for d in range(2, n_back + 1):
            edge = [lambda hf, p: up_ref[...]] if d == n_full else []
            tasks += [tile_task(g, WIN, K_W, i - d, q4b[g], edge) for g in groups]
        run_pipelined(tasks)

    for c in range(n_full):
        pl.when(i == c)(functools.partial(near_tiles, c))
    pl.when(i >= n_full)(functools.partial(near_tiles, n_full))

    def far_tasks(d):
        return [tile_task(g, SLC, K_S, i - d, q4b[g], [sel_add(g, i - d)]) for g in groups]

    def far_body(jj, c):
        d = 2 + 2 * jj
        run_pipelined(far_tasks(d) + far_tasks(d + 1))
        return c

    n_far = jnp.maximum(i - 1, 0)
    lax.fori_loop(0, n_far // 2, far_body, 0)
    pl.when(n_far % 2 == 1)(lambda: run_pipelined(far_tasks(i)))

    def to_rows(o_t, p):
        return o_t[:, p * TQ:(p + 1) * TQ].T

    far_of = lambda h: tab_ref[N_BUCKETS - 1, h]
    sg = jax.nn.sigmoid(gate_ref[0].astype(F32)).T
    sg_hi = sg.astype(BF16)
    sg_lo = (sg - sg_hi.astype(F32)).astype(BF16)
    gates_t = _dot(e3_ref[...], sg_hi) + _dot(e3_ref[...], sg_lo)
    for g in groups:
        def sink_term(hf, m, g=g):
            row = jnp.concatenate([jnp.full((1, TQ), (sink_ref[8 * g + 2 * p + hf] - far_of(8 * g + 2 * p + hf)) * LOG2E,
                                            F32) for p in range(PAIRS)], axis=1)
            return jnp.exp2(row - m)

        o_swa = finish(g, SWA, sink_term)
        o_slc = finish(g, SLC)
        o_win = finish(g, WIN)
        for p in range(PAIRS):
            cols = slice((PAIRS * g + p) * LANES, (PAIRS * g + p + 1) * LANES)
            oa_ref[0, :, cols] = to_rows(o_swa, p).astype(oa_ref.dtype)
            qs = slice(p * TQ, (p + 1) * TQ)
            gate = lambda br: gates_t[br * B_HEADS * HEAD_DIM + cols.start:br * B_HEADS * HEAD_DIM + cols.stop]
            o_t = gate(0) * o_cmp[g][:, qs] + gate(1) * o_slc[:, qs] + gate(2) * o_win[:, qs]
            ob_ref[0, :, cols] = o_t.T.astype(ob_ref.dtype)


def _even_attention(pe, kvc, rel_bias, sinks, ta, tb, up, cbias, e3, ovt):
    b, s, _ = pe.shape
    n_tiles = s // TQ
    width = GROUPS * PAIRS * LANES

    def seg(k):
        return pl.BlockSpec((1, s, LANES), lambda bi, i, k=k: (bi, 0, k))

    full = lambda a: pl.BlockSpec(a.shape, lambda bi, i: (0,) * a.ndim)
    in_specs = [
        pl.BlockSpec(memory_space=pltpu.SMEM),
        pl.BlockSpec(memory_space=pltpu.SMEM),
        pl.BlockSpec((1, TQ, width), lambda bi, i: (bi, i, COL_QA // width)),
        pl.BlockSpec((1, TQ, width), lambda bi, i: (bi, i, COL_QB // width)),
        pl.BlockSpec((1, TQ, LANES), lambda bi, i: (bi, i, SEG_GATE)),
        seg(SEG_KA), seg(SEG_VA), seg(SEG_KS), seg(SEG_VS), seg(SEG_KW), seg(SEG_VW),
        pl.BlockSpec((1, 1, TQ, LANES), lambda bi, i: (0, bi, 0, 0)),
        pl.BlockSpec((1, 1, TQ, LANES), lambda bi, i: (1, bi, 0, 0)),
        full(ta), full(tb), full(up),
        pl.BlockSpec((B_HEADS, TQ, TQ), lambda bi, i: (0, 0, i)),
        full(e3), full(ovt),
    ]
    out_spec = pl.BlockSpec((1, TQ, width), lambda bi, i: (bi, i, 0))
    return pl.pallas_call(
        functools.partial(_even_attn_kernel, n_tiles=n_tiles),
        grid=(b, n_tiles),
        in_specs=in_specs,
        out_specs=[out_spec, out_spec],
        out_shape=[jax.ShapeDtypeStruct((b, s, width), BF16)] * 2,
        scratch_shapes=[pltpu.VMEM((GROUPS, 3, n_tiles, 2 * TQ, LANES), BF16),
                        pltpu.VMEM((GROUPS, 3, n_tiles, VT_ROWS, 2 * TQ), BF16),
                        pltpu.VMEM((GROUPS, 2 * TQ, LANES), BF16),
                        pltpu.VMEM((GROUPS, LANES, 2 * TQ), BF16),
                        pltpu.VMEM((GROUPS, s // SLC_BLOCK, TQ), F32),
                        pltpu.VMEM((GROUPS, 3, VT_ROWS, W4), F32),
                        pltpu.VMEM((GROUPS, 3, 2, 1, W4), F32)],
        compiler_params=_params("arbitrary", "arbitrary"),
    )(rel_bias, sinks, pe, pe, pe, pe, pe, pe, pe, pe, pe, kvc, kvc, ta, tb, up, cbias, e3, ovt)


def _mla_prep_kernel(x_ref, g_ref, win_ref, qn_ref, kvn_ref, wqt_ref, wk_ref, wvt_ref,
                     cos_ref, sin_ref, cost_ref, sint_ref, qt_ref, k_ref, vt_ref):
    xn = _rms(x_ref[...], g_ref[...]).astype(BF16)
    proj = _dot(xn, win_ref[...])
    cq = _rms(proj[:, :Q_LORA], qn_ref[...])
    ckv = _rms(proj[:, Q_LORA:Q_LORA + KV_LORA], kvn_ref[...])
    cq_t = cq.T.astype(BF16)
    ckv_t = ckv.T.astype(BF16)
    ckv = ckv.astype(BF16)
    kr = proj[:, Q_LORA + KV_LORA:]
    kr = (kr * cos_ref[...] + pltpu.roll(kr, HEAD_DIM, 1) * sin_ref[...]).astype(BF16)
    cos_t = cost_ref[...]
    sin_t = sint_ref[...]
    scale = (NOPE_DIM + ROPE_DIM) ** -0.5 * LOG2E
    tm = ckv_t.shape[1]
    ones_rows = jnp.where(lax.broadcasted_iota(jnp.int32, (VT_ROWS - V_DIM, tm), 0) == 0, 1.0, 0.0).astype(BF16)
    for h in range(C_HEADS):
        q_t = _dot(wqt_ref[h], cq_t)
        rp = q_t[NOPE_DIM:]
        rp = rp * cos_t + pltpu.roll(rp, HEAD_DIM, 0) * sin_t
        qt_ref[0, h, 0, 0:NOPE_DIM, :] = (q_t[:NOPE_DIM] * scale).astype(BF16)
        qt_ref[0, h, 0, NOPE_DIM:, :] = (rp * scale).astype(BF16)
        if h % 2 == 0:
            k2 = _dot(ckv, wk_ref[:, h * LANES:(h + 2) * LANES]).astype(BF16)
        k_ref[:, h * MLA_QK:h * MLA_QK + NOPE_DIM] = k2[:, (h % 2) * LANES:(h % 2 + 1) * LANES]
        k_ref[:, h * MLA_QK + NOPE_DIM:(h + 1) * MLA_QK] = kr
        vt_ref[0, h, 0, 0:V_DIM, :] = _dot(wvt_ref[h], ckv_t).astype(BF16)
        vt_ref[0, h, 0, V_DIM:, :] = ones_rows


def _mla_prep(x, g, w_in, qn, kvn, wqt, wk, wvt, cos, sin, b, s):
    m, d = x.shape
    tm = MLA_TK
    pos_tiles = s // tm
    full = lambda a: pl.BlockSpec(a.shape, lambda i: (0,) * a.ndim, pipeline_mode=pl.Buffered(1))
    tile_t = lambda rows: pl.BlockSpec((1, C_HEADS, 1, rows, tm), lambda i: (i // pos_tiles, 0, i % pos_tiles, 0, 0))
    return pl.pallas_call(
        _mla_prep_kernel,
        grid=(m // tm,),
        in_specs=[pl.BlockSpec((tm, d), lambda i: (i, 0)),
                  pl.BlockSpec((1, d), lambda i: (0, 0)),
                  full(w_in), pl.BlockSpec((1, Q_LORA), lambda i: (0, 0)),
                  pl.BlockSpec((1, KV_LORA), lambda i: (0, 0)), full(wqt), full(wk), full(wvt),
                  pl.BlockSpec((tm, LANES), lambda i: (i % pos_tiles, 0)),
                  pl.BlockSpec((tm, LANES), lambda i: (i % pos_tiles, 0)),
                  pl.BlockSpec((LANES, tm), lambda i: (0, i % pos_tiles)),
                  pl.BlockSpec((LANES, tm), lambda i: (0, i % pos_tiles))],
        out_specs=[tile_t(MLA_QK),
                   pl.BlockSpec((tm, C_HEADS * MLA_QK), lambda i: (i, 0)),
                   tile_t(VT_ROWS)],
        out_shape=[jax.ShapeDtypeStruct((b, C_HEADS, pos_tiles, MLA_QK, tm), BF16),
                   jax.ShapeDtypeStruct((m, C_HEADS * MLA_QK), BF16),
                   jax.ShapeDtypeStruct((b, C_HEADS, pos_tiles, VT_ROWS, tm), BF16)],
        compiler_params=_params("parallel"),
    )(x, g.reshape(1, d), w_in, qn.reshape(1, -1), kvn.reshape(1, -1), wqt, wk, wvt, cos, sin, cos.T, sin.T)


def _mla_attn_kernel(qt_ref, k_ref, vt_ref, o_ref, s_scr, acc_scr, m_scr, *, n_q):
    tq = tk = MLA_TQ
    parts = tq // MLA_TK
    ki = lax.broadcasted_iota(jnp.int32, (tk, tq), 0)
    qi = lax.broadcasted_iota(jnp.int32, (tk, tq), 1)
    causal = ki <= qi
    heads = range(MLA_HEADS)

    def q_tile(t, c):
        q_ts = [jnp.concatenate([qt_ref[0, h, parts * t + d] for d in range(parts)], axis=1) for h in heads]

        def logits_to(slot, j):
            rows = pl.ds(pl.multiple_of(j * tk, tk), tk)
            for h in heads:
                s_scr[h, slot] = _dot(k_ref[0, rows, h * MLA_QK:(h + 1) * MLA_QK], q_ts[h])

        def update_from(slot, j, diagonal):
            for h in heads:
                s = s_scr[h, slot]
                if diagonal:
                    s = jnp.where(causal, s, NEG)
                m = m_scr[h]
                mn = jnp.maximum(m, jnp.max(s, axis=0, keepdims=True))
                p = jnp.exp2(s - mn).astype(BF16)
                m_scr[h] = mn
                v_t = jnp.concatenate([vt_ref[0, h, parts * j + d] for d in range(parts)], axis=1)
                acc_scr[h] = jnp.exp2(m - mn) * acc_scr[h] + _dot(v_t, p)

        m_scr[...] = jnp.full(m_scr.shape, NEG, F32)
        acc_scr[...] = jnp.zeros(acc_scr.shape, F32)
        logits_to(0, t)
        logits_to(1, 0)
        update_from(0, t, True)

        def body(jj, c2):
            logits_to(0, jnp.minimum(2 * jj + 1, t - 1))
            update_from(1, 2 * jj, False)
            logits_to(1, jnp.minimum(2 * jj + 2, t - 1))

            @pl.when(2 * jj + 1 < t)
            def _():
                update_from(0, 2 * jj + 1, False)
            return c2

        lax.fori_loop(0, (t + 1) // 2, body, 0)
        rows = pl.ds(pl.multiple_of(t * tq, tq), tq)
        for h in heads:
            acc = acc_scr[h]
            o = acc[:V_DIM] * (1.0 / acc[V_DIM:V_DIM + 1])
            o_ref[0, rows, h * LANES:(h + 1) * LANES] = o.T.astype(o_ref.dtype)
        return c

    lax.fori_loop(0, n_q, q_tile, 0)


def _mla_attention(qt, k, vt):
    b, s, _ = k.shape
    n_k = s // MLA_TK
    nh = MLA_HEADS
    return pl.pallas_call(
        functools.partial(_mla_attn_kernel, n_q=s // MLA_TQ),
        grid=(b, C_HEADS // nh),
        in_specs=[pl.BlockSpec((1, nh, n_k, MLA_QK, MLA_TK), lambda bi, h: (bi, h, 0, 0, 0)),
                  pl.BlockSpec((1, s, MLA_QK * nh), lambda bi, h: (bi, 0, h)),
                  pl.BlockSpec((1, nh, n_k, VT_ROWS, MLA_TK), lambda bi, h: (bi, h, 0, 0, 0))],
        out_specs=pl.BlockSpec((1, s, LANES * nh), lambda bi, h: (bi, 0, h)),
        out_shape=jax.ShapeDtypeStruct((b, s, C_HEADS * V_DIM), BF16),
        scratch_shapes=[pltpu.VMEM((nh, 2, MLA_TQ, MLA_TQ), F32),
                        pltpu.VMEM((nh, VT_ROWS, MLA_TQ), F32),
                        pltpu.VMEM((nh, 1, MLA_TQ), F32)],
        compiler_params=_params("parallel", "arbitrary"),
    )(qt, k, vt)


def _even_in_weight(w):
    cuts = np.cumsum([1024, 128, 128, 1024, 128, 128, 128, 128, 128, 128, 48])[:-1]
    qa, ka, va, qb, kc, vc, ks, vs, kw, vw, gate = jnp.split(w.astype(BF16), [int(c) for c in cuts], axis=1)
    gate = jnp.pad(gate, ((0, 0), (0, LANES - gate.shape[1])))
    return jnp.concatenate([qa, qb, ka, va, kc, vc, ks, vs, kw, vw, gate], axis=1)


def _compress_weights(pos, w1, w2):
    half = CMP_BLOCK // 2
    eye = jnp.eye(GROUPS, dtype=BF16)
    w1r = w1.astype(BF16).reshape(2, half, HEAD_DIM, CMP_HIDDEN)
    wab = jnp.einsum('rldh,gk->rlgdkh', w1r, eye).reshape(2, half * GROUPS * HEAD_DIM, GROUPS * CMP_HIDDEN)
    posr = jnp.broadcast_to(pos.reshape(2, half, 1, HEAD_DIM), (2, half, GROUPS, HEAD_DIM)).reshape(2, 1, -1)
    w2bd = jnp.einsum('hd,gk->ghkd', w2.astype(BF16), eye).reshape(GROUPS * CMP_HIDDEN, GROUPS * HEAD_DIM)
    return posr[0], posr[1], wab[0], wab[1], w2bd


def _rope_chunk_cols(w):
    z = jnp.zeros(w.shape[:-1] + (ROPE_DIM // 2,), w.dtype)
    return jnp.concatenate([w[..., :ROPE_DIM // 2], z, w[..., ROPE_DIM // 2:], z], axis=-1)


def _mla_weights(w_in, w_q_up, w_kv_up):
    w_in, w_q_up, w_kv_up = w_in.astype(BF16), w_q_up.astype(BF16), w_kv_up.astype(BF16)
    w_in2 = jnp.concatenate([w_in[:, :Q_LORA + KV_LORA], _rope_chunk_cols(w_in[:, Q_LORA + KV_LORA:])], axis=1)
    wq = w_q_up.reshape(Q_LORA, C_HEADS, NOPE_DIM + ROPE_DIM)
    wq2 = jnp.concatenate([wq[..., :NOPE_DIM], _rope_chunk_cols(wq[..., NOPE_DIM:])], axis=-1)
    wqt = wq2.transpose(1, 2, 0)
    wkv = w_kv_up.reshape(KV_LORA, C_HEADS, NOPE_DIM + V_DIM)
    wk = wkv[..., :NOPE_DIM].reshape(KV_LORA, C_HEADS * NOPE_DIM)
    wvt = wkv[..., NOPE_DIM:].transpose(1, 2, 0)
    return w_in2, wqt, wk, wvt


def _rope_tables(s):
    inv = 1.0 / (ROPE_THETA ** (jnp.arange(0, ROPE_DIM, 2, dtype=F32) / ROPE_DIM))
    ang = jnp.arange(s, dtype=F32)[:, None] * inv[None]
    cos, sin = jnp.cos(ang), jnp.sin(ang)
    z = jnp.zeros_like(cos)
    return jnp.concatenate([cos, z, cos, z], axis=1), jnp.concatenate([-sin, z, sin, z], axis=1)


def _static_tables(s):
    k = np.arange(TQ)[:, None]
    q = np.arange(TQ)[None, :]
    diag = np.where(k <= q, _bucket_np(q - k), -1)
    prev = _bucket_np(q - k + TQ)
    near_swa = np.stack([diag, np.where(k > q, prev, -1)])
    near_nsa = np.stack([diag, prev])
    upper = np.where(k > q, 0.0, NEG).astype(np.float32)
    c = np.arange(LANES)[:, None]
    t = np.arange(s)[None, :]
    cdist = t - (c * CMP_STRIDE + CMP_BLOCK - 1)
    cmp_map = np.where(cdist >= 0, _bucket_np(cdist), -1)
    ns = s // SLC_BLOCK
    nc = (s - CMP_BLOCK) // CMP_STRIDE + 1
    c_start = np.arange(LANES) * CMP_STRIDE
    s_start = np.arange(ns) * SLC_BLOCK
    ovt = ((c_start[None, :] <= s_start[:, None] + SLC_BLOCK - 1) &
           (c_start[None, :] + CMP_BLOCK - 1 >= s_start[:, None]) &
           (np.arange(LANES)[None, :] < nc)).astype(np.float32)
    e3 = np.zeros((3, B_HEADS * HEAD_DIM, LANES), np.float32)
    for h in range(B_HEADS):
        for br in range(3):
            e3[br, h * HEAD_DIM:(h + 1) * HEAD_DIM, h * 3 + br] = 1.0
    e3 = e3.reshape(3 * B_HEADS * HEAD_DIM, LANES)
    return near_swa, near_nsa, upper, cmp_map, ovt, e3


def _even_layer(x, b, s, rel_bias, norm, w_in, sinks, pos_k, pos_v, k_w1, k_w2, v_w1, v_w2, w_out):
    near_swa, near_nsa, upper, cmp_map, ovt, e3 = _static_tables(s)
    ta = _bias_tiles(rel_bias, near_swa, 0, A_HEADS)
    tb = _bias_tiles(rel_bias, near_nsa, A_HEADS, B_HEADS)
    cbias = _cmp_bias(rel_bias, cmp_map, A_HEADS, B_HEADS)

    pe = _norm_matmul(x, norm, _even_in_weight(w_in), IN_PROJ_TM, EVEN_COLS, BF16).reshape(b, s, EVEN_COLS)

    rows = s // (CMP_BLOCK // 2)
    hkv = jnp.stack([pe[:, :, SEG_KC * LANES:(SEG_KC + 1) * LANES],
                     pe[:, :, SEG_VC * LANES:(SEG_VC + 1) * LANES]]).reshape(2, b, rows, -1)
    ck = _compress_weights(pos_k, k_w1, k_w2)
    cv = _compress_weights(pos_v, v_w1, v_w2)
    kvc = _compress(hkv, *[jnp.stack([a, c]) for a, c in zip(ck, cv)])
    assert rows <= TQ
    kvc = jnp.pad(kvc, ((0, 0), (0, 0), (0, TQ - rows), (0, 0)))

    oa, ob = _even_attention(pe, kvc, rel_bias, sinks, ta, tb, jnp.asarray(upper), cbias,
                             jnp.asarray(e3, BF16), jnp.asarray(ovt, BF16))
    w_out = w_out.astype(BF16)
    n_a = A_HEADS * HEAD_DIM
    return _out_proj(x, [oa.reshape(b * s, -1), ob.reshape(b * s, -1)], [w_out[:n_a], w_out[n_a:]], OUT_PROJ_TM)


def _odd_layer(x, b, s, norm, w_in, q_norm, w_q_up, kv_norm, w_kv_up, w_out):
    w_in2, wqt, wk, wvt = _mla_weights(w_in, w_q_up, w_kv_up)
    cos, sin = _rope_tables(s)
    qt, k, vt = _mla_prep(x, norm, w_in2, q_norm, kv_norm, wqt, wk, wvt, cos, sin, b, s)
    o = _mla_attention(qt, k.reshape(b, s, -1), vt)
    return _out_proj(x, [o.reshape(b * s, -1)], [w_out.astype(BF16)], OUT_PROJ_TM)


def kernel(x, rel_bias, norm_mix_e, w_in_e, sinks, cmp_pos_k, cmp_pos_v, cmp_k_w1, cmp_k_w2, cmp_v_w1, cmp_v_w2, w_out_e, norm_mix_o, w_in_o, q_norm, w_q_up, kv_norm, w_kv_up, w_out_o, norm_mlp, w_up, w_down, norm_final):
    b, s, d = x.shape
    depth = norm_mlp.shape[0]
    assert s % MLA_TQ == 0 and s // (CMP_BLOCK // 2) <= TQ and s // SLC_BLOCK <= LANES
    assert (b * s) % MLP_TM == 0 and (b * s) % IN_PROJ_TM == 0 and w_up.shape[2] % MLP_TF == 0
    h = x.reshape(b * s, d)
    for layer in range(depth):
        i = layer // 2
        if layer % 2 == 0:
            h = _even_layer(h, b, s, rel_bias, norm_mix_e[i], w_in_e[i], sinks[i], cmp_pos_k[i], cmp_pos_v[i],
                            cmp_k_w1[i], cmp_k_w2[i], cmp_v_w1[i], cmp_v_w2[i], w_out_e[i])
        else:
            h = _odd_layer(h, b, s, norm_mix_o[i], w_in_o[i], q_norm[i], w_q_up[i], kv_norm[i], w_kv_up[i],
                           w_out_o[i])
        h = _mlp(h, norm_mlp[layer], w_up, w_down, layer, norm_final, MLP_TM, MLP_TF, layer == depth - 1)
    return h.reshape(b, s, d)
```

```python
import functools
import math

import numpy as np
import jax
import jax.numpy as jnp
from jax import lax
from jax.experimental import pallas as pl
from jax.experimental.pallas import tpu as pltpu

F32 = jnp.float32
BF16 = jnp.bfloat16

LANES = 128
VMEM_LIMIT_BYTES = 56 * 1024 * 1024

EPS = 1e-6
NEG = -1e30
BIG = 1e4
HEAD_DIM = 64
TQ = 128
MLA_TQ = 512
MLA_TK = 512
MLA_HEADS = 4
IN_PROJ_TM = 1024
OUT_PROJ_TM = 512
MLP_TM, MLP_TF = 1024, 512
N_BUCKETS = 32
MAX_DISTANCE = 128
A_HEADS = 16
B_HEADS = 16
GROUPS = 2
PAIRS = 4
W4 = PAIRS * TQ
CMP_BLOCK = 32
CMP_STRIDE = 16
CMP_HIDDEN = 256
SLC_BLOCK = 64
SLC_TOPK = 8
A_WINDOW = 128
B_WINDOW = 512
C_HEADS = 16
Q_LORA = 768
KV_LORA = 512
NOPE_DIM = 128
ROPE_DIM = 64
V_DIM = 128
MLA_QK = NOPE_DIM + LANES
VT_ROWS = V_DIM + 16
LOG2E = math.log2(math.e)
ROPE_THETA = 10000.0

COL_QA, COL_QB = 0, 1024
SEG_KA, SEG_VA, SEG_KC, SEG_VC, SEG_KS, SEG_VS, SEG_KW, SEG_VW, SEG_GATE = range(16, 25)
EVEN_COLS = 25 * LANES
K_A, K_S, K_W = 0, 1, 2


def _params(*sem):
    return pltpu.CompilerParams(dimension_semantics=sem, vmem_limit_bytes=VMEM_LIMIT_BYTES)


def _rms(x, g):
    return x * lax.rsqrt(jnp.mean(x * x, axis=-1, keepdims=True) + EPS) * g


def _dot(a, b):
    return jnp.dot(a, b, preferred_element_type=F32)


def _run_pipelined(tasks):
    s = tasks[0][0]()
    for n, (_, apply_fn) in enumerate(tasks):
        s_next = tasks[n + 1][0]() if n + 1 < len(tasks) else None
        apply_fn(s)
        s = s_next


def _bucket_np(dist):
    dist = np.maximum(dist, 0)
    max_exact = N_BUCKETS // 2
    d = np.maximum(dist, 1).astype(np.float32)
    large = max_exact + (np.log(d / np.float32(max_exact)) / np.float32(math.log(MAX_DISTANCE / max_exact))
                         * np.float32(N_BUCKETS - max_exact)).astype(np.int32)
    large = np.minimum(large, N_BUCKETS - 1)
    return np.where(dist < max_exact, dist, large).astype(np.int32)


def _bias_kernel(tab_ref, bm_ref, o_ref, *, head0):
    h = pl.program_id(0) + head0
    bm = bm_ref[...]
    acc = jnp.zeros(bm.shape, F32)
    for b in range(N_BUCKETS):
        acc = jnp.where(bm == b, tab_ref[b, h], acc)
    o_ref[0] = jnp.where(bm < 0, NEG, (acc - tab_ref[N_BUCKETS - 1, h]) * LOG2E)


def _cmp_bias_kernel(tab_ref, bm_ref, o_ref, *, head0, bands):
    h = pl.program_id(0) + head0
    far = tab_ref[N_BUCKETS - 1, h]
    rows = lax.broadcasted_iota(jnp.int32, (bm_ref.shape[0], TQ), 0)
    for i, (start, size) in enumerate(bands):
        cols = slice(i * TQ, (i + 1) * TQ)
        o_ref[0, :, cols] = jnp.where(rows < start, 0.0, NEG)
        bm = bm_ref[start:start + size, cols]
        acc = jnp.zeros(bm.shape, F32)
        for b in range(N_BUCKETS):
            acc = jnp.where(bm == b, tab_ref[b, h], acc)
        o_ref[0, start:start + size, cols] = jnp.where(bm < 0, NEG, (acc - far) * LOG2E)


def _cmp_bias(rel_bias, cmp_map, head0, n_heads):
    n_blocks, s = cmp_map.shape
    bands = []
    for i in range(s // TQ):
        tile = cmp_map[:, i * TQ:(i + 1) * TQ]
        varying = np.nonzero(((tile >= 0) & (tile < N_BUCKETS - 1)).any(axis=1))[0]
        lo = int(varying.min()) // 8 * 8 if varying.size else 0
        hi = -(-(int(varying.max()) + 1) // 8) * 8 if varying.size else 8
        assert (tile[:lo] == N_BUCKETS - 1).all() and (tile[hi:] < 0).all()
        bands.append((lo, hi - lo))
    return pl.pallas_call(
        functools.partial(_cmp_bias_kernel, head0=head0, bands=tuple(bands)),
        grid=(n_heads,),
        in_specs=[pl.BlockSpec(memory_space=pltpu.SMEM),
                  pl.BlockSpec(cmp_map.shape, lambda h: (0, 0))],
        out_specs=pl.BlockSpec((1, n_blocks, s), lambda h: (h, 0, 0)),
        out_shape=jax.ShapeDtypeStruct((n_heads, n_blocks, s), F32),
        compiler_params=_params("arbitrary"),
    )(rel_bias, jnp.asarray(cmp_map))


def _bias_tiles(rel_bias, bucket_map, head0, n_heads):
    shp = bucket_map.shape
    nd = len(shp)
    return pl.pallas_call(
        functools.partial(_bias_kernel, head0=head0),
        grid=(n_heads,),
        in_specs=[pl.BlockSpec(memory_space=pltpu.SMEM),
                  pl.BlockSpec(shp, lambda h: (0,) * nd)],
        out_specs=pl.BlockSpec((1,) + shp, lambda h: (h,) + (0,) * nd),
        out_shape=jax.ShapeDtypeStruct((n_heads,) + shp, F32),
        compiler_params=_params("arbitrary"),
    )(rel_bias, jnp.asarray(bucket_map))


def _norm_matmul_kernel(x_ref, g_ref, w_ref, o_ref, xn_ref):
    @pl.when(pl.program_id(1) == 0)
    def _():
        xn_ref[...] = _rms(x_ref[...], g_ref[...]).astype(BF16)

    o_ref[...] = _dot(xn_ref[...], w_ref[...]).astype(o_ref.dtype)


def _norm_matmul(x, g, w, tm, tn, out_dtype):
    m, d = x.shape
    n = w.shape[1]
    return pl.pallas_call(
        _norm_matmul_kernel,
        grid=(m // tm, n // tn),
        in_specs=[pl.BlockSpec((tm, d), lambda i, j: (i, 0)),
                  pl.BlockSpec((1, d), lambda i, j: (0, 0)),
                  pl.BlockSpec((d, tn), lambda i, j: (0, j),
                               pipeline_mode=pl.Buffered(1) if tn == n else None)],
        out_specs=pl.BlockSpec((tm, tn), lambda i, j: (i, j)),
        out_shape=jax.ShapeDtypeStruct((m, n), out_dtype),
        scratch_shapes=[pltpu.VMEM((tm, d), BF16)],
        compiler_params=_params("parallel", "arbitrary"),
    )(x, g.reshape(1, d), w)


def _out_proj_kernel(*refs, n_in):
    x_ref = refs[0]
    o_ref = refs[2 * n_in + 1]
    acc = x_ref[...]
    for t in range(n_in):
        acc = acc + _dot(refs[1 + t][...], refs[1 + n_in + t][...])
    o_ref[...] = acc


def _out_proj(x, acts, ws, tm):
    m, d = x.shape
    n_in = len(acts)
    in_specs = [pl.BlockSpec((tm, d), lambda i: (i, 0))]
    in_specs += [pl.BlockSpec((tm, a.shape[1]), lambda i: (i, 0)) for a in acts]
    in_specs += [pl.BlockSpec(w.shape, lambda i: (0, 0), pipeline_mode=pl.Buffered(1)) for w in ws]
    return pl.pallas_call(
        functools.partial(_out_proj_kernel, n_in=n_in),
        grid=(m // tm,),
        in_specs=in_specs,
        out_specs=pl.BlockSpec((tm, d), lambda i: (i, 0)),
        out_shape=jax.ShapeDtypeStruct((m, d), F32),
        compiler_params=_params("parallel"),
    )(x, *acts, *ws)


def _mlp_kernel(x_ref, g_ref, wu_ref, wd_ref, gf_ref, o_ref, xn_ref, *, final_norm):
    j = pl.program_id(1)

    @pl.when(j == 0)
    def _():
        x = x_ref[...]
        xn_ref[...] = _rms(x, g_ref[...]).astype(BF16)
        o_ref[...] = x

    h = _dot(xn_ref[...], wu_ref[0].astype(BF16))
    a = jnp.square(jnp.maximum(h, 0.0)).astype(BF16)
    o_ref[...] += _dot(a, wd_ref[0].astype(BF16))

    if final_norm:
        @pl.when(j == pl.num_programs(1) - 1)
        def _():
            o_ref[...] = _rms(o_ref[...], gf_ref[...])


def _mlp(x, g, w_up, w_down, layer, g_final, tm, tf, final_norm):
    m, d = x.shape
    ff = w_up.shape[2]
    return pl.pallas_call(
        functools.partial(_mlp_kernel, final_norm=final_norm),
        grid=(m // tm, ff // tf),
        in_specs=[pl.BlockSpec((tm, d), lambda i, j: (i, 0)),
                  pl.BlockSpec((1, d), lambda i, j: (0, 0)),
                  pl.BlockSpec((1, d, tf), lambda i, j: (layer, 0, j)),
                  pl.BlockSpec((1, tf, d), lambda i, j: (layer, j, 0)),
                  pl.BlockSpec((1, d), lambda i, j: (0, 0))],
        out_specs=pl.BlockSpec((tm, d), lambda i, j: (i, 0)),
        out_shape=jax.ShapeDtypeStruct((m, d), F32),
        scratch_shapes=[pltpu.VMEM((tm, d), BF16)],
        compiler_params=_params("parallel", "arbitrary"),
    )(x, g.reshape(1, d), w_up, w_down, g_final.reshape(1, d))


def _gelu_tanh(x):
    return 0.5 * x * (1.0 + jnp.tanh(math.sqrt(2.0 / math.pi) * (x + 0.044715 * (x * x * x))))


def _compress_kernel(h_ref, pa_ref, pb_ref, wa_ref, wb_ref, w2_ref, o_ref):
    h = h_ref[0, 0].astype(F32)
    ha = _dot((h + pa_ref[0]).astype(BF16), wa_ref[0])
    hb = _dot((h + pb_ref[0]).astype(BF16), wb_ref[0])
    n = hb.shape[0]
    pre = ha + pltpu.roll(hb, n - 1, 0)
    o_ref[0, 0] = _dot(_gelu_tanh(pre).astype(BF16), w2_ref[0]).astype(o_ref.dtype)


def _compress(hkv, pa, pb, wa, wb, w2):
    _, b, r, w = hkv.shape
    hid2 = wa.shape[2]
    return pl.pallas_call(
        _compress_kernel,
        grid=(2, b),
        in_specs=[pl.BlockSpec((1, 1, r, w), lambda t, i: (t, i, 0, 0)),
                  pl.BlockSpec((1, 1, w), lambda t, i: (t, 0, 0)),
                  pl.BlockSpec((1, 1, w), lambda t, i: (t, 0, 0)),
                  pl.BlockSpec((1, w, hid2), lambda t, i: (t, 0, 0)),
                  pl.BlockSpec((1, w, hid2), lambda t, i: (t, 0, 0)),
                  pl.BlockSpec((1, hid2, LANES), lambda t, i: (t, 0, 0))],
        out_specs=pl.BlockSpec((1, 1, r, LANES), lambda t, i: (t, i, 0, 0)),
        out_shape=jax.ShapeDtypeStruct((2, b, r, LANES), BF16),
        compiler_params=_params("arbitrary", "arbitrary"),
    )(hkv, pa, pb, wa, wb, w2)


SWA, SLC, WIN = 0, 1, 2
FAR_UNROLL = 4


def _stack_pairs_t(q_ref, g):
    cols = [q_ref[0, :, (PAIRS * g + p) * LANES:(PAIRS * g + p + 1) * LANES].astype(F32).T for p in range(PAIRS)]
    return (jnp.concatenate(cols, axis=1) * (HEAD_DIM ** -0.5 * LOG2E)).astype(BF16)


def _rows_by_half(a, b, rows):
    parts = [jnp.broadcast_to(a, (HEAD_DIM, W4)), jnp.broadcast_to(b, (HEAD_DIM, W4))]
    if rows > 2 * HEAD_DIM:
        first = lax.broadcasted_iota(jnp.int32, (rows - 2 * HEAD_DIM, W4), 0) == 0
        parts.append(jnp.where(first, a, b))
    return jnp.concatenate(parts, axis=0)


def _even_attn_kernel(tab_ref, sink_ref, qa_ref, qb_ref, gate_ref,
                      ka_ref, va_ref, ks_ref, vs_ref, kw_ref, vw_ref, kc_ref, vc_ref,
                      ta_ref, tb_ref, up_ref, cb_ref, e3_ref, ovt_ref,
                      oa_ref, ob_ref,
                      kbd_ref, vbd_ref, ckbd_ref, cvbd_ref, msk_ref, acc_scr, m_scr, *, n_tiles):
    i = pl.program_id(1)
    groups = range(GROUPS)

    def block_diag(x, g):
        r = pltpu.roll(x, HEAD_DIM, 1)
        lo = lax.broadcasted_iota(jnp.int32, x.shape, 1) < HEAD_DIM
        own, other = (x, r) if g == 0 else (r, x)
        return jnp.where(lo, own, 0.0), jnp.where(lo, 0.0, other)

    @pl.when(i == 0)
    def _build():
        r16 = lax.broadcasted_iota(jnp.int32, (VT_ROWS - LANES, 2 * TQ), 0)
        c16 = lax.broadcasted_iota(jnp.int32, (VT_ROWS - LANES, 2 * TQ), 1)
        ones_rows = jnp.where((r16 == 0) & (c16 < TQ) | (r16 == 1) & (c16 >= TQ), 1.0, 0.0).astype(BF16)
        for t, ref in enumerate((ka_ref, ks_ref, kw_ref)):
            def kbody(j, c, t=t, ref=ref):
                x = ref[0, pl.ds(pl.multiple_of(j * TQ, TQ), TQ), :].astype(F32)
                for g in groups:
                    top, bot = block_diag(x, g)
                    kbd_ref[g, t, j, 0:TQ, :] = top.astype(BF16)
                    kbd_ref[g, t, j, TQ:2 * TQ, :] = bot.astype(BF16)
                return c
            lax.fori_loop(0, n_tiles, kbody, 0)
        for t, ref in enumerate((va_ref, vs_ref, vw_ref)):
            def vbody(j, c, t=t, ref=ref):
                x = ref[0, pl.ds(pl.multiple_of(j * TQ, TQ), TQ), :].astype(F32)
                for g in groups:
                    top, bot = block_diag(x, g)
                    vbd_ref[g, t, j, 0:LANES, 0:TQ] = top.T.astype(BF16)
                    vbd_ref[g, t, j, 0:LANES, TQ:2 * TQ] = bot.T.astype(BF16)
                    vbd_ref[g, t, j, LANES:, :] = ones_rows
                return c
            lax.fori_loop(0, n_tiles, vbody, 0)
        xk = kc_ref[0, 0].astype(F32)
        xv = vc_ref[0, 0].astype(F32)
        for g in groups:
            top, bot = block_diag(xk, g)
            ckbd_ref[g, 0:TQ, :] = top.astype(BF16)
            ckbd_ref[g, TQ:2 * TQ, :] = bot.astype(BF16)
            top, bot = block_diag(xv, g)
            cvbd_ref[g, :, 0:TQ] = top.T.astype(BF16)
            cvbd_ref[g, :, TQ:2 * TQ] = bot.T.astype(BF16)

    def near_bias(t_ref, g, delta):
        return lambda hf, p: t_ref[8 * g + 2 * p + hf, delta]

    def update(g, st, s, adds, vbd_t, first):
        m_old = [None if first else m_scr[g, st, hf] for hf in range(2)]
        p_rows, m_new = [], []
        for hf in range(2):
            strips, mns = [], []
            for p in range(PAIRS):
                sh = s[hf * TQ:(hf + 1) * TQ, p * TQ:(p + 1) * TQ]
                for a in adds:
                    sh = sh + a(hf, p)
                mn = jnp.max(sh, axis=0, keepdims=True)
                if not first:
                    mn = jnp.maximum(m_old[hf][:, p * TQ:(p + 1) * TQ], mn)
                strips.append(jnp.exp2(sh - mn).astype(BF16))
                mns.append(mn)
            p_rows.append(jnp.concatenate(strips, axis=1))
            m_new.append(jnp.concatenate(mns, axis=1))
            m_scr[g, st, hf] = m_new[hf]
        pv = _dot(vbd_t, jnp.concatenate(p_rows, axis=0))
        if first:
            acc_scr[g, st] = pv
        else:
            alpha = _rows_by_half(jnp.exp2(m_old[0] - m_new[0]), jnp.exp2(m_old[1] - m_new[1]), VT_ROWS)
            acc_scr[g, st] = alpha * acc_scr[g, st] + pv

    run_pipelined = _run_pipelined

    def tile_task(g, st, slot, j, q4t, adds, first=False):
        return (lambda: _dot(kbd_ref[g, slot, j], q4t),
                lambda s: update(g, st, s, adds, vbd_ref[g, slot, j], first))

    def finish(g, st, extra=None):
        acc = acc_scr[g, st]
        den = [acc[LANES + hf:LANES + hf + 1] for hf in range(2)]
        if extra is not None:
            den = [den[hf] + extra(hf, m_scr[g, st, hf]) for hf in range(2)]
        return acc[:LANES] * _rows_by_half(1.0 / den[0], 1.0 / den[1], LANES)

    q4a = [_stack_pairs_t(qa_ref, g) for g in groups]
    q4b = [_stack_pairs_t(qb_ref, g) for g in groups]

    run_pipelined([tile_task(g, SWA, K_A, i, q4a[g], [near_bias(ta_ref, g, 0)], first=True) for g in groups] +
                  [tile_task(g, WIN, K_W, i, q4b[g], [near_bias(tb_ref, g, 0)], first=True) for g in groups])

    t_q = i * TQ + lax.broadcasted_iota(jnp.int32, (1, TQ), 1)
    anyvis = jnp.where(t_q >= CMP_BLOCK - 1, 1.0, 0.0)
    s_cmp = [_dot(ckbd_ref[g], q4b[g]) for g in groups]
    psum = [jnp.zeros((TQ, TQ), F32) for _ in groups]
    p_rows = [[] for _ in groups]
    for hf in range(2):
        strips = [[] for _ in groups]
        for p in range(PAIRS):
            for g in groups:
                sh = s_cmp[g][hf * TQ:(hf + 1) * TQ, p * TQ:(p + 1) * TQ] + cb_ref[8 * g + 2 * p + hf]
                pe = jnp.exp2(sh - jnp.max(sh, axis=0, keepdims=True))
                pc = pe * (anyvis / jnp.sum(pe, axis=0, keepdims=True))
                psum[g] = psum[g] + pc
                strips[g].append(pc.astype(BF16))
        for g in groups:
            p_rows[g].append(jnp.concatenate(strips[g], axis=1))
    o_cmp = [_dot(cvbd_ref[g], jnp.concatenate(p_rows[g], axis=0)) for g in groups]

    psum2 = jnp.concatenate(psum, axis=1)
    ph = psum2.astype(BF16)
    r1 = psum2 - ph.astype(F32)
    pm = r1.astype(BF16)
    pl_ = (r1 - pm.astype(F32)).astype(BF16)
    ovt = ovt_ref[...]
    pslc = _dot(ovt, ph) + _dot(ovt, pm) + _dot(ovt, pl_)
    ns = pslc.shape[0]
    nb = lax.broadcasted_iota(jnp.int32, (ns, GROUPS * TQ), 0)
    tq = i * TQ + (lax.broadcasted_iota(jnp.int32, (ns, GROUPS * TQ), 1) & (TQ - 1))
    cur = tq // SLC_BLOCK
    forced = (jnp.where(nb == 0, 1.0, 0.0) + jnp.where(nb == cur, 1.0, 0.0) +
              jnp.where(nb == cur - 1, 1.0, 0.0))
    score = jnp.where(nb * SLC_BLOCK > tq, NEG, jnp.where(forced > 0.0, BIG, pslc))
    rank = jnp.zeros((ns, GROUPS * TQ), F32)
    for mth in range(ns):
        sm = score[mth:mth + 1, :]
        tie = jnp.where(nb > mth, 1.0, 0.0)
        rank = rank + jnp.where(sm > score, 1.0, jnp.where(sm == score, tie, 0.0))
    sel_rows = jnp.where(rank < SLC_TOPK, 0.0, NEG)
    for g in groups:
        msk_ref[g] = sel_rows[:, g * TQ:(g + 1) * TQ]

    def sel_add(g, j):
        cache = []

        def tile(hf, p):
            if not cache:
                rows = [jnp.broadcast_to(msk_ref[g, pl.ds(2 * j + e, 1), :], (SLC_BLOCK, TQ)) for e in range(2)]
                cache.append(jnp.concatenate(rows, axis=0))
            return cache[0]
        return tile

    n_full = B_WINDOW // TQ

    def near_tiles(n_back):
        tasks = [tile_task(g, SLC, K_S, i, q4b[g], [near_bias(tb_ref, g, 0), sel_add(g, i)], first=True)
                 for g in groups]
        if n_back >= 1:
            j = i - 1
            tasks += [tile_task(g, SWA, K_A, j, q4a[g], [near_bias(ta_ref, g, 1)]) for g in groups]
            tasks += [tile_task(g, WIN, K_W, j, q4b[g], [near_bias(tb_ref, g, 1)]) for g in groups]
            tasks += [tile_task(g, SLC, K_S, j, q4b[g], [near_bias(tb_ref, g, 1), sel_add(g, j)]) for g in groups]
        for d in range(2, n_back + 1):
            edge = [lambda hf, p: up_ref[...]] if d == n_full else []
            tasks += [tile_task(g, WIN, K_W, i - d, q4b[g], edge) for g in groups]
        run_pipelined(tasks)

    for c in range(n_full):
        pl.when(i == c)(functools.partial(near_tiles, c))
    pl.when(i >= n_full)(functools.partial(near_tiles, n_full))

    def far_tasks(d0, count):
        return [tile_task(g, SLC, K_S, i - d, q4b[g], [sel_add(g, i - d)])
                for d in [d0 + e for e in range(count)] for g in groups]

    def far_body(jj, c):
        run_pipelined(far_tasks(2 + FAR_UNROLL * jj, FAR_UNROLL))
        return c

    n_far = jnp.maximum(i - 1, 0)
    lax.fori_loop(0, n_far // FAR_UNROLL, far_body, 0)
    for rem in range(1, FAR_UNROLL):
        pl.when(n_far % FAR_UNROLL == rem)(
            functools.partial(lambda rem: run_pipelined(far_tasks(i - rem + 1, rem)), rem))

    def to_rows(o_t, p):
        return o_t[:, p * TQ:(p + 1) * TQ].T

    far_of = lambda h: tab_ref[N_BUCKETS - 1, h]
    sg = jax.nn.sigmoid(gate_ref[0].astype(F32)).T
    sg_hi = sg.astype(BF16)
    sg_lo = (sg - sg_hi.astype(F32)).astype(BF16)
    gates_t = _dot(e3_ref[...], sg_hi) + _dot(e3_ref[...], sg_lo)
    for g in groups:
        def sink_term(hf, m, g=g):
            row = jnp.concatenate([jnp.full((1, TQ), (sink_ref[8 * g + 2 * p + hf] - far_of(8 * g + 2 * p + hf)) * LOG2E,
                                            F32) for p in range(PAIRS)], axis=1)
            return jnp.exp2(row - m)

        o_swa = finish(g, SWA, sink_term)
        o_slc = finish(g, SLC)
        o_win = finish(g, WIN)
        for p in range(PAIRS):
            cols = slice((PAIRS * g + p) * LANES, (PAIRS * g + p + 1) * LANES)
            oa_ref[0, :, cols] = to_rows(o_swa, p).astype(oa_ref.dtype)
            qs = slice(p * TQ, (p + 1) * TQ)
            gate = lambda br: gates_t[br * B_HEADS * HEAD_DIM + cols.start:br * B_HEADS * HEAD_DIM + cols.stop]
            o_t = gate(0) * o_cmp[g][:, qs] + gate(1) * o_slc[:, qs] + gate(2) * o_win[:, qs]
            ob_ref[0, :, cols] = o_t.T.astype(ob_ref.dtype)


def _even_attention(pe, kvc, rel_bias, sinks, ta, tb, up, cbias, e3, ovt):
    b, s, _ = pe.shape
    n_tiles = s // TQ
    width = GROUPS * PAIRS * LANES

    def seg(k):
        return pl.BlockSpec((1, s, LANES), lambda bi, i, k=k: (bi, 0, k))

    full = lambda a: pl.BlockSpec(a.shape, lambda bi, i: (0,) * a.ndim)
    in_specs = [
        pl.BlockSpec(memory_space=pltpu.SMEM),
        pl.BlockSpec(memory_space=pltpu.SMEM),
        pl.BlockSpec((1, TQ, width), lambda bi, i: (bi, i, COL_QA // width)),
        pl.BlockSpec((1, TQ, width), lambda bi, i: (bi, i, COL_QB // width)),
        pl.BlockSpec((1, TQ, LANES), lambda bi, i: (bi, i, SEG_GATE)),
        seg(SEG_KA), seg(SEG_VA), seg(SEG_KS), seg(SEG_VS), seg(SEG_KW), seg(SEG_VW),
        pl.BlockSpec((1, 1, TQ, LANES), lambda bi, i: (0, bi, 0, 0)),
        pl.BlockSpec((1, 1, TQ, LANES), lambda bi, i: (1, bi, 0, 0)),
        full(ta), full(tb), full(up),
        pl.BlockSpec((B_HEADS, TQ, TQ), lambda bi, i: (0, 0, i)),
        full(e3), full(ovt),
    ]
    out_spec = pl.BlockSpec((1, TQ, width), lambda bi, i: (bi, i, 0))
    return pl.pallas_call(
        functools.partial(_even_attn_kernel, n_tiles=n_tiles),
        grid=(b, n_tiles),
        in_specs=in_specs,
        out_specs=[out_spec, out_spec],
        out_shape=[jax.ShapeDtypeStruct((b, s, width), BF16)] * 2,
        scratch_shapes=[pltpu.VMEM((GROUPS, 3, n_tiles, 2 * TQ, LANES), BF16),
                        pltpu.VMEM((GROUPS, 3, n_tiles, VT_ROWS, 2 * TQ), BF16),
                        pltpu.VMEM((GROUPS, 2 * TQ, LANES), BF16),
                        pltpu.VMEM((GROUPS, LANES, 2 * TQ), BF16),
                        pltpu.VMEM((GROUPS, s // SLC_BLOCK, TQ), F32),
                        pltpu.VMEM((GROUPS, 3, VT_ROWS, W4), F32),
                        pltpu.VMEM((GROUPS, 3, 2, 1, W4), F32)],
        compiler_params=_params("arbitrary", "arbitrary"),
    )(rel_bias, sinks, pe, pe, pe, pe, pe, pe, pe, pe, pe, kvc, kvc, ta, tb, up, cbias, e3, ovt)


def _mla_prep_kernel(x_ref, g_ref, win_ref, qn_ref, kvn_ref, wqt_ref, wk_ref, wvt_ref,
                     cos_ref, sin_ref, cost_ref, sint_ref, qt_ref, k_ref, vt_ref):
    xn = _rms(x_ref[...], g_ref[...]).astype(BF16)
    proj = _dot(xn, win_ref[...])
    cq = _rms(proj[:, :Q_LORA], qn_ref[...])
    ckv = _rms(proj[:, Q_LORA:Q_LORA + KV_LORA], kvn_ref[...])
    cq_t = cq.T.astype(BF16)
    ckv_t = ckv.T.astype(BF16)
    ckv = ckv.astype(BF16)
    kr = proj[:, Q_LORA + KV_LORA:]
    kr = (kr * cos_ref[...] + pltpu.roll(kr, HEAD_DIM, 1) * sin_ref[...]).astype(BF16)
    cos_t = cost_ref[...]
    sin_t = sint_ref[...]
    scale = (NOPE_DIM + ROPE_DIM) ** -0.5 * LOG2E
    tm = ckv_t.shape[1]
    ones_rows = jnp.where(lax.broadcasted_iota(jnp.int32, (VT_ROWS - V_DIM, tm), 0) == 0, 1.0, 0.0).astype(BF16)
    for h in range(C_HEADS):
        q_t = _dot(wqt_ref[h], cq_t)
        rp = q_t[NOPE_DIM:]
        rp = rp * cos_t + pltpu.roll(rp, HEAD_DIM, 0) * sin_t
        qt_ref[0, h, 0, 0:NOPE_DIM, :] = (q_t[:NOPE_DIM] * scale).astype(BF16)
        qt_ref[0, h, 0, NOPE_DIM:, :] = (rp * scale).astype(BF16)
        if h % 2 == 0:
            k2 = _dot(ckv, wk_ref[:, h * LANES:(h + 2) * LANES]).astype(BF16)
        k_ref[:, h * MLA_QK:h * MLA_QK + NOPE_DIM] = k2[:, (h % 2) * LANES:(h % 2 + 1) * LANES]
        k_ref[:, h * MLA_QK + NOPE_DIM:(h + 1) * MLA_QK] = kr
        vt_ref[0, h, 0, 0:V_DIM, :] = _dot(wvt_ref[h], ckv_t).astype(BF16)
        vt_ref[0, h, 0, V_DIM:, :] = ones_rows


def _mla_prep(x, g, w_in, qn, kvn, wqt, wk, wvt, cos, sin, b, s):
    m, d = x.shape
    tm = MLA_TK
    pos_tiles = s // tm
    full = lambda a: pl.BlockSpec(a.shape, lambda i: (0,) * a.ndim, pipeline_mode=pl.Buffered(1))
    tile_t = lambda rows: pl.BlockSpec((1, C_HEADS, 1, rows, tm), lambda i: (i // pos_tiles, 0, i % pos_tiles, 0, 0))
    return pl.pallas_call(
        _mla_prep_kernel,
        grid=(m // tm,),
        in_specs=[pl.BlockSpec((tm, d), lambda i: (i, 0)),
                  pl.BlockSpec((1, d), lambda i: (0, 0)),
                  full(w_in), pl.BlockSpec((1, Q_LORA), lambda i: (0, 0)),
                  pl.BlockSpec((1, KV_LORA), lambda i: (0, 0)), full(wqt), full(wk), full(wvt),
                  pl.BlockSpec((tm, LANES), lambda i: (i % pos_tiles, 0)),
                  pl.BlockSpec((tm, LANES), lambda i: (i % pos_tiles, 0)),
                  pl.BlockSpec((LANES, tm), lambda i: (0, i % pos_tiles)),
                  pl.BlockSpec((LANES, tm), lambda i: (0, i % pos_tiles))],
        out_specs=[tile_t(MLA_QK),
                   pl.BlockSpec((tm, C_HEADS * MLA_QK), lambda i: (i, 0)),
                   tile_t(VT_ROWS)],
        out_shape=[jax.ShapeDtypeStruct((b, C_HEADS, pos_tiles, MLA_QK, tm), BF16),
                   jax.ShapeDtypeStruct((m, C_HEADS * MLA_QK), BF16),
                   jax.ShapeDtypeStruct((b, C_HEADS, pos_tiles, VT_ROWS, tm), BF16)],
        compiler_params=_params("parallel"),
    )(x, g.reshape(1, d), w_in, qn.reshape(1, -1), kvn.reshape(1, -1), wqt, wk, wvt, cos, sin, cos.T, sin.T)


def _mla_attn_kernel(qt_ref, k_ref, vt_ref, o_ref, s_scr, acc_scr, m_scr, *, n_q):
    tq = tk = MLA_TQ
    parts = tq // MLA_TK
    ki = lax.broadcasted_iota(jnp.int32, (tk, tq), 0)
    qi = lax.broadcasted_iota(jnp.int32, (tk, tq), 1)
    causal = ki <= qi
    heads = range(MLA_HEADS)

    def q_tile(t, c):
        q_ts = [jnp.concatenate([qt_ref[0, h, parts * t + d] for d in range(parts)], axis=1) for h in heads]

        def logits_to(slot, j):
            rows = pl.ds(pl.multiple_of(j * tk, tk), tk)
            for h in heads:
                s_scr[h, slot] = _dot(k_ref[0, rows, h * MLA_QK:(h + 1) * MLA_QK], q_ts[h])

        def update_from(slot, j, diagonal):
            for h in heads:
                s = s_scr[h, slot]
                if diagonal:
                    s = jnp.where(causal, s, NEG)
                m = m_scr[h]
                mn = jnp.maximum(m, jnp.max(s, axis=0, keepdims=True))
                p = jnp.exp2(s - mn).astype(BF16)
                m_scr[h] = mn
                v_t = jnp.concatenate([vt_ref[0, h, parts * j + d] for d in range(parts)], axis=1)
                acc_scr[h] = jnp.exp2(m - mn) * acc_scr[h] + _dot(v_t, p)

        m_scr[...] = jnp.full(m_scr.shape, NEG, F32)
        acc_scr[...] = jnp.zeros(acc_scr.shape, F32)
        logits_to(0, t)
        logits_to(1, 0)
        update_from(0, t, True)

        def body(jj, c2):
            logits_to(0, jnp.minimum(2 * jj + 1, t - 1))
            update_from(1, 2 * jj, False)
            logits_to(1, jnp.minimum(2 * jj + 2, t - 1))

            @pl.when(2 * jj + 1 < t)
            def _():
                update_from(0, 2 * jj + 1, False)
            return c2

        lax.fori_loop(0, (t + 1) // 2, body, 0)
        rows = pl.ds(pl.multiple_of(t * tq, tq), tq)
        for h in heads:
            acc = acc_scr[h]
            o = acc[:V_DIM] * (1.0 / acc[V_DIM:V_DIM + 1])
            o_ref[0, rows, h * LANES:(h + 1) * LANES] = o.T.astype(o_ref.dtype)
        return c

    lax.fori_loop(0, n_q, q_tile, 0)


def _mla_attention(qt, k, vt):
    b, s, _ = k.shape
    n_k = s // MLA_TK
    nh = MLA_HEADS
    return pl.pallas_call(
        functools.partial(_mla_attn_kernel, n_q=s // MLA_TQ),
        grid=(b, C_HEADS // nh),
        in_specs=[pl.BlockSpec((1, nh, n_k, MLA_QK, MLA_TK), lambda bi, h: (bi, h, 0, 0, 0)),
                  pl.BlockSpec((1, s, MLA_QK * nh), lambda bi, h: (bi, 0, h)),
                  pl.BlockSpec((1, nh, n_k, VT_ROWS, MLA_TK), lambda bi, h: (bi, h, 0, 0, 0))],
        out_specs=pl.BlockSpec((1, s, LANES * nh), lambda bi, h: (bi, 0, h)),
        out_shape=jax.ShapeDtypeStruct((b, s, C_HEADS * V_DIM), BF16),
        scratch_shapes=[pltpu.VMEM((nh, 2, MLA_TQ, MLA_TQ), F32),
                        pltpu.VMEM((nh, VT_ROWS, MLA_TQ), F32),
                        pltpu.VMEM((nh, 1, MLA_TQ), F32)],
        compiler_params=_params("parallel", "arbitrary"),
    )(qt, k, vt)


def _even_in_weight(w):
    cuts = np.cumsum([1024, 128, 128, 1024, 128, 128, 128, 128, 128, 128, 48])[:-1]
    qa, ka, va, qb, kc, vc, ks, vs, kw, vw, gate = jnp.split(w.astype(BF16), [int(c) for c in cuts], axis=1)
    gate = jnp.pad(gate, ((0, 0), (0, LANES - gate.shape[1])))
    return jnp.concatenate([qa, qb, ka, va, kc, vc, ks, vs, kw, vw, gate], axis=1)


def _compress_weights(pos, w1, w2):
    half = CMP_BLOCK // 2
    eye = jnp.eye(GROUPS, dtype=BF16)
    w1r = w1.astype(BF16).reshape(2, half, HEAD_DIM, CMP_HIDDEN)
    wab = jnp.einsum('rldh,gk->rlgdkh', w1r, eye).reshape(2, half * GROUPS * HEAD_DIM, GROUPS * CMP_HIDDEN)
    posr = jnp.broadcast_to(pos.reshape(2, half, 1, HEAD_DIM), (2, half, GROUPS, HEAD_DIM)).reshape(2, 1, -1)
    w2bd = jnp.einsum('hd,gk->ghkd', w2.astype(BF16), eye).reshape(GROUPS * CMP_HIDDEN, GROUPS * HEAD_DIM)
    return posr[0], posr[1], wab[0], wab[1], w2bd


def _rope_chunk_cols(w):
    z = jnp.zeros(w.shape[:-1] + (ROPE_DIM // 2,), w.dtype)
    return jnp.concatenate([w[..., :ROPE_DIM // 2], z, w[..., ROPE_DIM // 2:], z], axis=-1)


def _mla_weights(w_in, w_q_up, w_kv_up):
    w_in, w_q_up, w_kv_up = w_in.astype(BF16), w_q_up.astype(BF16), w_kv_up.astype(BF16)
    w_in2 = jnp.concatenate([w_in[:, :Q_LORA + KV_LORA], _rope_chunk_cols(w_in[:, Q_LORA + KV_LORA:])], axis=1)
    wq = w_q_up.reshape(Q_LORA, C_HEADS, NOPE_DIM + ROPE_DIM)
    wq2 = jnp.concatenate([wq[..., :NOPE_DIM], _rope_chunk_cols(wq[..., NOPE_DIM:])], axis=-1)
    wqt = wq2.transpose(1, 2, 0)
    wkv = w_kv_up.reshape(KV_LORA, C_HEADS, NOPE_DIM + V_DIM)
    wk = wkv[..., :NOPE_DIM].reshape(KV_LORA, C_HEADS * NOPE_DIM)
    wvt = wkv[..., NOPE_DIM:].transpose(1, 2, 0)
    return w_in2, wqt, wk, wvt


def _rope_tables(s):
    inv = 1.0 / (ROPE_THETA ** (jnp.arange(0, ROPE_DIM, 2, dtype=F32) / ROPE_DIM))
    ang = jnp.arange(s, dtype=F32)[:, None] * inv[None]
    cos, sin = jnp.cos(ang), jnp.sin(ang)
    z = jnp.zeros_like(cos)
    return jnp.concatenate([cos, z, cos, z], axis=1), jnp.concatenate([-sin, z, sin, z], axis=1)


def _static_tables(s):
    k = np.arange(TQ)[:, None]
    q = np.arange(TQ)[None, :]
    diag = np.where(k <= q, _bucket_np(q - k), -1)
    prev = _bucket_np(q - k + TQ)
    near_swa = np.stack([diag, np.where(k > q, prev, -1)])
    near_nsa = np.stack([diag, prev])
    upper = np.where(k > q, 0.0, NEG).astype(np.float32)
    c = np.arange(LANES)[:, None]
    t = np.arange(s)[None, :]
    cdist = t - (c * CMP_STRIDE + CMP_BLOCK - 1)
    cmp_map = np.where(cdist >= 0, _bucket_np(cdist), -1)
    ns = s // SLC_BLOCK
    nc = (s - CMP_BLOCK) // CMP_STRIDE + 1
    c_start = np.arange(LANES) * CMP_STRIDE
    s_start = np.arange(ns) * SLC_BLOCK
    ovt = ((c_start[None, :] <= s_start[:, None] + SLC_BLOCK - 1) &
           (c_start[None, :] + CMP_BLOCK - 1 >= s_start[:, None]) &
           (np.arange(LANES)[None, :] < nc)).astype(np.float32)
    e3 = np.zeros((3, B_HEADS * HEAD_DIM, LANES), np.float32)
    for h in range(B_HEADS):
        for br in range(3):
            e3[br, h * HEAD_DIM:(h + 1) * HEAD_DIM, h * 3 + br] = 1.0
    e3 = e3.reshape(3 * B_HEADS * HEAD_DIM, LANES)
    return near_swa, near_nsa, upper, cmp_map, ovt, e3


def _even_layer(x, b, s, rel_bias, norm, w_in, sinks, pos_k, pos_v, k_w1, k_w2, v_w1, v_w2, w_out):
    near_swa, near_nsa, upper, cmp_map, ovt, e3 = _static_tables(s)
    ta = _bias_tiles(rel_bias, near_swa, 0, A_HEADS)
    tb = _bias_tiles(rel_bias, near_nsa, A_HEADS, B_HEADS)
    cbias = _cmp_bias(rel_bias, cmp_map, A_HEADS, B_HEADS)

    pe = _norm_matmul(x, norm, _even_in_weight(w_in), IN_PROJ_TM, EVEN_COLS, BF16).reshape(b, s, EVEN_COLS)

    rows = s // (CMP_BLOCK // 2)
    hkv = jnp.stack([pe[:, :, SEG_KC * LANES:(SEG_KC + 1) * LANES],
                     pe[:, :, SEG_VC * LANES:(SEG_VC + 1) * LANES]]).reshape(2, b, rows, -1)
    ck = _compress_weights(pos_k, k_w1, k_w2)
    cv = _compress_weights(pos_v, v_w1, v_w2)
    kvc = _compress(hkv, *[jnp.stack([a, c]) for a, c in zip(ck, cv)])
    assert rows <= TQ
    kvc = jnp.pad(kvc, ((0, 0), (0, 0), (0, TQ - rows), (0, 0)))

    oa, ob = _even_attention(pe, kvc, rel_bias, sinks, ta, tb, jnp.asarray(upper), cbias,
                             jnp.asarray(e3, BF16), jnp.asarray(ovt, BF16))
    w_out = w_out.astype(BF16)
    n_a = A_HEADS * HEAD_DIM
    return _out_proj(x, [oa.reshape(b * s, -1), ob.reshape(b * s, -1)], [w_out[:n_a], w_out[n_a:]], OUT_PROJ_TM)


def _odd_layer(x, b, s, norm, w_in, q_norm, w_q_up, kv_norm, w_kv_up, w_out):
    w_in2, wqt, wk, wvt = _mla_weights(w_in, w_q_up, w_kv_up)
    cos, sin = _rope_tables(s)
    qt, k, vt = _mla_prep(x, norm, w_in2, q_norm, kv_norm, wqt, wk, wvt, cos, sin, b, s)
    o = _mla_attention(qt, k.reshape(b, s, -1), vt)
    return _out_proj(x, [o.reshape(b * s, -1)], [w_out.astype(BF16)], OUT_PROJ_TM)


def kernel(x, rel_bias, norm_mix_e, w_in_e, sinks, cmp_pos_k, cmp_pos_v, cmp_k_w1, cmp_k_w2, cmp_v_w1, cmp_v_w2, w_out_e, norm_mix_o, w_in_o, q_norm, w_q_up, kv_norm, w_kv_up, w_out_o, norm_mlp, w_up, w_down, norm_final):
    b, s, d = x.shape
    depth = norm_mlp.shape[0]
    assert s % MLA_TQ == 0 and s // (CMP_BLOCK // 2) <= TQ and s // SLC_BLOCK <= LANES
    assert (b * s) % MLP_TM == 0 and (b * s) % IN_PROJ_TM == 0 and w_up.shape[2] % MLP_TF == 0
    h = x.reshape(b * s, d)
    for layer in range(depth):
        i = layer // 2
        if layer % 2 == 0:
            h = _even_layer(h, b, s, rel_bias, norm_mix_e[i], w_in_e[i], sinks[i], cmp_pos_k[i], cmp_pos_v[i],
                            cmp_k_w1[i], cmp_k_w2[i], cmp_v_w1[i], cmp_v_w2[i], w_out_e[i])
        else:
            h = _odd_layer(h, b, s, norm_mix_o[i], w_in_o[i], q_norm[i], w_q_up[i], kv_norm[i], w_kv_up[i],
                           w_out_o[i])
        h = _mlp(h, norm_mlp[layer], w_up, w_down, layer, norm_final, MLP_TM, MLP_TF, layer == depth - 1)
    return h.reshape(b, s, d)
```

```python
import functools
import math

import numpy as np
import jax
import jax.numpy as jnp
from jax import lax
from jax.experimental import pallas as pl
from jax.experimental.pallas import tpu as pltpu

F32 = jnp.float32
BF16 = jnp.bfloat16

LANES = 128
VMEM_LIMIT_BYTES = 56 * 1024 * 1024

EPS = 1e-6
NEG = -1e30
BIG = 1e4
HEAD_DIM = 64
TQ = 128
MLA_TQ = 512
MLA_TK = 512
MLA_HEADS = 4
IN_PROJ_TM = 1024
OUT_PROJ_TM = 512
MLP_TM, MLP_TF = 1024, 512
N_BUCKETS = 32
MAX_DISTANCE = 128
A_HEADS = 16
B_HEADS = 16
GROUPS = 2
PAIRS = 4
W4 = PAIRS * TQ
CMP_BLOCK = 32
CMP_STRIDE = 16
CMP_HIDDEN = 256
SLC_BLOCK = 64
SLC_TOPK = 8
A_WINDOW = 128
B_WINDOW = 512
C_HEADS = 16
Q_LORA = 768
KV_LORA = 512
NOPE_DIM = 128
ROPE_DIM = 64
V_DIM = 128
MLA_QK = NOPE_DIM + LANES
VT_ROWS = V_DIM + 16
LOG2E = math.log2(math.e)
ROPE_THETA = 10000.0

COL_QA, COL_QB = 0, 1024
SEG_KA, SEG_VA, SEG_KC, SEG_VC, SEG_KS, SEG_VS, SEG_KW, SEG_VW, SEG_GATE = range(16, 25)
EVEN_COLS = 25 * LANES
K_A, K_S, K_W = 0, 1, 2


def _params(*sem):
    return pltpu.CompilerParams(dimension_semantics=sem, vmem_limit_bytes=VMEM_LIMIT_BYTES)


def _rms(x, g):
    return x * lax.rsqrt(jnp.mean(x * x, axis=-1, keepdims=True) + EPS) * g


def _dot(a, b):
    return jnp.dot(a, b, preferred_element_type=F32)


def _run_pipelined(tasks):
    s = tasks[0][0]()
    for n, (_, apply_fn) in enumerate(tasks):
        s_next = tasks[n + 1][0]() if n + 1 < len(tasks) else None
        apply_fn(s)
        s = s_next


def _bucket_np(dist):
    dist = np.maximum(dist, 0)
    max_exact = N_BUCKETS // 2
    d = np.maximum(dist, 1).astype(np.float32)
    large = max_exact + (np.log(d / np.float32(max_exact)) / np.float32(math.log(MAX_DISTANCE / max_exact))
                         * np.float32(N_BUCKETS - max_exact)).astype(np.int32)
    large = np.minimum(large, N_BUCKETS - 1)
    return np.where(dist < max_exact, dist, large).astype(np.int32)


def _bias_kernel(tab_ref, bm_ref, o_ref, *, head0):
    h = pl.program_id(0) + head0
    bm = bm_ref[...]
    acc = jnp.zeros(bm.shape, F32)
    for b in range(N_BUCKETS):
        acc = jnp.where(bm == b, tab_ref[b, h], acc)
    o_ref[0] = jnp.where(bm < 0, NEG, (acc - tab_ref[N_BUCKETS - 1, h]) * LOG2E)


def _cmp_bias_kernel(tab_ref, bm_ref, o_ref, *, head0, bands):
    h = pl.program_id(0) + head0
    far = tab_ref[N_BUCKETS - 1, h]
    rows = lax.broadcasted_iota(jnp.int32, (bm_ref.shape[0], TQ), 0)
    for i, (start, size) in enumerate(bands):
        cols = slice(i * TQ, (i + 1) * TQ)
        o_ref[0, :, cols] = jnp.where(rows < start, 0.0, NEG)
        bm = bm_ref[start:start + size, cols]
        acc = jnp.zeros(bm.shape, F32)
        for b in range(N_BUCKETS):
            acc = jnp.where(bm == b, tab_ref[b, h], acc)
        o_ref[0, start:start + size, cols] = jnp.where(bm < 0, NEG, (acc - far) * LOG2E)


def _cmp_bias(rel_bias, cmp_map, head0, n_heads):
    n_blocks, s = cmp_map.shape
    bands = []
    for i in range(s // TQ):
        tile = cmp_map[:, i * TQ:(i + 1) * TQ]
        varying = np.nonzero(((tile >= 0) & (tile < N_BUCKETS - 1)).any(axis=1))[0]
        lo = int(varying.min()) // 8 * 8 if varying.size else 0
        hi = -(-(int(varying.max()) + 1) // 8) * 8 if varying.size else 8
        assert (tile[:lo] == N_BUCKETS - 1).all() and (tile[hi:] < 0).all()
        bands.append((lo, hi - lo))
    return pl.pallas_call(
        functools.partial(_cmp_bias_kernel, head0=head0, bands=tuple(bands)),
        grid=(n_heads,),
        in_specs=[pl.BlockSpec(memory_space=pltpu.SMEM),
                  pl.BlockSpec(cmp_map.shape, lambda h: (0, 0))],
        out_specs=pl.BlockSpec((1, n_blocks, s), lambda h: (h, 0, 0)),
        out_shape=jax.ShapeDtypeStruct((n_heads, n_blocks, s), F32),
        compiler_params=_params("arbitrary"),
    )(rel_bias, jnp.asarray(cmp_map))


def _bias_tiles(rel_bias, bucket_map, head0, n_heads):
    shp = bucket_map.shape
    nd = len(shp)
    return pl.pallas_call(
        functools.partial(_bias_kernel, head0=head0),
        grid=(n_heads,),
        in_specs=[pl.BlockSpec(memory_space=pltpu.SMEM),
                  pl.BlockSpec(shp, lambda h: (0,) * nd)],
        out_specs=pl.BlockSpec((1,) + shp, lambda h: (h,) + (0,) * nd),
        out_shape=jax.ShapeDtypeStruct((n_heads,) + shp, F32),
        compiler_params=_params("arbitrary"),
    )(rel_bias, jnp.asarray(bucket_map))


def _norm_matmul_kernel(x_ref, g_ref, w_ref, o_ref, xn_ref):
    @pl.when(pl.program_id(1) == 0)
    def _():
        xn_ref[...] = _rms(x_ref[...], g_ref[...]).astype(BF16)

    o_ref[...] = _dot(xn_ref[...], w_ref[...]).astype(o_ref.dtype)


def _norm_matmul(x, g, w, tm, tn, out_dtype):
    m, d = x.shape
    n = w.shape[1]
    return pl.pallas_call(
        _norm_matmul_kernel,
        grid=(m // tm, n // tn),
        in_specs=[pl.BlockSpec((tm, d), lambda i, j: (i, 0)),
                  pl.BlockSpec((1, d), lambda i, j: (0, 0)),
                  pl.BlockSpec((d, tn), lambda i, j: (0, j),
                               pipeline_mode=pl.Buffered(1) if tn == n else None)],
        out_specs=pl.BlockSpec((tm, tn), lambda i, j: (i, j)),
        out_shape=jax.ShapeDtypeStruct((m, n), out_dtype),
        scratch_shapes=[pltpu.VMEM((tm, d), BF16)],
        compiler_params=_params("parallel", "arbitrary"),
    )(x, g.reshape(1, d), w)


def _out_proj_kernel(*refs, n_in):
    x_ref = refs[0]
    o_ref = refs[2 * n_in + 1]
    acc = x_ref[...]
    for t in range(n_in):
        acc = acc + _dot(refs[1 + t][...], refs[1 + n_in + t][...])
    o_ref[...] = acc


def _out_proj(x, acts, ws, tm):
    m, d = x.shape
    n_in = len(acts)
    in_specs = [pl.BlockSpec((tm, d), lambda i: (i, 0))]
    in_specs += [pl.BlockSpec((tm, a.shape[1]), lambda i: (i, 0)) for a in acts]
    in_specs += [pl.BlockSpec(w.shape, lambda i: (0, 0), pipeline_mode=pl.Buffered(1)) for w in ws]
    return pl.pallas_call(
        functools.partial(_out_proj_kernel, n_in=n_in),
        grid=(m // tm,),
        in_specs=in_specs,
        out_specs=pl.BlockSpec((tm, d), lambda i: (i, 0)),
        out_shape=jax.ShapeDtypeStruct((m, d), F32),
        compiler_params=_params("parallel"),
    )(x, *acts, *ws)


def _mlp_kernel(x_ref, g_ref, wu_ref, wd_ref, gf_ref, o_ref, xn_ref, *, final_norm):
    j = pl.program_id(1)

    @pl.when(j == 0)
    def _():
        x = x_ref[...]
        xn_ref[...] = _rms(x, g_ref[...]).astype(BF16)
        o_ref[...] = x

    h = _dot(xn_ref[...], wu_ref[0].astype(BF16))
    a = jnp.square(jnp.maximum(h, 0.0)).astype(BF16)
    o_ref[...] += _dot(a, wd_ref[0].astype(BF16))

    if final_norm:
        @pl.when(j == pl.num_programs(1) - 1)
        def _():
            o_ref[...] = _rms(o_ref[...], gf_ref[...])


def _mlp(x, g, w_up, w_down, layer, g_final, tm, tf, final_norm):
    m, d = x.shape
    ff = w_up.shape[2]
    return pl.pallas_call(
        functools.partial(_mlp_kernel, final_norm=final_norm),
        grid=(m // tm, ff // tf),
        in_specs=[pl.BlockSpec((tm, d), lambda i, j: (i, 0)),
                  pl.BlockSpec((1, d), lambda i, j: (0, 0)),
                  pl.BlockSpec((1, d, tf), lambda i, j: (layer, 0, j)),
                  pl.BlockSpec((1, tf, d), lambda i, j: (layer, j, 0)),
                  pl.BlockSpec((1, d), lambda i, j: (0, 0))],
        out_specs=pl.BlockSpec((tm, d), lambda i, j: (i, 0)),
        out_shape=jax.ShapeDtypeStruct((m, d), F32),
        scratch_shapes=[pltpu.VMEM((tm, d), BF16)],
        compiler_params=_params("parallel", "arbitrary"),
    )(x, g.reshape(1, d), w_up, w_down, g_final.reshape(1, d))


def _gelu_tanh(x):
    return 0.5 * x * (1.0 + jnp.tanh(math.sqrt(2.0 / math.pi) * (x + 0.044715 * (x * x * x))))


def _compress_kernel(h_ref, pa_ref, pb_ref, wa_ref, wb_ref, w2_ref, o_ref):
    h = h_ref[0, 0].astype(F32)
    ha = _dot((h + pa_ref[0]).astype(BF16), wa_ref[0])
    hb = _dot((h + pb_ref[0]).astype(BF16), wb_ref[0])
    n = hb.shape[0]
    pre = ha + pltpu.roll(hb, n - 1, 0)
    o_ref[0, 0] = _dot(_gelu_tanh(pre).astype(BF16), w2_ref[0]).astype(o_ref.dtype)


def _compress(hkv, pa, pb, wa, wb, w2):
    _, b, r, w = hkv.shape
    hid2 = wa.shape[2]
    return pl.pallas_call(
        _compress_kernel,
        grid=(2, b),
        in_specs=[pl.BlockSpec((1, 1, r, w), lambda t, i: (t, i, 0, 0)),
                  pl.BlockSpec((1, 1, w), lambda t, i: (t, 0, 0)),
                  pl.BlockSpec((1, 1, w), lambda t, i: (t, 0, 0)),
                  pl.BlockSpec((1, w, hid2), lambda t, i: (t, 0, 0)),
                  pl.BlockSpec((1, w, hid2), lambda t, i: (t, 0, 0)),
                  pl.BlockSpec((1, hid2, LANES), lambda t, i: (t, 0, 0))],
        out_specs=pl.BlockSpec((1, 1, r, LANES), lambda t, i: (t, i, 0, 0)),
        out_shape=jax.ShapeDtypeStruct((2, b, r, LANES), BF16),
        compiler_params=_params("arbitrary", "arbitrary"),
    )(hkv, pa, pb, wa, wb, w2)


SWA, SLC, WIN = 0, 1, 2
FAR_UNROLL = 4


def _stack_pairs_t(q_ref, g):
    cols = [q_ref[0, :, (PAIRS * g + p) * LANES:(PAIRS * g + p + 1) * LANES].astype(F32).T for p in range(PAIRS)]
    return (jnp.concatenate(cols, axis=1) * (HEAD_DIM ** -0.5 * LOG2E)).astype(BF16)


def _rows_by_half(a, b, rows):
    parts = [jnp.broadcast_to(a, (HEAD_DIM, W4)), jnp.broadcast_to(b, (HEAD_DIM, W4))]
    if rows > 2 * HEAD_DIM:
        first = lax.broadcasted_iota(jnp.int32, (rows - 2 * HEAD_DIM, W4), 0) == 0
        parts.append(jnp.where(first, a, b))
    return jnp.concatenate(parts, axis=0)


def _even_attn_kernel(tab_ref, sink_ref, qa_ref, qb_ref, gate_ref,
                      ka_ref, va_ref, ks_ref, vs_ref, kw_ref, vw_ref, kc_ref, vc_ref,
                      ta_ref, tb_ref, up_ref, cb_ref, e3_ref, ovt_ref,
                      oa_ref, ob_ref,
                      kbd_ref, vbd_ref, ckbd_ref, cvbd_ref, msk_ref, acc_scr, m_scr, *, n_tiles):
    i = pl.program_id(1)
    groups = range(GROUPS)

    def block_diag(x, g):
        r = pltpu.roll(x, HEAD_DIM, 1)
        lo = lax.broadcasted_iota(jnp.int32, x.shape, 1) < HEAD_DIM
        own, other = (x, r) if g == 0 else (r, x)
        return jnp.where(lo, own, 0.0), jnp.where(lo, 0.0, other)

    @pl.when(i == 0)
    def _build():
        r16 = lax.broadcasted_iota(jnp.int32, (VT_ROWS - LANES, 2 * TQ), 0)
        c16 = lax.broadcasted_iota(jnp.int32, (VT_ROWS - LANES, 2 * TQ), 1)
        ones_rows = jnp.where((r16 == 0) & (c16 < TQ) | (r16 == 1) & (c16 >= TQ), 1.0, 0.0).astype(BF16)
        for t, ref in enumerate((ka_ref, ks_ref, kw_ref)):
            def kbody(j, c, t=t, ref=ref):
                x = ref[0, pl.ds(pl.multiple_of(j * TQ, TQ), TQ), :].astype(F32)
                for g in groups:
                    top, bot = block_diag(x, g)
                    kbd_ref[g, t, j, 0:TQ, :] = top.astype(BF16)
                    kbd_ref[g, t, j, TQ:2 * TQ, :] = bot.astype(BF16)
                return c
            lax.fori_loop(0, n_tiles, kbody, 0)
        for t, ref in enumerate((va_ref, vs_ref, vw_ref)):
            def vbody(j, c, t=t, ref=ref):
                x = ref[0, pl.ds(pl.multiple_of(j * TQ, TQ), TQ), :].astype(F32)
                for g in groups:
                    top, bot = block_diag(x, g)
                    vbd_ref[g, t, j, 0:LANES, 0:TQ] = top.T.astype(BF16)
                    vbd_ref[g, t, j, 0:LANES, TQ:2 * TQ] = bot.T.astype(BF16)
                    vbd_ref[g, t, j, LANES:, :] = ones_rows
                return c
            lax.fori_loop(0, n_tiles, vbody, 0)
        xk = kc_ref[0, 0].astype(F32)
        xv = vc_ref[0, 0].astype(F32)
        for g in groups:
            top, bot = block_diag(xk, g)
            ckbd_ref[g, 0:TQ, :] = top.astype(BF16)
            ckbd_ref[g, TQ:2 * TQ, :] = bot.astype(BF16)
            top, bot = block_diag(xv, g)
            cvbd_ref[g, :, 0:TQ] = top.T.astype(BF16)
            cvbd_ref[g, :, TQ:2 * TQ] = bot.T.astype(BF16)

    def near_bias(t_ref, g, delta):
        return lambda hf, p: t_ref[8 * g + 2 * p + hf, delta]

    def update(g, st, s, adds, vbd_t, first):
        m_old = [None if first else m_scr[g, st, hf] for hf in range(2)]
        p_rows, m_new = [], []
        for hf in range(2):
            strips, mns = [], []
            for p in range(PAIRS):
                sh = s[hf * TQ:(hf + 1) * TQ, p * TQ:(p + 1) * TQ]
                for a in adds:
                    sh = sh + a(hf, p)
                mn = jnp.max(sh, axis=0, keepdims=True)
                if not first:
                    mn = jnp.maximum(m_old[hf][:, p * TQ:(p + 1) * TQ], mn)
                strips.append(jnp.exp2(sh - mn).astype(BF16))
                mns.append(mn)
            p_rows.append(jnp.concatenate(strips, axis=1))
            m_new.append(jnp.concatenate(mns, axis=1))
            m_scr[g, st, hf] = m_new[hf]
        pv = _dot(vbd_t, jnp.concatenate(p_rows, axis=0))
        if first:
            acc_scr[g, st] = pv
        else:
            alpha = _rows_by_half(jnp.exp2(m_old[0] - m_new[0]), jnp.exp2(m_old[1] - m_new[1]), VT_ROWS)
            acc_scr[g, st] = alpha * acc_scr[g, st] + pv

    run_pipelined = _run_pipelined

    def tile_task(g, st, slot, j, q4t, adds, first=False):
        return (lambda: _dot(kbd_ref[g, slot, j], q4t),
                lambda s: update(g, st, s, adds, vbd_ref[g, slot, j], first))

    def finish(g, st, extra=None):
        acc = acc_scr[g, st]
        den = [acc[LANES + hf:LANES + hf + 1] for hf in range(2)]
        if extra is not None:
            den = [den[hf] + extra(hf, m_scr[g, st, hf]) for hf in range(2)]
        return acc[:LANES] * _rows_by_half(1.0 / den[0], 1.0 / den[1], LANES)

    q4a = [_stack_pairs_t(qa_ref, g) for g in groups]
    q4b = [_stack_pairs_t(qb_ref, g) for g in groups]

    t_q = i * TQ + lax.broadcasted_iota(jnp.int32, (1, TQ), 1)
    anyvis = jnp.where(t_q >= CMP_BLOCK - 1, 1.0, 0.0)
    s_cmp = [_dot(ckbd_ref[g], q4b[g]) for g in groups]
    psum = [jnp.zeros((TQ, TQ), F32) for _ in groups]
    p_rows = [[] for _ in groups]
    for hf in range(2):
        strips = [[] for _ in groups]
        for p in range(PAIRS):
            for g in groups:
                sh = s_cmp[g][hf * TQ:(hf + 1) * TQ, p * TQ:(p + 1) * TQ] + cb_ref[8 * g + 2 * p + hf]
                pe = jnp.exp2(sh - jnp.max(sh, axis=0, keepdims=True))
                pc = pe * (anyvis / jnp.sum(pe, axis=0, keepdims=True))
                psum[g] = psum[g] + pc
                strips[g].append(pc.astype(BF16))
        for g in groups:
            p_rows[g].append(jnp.concatenate(strips[g], axis=1))
    o_cmp = [_dot(cvbd_ref[g], jnp.concatenate(p_rows[g], axis=0)) for g in groups]

    psum2 = jnp.concatenate(psum, axis=1)
    ph = psum2.astype(BF16)
    r1 = psum2 - ph.astype(F32)
    pm = r1.astype(BF16)
    pl_ = (r1 - pm.astype(F32)).astype(BF16)
    ovt = ovt_ref[...]
    pslc = _dot(ovt, ph) + _dot(ovt, pm) + _dot(ovt, pl_)
    ns = pslc.shape[0]
    nb = lax.broadcasted_iota(jnp.int32, (ns, GROUPS * TQ), 0)
    tq = i * TQ + (lax.broadcasted_iota(jnp.int32, (ns, GROUPS * TQ), 1) & (TQ - 1))
    cur = tq // SLC_BLOCK
    forced = (jnp.where(nb == 0, 1.0, 0.0) + jnp.where(nb == cur, 1.0, 0.0) +
              jnp.where(nb == cur - 1, 1.0, 0.0))
    score = jnp.where(nb * SLC_BLOCK > tq, NEG, jnp.where(forced > 0.0, BIG, pslc))
    rank = jnp.zeros((ns, GROUPS * TQ), F32)
    for mth in range(ns):
        sm = score[mth:mth + 1, :]
        tie = jnp.where(nb > mth, 1.0, 0.0)
        rank = rank + jnp.where(sm > score, 1.0, jnp.where(sm == score, tie, 0.0))
    sel_rows = jnp.where(rank < SLC_TOPK, 0.0, NEG)
    for g in groups:
        msk_ref[g] = sel_rows[:, g * TQ:(g + 1) * TQ]

    def sel_add(g, j):
        cache = []

        def tile(hf, p):
            if not cache:
                rows = [jnp.broadcast_to(msk_ref[g, pl.ds(2 * j + e, 1), :], (SLC_BLOCK, TQ)) for e in range(2)]
                cache.append(jnp.concatenate(rows, axis=0))
            return cache[0]
        return tile

    n_full = B_WINDOW // TQ

    def near_tiles(n_back):
        tasks = [tile_task(g, SWA, K_A, i, q4a[g], [near_bias(ta_ref, g, 0)], first=True) for g in groups]
        tasks += [tile_task(g, WIN, K_W, i, q4b[g], [near_bias(tb_ref, g, 0)], first=True) for g in groups]
        tasks += [tile_task(g, SLC, K_S, i, q4b[g], [near_bias(tb_ref, g, 0), sel_add(g, i)], first=True)
                  for g in groups]
        if n_back >= 1:
            j = i - 1
            tasks += [tile_task(g, SWA, K_A, j, q4a[g], [near_bias(ta_ref, g, 1)]) for g in groups]
            tasks += [tile_task(g, WIN, K_W, j, q4b[g], [near_bias(tb_ref, g, 1)]) for g in groups]
            tasks += [tile_task(g, SLC, K_S, j, q4b[g], [near_bias(tb_ref, g, 1), sel_add(g, j)]) for g in groups]
        for d in range(2, n_back + 1):
            edge = [lambda hf, p: up_ref[...]] if d == n_full else []
            tasks += [tile_task(g, WIN, K_W, i - d, q4b[g], edge) for g in groups]
        run_pipelined(tasks)

    for c in range(n_full):
        pl.when(i == c)(functools.partial(near_tiles, c))
    pl.when(i >= n_full)(functools.partial(near_tiles, n_full))

    def far_tasks(d0, count):
        return [tile_task(g, SLC, K_S, i - d, q4b[g], [sel_add(g, i - d)])
                for d in [d0 + e for e in range(count)] for g in groups]

    def far_body(jj, c):
        run_pipelined(far_tasks(2 + FAR_UNROLL * jj, FAR_UNROLL))
        return c

    n_far = jnp.maximum(i - 1, 0)
    lax.fori_loop(0, n_far // FAR_UNROLL, far_body, 0)
    for rem in range(1, FAR_UNROLL):
        pl.when(n_far % FAR_UNROLL == rem)(
            functools.partial(lambda rem: run_pipelined(far_tasks(i - rem + 1, rem)), rem))

    def to_rows(o_t, p):
        return o_t[:, p * TQ:(p + 1) * TQ].T

    far_of = lambda h: tab_ref[N_BUCKETS - 1, h]
    sg = jax.nn.sigmoid(gate_ref[0].astype(F32)).T
    sg_hi = sg.astype(BF16)
    sg_lo = (sg - sg_hi.astype(F32)).astype(BF16)
    gates_t = _dot(e3_ref[...], sg_hi) + _dot(e3_ref[...], sg_lo)
    for g in groups:
        def sink_term(hf, m, g=g):
            row = jnp.concatenate([jnp.full((1, TQ), (sink_ref[8 * g + 2 * p + hf] - far_of(8 * g + 2 * p + hf)) * LOG2E,
                                            F32) for p in range(PAIRS)], axis=1)
            return jnp.exp2(row - m)

        o_swa = finish(g, SWA, sink_term)
        o_slc = finish(g, SLC)
        o_win = finish(g, WIN)
        for p in range(PAIRS):
            cols = slice((PAIRS * g + p) * LANES, (PAIRS * g + p + 1) * LANES)
            oa_ref[0, :, cols] = to_rows(o_swa, p).astype(oa_ref.dtype)
            qs = slice(p * TQ, (p + 1) * TQ)
            gate = lambda br: gates_t[br * B_HEADS * HEAD_DIM + cols.start:br * B_HEADS * HEAD_DIM + cols.stop]
            o_t = gate(0) * o_cmp[g][:, qs] + gate(1) * o_slc[:, qs] + gate(2) * o_win[:, qs]
            ob_ref[0, :, cols] = o_t.T.astype(ob_ref.dtype)


def _even_attention(pe, kvc, rel_bias, sinks, ta, tb, up, cbias, e3, ovt):
    b, s, _ = pe.shape
    n_tiles = s // TQ
    width = GROUPS * PAIRS * LANES

    def seg(k):
        return pl.BlockSpec((1, s, LANES), lambda bi, i, k=k: (bi, 0, k))

    full = lambda a: pl.BlockSpec(a.shape, lambda bi, i: (0,) * a.ndim)
    in_specs = [
        pl.BlockSpec(memory_space=pltpu.SMEM),
        pl.BlockSpec(memory_space=pltpu.SMEM),
        pl.BlockSpec((1, TQ, width), lambda bi, i: (bi, i, COL_QA // width)),
        pl.BlockSpec((1, TQ, width), lambda bi, i: (bi, i, COL_QB // width)),
        pl.BlockSpec((1, TQ, LANES), lambda bi, i: (bi, i, SEG_GATE)),
        seg(SEG_KA), seg(SEG_VA), seg(SEG_KS), seg(SEG_VS), seg(SEG_KW), seg(SEG_VW),
        pl.BlockSpec((1, 1, TQ, LANES), lambda bi, i: (0, bi, 0, 0)),
        pl.BlockSpec((1, 1, TQ, LANES), lambda bi, i: (1, bi, 0, 0)),
        full(ta), full(tb), full(up),
        pl.BlockSpec((B_HEADS, TQ, TQ), lambda bi, i: (0, 0, i)),
        full(e3), full(ovt),
    ]
    out_spec = pl.BlockSpec((1, TQ, width), lambda bi, i: (bi, i, 0))
    return pl.pallas_call(
        functools.partial(_even_attn_kernel, n_tiles=n_tiles),
        grid=(b, n_tiles),
        in_specs=in_specs,
        out_specs=[out_spec, out_spec],
        out_shape=[jax.ShapeDtypeStruct((b, s, width), BF16)] * 2,
        scratch_shapes=[pltpu.VMEM((GROUPS, 3, n_tiles, 2 * TQ, LANES), BF16),
                        pltpu.VMEM((GROUPS, 3, n_tiles, VT_ROWS, 2 * TQ), BF16),
                        pltpu.VMEM((GROUPS, 2 * TQ, LANES), BF16),
                        pltpu.VMEM((GROUPS, LANES, 2 * TQ), BF16),
                        pltpu.VMEM((GROUPS, s // SLC_BLOCK, TQ), F32),
                        pltpu.VMEM((GROUPS, 3, VT_ROWS, W4), F32),
                        pltpu.VMEM((GROUPS, 3, 2, 1, W4), F32)],
        compiler_params=_params("arbitrary", "arbitrary"),
    )(rel_bias, sinks, pe, pe, pe, pe, pe, pe, pe, pe, pe, kvc, kvc, ta, tb, up, cbias, e3, ovt)


def _mla_prep_kernel(x_ref, g_ref, win_ref, qn_ref, kvn_ref, wqt_ref, wk_ref, wvt_ref,
                     cos_ref, sin_ref, cost_ref, sint_ref, qt_ref, k_ref, vt_ref):
    xn = _rms(x_ref[...], g_ref[...]).astype(BF16)
    proj = _dot(xn, win_ref[...])
    cq = _rms(proj[:, :Q_LORA], qn_ref[...])
    ckv = _rms(proj[:, Q_LORA:Q_LORA + KV_LORA], kvn_ref[...])
    cq_t = cq.T.astype(BF16)
    ckv_t = ckv.T.astype(BF16)
    ckv = ckv.astype(BF16)
    kr = proj[:, Q_LORA + KV_LORA:]
    kr = (kr * cos_ref[...] + pltpu.roll(kr, HEAD_DIM, 1) * sin_ref[...]).astype(BF16)
    cos_t = cost_ref[...]
    sin_t = sint_ref[...]
    scale = (NOPE_DIM + ROPE_DIM) ** -0.5 * LOG2E
    tm = ckv_t.shape[1]
    ones_rows = jnp.where(lax.broadcasted_iota(jnp.int32, (VT_ROWS - V_DIM, tm), 0) == 0, 1.0, 0.0).astype(BF16)
    for h in range(C_HEADS):
        q_t = _dot(wqt_ref[h], cq_t)
        rp = q_t[NOPE_DIM:]
        rp = rp * cos_t + pltpu.roll(rp, HEAD_DIM, 0) * sin_t
        qt_ref[0, h, 0, 0:NOPE_DIM, :] = (q_t[:NOPE_DIM] * scale).astype(BF16)
        qt_ref[0, h, 0, NOPE_DIM:, :] = (rp * scale).astype(BF16)
        if h % 2 == 0:
            k2 = _dot(ckv, wk_ref[:, h * LANES:(h + 2) * LANES]).astype(BF16)
        k_ref[:, h * MLA_QK:h * MLA_QK + NOPE_DIM] = k2[:, (h % 2) * LANES:(h % 2 + 1) * LANES]
        k_ref[:, h * MLA_QK + NOPE_DIM:(h + 1) * MLA_QK] = kr
        vt_ref[0, h, 0, 0:V_DIM, :] = _dot(wvt_ref[h], ckv_t).astype(BF16)
        vt_ref[0, h, 0, V_DIM:, :] = ones_rows


def _mla_prep(x, g, w_in, qn, kvn, wqt, wk, wvt, cos, sin, b, s):
    m, d = x.shape
    tm = MLA_TK
    pos_tiles = s // tm
    full = lambda a: pl.BlockSpec(a.shape, lambda i: (0,) * a.ndim, pipeline_mode=pl.Buffered(1))
    tile_t = lambda rows: pl.BlockSpec((1, C_HEADS, 1, rows, tm), lambda i: (i // pos_tiles, 0, i % pos_tiles, 0, 0))
    return pl.pallas_call(
        _mla_prep_kernel,
        grid=(m // tm,),
        in_specs=[pl.BlockSpec((tm, d), lambda i: (i, 0)),
                  pl.BlockSpec((1, d), lambda i: (0, 0)),
                  full(w_in), pl.BlockSpec((1, Q_LORA), lambda i: (0, 0)),
                  pl.BlockSpec((1, KV_LORA), lambda i: (0, 0)), full(wqt), full(wk), full(wvt),
                  pl.BlockSpec((tm, LANES), lambda i: (i % pos_tiles, 0)),
                  pl.BlockSpec((tm, LANES), lambda i: (i % pos_tiles, 0)),
                  pl.BlockSpec((LANES, tm), lambda i: (0, i % pos_tiles)),
                  pl.BlockSpec((LANES, tm), lambda i: (0, i % pos_tiles))],
        out_specs=[tile_t(MLA_QK),
                   pl.BlockSpec((tm, C_HEADS * MLA_QK), lambda i: (i, 0)),
                   tile_t(VT_ROWS)],
        out_shape=[jax.ShapeDtypeStruct((b, C_HEADS, pos_tiles, MLA_QK, tm), BF16),
                   jax.ShapeDtypeStruct((m, C_HEADS * MLA_QK), BF16),
                   jax.ShapeDtypeStruct((b, C_HEADS, pos_tiles, VT_ROWS, tm), BF16)],
        compiler_params=_params("parallel"),
    )(x, g.reshape(1, d), w_in, qn.reshape(1, -1), kvn.reshape(1, -1), wqt, wk, wvt, cos, sin, cos.T, sin.T)


def _mla_attn_kernel(qt_ref, k_ref, vt_ref, o_ref, s_scr, acc_scr, m_scr, *, n_q):
    tq = tk = MLA_TQ
    parts = tq // MLA_TK
    ki = lax.broadcasted_iota(jnp.int32, (tk, tq), 0)
    qi = lax.broadcasted_iota(jnp.int32, (tk, tq), 1)
    causal = ki <= qi
    heads = range(MLA_HEADS)

    def q_tile(t, c):
        q_ts = [jnp.concatenate([qt_ref[0, h, parts * t + d] for d in range(parts)], axis=1) for h in heads]

        def logits_to(slot, j):
            rows = pl.ds(pl.multiple_of(j * tk, tk), tk)
            for h in heads:
                s_scr[h, slot] = _dot(k_ref[0, rows, h * MLA_QK:(h + 1) * MLA_QK], q_ts[h])

        def update_from(slot, j, diagonal):
            for h in heads:
                s = s_scr[h, slot]
                if diagonal:
                    s = jnp.where(causal, s, NEG)
                m = m_scr[h]
                mn = jnp.maximum(m, jnp.max(s, axis=0, keepdims=True))
                p = jnp.exp2(s - mn).astype(BF16)
                m_scr[h] = mn
                v_t = jnp.concatenate([vt_ref[0, h, parts * j + d] for d in range(parts)], axis=1)
                acc_scr[h] = jnp.exp2(m - mn) * acc_scr[h] + _dot(v_t, p)

        m_scr[...] = jnp.full(m_scr.shape, NEG, F32)
        acc_scr[...] = jnp.zeros(acc_scr.shape, F32)
        logits_to(0, t)
        logits_to(1, 0)
        update_from(0, t, True)

        def body(jj, c2):
            logits_to(0, jnp.minimum(2 * jj + 1, t - 1))
            update_from(1, 2 * jj, False)
            logits_to(1, jnp.minimum(2 * jj + 2, t - 1))

            @pl.when(2 * jj + 1 < t)
            def _():
                update_from(0, 2 * jj + 1, False)
            return c2

        lax.fori_loop(0, (t + 1) // 2, body, 0)
        rows = pl.ds(pl.multiple_of(t * tq, tq), tq)
        for h in heads:
            acc = acc_scr[h]
            o = acc[:V_DIM] * (1.0 / acc[V_DIM:V_DIM + 1])
            o_ref[0, rows, h * LANES:(h + 1) * LANES] = o.T.astype(o_ref.dtype)
        return c

    lax.fori_loop(0, n_q, q_tile, 0)


def _mla_attention(qt, k, vt):
    b, s, _ = k.shape
    n_k = s // MLA_TK
    nh = MLA_HEADS
    return pl.pallas_call(
        functools.partial(_mla_attn_kernel, n_q=s // MLA_TQ),
        grid=(b, C_HEADS // nh),
        in_specs=[pl.BlockSpec((1, nh, n_k, MLA_QK, MLA_TK), lambda bi, h: (bi, h, 0, 0, 0)),
                  pl.BlockSpec((1, s, MLA_QK * nh), lambda bi, h: (bi, 0, h)),
                  pl.BlockSpec((1, nh, n_k, VT_ROWS, MLA_TK), lambda bi, h: (bi, h, 0, 0, 0))],
        out_specs=pl.BlockSpec((1, s, LANES * nh), lambda bi, h: (bi, 0, h)),
        out_shape=jax.ShapeDtypeStruct((b, s, C_HEADS * V_DIM), BF16),
        scratch_shapes=[pltpu.VMEM((nh, 2, MLA_TQ, MLA_TQ), F32),
                        pltpu.VMEM((nh, VT_ROWS, MLA_TQ), F32),
                        pltpu.VMEM((nh, 1, MLA_TQ), F32)],
        compiler_params=_params("parallel", "arbitrary"),
    )(qt, k, vt)


def _even_in_weight(w):
    cuts = np.cumsum([1024, 128, 128, 1024, 128, 128, 128, 128, 128, 128, 48])[:-1]
    qa, ka, va, qb, kc, vc, ks, vs, kw, vw, gate = jnp.split(w.astype(BF16), [int(c) for c in cuts], axis=1)
    gate = jnp.pad(gate, ((0, 0), (0, LANES - gate.shape[1])))
    return jnp.concatenate([qa, qb, ka, va, kc, vc, ks, vs, kw, vw, gate], axis=1)


def _compress_weights(pos, w1, w2):
    half = CMP_BLOCK // 2
    eye = jnp.eye(GROUPS, dtype=BF16)
    w1r = w1.astype(BF16).reshape(2, half, HEAD_DIM, CMP_HIDDEN)
    wab = jnp.einsum('rldh,gk->rlgdkh', w1r, eye).reshape(2, half * GROUPS * HEAD_DIM, GROUPS * CMP_HIDDEN)
    posr = jnp.broadcast_to(pos.reshape(2, half, 1, HEAD_DIM), (2, half, GROUPS, HEAD_DIM)).reshape(2, 1, -1)
    w2bd = jnp.einsum('hd,gk->ghkd', w2.astype(BF16), eye).reshape(GROUPS * CMP_HIDDEN, GROUPS * HEAD_DIM)
    return posr[0], posr[1], wab[0], wab[1], w2bd


def _rope_chunk_cols(w):
    z = jnp.zeros(w.shape[:-1] + (ROPE_DIM // 2,), w.dtype)
    return jnp.concatenate([w[..., :ROPE_DIM // 2], z, w[..., ROPE_DIM // 2:], z], axis=-1)


def _mla_weights(w_in, w_q_up, w_kv_up):
    w_in, w_q_up, w_kv_up = w_in.astype(BF16), w_q_up.astype(BF16), w_kv_up.astype(BF16)
    w_in2 = jnp.concatenate([w_in[:, :Q_LORA + KV_LORA], _rope_chunk_cols(w_in[:, Q_LORA + KV_LORA:])], axis=1)
    wq = w_q_up.reshape(Q_LORA, C_HEADS, NOPE_DIM + ROPE_DIM)
    wq2 = jnp.concatenate([wq[..., :NOPE_DIM], _rope_chunk_cols(wq[..., NOPE_DIM:])], axis=-1)
    wqt = wq2.transpose(1, 2, 0)
    wkv = w_kv_up.reshape(KV_LORA, C_HEADS, NOPE_DIM + V_DIM)
    wk = wkv[..., :NOPE_DIM].reshape(KV_LORA, C_HEADS * NOPE_DIM)
    wvt = wkv[..., NOPE_DIM:].transpose(1, 2, 0)
    return w_in2, wqt, wk, wvt


def _rope_tables(s):
    inv = 1.0 / (ROPE_THETA ** (jnp.arange(0, ROPE_DIM, 2, dtype=F32) / ROPE_DIM))
    ang = jnp.arange(s, dtype=F32)[:, None] * inv[None]
    cos, sin = jnp.cos(ang), jnp.sin(ang)
    z = jnp.zeros_like(cos)
    return jnp.concatenate([cos, z, cos, z], axis=1), jnp.concatenate([-sin, z, sin, z], axis=1)


def _static_tables(s):
    k = np.arange(TQ)[:, None]
    q = np.arange(TQ)[None, :]
    diag = np.where(k <= q, _bucket_np(q - k), -1)
    prev = _bucket_np(q - k + TQ)
    near_swa = np.stack([diag, np.where(k > q, prev, -1)])
    near_nsa = np.stack([diag, prev])
    upper = np.where(k > q, 0.0, NEG).astype(np.float32)
    c = np.arange(LANES)[:, None]
    t = np.arange(s)[None, :]
    cdist = t - (c * CMP_STRIDE + CMP_BLOCK - 1)
    cmp_map = np.where(cdist >= 0, _bucket_np(cdist), -1)
    ns = s // SLC_BLOCK
    nc = (s - CMP_BLOCK) // CMP_STRIDE + 1
    c_start = np.arange(LANES) * CMP_STRIDE
    s_start = np.arange(ns) * SLC_BLOCK
    ovt = ((c_start[None, :] <= s_start[:, None] + SLC_BLOCK - 1) &
           (c_start[None, :] + CMP_BLOCK - 1 >= s_start[:, None]) &
           (np.arange(LANES)[None, :] < nc)).astype(np.float32)
    e3 = np.zeros((3, B_HEADS * HEAD_DIM, LANES), np.float32)
    for h in range(B_HEADS):
        for br in range(3):
            e3[br, h * HEAD_DIM:(h + 1) * HEAD_DIM, h * 3 + br] = 1.0
    e3 = e3.reshape(3 * B_HEADS * HEAD_DIM, LANES)
    return near_swa, near_nsa, upper, cmp_map, ovt, e3


def _even_layer(x, b, s, rel_bias, norm, w_in, sinks, pos_k, pos_v, k_w1, k_w2, v_w1, v_w2, w_out):
    near_swa, near_nsa, upper, cmp_map, ovt, e3 = _static_tables(s)
    ta = _bias_tiles(rel_bias, near_swa, 0, A_HEADS)
    tb = _bias_tiles(rel_bias, near_nsa, A_HEADS, B_HEADS)
    cbias = _cmp_bias(rel_bias, cmp_map, A_HEADS, B_HEADS)

    pe = _norm_matmul(x, norm, _even_in_weight(w_in), IN_PROJ_TM, EVEN_COLS, BF16).reshape(b, s, EVEN_COLS)

    rows = s // (CMP_BLOCK // 2)
    hkv = jnp.stack([pe[:, :, SEG_KC * LANES:(SEG_KC + 1) * LANES],
                     pe[:, :, SEG_VC * LANES:(SEG_VC + 1) * LANES]]).reshape(2, b, rows, -1)
    ck = _compress_weights(pos_k, k_w1, k_w2)
    cv = _compress_weights(pos_v, v_w1, v_w2)
    kvc = _compress(hkv, *[jnp.stack([a, c]) for a, c in zip(ck, cv)])
    assert rows <= TQ
    kvc = jnp.pad(kvc, ((0, 0), (0, 0), (0, TQ - rows), (0, 0)))

    oa, ob = _even_attention(pe, kvc, rel_bias, sinks, ta, tb, jnp.asarray(upper), cbias,
                             jnp.asarray(e3, BF16), jnp.asarray(ovt, BF16))
    w_out = w_out.astype(BF16)
    n_a = A_HEADS * HEAD_DIM
    return _out_proj(x, [oa.reshape(b * s, -1), ob.reshape(b * s, -1)], [w_out[:n_a], w_out[n_a:]], OUT_PROJ_TM)


def _odd_layer(x, b, s, norm, w_in, q_norm, w_q_up, kv_norm, w_kv_up, w_out):
    w_in2, wqt, wk, wvt = _mla_weights(w_in, w_q_up, w_kv_up)
    cos, sin = _rope_tables(s)
    qt, k, vt = _mla_prep(x, norm, w_in2, q_norm, kv_norm, wqt, wk, wvt, cos, sin, b, s)
    o = _mla_attention(qt, k.reshape(b, s, -1), vt)
    return _out_proj(x, [o.reshape(b * s, -1)], [w_out.astype(BF16)], OUT_PROJ_TM)


def kernel(x, rel_bias, norm_mix_e, w_in_e, sinks, cmp_pos_k, cmp_pos_v, cmp_k_w1, cmp_k_w2, cmp_v_w1, cmp_v_w2, w_out_e, norm_mix_o, w_in_o, q_norm, w_q_up, kv_norm, w_kv_up, w_out_o, norm_mlp, w_up, w_down, norm_final):
    b, s, d = x.shape
    depth = norm_mlp.shape[0]
    assert s % MLA_TQ == 0 and s // (CMP_BLOCK // 2) <= TQ and s // SLC_BLOCK <= LANES
    assert (b * s) % MLP_TM == 0 and (b * s) % IN_PROJ_TM == 0 and w_up.shape[2] % MLP_TF == 0
    h = x.reshape(b * s, d)
    for layer in range(depth):
        i = layer // 2
        if layer % 2 == 0:
            h = _even_layer(h, b, s, rel_bias, norm_mix_e[i], w_in_e[i], sinks[i], cmp_pos_k[i], cmp_pos_v[i],
                            cmp_k_w1[i], cmp_k_w2[i], cmp_v_w1[i], cmp_v_w2[i], w_out_e[i])
        else:
            h = _odd_layer(h, b, s, norm_mix_o[i], w_in_o[i], q_norm[i], w_q_up[i], kv_norm[i], w_kv_up[i],
                           w_out_o[i])
        h = _mlp(h, norm_mlp[layer], w_up, w_down, layer, norm_final, MLP_TM, MLP_TF, layer == depth - 1)
    return h.reshape(b, s, d)
```

```python
import functools
import math

import numpy as np
import jax
import jax.numpy as jnp
from jax import lax
from jax.experimental import pallas as pl
from jax.experimental.pallas import tpu as pltpu

F32 = jnp.float32
BF16 = jnp.bfloat16

LANES = 128
VMEM_LIMIT_BYTES = 56 * 1024 * 1024

EPS = 1e-6
NEG = -1e30
BIG = 1e4
HEAD_DIM = 64
TQ = 128
MLA_TQ = 512
MLA_TK = 512
MLA_HEADS = 4
IN_PROJ_TM = 1024
OUT_PROJ_TM = 512
MLP_TM, MLP_TF = 1024, 512
N_BUCKETS = 32
MAX_DISTANCE = 128
A_HEADS = 16
B_HEADS = 16
GROUPS = 2
PAIRS = 4
W4 = PAIRS * TQ
CMP_BLOCK = 32
CMP_STRIDE = 16
CMP_HIDDEN = 256
SLC_BLOCK = 64
SLC_TOPK = 8
A_WINDOW = 128
B_WINDOW = 512
C_HEADS = 16
Q_LORA = 768
KV_LORA = 512
NOPE_DIM = 128
ROPE_DIM = 64
V_DIM = 128
MLA_QK = NOPE_DIM + LANES
VT_ROWS = V_DIM + 16
LOG2E = math.log2(math.e)
ROPE_THETA = 10000.0

COL_QA, COL_QB = 0, 1024
SEG_KA, SEG_VA, SEG_KC, SEG_VC, SEG_KS, SEG_VS, SEG_KW, SEG_VW, SEG_GATE = range(16, 25)
EVEN_COLS = 25 * LANES
K_A, K_S, K_W = 0, 1, 2


def _params(*sem):
    return pltpu.CompilerParams(dimension_semantics=sem, vmem_limit_bytes=VMEM_LIMIT_BYTES)


def _rms(x, g):
    return x * lax.rsqrt(jnp.mean(x * x, axis=-1, keepdims=True) + EPS) * g


def _dot(a, b):
    return jnp.dot(a, b, preferred_element_type=F32)


def _run_pipelined(tasks):
    s = tasks[0][0]()
    for n, (_, apply_fn) in enumerate(tasks):
        s_next = tasks[n + 1][0]() if n + 1 < len(tasks) else None
        apply_fn(s)
        s = s_next


def _bucket_np(dist):
    dist = np.maximum(dist, 0)
    max_exact = N_BUCKETS // 2
    d = np.maximum(dist, 1).astype(np.float32)
    large = max_exact + (np.log(d / np.float32(max_exact)) / np.float32(math.log(MAX_DISTANCE / max_exact))
                         * np.float32(N_BUCKETS - max_exact)).astype(np.int32)
    large = np.minimum(large, N_BUCKETS - 1)
    return np.where(dist < max_exact, dist, large).astype(np.int32)


def _bias_kernel(tab_ref, bm_ref, o_ref, *, head0):
    h = pl.program_id(0) + head0
    bm = bm_ref[...]
    acc = jnp.zeros(bm.shape, F32)
    for b in range(N_BUCKETS):
        acc = jnp.where(bm == b, tab_ref[b, h], acc)
    o_ref[0] = jnp.where(bm < 0, NEG, (acc - tab_ref[N_BUCKETS - 1, h]) * LOG2E)


def _cmp_bias_kernel(tab_ref, bm_ref, o_ref, *, head0, bands):
    h = pl.program_id(0) + head0
    far = tab_ref[N_BUCKETS - 1, h]
    rows = lax.broadcasted_iota(jnp.int32, (bm_ref.shape[0], TQ), 0)
    for i, (start, size) in enumerate(bands):
        cols = slice(i * TQ, (i + 1) * TQ)
        o_ref[0, :, cols] = jnp.where(rows < start, 0.0, NEG)
        bm = bm_ref[start:start + size, cols]
        acc = jnp.zeros(bm.shape, F32)
        for b in range(N_BUCKETS):
            acc = jnp.where(bm == b, tab_ref[b, h], acc)
        o_ref[0, start:start + size, cols] = jnp.where(bm < 0, NEG, (acc - far) * LOG2E)


def _cmp_bias(rel_bias, cmp_map, head0, n_heads):
    n_blocks, s = cmp_map.shape
    bands = []
    for i in range(s // TQ):
        tile = cmp_map[:, i * TQ:(i + 1) * TQ]
        varying = np.nonzero(((tile >= 0) & (tile < N_BUCKETS - 1)).any(axis=1))[0]
        lo = int(varying.min()) // 8 * 8 if varying.size else 0
        hi = -(-(int(varying.max()) + 1) // 8) * 8 if varying.size else 8
        assert (tile[:lo] == N_BUCKETS - 1).all() and (tile[hi:] < 0).all()
        bands.append((lo, hi - lo))
    return pl.pallas_call(
        functools.partial(_cmp_bias_kernel, head0=head0, bands=tuple(bands)),
        grid=(n_heads,),
        in_specs=[pl.BlockSpec(memory_space=pltpu.SMEM),
                  pl.BlockSpec(cmp_map.shape, lambda h: (0, 0))],
        out_specs=pl.BlockSpec((1, n_blocks, s), lambda h: (h, 0, 0)),
        out_shape=jax.ShapeDtypeStruct((n_heads, n_blocks, s), F32),
        compiler_params=_params("arbitrary"),
    )(rel_bias, jnp.asarray(cmp_map))


def _bias_tiles(rel_bias, bucket_map, head0, n_heads):
    shp = bucket_map.shape
    nd = len(shp)
    return pl.pallas_call(
        functools.partial(_bias_kernel, head0=head0),
        grid=(n_heads,),
        in_specs=[pl.BlockSpec(memory_space=pltpu.SMEM),
                  pl.BlockSpec(shp, lambda h: (0,) * nd)],
        out_specs=pl.BlockSpec((1,) + shp, lambda h: (h,) + (0,) * nd),
        out_shape=jax.ShapeDtypeStruct((n_heads,) + shp, F32),
        compiler_params=_params("arbitrary"),
    )(rel_bias, jnp.asarray(bucket_map))


def _norm_matmul_kernel(x_ref, g_ref, w_ref, o_ref, xn_ref):
    @pl.when(pl.program_id(1) == 0)
    def _():
        xn_ref[...] = _rms(x_ref[...], g_ref[...]).astype(BF16)

    o_ref[...] = _dot(xn_ref[...], w_ref[...]).astype(o_ref.dtype)


def _norm_matmul(x, g, w, tm, tn, out_dtype):
    m, d = x.shape
    n = w.shape[1]
    return pl.pallas_call(
        _norm_matmul_kernel,
        grid=(m // tm, n // tn),
        in_specs=[pl.BlockSpec((tm, d), lambda i, j: (i, 0)),
                  pl.BlockSpec((1, d), lambda i, j: (0, 0)),
                  pl.BlockSpec((d, tn), lambda i, j: (0, j),
                               pipeline_mode=pl.Buffered(1) if tn == n else None)],
        out_specs=pl.BlockSpec((tm, tn), lambda i, j: (i, j)),
        out_shape=jax.ShapeDtypeStruct((m, n), out_dtype),
        scratch_shapes=[pltpu.VMEM((tm, d), BF16)],
        compiler_params=_params("parallel", "arbitrary"),
    )(x, g.reshape(1, d), w)


def _out_proj_kernel(*refs, n_in):
    x_ref = refs[0]
    o_ref = refs[2 * n_in + 1]
    acc = x_ref[...]
    for t in range(n_in):
        acc = acc + _dot(refs[1 + t][...], refs[1 + n_in + t][...])
    o_ref[...] = acc


def _out_proj(x, acts, ws, tm):
    m, d = x.shape
    n_in = len(acts)
    in_specs = [pl.BlockSpec((tm, d), lambda i: (i, 0))]
    in_specs += [pl.BlockSpec((tm, a.shape[1]), lambda i: (i, 0)) for a in acts]
    in_specs += [pl.BlockSpec(w.shape, lambda i: (0, 0), pipeline_mode=pl.Buffered(1)) for w in ws]
    return pl.pallas_call(
        functools.partial(_out_proj_kernel, n_in=n_in),
        grid=(m // tm,),
        in_specs=in_specs,
        out_specs=pl.BlockSpec((tm, d), lambda i: (i, 0)),
        out_shape=jax.ShapeDtypeStruct((m, d), F32),
        compiler_params=_params("parallel"),
    )(x, *acts, *ws)


def _mlp_kernel(x_ref, g_ref, wu_ref, wd_ref, gf_ref, o_ref, xn_ref, *, final_norm):
    j = pl.program_id(1)

    @pl.when(j == 0)
    def _():
        x = x_ref[...]
        xn_ref[...] = _rms(x, g_ref[...]).astype(BF16)
        o_ref[...] = x

    h = _dot(xn_ref[...], wu_ref[0].astype(BF16))
    a = jnp.square(jnp.maximum(h, 0.0)).astype(BF16)
    o_ref[...] += _dot(a, wd_ref[0].astype(BF16))

    if final_norm:
        @pl.when(j == pl.num_programs(1) - 1)
        def _():
            o_ref[...] = _rms(o_ref[...], gf_ref[...])


def _mlp(x, g, w_up, w_down, layer, g_final, tm, tf, final_norm):
    m, d = x.shape
    ff = w_up.shape[2]
    return pl.pallas_call(
        functools.partial(_mlp_kernel, final_norm=final_norm),
        grid=(m // tm, ff // tf),
        in_specs=[pl.BlockSpec((tm, d), lambda i, j: (i, 0)),
                  pl.BlockSpec((1, d), lambda i, j: (0, 0)),
                  pl.BlockSpec((1, d, tf), lambda i, j: (layer, 0, j)),
                  pl.BlockSpec((1, tf, d), lambda i, j: (layer, j, 0)),
                  pl.BlockSpec((1, d), lambda i, j: (0, 0))],
        out_specs=pl.BlockSpec((tm, d), lambda i, j: (i, 0)),
        out_shape=jax.ShapeDtypeStruct((m, d), F32),
        scratch_shapes=[pltpu.VMEM((tm, d), BF16)],
        compiler_params=_params("parallel", "arbitrary"),
    )(x, g.reshape(1, d), w_up, w_down, g_final.reshape(1, d))


def _gelu_tanh(x):
    return 0.5 * x * (1.0 + jnp.tanh(math.sqrt(2.0 / math.pi) * (x + 0.044715 * (x * x * x))))


def _compress_kernel(h_ref, pa_ref, pb_ref, wa_ref, wb_ref, w2_ref, o_ref):
    h = h_ref[0, 0].astype(F32)
    ha = _dot((h + pa_ref[0]).astype(BF16), wa_ref[0])
    hb = _dot((h + pb_ref[0]).astype(BF16), wb_ref[0])
    n = hb.shape[0]
    pre = ha + pltpu.roll(hb, n - 1, 0)
    o_ref[0, 0] = _dot(_gelu_tanh(pre).astype(BF16), w2_ref[0]).astype(o_ref.dtype)


def _compress(hkv, pa, pb, wa, wb, w2):
    _, b, r, w = hkv.shape
    hid2 = wa.shape[2]
    return pl.pallas_call(
        _compress_kernel,
        grid=(2, b),
        in_specs=[pl.BlockSpec((1, 1, r, w), lambda t, i: (t, i, 0, 0)),
                  pl.BlockSpec((1, 1, w), lambda t, i: (t, 0, 0)),
                  pl.BlockSpec((1, 1, w), lambda t, i: (t, 0, 0)),
                  pl.BlockSpec((1, w, hid2), lambda t, i: (t, 0, 0)),
                  pl.BlockSpec((1, w, hid2), lambda t, i: (t, 0, 0)),
                  pl.BlockSpec((1, hid2, LANES), lambda t, i: (t, 0, 0))],
        out_specs=pl.BlockSpec((1, 1, r, LANES), lambda t, i: (t, i, 0, 0)),
        out_shape=jax.ShapeDtypeStruct((2, b, r, LANES), BF16),
        compiler_params=_params("arbitrary", "arbitrary"),
    )(hkv, pa, pb, wa, wb, w2)


SWA, SLC, WIN = 0, 1, 2
FAR_UNROLL = 4


def _stack_pairs_t(q_ref, g):
    cols = [q_ref[0, :, (PAIRS * g + p) * LANES:(PAIRS * g + p + 1) * LANES].astype(F32).T for p in range(PAIRS)]
    return (jnp.concatenate(cols, axis=1) * (HEAD_DIM ** -0.5 * LOG2E)).astype(BF16)


def _rows_by_half(a, b, rows):
    parts = [jnp.broadcast_to(a, (HEAD_DIM, W4)), jnp.broadcast_to(b, (HEAD_DIM, W4))]
    if rows > 2 * HEAD_DIM:
        first = lax.broadcasted_iota(jnp.int32, (rows - 2 * HEAD_DIM, W4), 0) == 0
        parts.append(jnp.where(first, a, b))
    return jnp.concatenate(parts, axis=0)


def _even_attn_kernel(tab_ref, sink_ref, qa_ref, qb_ref, gate_ref,
                      ka_ref, va_ref, ks_ref, vs_ref, kw_ref, vw_ref, kc_ref, vc_ref,
                      ta_ref, tb_ref, up_ref, cb_ref, e3_ref, ovt_ref,
                      oa_ref, ob_ref,
                      kbd_ref, vbd_ref, ckbd_ref, cvbd_ref, msk_ref, acc_scr, m_scr, *, n_tiles):
    i = pl.program_id(1)
    groups = range(GROUPS)

    def block_diag(x, g):
        r = pltpu.roll(x, HEAD_DIM, 1)
        lo = lax.broadcasted_iota(jnp.int32, x.shape, 1) < HEAD_DIM
        own, other = (x, r) if g == 0 else (r, x)
        return jnp.where(lo, own, 0.0), jnp.where(lo, 0.0, other)

    @pl.when(i == 0)
    def _build():
        r16 = lax.broadcasted_iota(jnp.int32, (VT_ROWS - LANES, 2 * TQ), 0)
        c16 = lax.broadcasted_iota(jnp.int32, (VT_ROWS - LANES, 2 * TQ), 1)
        ones_rows = jnp.where((r16 == 0) & (c16 < TQ) | (r16 == 1) & (c16 >= TQ), 1.0, 0.0).astype(BF16)
        for t, ref in enumerate((ka_ref, ks_ref, kw_ref)):
            def kbody(j, c, t=t, ref=ref):
                x = ref[0, pl.ds(pl.multiple_of(j * TQ, TQ), TQ), :].astype(F32)
                for g in groups:
                    top, bot = block_diag(x, g)
                    kbd_ref[g, t, j, 0:TQ, :] = top.astype(BF16)
                    kbd_ref[g, t, j, TQ:2 * TQ, :] = bot.astype(BF16)
                return c
            lax.fori_loop(0, n_tiles, kbody, 0)
        for t, ref in enumerate((va_ref, vs_ref, vw_ref)):
            def vbody(j, c, t=t, ref=ref):
                x = ref[0, pl.ds(pl.multiple_of(j * TQ, TQ), TQ), :].astype(F32)
                for g in groups:
                    top, bot = block_diag(x, g)
                    vbd_ref[g, t, j, 0:LANES, 0:TQ] = top.T.astype(BF16)
                    vbd_ref[g, t, j, 0:LANES, TQ:2 * TQ] = bot.T.astype(BF16)
                    vbd_ref[g, t, j, LANES:, :] = ones_rows
                return c
            lax.fori_loop(0, n_tiles, vbody, 0)
        xk = kc_ref[0, 0].astype(F32)
        xv = vc_ref[0, 0].astype(F32)
        for g in groups:
            top, bot = block_diag(xk, g)
            ckbd_ref[g, 0:TQ, :] = top.astype(BF16)
            ckbd_ref[g, TQ:2 * TQ, :] = bot.astype(BF16)
            top, bot = block_diag(xv, g)
            cvbd_ref[g, :, 0:TQ] = top.T.astype(BF16)
            cvbd_ref[g, :, TQ:2 * TQ] = bot.T.astype(BF16)

    def near_bias(t_ref, g, delta):
        return lambda hf, p: t_ref[8 * g + 2 * p + hf, delta]

    def update(g, st, s, adds, vbd_t, first):
        m_old = [None if first else m_scr[g, st, hf] for hf in range(2)]
        p_rows, m_new = [], []
        for hf in range(2):
            strips, mns = [], []
            for p in range(PAIRS):
                sh = s[hf * TQ:(hf + 1) * TQ, p * TQ:(p + 1) * TQ]
                for a in adds:
                    sh = sh + a(hf, p)
                mn = jnp.max(sh, axis=0, keepdims=True)
                if not first:
                    mn = jnp.maximum(m_old[hf][:, p * TQ:(p + 1) * TQ], mn)
                strips.append(jnp.exp2(sh - mn).astype(BF16))
                mns.append(mn)
            p_rows.append(jnp.concatenate(strips, axis=1))
            m_new.append(jnp.concatenate(mns, axis=1))
            m_scr[g, st, hf] = m_new[hf]
        pv = _dot(vbd_t, jnp.concatenate(p_rows, axis=0))
        if first:
            acc_scr[g, st] = pv
        else:
            alpha = _rows_by_half(jnp.exp2(m_old[0] - m_new[0]), jnp.exp2(m_old[1] - m_new[1]), VT_ROWS)
            acc_scr[g, st] = alpha * acc_scr[g, st] + pv

    run_pipelined = _run_pipelined

    def tile_task(g, st, slot, j, q4t, adds, first=False):
        return (lambda: _dot(kbd_ref[g, slot, j], q4t),
                lambda s: update(g, st, s, adds, vbd_ref[g, slot, j], first))

    def finish(g, st, extra=None):
        acc = acc_scr[g, st]
        den = [acc[LANES + hf:LANES + hf + 1] for hf in range(2)]
        if extra is not None:
            den = [den[hf] + extra(hf, m_scr[g, st, hf]) for hf in range(2)]
        return acc[:LANES] * _rows_by_half(1.0 / den[0], 1.0 / den[1], LANES)

    q4a = [_stack_pairs_t(qa_ref, g) for g in groups]
    q4b = [_stack_pairs_t(qb_ref, g) for g in groups]

    t_q = i * TQ + lax.broadcasted_iota(jnp.int32, (1, TQ), 1)
    anyvis = jnp.where(t_q >= CMP_BLOCK - 1, 1.0, 0.0)
    s_cmp = [_dot(ckbd_ref[g], q4b[g]) for g in groups]
    psum = [jnp.zeros((TQ, TQ), F32) for _ in groups]
    p_rows = [[] for _ in groups]
    for hf in range(2):
        strips = [[] for _ in groups]
        for p in range(PAIRS):
            for g in groups:
                sh = s_cmp[g][hf * TQ:(hf + 1) * TQ, p * TQ:(p + 1) * TQ] + cb_ref[8 * g + 2 * p + hf]
                pe = jnp.exp2(sh - jnp.max(sh, axis=0, keepdims=True))
                pc = pe * (anyvis / jnp.sum(pe, axis=0, keepdims=True))
                psum[g] = psum[g] + pc
                strips[g].append(pc.astype(BF16))
        for g in groups:
            p_rows[g].append(jnp.concatenate(strips[g], axis=1))
    o_cmp = [_dot(cvbd_ref[g], jnp.concatenate(p_rows[g], axis=0)) for g in groups]

    psum2 = jnp.concatenate(psum, axis=1)
    ph = psum2.astype(BF16)
    r1 = psum2 - ph.astype(F32)
    pm = r1.astype(BF16)
    pl_ = (r1 - pm.astype(F32)).astype(BF16)
    ovt = ovt_ref[...]
    pslc = _dot(ovt, ph) + _dot(ovt, pm) + _dot(ovt, pl_)
    ns = pslc.shape[0]
    nb = lax.broadcasted_iota(jnp.int32, (ns, GROUPS * TQ), 0)
    tq = i * TQ + (lax.broadcasted_iota(jnp.int32, (ns, GROUPS * TQ), 1) & (TQ - 1))
    cur = tq // SLC_BLOCK
    forced = (jnp.where(nb == 0, 1.0, 0.0) + jnp.where(nb == cur, 1.0, 0.0) +
              jnp.where(nb == cur - 1, 1.0, 0.0))
    score = jnp.where(nb * SLC_BLOCK > tq, NEG, jnp.where(forced > 0.0, BIG, pslc))
    rank = jnp.zeros((ns, GROUPS * TQ), F32)
    for mth in range(ns):
        sm = score[mth:mth + 1, :]
        tie = jnp.where(nb > mth, 1.0, 0.0)
        rank = rank + jnp.where(sm > score, 1.0, jnp.where(sm == score, tie, 0.0))
    sel_rows = jnp.where(rank < SLC_TOPK, 0.0, NEG)
    for g in groups:
        msk_ref[g] = sel_rows[:, g * TQ:(g + 1) * TQ]

    def sel_add(g, j):
        cache = []

        def tile(hf, p):
            if not cache:
                rows = [jnp.broadcast_to(msk_ref[g, pl.ds(2 * j + e, 1), :], (SLC_BLOCK, TQ)) for e in range(2)]
                cache.append(jnp.concatenate(rows, axis=0))
            return cache[0]
        return tile

    n_full = B_WINDOW // TQ

    def near_tiles(n_back):
        tasks = [tile_task(g, SWA, K_A, i, q4a[g], [near_bias(ta_ref, g, 0)], first=True) for g in groups]
        tasks += [tile_task(g, WIN, K_W, i, q4b[g], [near_bias(tb_ref, g, 0)], first=True) for g in groups]
        tasks += [tile_task(g, SLC, K_S, i, q4b[g], [near_bias(tb_ref, g, 0), sel_add(g, i)], first=True)
                  for g in groups]
        if n_back >= 1:
            j = i - 1
            tasks += [tile_task(g, SWA, K_A, j, q4a[g], [near_bias(ta_ref, g, 1)]) for g in groups]
            tasks += [tile_task(g, WIN, K_W, j, q4b[g], [near_bias(tb_ref, g, 1)]) for g in groups]
            tasks += [tile_task(g, SLC, K_S, j, q4b[g], [near_bias(tb_ref, g, 1), sel_add(g, j)]) for g in groups]
        for d in range(2, n_back + 1):
            edge = [lambda hf, p: up_ref[...]] if d == n_full else []
            tasks += [tile_task(g, WIN, K_W, i - d, q4b[g], edge) for g in groups]
        run_pipelined(tasks)

    for c in range(n_full):
        pl.when(i == c)(functools.partial(near_tiles, c))
    pl.when(i >= n_full)(functools.partial(near_tiles, n_full))

    def far_tasks(d0, count):
        return [tile_task(g, SLC, K_S, i - d, q4b[g], [sel_add(g, i - d)])
                for d in [d0 + e for e in range(count)] for g in groups]

    def far_body(jj, c):
        run_pipelined(far_tasks(2 + FAR_UNROLL * jj, FAR_UNROLL))
        return c

    n_far = jnp.maximum(i - 1, 0)
    lax.fori_loop(0, n_far // FAR_UNROLL, far_body, 0)
    for rem in range(1, FAR_UNROLL):
        pl.when(n_far % FAR_UNROLL == rem)(
            functools.partial(lambda rem: run_pipelined(far_tasks(i - rem + 1, rem)), rem))

    def to_rows(o_t, p):
        return o_t[:, p * TQ:(p + 1) * TQ].T

    far_of = lambda h: tab_ref[N_BUCKETS - 1, h]
    sg = jax.nn.sigmoid(gate_ref[0].astype(F32)).T
    sg_hi = sg.astype(BF16)
    sg_lo = (sg - sg_hi.astype(F32)).astype(BF16)
    gates_t = _dot(e3_ref[...], sg_hi) + _dot(e3_ref[...], sg_lo)
    for g in groups:
        def sink_term(hf, m, g=g):
            row = jnp.concatenate([jnp.full((1, TQ), (sink_ref[8 * g + 2 * p + hf] - far_of(8 * g + 2 * p + hf)) * LOG2E,
                                            F32) for p in range(PAIRS)], axis=1)
            return jnp.exp2(row - m)

        o_swa = finish(g, SWA, sink_term)
        o_slc = finish(g, SLC)
        o_win = finish(g, WIN)
        for p in range(PAIRS):
            cols = slice((PAIRS * g + p) * LANES, (PAIRS * g + p + 1) * LANES)
            oa_ref[0, :, cols] = to_rows(o_swa, p).astype(oa_ref.dtype)
            qs = slice(p * TQ, (p + 1) * TQ)
            gate = lambda br: gates_t[br * B_HEADS * HEAD_DIM + cols.start:br * B_HEADS * HEAD_DIM + cols.stop]
            o_t = gate(0) * o_cmp[g][:, qs] + gate(1) * o_slc[:, qs] + gate(2) * o_win[:, qs]
            ob_ref[0, :, cols] = o_t.T.astype(ob_ref.dtype)


def _even_attention(pe, kvc, rel_bias, sinks, ta, tb, up, cbias, e3, ovt):
    b, s, _ = pe.shape
    n_tiles = s // TQ
    width = GROUPS * PAIRS * LANES

    def seg(k):
        return pl.BlockSpec((1, s, LANES), lambda bi, i, k=k: (bi, 0, k))

    full = lambda a: pl.BlockSpec(a.shape, lambda bi, i: (0,) * a.ndim)
    in_specs = [
        pl.BlockSpec(memory_space=pltpu.SMEM),
        pl.BlockSpec(memory_space=pltpu.SMEM),
        pl.BlockSpec((1, TQ, width), lambda bi, i: (bi, i, COL_QA // width)),
        pl.BlockSpec((1, TQ, width), lambda bi, i: (bi, i, COL_QB // width)),
        pl.BlockSpec((1, TQ, LANES), lambda bi, i: (bi, i, SEG_GATE)),
        seg(SEG_KA), seg(SEG_VA), seg(SEG_KS), seg(SEG_VS), seg(SEG_KW), seg(SEG_VW),
        pl.BlockSpec((1, 1, TQ, LANES), lambda bi, i: (0, bi, 0, 0)),
        pl.BlockSpec((1, 1, TQ, LANES), lambda bi, i: (1, bi, 0, 0)),
        full(ta), full(tb), full(up),
        pl.BlockSpec((B_HEADS, TQ, TQ), lambda bi, i: (0, 0, i)),
        full(e3), full(ovt),
    ]
    out_spec = pl.BlockSpec((1, TQ, width), lambda bi, i: (bi, i, 0))
    return pl.pallas_call(
        functools.partial(_even_attn_kernel, n_tiles=n_tiles),
        grid=(b, n_tiles),
        in_specs=in_specs,
        out_specs=[out_spec, out_spec],
        out_shape=[jax.ShapeDtypeStruct((b, s, width), BF16)] * 2,
        scratch_shapes=[pltpu.VMEM((GROUPS, 3, n_tiles, 2 * TQ, LANES), BF16),
                        pltpu.VMEM((GROUPS, 3, n_tiles, VT_ROWS, 2 * TQ), BF16),
                        pltpu.VMEM((GROUPS, 2 * TQ, LANES), BF16),
                        pltpu.VMEM((GROUPS, LANES, 2 * TQ), BF16),
                        pltpu.VMEM((GROUPS, s // SLC_BLOCK, TQ), F32),
                        pltpu.VMEM((GROUPS, 3, VT_ROWS, W4), F32),
                        pltpu.VMEM((GROUPS, 3, 2, 1, W4), F32)],
        compiler_params=_params("arbitrary", "arbitrary"),
    )(rel_bias, sinks, pe, pe, pe, pe, pe, pe, pe, pe, pe, kvc, kvc, ta, tb, up, cbias, e3, ovt)


def _mla_prep_kernel(x_ref, g_ref, win_ref, qn_ref, kvn_ref, wqt_ref, wk_ref, wvt_ref,
                     cos_ref, sin_ref, cost_ref, sint_ref, qt_ref, k_ref, vt_ref):
    xn = _rms(x_ref[...], g_ref[...]).astype(BF16)
    proj = _dot(xn, win_ref[...])
    cq = _rms(proj[:, :Q_LORA], qn_ref[...])
    ckv = _rms(proj[:, Q_LORA:Q_LORA + KV_LORA], kvn_ref[...])
    cq_t = cq.T.astype(BF16)
    ckv_t = ckv.T.astype(BF16)
    ckv = ckv.astype(BF16)
    kr = proj[:, Q_LORA + KV_LORA:]
    kr = (kr * cos_ref[...] + pltpu.roll(kr, HEAD_DIM, 1) * sin_ref[...]).astype(BF16)
    cos_t = cost_ref[...]
    sin_t = sint_ref[...]
    scale = (NOPE_DIM + ROPE_DIM) ** -0.5 * LOG2E
    tm = ckv_t.shape[1]
    ones_rows = jnp.where(lax.broadcasted_iota(jnp.int32, (VT_ROWS - V_DIM, tm), 0) == 0, 1.0, 0.0).astype(BF16)
    for h in range(C_HEADS):
        q_t = _dot(wqt_ref[h], cq_t)
        rp = q_t[NOPE_DIM:]
        rp = rp * cos_t + pltpu.roll(rp, HEAD_DIM, 0) * sin_t
        qt_ref[0, h, 0, 0:NOPE_DIM, :] = (q_t[:NOPE_DIM] * scale).astype(BF16)
        qt_ref[0, h, 0, NOPE_DIM:, :] = (rp * scale).astype(BF16)
        if h % 2 == 0:
            k2 = _dot(ckv, wk_ref[:, h * LANES:(h + 2) * LANES]).astype(BF16)
        k_ref[:, h * MLA_QK:h * MLA_QK + NOPE_DIM] = k2[:, (h % 2) * LANES:(h % 2 + 1) * LANES]
        k_ref[:, h * MLA_QK + NOPE_DIM:(h + 1) * MLA_QK] = kr
        vt_ref[0, h, 0, 0:V_DIM, :] = _dot(wvt_ref[h], ckv_t).astype(BF16)
        vt_ref[0, h, 0, V_DIM:, :] = ones_rows


def _mla_prep(x, g, w_in, qn, kvn, wqt, wk, wvt, cos, sin, b, s):
    m, d = x.shape
    tm = MLA_TK
    pos_tiles = s // tm
    full = lambda a: pl.BlockSpec(a.shape, lambda i: (0,) * a.ndim, pipeline_mode=pl.Buffered(1))
    tile_t = lambda rows: pl.BlockSpec((1, C_HEADS, 1, rows, tm), lambda i: (i // pos_tiles, 0, i % pos_tiles, 0, 0))
    return pl.pallas_call(
        _mla_prep_kernel,
        grid=(m // tm,),
        in_specs=[pl.BlockSpec((tm, d), lambda i: (i, 0)),
                  pl.BlockSpec((1, d), lambda i: (0, 0)),
                  full(w_in), pl.BlockSpec((1, Q_LORA), lambda i: (0, 0)),
                  pl.BlockSpec((1, KV_LORA), lambda i: (0, 0)), full(wqt), full(wk), full(wvt),
                  pl.BlockSpec((tm, LANES), lambda i: (i % pos_tiles, 0)),
                  pl.BlockSpec((tm, LANES), lambda i: (i % pos_tiles, 0)),
                  pl.BlockSpec((LANES, tm), lambda i: (0, i % pos_tiles)),
                  pl.BlockSpec((LANES, tm), lambda i: (0, i % pos_tiles))],
        out_specs=[tile_t(MLA_QK),
                   pl.BlockSpec((tm, C_HEADS * MLA_QK), lambda i: (i, 0)),
                   tile_t(VT_ROWS)],
        out_shape=[jax.ShapeDtypeStruct((b, C_HEADS, pos_tiles, MLA_QK, tm), BF16),
                   jax.ShapeDtypeStruct((m, C_HEADS * MLA_QK), BF16),
                   jax.ShapeDtypeStruct((b, C_HEADS, pos_tiles, VT_ROWS, tm), BF16)],
        compiler_params=_params("parallel"),
    )(x, g.reshape(1, d), w_in, qn.reshape(1, -1), kvn.reshape(1, -1), wqt, wk, wvt, cos, sin, cos.T, sin.T)


def _mla_attn_kernel(qt_ref, k_ref, vt_ref, o_ref, s_scr, acc_scr, m_scr, *, n_q):
    tq = tk = MLA_TQ
    parts = tq // MLA_TK
    ki = lax.broadcasted_iota(jnp.int32, (tk, tq), 0)
    qi = lax.broadcasted_iota(jnp.int32, (tk, tq), 1)
    causal = ki <= qi
    heads = range(MLA_HEADS)

    def q_tile(t, c):
        q_ts = [jnp.concatenate([qt_ref[0, h, parts * t + d] for d in range(parts)], axis=1) for h in heads]

        def logits_to(slot, j, hs=heads):
            rows = pl.ds(pl.multiple_of(j * tk, tk), tk)
            for h in hs:
                s_scr[h, slot] = _dot(k_ref[0, rows, h * MLA_QK:(h + 1) * MLA_QK], q_ts[h])

        def update_from(slot, j, diagonal, hs=heads):
            for h in hs:
                s = s_scr[h, slot]
                if diagonal:
                    s = jnp.where(causal, s, NEG)
                m = m_scr[h]
                mn = jnp.maximum(m, jnp.max(s, axis=0, keepdims=True))
                p = jnp.exp2(s - mn).astype(BF16)
                m_scr[h] = mn
                v_t = jnp.concatenate([vt_ref[0, h, parts * j + d] for d in range(parts)], axis=1)
                acc_scr[h] = jnp.exp2(m - mn) * acc_scr[h] + _dot(v_t, p)

        m_scr[...] = jnp.full(m_scr.shape, NEG, F32)
        acc_scr[...] = jnp.zeros(acc_scr.shape, F32)
        logits_to(0, t)
        for h in heads:
            logits_to(1, 0, [h])
            update_from(0, t, True, [h])

        def body(jj, c2):
            for h in heads:
                logits_to(0, 2 * jj + 1, [h])
                update_from(1, 2 * jj, False, [h])
            for h in heads:
                logits_to(1, jnp.minimum(2 * jj + 2, t - 1), [h])
                update_from(0, 2 * jj + 1, False, [h])
            return c2

        lax.fori_loop(0, t // 2, body, 0)
        pl.when(t % 2 == 1)(lambda: update_from(1, t - 1, False))
        rows = pl.ds(pl.multiple_of(t * tq, tq), tq)
        for h in heads:
            acc = acc_scr[h]
            o = acc[:V_DIM] * (1.0 / acc[V_DIM:V_DIM + 1])
            o_ref[0, rows, h * LANES:(h + 1) * LANES] = o.T.astype(o_ref.dtype)
        return c

    lax.fori_loop(0, n_q, q_tile, 0)


def _mla_attention(qt, k, vt):
    b, s, _ = k.shape
    n_k = s // MLA_TK
    nh = MLA_HEADS
    return pl.pallas_call(
        functools.partial(_mla_attn_kernel, n_q=s // MLA_TQ),
        grid=(b, C_HEADS // nh),
        in_specs=[pl.BlockSpec((1, nh, n_k, MLA_QK, MLA_TK), lambda bi, h: (bi, h, 0, 0, 0)),
                  pl.BlockSpec((1, s, MLA_QK * nh), lambda bi, h: (bi, 0, h)),
                  pl.BlockSpec((1, nh, n_k, VT_ROWS, MLA_TK), lambda bi, h: (bi, h, 0, 0, 0))],
        out_specs=pl.BlockSpec((1, s, LANES * nh), lambda bi, h: (bi, 0, h)),
        out_shape=jax.ShapeDtypeStruct((b, s, C_HEADS * V_DIM), BF16),
        scratch_shapes=[pltpu.VMEM((nh, 2, MLA_TQ, MLA_TQ), F32),
                        pltpu.VMEM((nh, VT_ROWS, MLA_TQ), F32),
                        pltpu.VMEM((nh, 1, MLA_TQ), F32)],
        compiler_params=_params("parallel", "arbitrary"),
    )(qt, k, vt)


def _even_in_weight(w):
    cuts = np.cumsum([1024, 128, 128, 1024, 128, 128, 128, 128, 128, 128, 48])[:-1]
    qa, ka, va, qb, kc, vc, ks, vs, kw, vw, gate = jnp.split(w.astype(BF16), [int(c) for c in cuts], axis=1)
    gate = jnp.pad(gate, ((0, 0), (0, LANES - gate.shape[1])))
    return jnp.concatenate([qa, qb, ka, va, kc, vc, ks, vs, kw, vw, gate], axis=1)


def _compress_weights(pos, w1, w2):
    half = CMP_BLOCK // 2
    eye = jnp.eye(GROUPS, dtype=BF16)
    w1r = w1.astype(BF16).reshape(2, half, HEAD_DIM, CMP_HIDDEN)
    wab = jnp.einsum('rldh,gk->rlgdkh', w1r, eye).reshape(2, half * GROUPS * HEAD_DIM, GROUPS * CMP_HIDDEN)
    posr = jnp.broadcast_to(pos.reshape(2, half, 1, HEAD_DIM), (2, half, GROUPS, HEAD_DIM)).reshape(2, 1, -1)
    w2bd = jnp.einsum('hd,gk->ghkd', w2.astype(BF16), eye).reshape(GROUPS * CMP_HIDDEN, GROUPS * HEAD_DIM)
    return posr[0], posr[1], wab[0], wab[1], w2bd


def _rope_chunk_cols(w):
    z = jnp.zeros(w.shape[:-1] + (ROPE_DIM // 2,), w.dtype)
    return jnp.concatenate([w[..., :ROPE_DIM // 2], z, w[..., ROPE_DIM // 2:], z], axis=-1)


def _mla_weights(w_in, w_q_up, w_kv_up):
    w_in, w_q_up, w_kv_up = w_in.astype(BF16), w_q_up.astype(BF16), w_kv_up.astype(BF16)
    w_in2 = jnp.concatenate([w_in[:, :Q_LORA + KV_LORA], _rope_chunk_cols(w_in[:, Q_LORA + KV_LORA:])], axis=1)
    wq = w_q_up.reshape(Q_LORA, C_HEADS, NOPE_DIM + ROPE_DIM)
    wq2 = jnp.concatenate([wq[..., :NOPE_DIM], _rope_chunk_cols(wq[..., NOPE_DIM:])], axis=-1)
    wqt = wq2.transpose(1, 2, 0)
    wkv = w_kv_up.reshape(KV_LORA, C_HEADS, NOPE_DIM + V_DIM)
    wk = wkv[..., :NOPE_DIM].reshape(KV_LORA, C_HEADS * NOPE_DIM)
    wvt = wkv[..., NOPE_DIM:].transpose(1, 2, 0)
    return w_in2, wqt, wk, wvt


def _rope_tables(s):
    inv = 1.0 / (ROPE_THETA ** (jnp.arange(0, ROPE_DIM, 2, dtype=F32) / ROPE_DIM))
    ang = jnp.arange(s, dtype=F32)[:, None] * inv[None]
    cos, sin = jnp.cos(ang), jnp.sin(ang)
    z = jnp.zeros_like(cos)
    return jnp.concatenate([cos, z, cos, z], axis=1), jnp.concatenate([-sin, z, sin, z], axis=1)


def _static_tables(s):
    k = np.arange(TQ)[:, None]
    q = np.arange(TQ)[None, :]
    diag = np.where(k <= q, _bucket_np(q - k), -1)
    prev = _bucket_np(q - k + TQ)
    near_swa = np.stack([diag, np.where(k > q, prev, -1)])
    near_nsa = np.stack([diag, prev])
    upper = np.where(k > q, 0.0, NEG).astype(np.float32)
    c = np.arange(LANES)[:, None]
    t = np.arange(s)[None, :]
    cdist = t - (c * CMP_STRIDE + CMP_BLOCK - 1)
    cmp_map = np.where(cdist >= 0, _bucket_np(cdist), -1)
    ns = s // SLC_BLOCK
    nc = (s - CMP_BLOCK) // CMP_STRIDE + 1
    c_start = np.arange(LANES) * CMP_STRIDE
    s_start = np.arange(ns) * SLC_BLOCK
    ovt = ((c_start[None, :] <= s_start[:, None] + SLC_BLOCK - 1) &
           (c_start[None, :] + CMP_BLOCK - 1 >= s_start[:, None]) &
           (np.arange(LANES)[None, :] < nc)).astype(np.float32)
    e3 = np.zeros((3, B_HEADS * HEAD_DIM, LANES), np.float32)
    for h in range(B_HEADS):
        for br in range(3):
            e3[br, h * HEAD_DIM:(h + 1) * HEAD_DIM, h * 3 + br] = 1.0
    e3 = e3.reshape(3 * B_HEADS * HEAD_DIM, LANES)
    return near_swa, near_nsa, upper, cmp_map, ovt, e3


def _even_layer(x, b, s, rel_bias, norm, w_in, sinks, pos_k, pos_v, k_w1, k_w2, v_w1, v_w2, w_out):
    near_swa, near_nsa, upper, cmp_map, ovt, e3 = _static_tables(s)
    ta = _bias_tiles(rel_bias, near_swa, 0, A_HEADS)
    tb = _bias_tiles(rel_bias, near_nsa, A_HEADS, B_HEADS)
    cbias = _cmp_bias(rel_bias, cmp_map, A_HEADS, B_HEADS)

    pe = _norm_matmul(x, norm, _even_in_weight(w_in), IN_PROJ_TM, EVEN_COLS, BF16).reshape(b, s, EVEN_COLS)

    rows = s // (CMP_BLOCK // 2)
    hkv = jnp.stack([pe[:, :, SEG_KC * LANES:(SEG_KC + 1) * LANES],
                     pe[:, :, SEG_VC * LANES:(SEG_VC + 1) * LANES]]).reshape(2, b, rows, -1)
    ck = _compress_weights(pos_k, k_w1, k_w2)
    cv = _compress_weights(pos_v, v_w1, v_w2)
    kvc = _compress(hkv, *[jnp.stack([a, c]) for a, c in zip(ck, cv)])
    assert rows <= TQ
    kvc = jnp.pad(kvc, ((0, 0), (0, 0), (0, TQ - rows), (0, 0)))

    oa, ob = _even_attention(pe, kvc, rel_bias, sinks, ta, tb, jnp.asarray(upper), cbias,
                             jnp.asarray(e3, BF16), jnp.asarray(ovt, BF16))
    w_out = w_out.astype(BF16)
    n_a = A_HEADS * HEAD_DIM
    return _out_proj(x, [oa.reshape(b * s, -1), ob.reshape(b * s, -1)], [w_out[:n_a], w_out[n_a:]], OUT_PROJ_TM)


def _odd_layer(x, b, s, norm, w_in, q_norm, w_q_up, kv_norm, w_kv_up, w_out):
    w_in2, wqt, wk, wvt = _mla_weights(w_in, w_q_up, w_kv_up)
    cos, sin = _rope_tables(s)
    qt, k, vt = _mla_prep(x, norm, w_in2, q_norm, kv_norm, wqt, wk, wvt, cos, sin, b, s)
    o = _mla_attention(qt, k.reshape(b, s, -1), vt)
    return _out_proj(x, [o.reshape(b * s, -1)], [w_out.astype(BF16)], OUT_PROJ_TM)


def kernel(x, rel_bias, norm_mix_e, w_in_e, sinks, cmp_pos_k, cmp_pos_v, cmp_k_w1, cmp_k_w2, cmp_v_w1, cmp_v_w2, w_out_e, norm_mix_o, w_in_o, q_norm, w_q_up, kv_norm, w_kv_up, w_out_o, norm_mlp, w_up, w_down, norm_final):
    b, s, d = x.shape
    depth = norm_mlp.shape[0]
    assert s % MLA_TQ == 0 and s // (CMP_BLOCK // 2) <= TQ and s // SLC_BLOCK <= LANES
    assert (b * s) % MLP_TM == 0 and (b * s) % IN_PROJ_TM == 0 and w_up.shape[2] % MLP_TF == 0
    h = x.reshape(b * s, d)
    for layer in range(depth):
        i = layer // 2
        if layer % 2 == 0:
            h = _even_layer(h, b, s, rel_bias, norm_mix_e[i], w_in_e[i], sinks[i], cmp_pos_k[i], cmp_pos_v[i],
                            cmp_k_w1[i], cmp_k_w2[i], cmp_v_w1[i], cmp_v_w2[i], w_out_e[i])
        else:
            h = _odd_layer(h, b, s, norm_mix_o[i], w_in_o[i], q_norm[i], w_q_up[i], kv_norm[i], w_kv_up[i],
                           w_out_o[i])
        h = _mlp(h, norm_mlp[layer], w_up, w_down, layer, norm_final, MLP_TM, MLP_TF, layer == depth - 1)
    return h.reshape(b, s, d)
```

```python
import functools
import math

import numpy as np
import jax
import jax.numpy as jnp
from jax import lax
from jax.experimental import pallas as pl
from jax.experimental.pallas import tpu as pltpu

F32 = jnp.float32
BF16 = jnp.bfloat16

LANES = 128
VMEM_LIMIT_BYTES = 56 * 1024 * 1024

EPS = 1e-6
NEG = -1e30
BIG = 1e4
HEAD_DIM = 64
TQ = 128
MLA_TQ = 512
MLA_TK = 512
MLA_HEADS = 4
IN_PROJ_TM = 1024
OUT_PROJ_TM = 512
MLP_TM, MLP_TF = 1024, 512
N_BUCKETS = 32
MAX_DISTANCE = 128
A_HEADS = 16
B_HEADS = 16
GROUPS = 2
PAIRS = 4
W4 = PAIRS * TQ
CMP_BLOCK = 32
CMP_STRIDE = 16
CMP_HIDDEN = 256
SLC_BLOCK = 64
SLC_TOPK = 8
A_WINDOW = 128
B_WINDOW = 512
C_HEADS = 16
Q_LORA = 768
KV_LORA = 512
NOPE_DIM = 128
ROPE_DIM = 64
V_DIM = 128
MLA_QK = NOPE_DIM + LANES
VT_ROWS = V_DIM + 16
LOG2E = math.log2(math.e)
ROPE_THETA = 10000.0

COL_QA, COL_QB = 0, 1024
SEG_KA, SEG_VA, SEG_KC, SEG_VC, SEG_KS, SEG_VS, SEG_KW, SEG_VW, SEG_GATE = range(16, 25)
EVEN_COLS = 25 * LANES
K_A, K_S, K_W = 0, 1, 2


def _params(*sem):
    return pltpu.CompilerParams(dimension_semantics=sem, vmem_limit_bytes=VMEM_LIMIT_BYTES)


def _rms(x, g):
    return x * lax.rsqrt(jnp.mean(x * x, axis=-1, keepdims=True) + EPS) * g


def _dot(a, b):
    return jnp.dot(a, b, preferred_element_type=F32)


def _run_pipelined(tasks):
    s = tasks[0][0]()
    for n, (_, apply_fn) in enumerate(tasks):
        s_next = tasks[n + 1][0]() if n + 1 < len(tasks) else None
        apply_fn(s)
        s = s_next


def _bucket_np(dist):
    dist = np.maximum(dist, 0)
    max_exact = N_BUCKETS // 2
    d = np.maximum(dist, 1).astype(np.float32)
    large = max_exact + (np.log(d / np.float32(max_exact)) / np.float32(math.log(MAX_DISTANCE / max_exact))
                         * np.float32(N_BUCKETS - max_exact)).astype(np.int32)
    large = np.minimum(large, N_BUCKETS - 1)
    return np.where(dist < max_exact, dist, large).astype(np.int32)


def _bias_kernel(tab_ref, bm_ref, o_ref, *, head0):
    h = pl.program_id(0) + head0
    bm = bm_ref[...]
    acc = jnp.zeros(bm.shape, F32)
    for b in range(N_BUCKETS):
        acc = jnp.where(bm == b, tab_ref[b, h], acc)
    o_ref[0] = jnp.where(bm < 0, NEG, (acc - tab_ref[N_BUCKETS - 1, h]) * LOG2E)


def _cmp_bias_kernel(tab_ref, bm_ref, o_ref, *, head0, bands):
    h = pl.program_id(0) + head0
    far = tab_ref[N_BUCKETS - 1, h]
    rows = lax.broadcasted_iota(jnp.int32, (bm_ref.shape[0], TQ), 0)
    for i, (start, size) in enumerate(bands):
        cols = slice(i * TQ, (i + 1) * TQ)
        o_ref[0, :, cols] = jnp.where(rows < start, 0.0, NEG)
        bm = bm_ref[start:start + size, cols]
        acc = jnp.zeros(bm.shape, F32)
        for b in range(N_BUCKETS):
            acc = jnp.where(bm == b, tab_ref[b, h], acc)
        o_ref[0, start:start + size, cols] = jnp.where(bm < 0, NEG, (acc - far) * LOG2E)


def _cmp_bias(rel_bias, cmp_map, head0, n_heads):
    n_blocks, s = cmp_map.shape
    bands = []
    for i in range(s // TQ):
        tile = cmp_map[:, i * TQ:(i + 1) * TQ]
        varying = np.nonzero(((tile >= 0) & (tile < N_BUCKETS - 1)).any(axis=1))[0]
        lo = int(varying.min()) // 8 * 8 if varying.size else 0
        hi = -(-(int(varying.max()) + 1) // 8) * 8 if varying.size else 8
        assert (tile[:lo] == N_BUCKETS - 1).all() and (tile[hi:] < 0).all()
        bands.append((lo, hi - lo))
    return pl.pallas_call(
        functools.partial(_cmp_bias_kernel, head0=head0, bands=tuple(bands)),
        grid=(n_heads,),
        in_specs=[pl.BlockSpec(memory_space=pltpu.SMEM),
                  pl.BlockSpec(cmp_map.shape, lambda h: (0, 0))],
        out_specs=pl.BlockSpec((1, n_blocks, s), lambda h: (h, 0, 0)),
        out_shape=jax.ShapeDtypeStruct((n_heads, n_blocks, s), F32),
        compiler_params=_params("arbitrary"),
    )(rel_bias, jnp.asarray(cmp_map))


def _bias_tiles(rel_bias, bucket_map, head0, n_heads):
    shp = bucket_map.shape
    nd = len(shp)
    return pl.pallas_call(
        functools.partial(_bias_kernel, head0=head0),
        grid=(n_heads,),
        in_specs=[pl.BlockSpec(memory_space=pltpu.SMEM),
                  pl.BlockSpec(shp, lambda h: (0,) * nd)],
        out_specs=pl.BlockSpec((1,) + shp, lambda h: (h,) + (0,) * nd),
        out_shape=jax.ShapeDtypeStruct((n_heads,) + shp, F32),
        compiler_params=_params("arbitrary"),
    )(rel_bias, jnp.asarray(bucket_map))


def _norm_matmul_kernel(x_ref, g_ref, w_ref, o_ref, xn_ref):
    @pl.when(pl.program_id(1) == 0)
    def _():
        xn_ref[...] = _rms(x_ref[...], g_ref[...]).astype(BF16)

    o_ref[...] = _dot(xn_ref[...], w_ref[...]).astype(o_ref.dtype)


def _norm_matmul(x, g, w, tm, tn, out_dtype):
    m, d = x.shape
    n = w.shape[1]
    return pl.pallas_call(
        _norm_matmul_kernel,
        grid=(m // tm, n // tn),
        in_specs=[pl.BlockSpec((tm, d), lambda i, j: (i, 0)),
                  pl.BlockSpec((1, d), lambda i, j: (0, 0)),
                  pl.BlockSpec((d, tn), lambda i, j: (0, j),
                               pipeline_mode=pl.Buffered(1) if tn == n else None)],
        out_specs=pl.BlockSpec((tm, tn), lambda i, j: (i, j)),
        out_shape=jax.ShapeDtypeStruct((m, n), out_dtype),
        scratch_shapes=[pltpu.VMEM((tm, d), BF16)],
        compiler_params=_params("parallel", "arbitrary"),
    )(x, g.reshape(1, d), w)


def _out_proj_kernel(*refs, n_in):
    x_ref = refs[0]
    o_ref = refs[2 * n_in + 1]
    acc = x_ref[...]
    for t in range(n_in):
        acc = acc + _dot(refs[1 + t][...], refs[1 + n_in + t][...])
    o_ref[...] = acc


def _out_proj(x, acts, ws, tm):
    m, d = x.shape
    n_in = len(acts)
    in_specs = [pl.BlockSpec((tm, d), lambda i: (i, 0))]
    in_specs += [pl.BlockSpec((tm, a.shape[1]), lambda i: (i, 0)) for a in acts]
    in_specs += [pl.BlockSpec(w.shape, lambda i: (0, 0), pipeline_mode=pl.Buffered(1)) for w in ws]
    return pl.pallas_call(
        functools.partial(_out_proj_kernel, n_in=n_in),
        grid=(m // tm,),
        in_specs=in_specs,
        out_specs=pl.BlockSpec((tm, d), lambda i: (i, 0)),
        out_shape=jax.ShapeDtypeStruct((m, d), F32),
        compiler_params=_params("parallel"),
    )(x, *acts, *ws)


def _mlp_kernel(x_ref, g_ref, wu_ref, wd_ref, gf_ref, o_ref, xn_ref, *, final_norm):
    j = pl.program_id(1)

    @pl.when(j == 0)
    def _():
        x = x_ref[...]
        xn_ref[...] = _rms(x, g_ref[...]).astype(BF16)
        o_ref[...] = x

    h = _dot(xn_ref[...], wu_ref[0].astype(BF16))
    a = jnp.square(jnp.maximum(h, 0.0)).astype(BF16)
    o_ref[...] += _dot(a, wd_ref[0].astype(BF16))

    if final_norm:
        @pl.when(j == pl.num_programs(1) - 1)
        def _():
            o_ref[...] = _rms(o_ref[...], gf_ref[...])


def _mlp(x, g, w_up, w_down, layer, g_final, tm, tf, final_norm):
    m, d = x.shape
    ff = w_up.shape[2]
    return pl.pallas_call(
        functools.partial(_mlp_kernel, final_norm=final_norm),
        grid=(m // tm, ff // tf),
        in_specs=[pl.BlockSpec((tm, d), lambda i, j: (i, 0)),
                  pl.BlockSpec((1, d), lambda i, j: (0, 0)),
                  pl.BlockSpec((1, d, tf), lambda i, j: (layer, 0, j)),
                  pl.BlockSpec((1, tf, d), lambda i, j: (layer, j, 0)),
                  pl.BlockSpec((1, d), lambda i, j: (0, 0))],
        out_specs=pl.BlockSpec((tm, d), lambda i, j: (i, 0)),
        out_shape=jax.ShapeDtypeStruct((m, d), F32),
        scratch_shapes=[pltpu.VMEM((tm, d), BF16)],
        compiler_params=_params("parallel", "arbitrary"),
    )(x, g.reshape(1, d), w_up, w_down, g_final.reshape(1, d))


def _gelu_tanh(x):
    return 0.5 * x * (1.0 + jnp.tanh(math.sqrt(2.0 / math.pi) * (x + 0.044715 * (x * x * x))))


def _compress_kernel(h_ref, pa_ref, pb_ref, wa_ref, wb_ref, w2_ref, o_ref):
    h = h_ref[0, 0].astype(F32)
    ha = _dot((h + pa_ref[0]).astype(BF16), wa_ref[0])
    hb = _dot((h + pb_ref[0]).astype(BF16), wb_ref[0])
    n = hb.shape[0]
    pre = ha + pltpu.roll(hb, n - 1, 0)
    o_ref[0, 0] = _dot(_gelu_tanh(pre).astype(BF16), w2_ref[0]).astype(o_ref.dtype)


def _compress(hkv, pa, pb, wa, wb, w2):
    _, b, r, w = hkv.shape
    hid2 = wa.shape[2]
    return pl.pallas_call(
        _compress_kernel,
        grid=(2, b),
        in_specs=[pl.BlockSpec((1, 1, r, w), lambda t, i: (t, i, 0, 0)),
                  pl.BlockSpec((1, 1, w), lambda t, i: (t, 0, 0)),
                  pl.BlockSpec((1, 1, w), lambda t, i: (t, 0, 0)),
                  pl.BlockSpec((1, w, hid2), lambda t, i: (t, 0, 0)),
                  pl.BlockSpec((1, w, hid2), lambda t, i: (t, 0, 0)),
                  pl.BlockSpec((1, hid2, LANES), lambda t, i: (t, 0, 0))],
        out_specs=pl.BlockSpec((1, 1, r, LANES), lambda t, i: (t, i, 0, 0)),
        out_shape=jax.ShapeDtypeStruct((2, b, r, LANES), BF16),
        compiler_params=_params("arbitrary", "arbitrary"),
    )(hkv, pa, pb, wa, wb, w2)


SWA, SLC, WIN = 0, 1, 2
FAR_UNROLL = 4


def _stack_pairs_t(q_ref, g):
    cols = [q_ref[0, :, (PAIRS * g + p) * LANES:(PAIRS * g + p + 1) * LANES].astype(F32).T for p in range(PAIRS)]
    return (jnp.concatenate(cols, axis=1) * (HEAD_DIM ** -0.5 * LOG2E)).astype(BF16)


def _rows_by_half(a, b, rows):
    parts = [jnp.broadcast_to(a, (HEAD_DIM, W4)), jnp.broadcast_to(b, (HEAD_DIM, W4))]
    if rows > 2 * HEAD_DIM:
        first = lax.broadcasted_iota(jnp.int32, (rows - 2 * HEAD_DIM, W4), 0) == 0
        parts.append(jnp.where(first, a, b))
    return jnp.concatenate(parts, axis=0)


def _even_attn_kernel(tab_ref, sink_ref, qa_ref, qb_ref, gate_ref,
                      ka_ref, va_ref, ks_ref, vs_ref, kw_ref, vw_ref, kc_ref, vc_ref,
                      ta_ref, tb_ref, up_ref, cb_ref, e3_ref, ovt_ref,
                      oa_ref, ob_ref,
                      kbd_ref, vbd_ref, ckbd_ref, cvbd_ref, msk_ref, acc_scr, m_scr, *, n_tiles):
    i = pl.program_id(1)
    groups = range(GROUPS)

    def block_diag(x, g):
        r = pltpu.roll(x, HEAD_DIM, 1)
        lo = lax.broadcasted_iota(jnp.int32, x.shape, 1) < HEAD_DIM
        own, other = (x, r) if g == 0 else (r, x)
        return jnp.where(lo, own, 0.0), jnp.where(lo, 0.0, other)

    @pl.when(i == 0)
    def _build():
        r16 = lax.broadcasted_iota(jnp.int32, (VT_ROWS - LANES, 2 * TQ), 0)
        c16 = lax.broadcasted_iota(jnp.int32, (VT_ROWS - LANES, 2 * TQ), 1)
        ones_rows = jnp.where((r16 == 0) & (c16 < TQ) | (r16 == 1) & (c16 >= TQ), 1.0, 0.0).astype(BF16)
        for t, ref in enumerate((ka_ref, ks_ref, kw_ref)):
            def kbody(j, c, t=t, ref=ref):
                x = ref[0, pl.ds(pl.multiple_of(j * TQ, TQ), TQ), :].astype(F32)
                for g in groups:
                    top, bot = block_diag(x, g)
                    kbd_ref[g, t, j, 0:TQ, :] = top.astype(BF16)
                    kbd_ref[g, t, j, TQ:2 * TQ, :] = bot.astype(BF16)
                return c
            lax.fori_loop(0, n_tiles, kbody, 0)
        for t, ref in enumerate((va_ref, vs_ref, vw_ref)):
            def vbody(j, c, t=t, ref=ref):
                x = ref[0, pl.ds(pl.multiple_of(j * TQ, TQ), TQ), :].astype(F32)
                for g in groups:
                    top, bot = block_diag(x, g)
                    vbd_ref[g, t, j, 0:LANES, 0:TQ] = top.T.astype(BF16)
                    vbd_ref[g, t, j, 0:LANES, TQ:2 * TQ] = bot.T.astype(BF16)
                    vbd_ref[g, t, j, LANES:, :] = ones_rows
                return c
            lax.fori_loop(0, n_tiles, vbody, 0)
        xk = kc_ref[0, 0].astype(F32)
        xv = vc_ref[0, 0].astype(F32)
        for g in groups:
            top, bot = block_diag(xk, g)
            ckbd_ref[g, 0:TQ, :] = top.astype(BF16)
            ckbd_ref[g, TQ:2 * TQ, :] = bot.astype(BF16)
            top, bot = block_diag(xv, g)
            cvbd_ref[g, :, 0:TQ] = top.T.astype(BF16)
            cvbd_ref[g, :, TQ:2 * TQ] = bot.T.astype(BF16)

    def near_bias(t_ref, g, delta):
        return lambda hf, p: t_ref[8 * g + 2 * p + hf, delta]

    def update(g, st, s, adds, vbd_t, first):
        m_old = [None if first else m_scr[g, st, hf] for hf in range(2)]
        p_rows, m_new = [], []
        for hf in range(2):
            strips, mns = [], []
            for p in range(PAIRS):
                sh = s[hf * TQ:(hf + 1) * TQ, p * TQ:(p + 1) * TQ]
                for a in adds:
                    sh = sh + a(hf, p)
                mn = jnp.max(sh, axis=0, keepdims=True)
                if not first:
                    mn = jnp.maximum(m_old[hf][:, p * TQ:(p + 1) * TQ], mn)
                strips.append(jnp.exp2(sh - mn).astype(BF16))
                mns.append(mn)
            p_rows.append(jnp.concatenate(strips, axis=1))
            m_new.append(jnp.concatenate(mns, axis=1))
            m_scr[g, st, hf] = m_new[hf]
        pv = _dot(vbd_t, jnp.concatenate(p_rows, axis=0))
        if first:
            acc_scr[g, st] = pv
        else:
            alpha = _rows_by_half(jnp.exp2(m_old[0] - m_new[0]), jnp.exp2(m_old[1] - m_new[1]), VT_ROWS)
            acc_scr[g, st] = alpha * acc_scr[g, st] + pv

    run_pipelined = _run_pipelined

    def tile_task(g, st, slot, j, q4t, adds, first=False):
        return (lambda: _dot(kbd_ref[g, slot, j], q4t),
                lambda s: update(g, st, s, adds, vbd_ref[g, slot, j], first))

    def finish(g, st, extra=None):
        acc = acc_scr[g, st]
        den = [acc[LANES + hf:LANES + hf + 1] for hf in range(2)]
        if extra is not None:
            den = [den[hf] + extra(hf, m_scr[g, st, hf]) for hf in range(2)]
        return acc[:LANES] * _rows_by_half(1.0 / den[0], 1.0 / den[1], LANES)

    q4a = [_stack_pairs_t(qa_ref, g) for g in groups]
    q4b = [_stack_pairs_t(qb_ref, g) for g in groups]

    t_q = i * TQ + lax.broadcasted_iota(jnp.int32, (1, TQ), 1)
    anyvis = jnp.where(t_q >= CMP_BLOCK - 1, 1.0, 0.0)
    s_cmp = [_dot(ckbd_ref[g], q4b[g]) for g in groups]
    psum = [jnp.zeros((TQ, TQ), F32) for _ in groups]
    p_rows = [[] for _ in groups]
    for hf in range(2):
        strips = [[] for _ in groups]
        for p in range(PAIRS):
            for g in groups:
                sh = s_cmp[g][hf * TQ:(hf + 1) * TQ, p * TQ:(p + 1) * TQ] + cb_ref[8 * g + 2 * p + hf]
                pe = jnp.exp2(sh - jnp.max(sh, axis=0, keepdims=True))
                pc = pe * (anyvis / jnp.sum(pe, axis=0, keepdims=True))
                psum[g] = psum[g] + pc
                strips[g].append(pc.astype(BF16))
        for g in groups:
            p_rows[g].append(jnp.concatenate(strips[g], axis=1))
    o_cmp = [_dot(cvbd_ref[g], jnp.concatenate(p_rows[g], axis=0)) for g in groups]

    psum2 = jnp.concatenate(psum, axis=1)
    ph = psum2.astype(BF16)
    r1 = psum2 - ph.astype(F32)
    pm = r1.astype(BF16)
    pl_ = (r1 - pm.astype(F32)).astype(BF16)
    ovt = ovt_ref[...]
    pslc = _dot(ovt, ph) + _dot(ovt, pm) + _dot(ovt, pl_)
    ns = pslc.shape[0]
    nb = lax.broadcasted_iota(jnp.int32, (ns, GROUPS * TQ), 0)
    tq = i * TQ + (lax.broadcasted_iota(jnp.int32, (ns, GROUPS * TQ), 1) & (TQ - 1))
    cur = tq // SLC_BLOCK
    forced = (jnp.where(nb == 0, 1.0, 0.0) + jnp.where(nb == cur, 1.0, 0.0) +
              jnp.where(nb == cur - 1, 1.0, 0.0))
    score = jnp.where(nb * SLC_BLOCK > tq, NEG, jnp.where(forced > 0.0, BIG, pslc))
    rank = jnp.zeros((ns, GROUPS * TQ), F32)
    for mth in range(ns):
        sm = score[mth:mth + 1, :]
        tie = jnp.where(nb > mth, 1.0, 0.0)
        rank = rank + jnp.where(sm > score, 1.0, jnp.where(sm == score, tie, 0.0))
    sel_rows = jnp.where(rank < SLC_TOPK, 0.0, NEG)
    for g in groups:
        msk_ref[g] = sel_rows[:, g * TQ:(g + 1) * TQ]

    def sel_add(g, j):
        cache = []

        def tile(hf, p):
            if not cache:
                rows = [jnp.broadcast_to(msk_ref[g, pl.ds(2 * j + e, 1), :], (SLC_BLOCK, TQ)) for e in range(2)]
                cache.append(jnp.concatenate(rows, axis=0))
            return cache[0]
        return tile

    n_full = B_WINDOW // TQ

    def near_tiles(n_back):
        tasks = [tile_task(g, SWA, K_A, i, q4a[g], [near_bias(ta_ref, g, 0)], first=True) for g in groups]
        tasks += [tile_task(g, WIN, K_W, i, q4b[g], [near_bias(tb_ref, g, 0)], first=True) for g in groups]
        tasks += [tile_task(g, SLC, K_S, i, q4b[g], [near_bias(tb_ref, g, 0), sel_add(g, i)], first=True)
                  for g in groups]
        if n_back >= 1:
            j = i - 1
            tasks += [tile_task(g, SWA, K_A, j, q4a[g], [near_bias(ta_ref, g, 1)]) for g in groups]
            tasks += [tile_task(g, WIN, K_W, j, q4b[g], [near_bias(tb_ref, g, 1)]) for g in groups]
            tasks += [tile_task(g, SLC, K_S, j, q4b[g], [near_bias(tb_ref, g, 1), sel_add(g, j)]) for g in groups]
        for d in range(2, n_back + 1):
            edge = [lambda hf, p: up_ref[...]] if d == n_full else []
            tasks += [tile_task(g, WIN, K_W, i - d, q4b[g], edge) for g in groups]
        run_pipelined(tasks)

    for c in range(n_full):
        pl.when(i == c)(functools.partial(near_tiles, c))
    pl.when(i >= n_full)(functools.partial(near_tiles, n_full))

    def far_tasks(d0, count):
        return [tile_task(g, SLC, K_S, i - d, q4b[g], [sel_add(g, i - d)])
                for d in [d0 + e for e in range(count)] for g in groups]

    def far_body(jj, c):
        run_pipelined(far_tasks(2 + FAR_UNROLL * jj, FAR_UNROLL))
        return c

    n_far = jnp.maximum(i - 1, 0)
    lax.fori_loop(0, n_far // FAR_UNROLL, far_body, 0)
    for rem in range(1, FAR_UNROLL):
        pl.when(n_far % FAR_UNROLL == rem)(
            functools.partial(lambda rem: run_pipelined(far_tasks(i - rem + 1, rem)), rem))

    def to_rows(o_t, p):
        return o_t[:, p * TQ:(p + 1) * TQ].T

    far_of = lambda h: tab_ref[N_BUCKETS - 1, h]
    sg = jax.nn.sigmoid(gate_ref[0].astype(F32)).T
    sg_hi = sg.astype(BF16)
    sg_lo = (sg - sg_hi.astype(F32)).astype(BF16)
    gates_t = _dot(e3_ref[...], sg_hi) + _dot(e3_ref[...], sg_lo)
    for g in groups:
        def sink_term(hf, m, g=g):
            row = jnp.concatenate([jnp.full((1, TQ), (sink_ref[8 * g + 2 * p + hf] - far_of(8 * g + 2 * p + hf)) * LOG2E,
                                            F32) for p in range(PAIRS)], axis=1)
            return jnp.exp2(row - m)

        o_swa = finish(g, SWA, sink_term)
        o_slc = finish(g, SLC)
        o_win = finish(g, WIN)
        for p in range(PAIRS):
            cols = slice((PAIRS * g + p) * LANES, (PAIRS * g + p + 1) * LANES)
            oa_ref[0, :, cols] = to_rows(o_swa, p).astype(oa_ref.dtype)
            qs = slice(p * TQ, (p + 1) * TQ)
            gate = lambda br: gates_t[br * B_HEADS * HEAD_DIM + cols.start:br * B_HEADS * HEAD_DIM + cols.stop]
            o_t = gate(0) * o_cmp[g][:, qs] + gate(1) * o_slc[:, qs] + gate(2) * o_win[:, qs]
            ob_ref[0, :, cols] = o_t.T.astype(ob_ref.dtype)


def _even_attention(pe, kvc, rel_bias, sinks, ta, tb, up, cbias, e3, ovt):
    b, s, _ = pe.shape
    n_tiles = s // TQ
    width = GROUPS * PAIRS * LANES

    def seg(k):
        return pl.BlockSpec((1, s, LANES), lambda bi, i, k=k: (bi, 0, k))

    full = lambda a: pl.BlockSpec(a.shape, lambda bi, i: (0,) * a.ndim)
    in_specs = [
        pl.BlockSpec(memory_space=pltpu.SMEM),
        pl.BlockSpec(memory_space=pltpu.SMEM),
        pl.BlockSpec((1, TQ, width), lambda bi, i: (bi, i, COL_QA // width)),
        pl.BlockSpec((1, TQ, width), lambda bi, i: (bi, i, COL_QB // width)),
        pl.BlockSpec((1, TQ, LANES), lambda bi, i: (bi, i, SEG_GATE)),
        seg(SEG_KA), seg(SEG_VA), seg(SEG_KS), seg(SEG_VS), seg(SEG_KW), seg(SEG_VW),
        pl.BlockSpec((1, 1, TQ, LANES), lambda bi, i: (0, bi, 0, 0)),
        pl.BlockSpec((1, 1, TQ, LANES), lambda bi, i: (1, bi, 0, 0)),
        full(ta), full(tb), full(up),
        pl.BlockSpec((B_HEADS, TQ, TQ), lambda bi, i: (0, 0, i)),
        full(e3), full(ovt),
    ]
    out_spec = pl.BlockSpec((1, TQ, width), lambda bi, i: (bi, i, 0))
    return pl.pallas_call(
        functools.partial(_even_attn_kernel, n_tiles=n_tiles),
        grid=(b, n_tiles),
        in_specs=in_specs,
        out_specs=[out_spec, out_spec],
        out_shape=[jax.ShapeDtypeStruct((b, s, width), BF16)] * 2,
        scratch_shapes=[pltpu.VMEM((GROUPS, 3, n_tiles, 2 * TQ, LANES), BF16),
                        pltpu.VMEM((GROUPS, 3, n_tiles, VT_ROWS, 2 * TQ), BF16),
                        pltpu.VMEM((GROUPS, 2 * TQ, LANES), BF16),
                        pltpu.VMEM((GROUPS, LANES, 2 * TQ), BF16),
                        pltpu.VMEM((GROUPS, s // SLC_BLOCK, TQ), F32),
                        pltpu.VMEM((GROUPS, 3, VT_ROWS, W4), F32),
                        pltpu.VMEM((GROUPS, 3, 2, 1, W4), F32)],
        compiler_params=_params("arbitrary", "arbitrary"),
    )(rel_bias, sinks, pe, pe, pe, pe, pe, pe, pe, pe, pe, kvc, kvc, ta, tb, up, cbias, e3, ovt)


def _mla_prep_kernel(x_ref, g_ref, win_ref, qn_ref, kvn_ref, wqt_ref, wk_ref, wvt_ref,
                     cos_ref, sin_ref, cost_ref, sint_ref, qt_ref, k_ref, vt_ref):
    xn = _rms(x_ref[...], g_ref[...]).astype(BF16)
    proj = _dot(xn, win_ref[...])
    cq = _rms(proj[:, :Q_LORA], qn_ref[...])
    ckv = _rms(proj[:, Q_LORA:Q_LORA + KV_LORA], kvn_ref[...])
    cq_t = cq.T.astype(BF16)
    ckv_t = ckv.T.astype(BF16)
    ckv = ckv.astype(BF16)
    kr = proj[:, Q_LORA + KV_LORA:]
    kr = (kr * cos_ref[...] + pltpu.roll(kr, HEAD_DIM, 1) * sin_ref[...]).astype(BF16)
    cos_t = cost_ref[...]
    sin_t = sint_ref[...]
    scale = (NOPE_DIM + ROPE_DIM) ** -0.5 * LOG2E
    tm = ckv_t.shape[1]
    ones_rows = jnp.where(lax.broadcasted_iota(jnp.int32, (VT_ROWS - V_DIM, tm), 0) == 0, 1.0, 0.0).astype(BF16)
    for h in range(C_HEADS):
        q_t = _dot(wqt_ref[h], cq_t)
        rp = q_t[NOPE_DIM:]
        rp = rp * cos_t + pltpu.roll(rp, HEAD_DIM, 0) * sin_t
        qt_ref[0, h, 0, 0:NOPE_DIM, :] = (q_t[:NOPE_DIM] * scale).astype(BF16)
        qt_ref[0, h, 0, NOPE_DIM:, :] = (rp * scale).astype(BF16)
        if h % 2 == 0:
            k2 = _dot(ckv, wk_ref[:, h * LANES:(h + 2) * LANES]).astype(BF16)
        k_ref[:, h * MLA_QK:h * MLA_QK + NOPE_DIM] = k2[:, (h % 2) * LANES:(h % 2 + 1) * LANES]
        k_ref[:, h * MLA_QK + NOPE_DIM:(h + 1) * MLA_QK] = kr
        vt_ref[0, h, 0, 0:V_DIM, :] = _dot(wvt_ref[h], ckv_t).astype(BF16)
        vt_ref[0, h, 0, V_DIM:, :] = ones_rows


def _mla_prep(x, g, w_in, qn, kvn, wqt, wk, wvt, cos, sin, b, s):
    m, d = x.shape
    tm = MLA_TK
    pos_tiles = s // tm
    full = lambda a: pl.BlockSpec(a.shape, lambda i: (0,) * a.ndim, pipeline_mode=pl.Buffered(1))
    tile_t = lambda rows: pl.BlockSpec((1, C_HEADS, 1, rows, tm), lambda i: (i // pos_tiles, 0, i % pos_tiles, 0, 0))
    return pl.pallas_call(
        _mla_prep_kernel,
        grid=(m // tm,),
        in_specs=[pl.BlockSpec((tm, d), lambda i: (i, 0)),
                  pl.BlockSpec((1, d), lambda i: (0, 0)),
                  full(w_in), pl.BlockSpec((1, Q_LORA), lambda i: (0, 0)),
                  pl.BlockSpec((1, KV_LORA), lambda i: (0, 0)), full(wqt), full(wk), full(wvt),
                  pl.BlockSpec((tm, LANES), lambda i: (i % pos_tiles, 0)),
                  pl.BlockSpec((tm, LANES), lambda i: (i % pos_tiles, 0)),
                  pl.BlockSpec((LANES, tm), lambda i: (0, i % pos_tiles)),
                  pl.BlockSpec((LANES, tm), lambda i: (0, i % pos_tiles))],
        out_specs=[tile_t(MLA_QK),
                   pl.BlockSpec((tm, C_HEADS * MLA_QK), lambda i: (i, 0)),
                   tile_t(VT_ROWS)],
        out_shape=[jax.ShapeDtypeStruct((b, C_HEADS, pos_tiles, MLA_QK, tm), BF16),
                   jax.ShapeDtypeStruct((m, C_HEADS * MLA_QK), BF16),
                   jax.ShapeDtypeStruct((b, C_HEADS, pos_tiles, VT_ROWS, tm), BF16)],
        compiler_params=_params("parallel"),
    )(x, g.reshape(1, d), w_in, qn.reshape(1, -1), kvn.reshape(1, -1), wqt, wk, wvt, cos, sin, cos.T, sin.T)


def _mla_attn_kernel(qt_ref, k_ref, vt_ref, o_ref, s_scr, acc_scr, m_scr, *, n_q):
    tq = tk = MLA_TQ
    parts = tq // MLA_TK
    ki = lax.broadcasted_iota(jnp.int32, (tk, tq), 0)
    qi = lax.broadcasted_iota(jnp.int32, (tk, tq), 1)
    causal = ki <= qi
    heads = range(MLA_HEADS)

    def q_tile(t):
        q_ts = [jnp.concatenate([qt_ref[0, h, parts * t + d] for d in range(parts)], axis=1) for h in heads]

        def logits_to(slot, j, hs=heads):
            rows = pl.ds(j * tk, tk)
            for h in hs:
                s_scr[h, slot] = _dot(k_ref[0, rows, h * MLA_QK:(h + 1) * MLA_QK], q_ts[h])

        def update_from(slot, j, diagonal, hs=heads):
            for h in hs:
                s = s_scr[h, slot]
                if diagonal:
                    s = jnp.where(causal, s, NEG)
                m = m_scr[h]
                mn = jnp.maximum(m, jnp.max(s, axis=0, keepdims=True))
                p = jnp.exp2(s - mn).astype(BF16)
                m_scr[h] = mn
                v_t = jnp.concatenate([vt_ref[0, h, parts * j + d] for d in range(parts)], axis=1)
                acc_scr[h] = jnp.exp2(m - mn) * acc_scr[h] + _dot(v_t, p)

        m_scr[...] = jnp.full(m_scr.shape, NEG, F32)
        acc_scr[...] = jnp.zeros(acc_scr.shape, F32)
        logits_to(0, t)
        for h in heads:
            if t > 0:
                logits_to(1, 0, [h])
            update_from(0, t, True, [h])

        for jj in range(t // 2):
            for h in heads:
                logits_to(0, 2 * jj + 1, [h])
                update_from(1, 2 * jj, False, [h])
            for h in heads:
                if 2 * jj + 2 < t:
                    logits_to(1, 2 * jj + 2, [h])
                update_from(0, 2 * jj + 1, False, [h])
        if t % 2 == 1:
            update_from(1, t - 1, False)
        rows = pl.ds(t * tq, tq)
        for h in heads:
            acc = acc_scr[h]
            o = acc[:V_DIM] * (1.0 / acc[V_DIM:V_DIM + 1])
            o_ref[0, rows, h * LANES:(h + 1) * LANES] = o.T.astype(o_ref.dtype)

    for t in range(n_q):
        q_tile(t)


def _mla_attention(qt, k, vt):
    b, s, _ = k.shape
    n_k = s // MLA_TK
    nh = MLA_HEADS
    return pl.pallas_call(
        functools.partial(_mla_attn_kernel, n_q=s // MLA_TQ),
        grid=(b, C_HEADS // nh),
        in_specs=[pl.BlockSpec((1, nh, n_k, MLA_QK, MLA_TK), lambda bi, h: (bi, h, 0, 0, 0)),
                  pl.BlockSpec((1, s, MLA_QK * nh), lambda bi, h: (bi, 0, h)),
                  pl.BlockSpec((1, nh, n_k, VT_ROWS, MLA_TK), lambda bi, h: (bi, h, 0, 0, 0))],
        out_specs=pl.BlockSpec((1, s, LANES * nh), lambda bi, h: (bi, 0, h)),
        out_shape=jax.ShapeDtypeStruct((b, s, C_HEADS * V_DIM), BF16),
        scratch_shapes=[pltpu.VMEM((nh, 2, MLA_TQ, MLA_TQ), F32),
                        pltpu.VMEM((nh, VT_ROWS, MLA_TQ), F32),
                        pltpu.VMEM((nh, 1, MLA_TQ), F32)],
        compiler_params=_params("parallel", "arbitrary"),
    )(qt, k, vt)


def _even_in_weight(w):
    cuts = np.cumsum([1024, 128, 128, 1024, 128, 128, 128, 128, 128, 128, 48])[:-1]
    qa, ka, va, qb, kc, vc, ks, vs, kw, vw, gate = jnp.split(w.astype(BF16), [int(c) for c in cuts], axis=1)
    gate = jnp.pad(gate, ((0, 0), (0, LANES - gate.shape[1])))
    return jnp.concatenate([qa, qb, ka, va, kc, vc, ks, vs, kw, vw, gate], axis=1)


def _compress_weights(pos, w1, w2):
    half = CMP_BLOCK // 2
    eye = jnp.eye(GROUPS, dtype=BF16)
    w1r = w1.astype(BF16).reshape(2, half, HEAD_DIM, CMP_HIDDEN)
    wab = jnp.einsum('rldh,gk->rlgdkh', w1r, eye).reshape(2, half * GROUPS * HEAD_DIM, GROUPS * CMP_HIDDEN)
    posr = jnp.broadcast_to(pos.reshape(2, half, 1, HEAD_DIM), (2, half, GROUPS, HEAD_DIM)).reshape(2, 1, -1)
    w2bd = jnp.einsum('hd,gk->ghkd', w2.astype(BF16), eye).reshape(GROUPS * CMP_HIDDEN, GROUPS * HEAD_DIM)
    return posr[0], posr[1], wab[0], wab[1], w2bd


def _rope_chunk_cols(w):
    z = jnp.zeros(w.shape[:-1] + (ROPE_DIM // 2,), w.dtype)
    return jnp.concatenate([w[..., :ROPE_DIM // 2], z, w[..., ROPE_DIM // 2:], z], axis=-1)


def _mla_weights(w_in, w_q_up, w_kv_up):
    w_in, w_q_up, w_kv_up = w_in.astype(BF16), w_q_up.astype(BF16), w_kv_up.astype(BF16)
    w_in2 = jnp.concatenate([w_in[:, :Q_LORA + KV_LORA], _rope_chunk_cols(w_in[:, Q_LORA + KV_LORA:])], axis=1)
    wq = w_q_up.reshape(Q_LORA, C_HEADS, NOPE_DIM + ROPE_DIM)
    wq2 = jnp.concatenate([wq[..., :NOPE_DIM], _rope_chunk_cols(wq[..., NOPE_DIM:])], axis=-1)
    wqt = wq2.transpose(1, 2, 0)
    wkv = w_kv_up.reshape(KV_LORA, C_HEADS, NOPE_DIM + V_DIM)
    wk = wkv[..., :NOPE_DIM].reshape(KV_LORA, C_HEADS * NOPE_DIM)
    wvt = wkv[..., NOPE_DIM:].transpose(1, 2, 0)
    return w_in2, wqt, wk, wvt


def _rope_tables(s):
    inv = 1.0 / (ROPE_THETA ** (jnp.arange(0, ROPE_DIM, 2, dtype=F32) / ROPE_DIM))
    ang = jnp.arange(s, dtype=F32)[:, None] * inv[None]
    cos, sin = jnp.cos(ang), jnp.sin(ang)
    z = jnp.zeros_like(cos)
    return jnp.concatenate([cos, z, cos, z], axis=1), jnp.concatenate([-sin, z, sin, z], axis=1)


def _static_tables(s):
    k = np.arange(TQ)[:, None]
    q = np.arange(TQ)[None, :]
    diag = np.where(k <= q, _bucket_np(q - k), -1)
    prev = _bucket_np(q - k + TQ)
    near_swa = np.stack([diag, np.where(k > q, prev, -1)])
    near_nsa = np.stack([diag, prev])
    upper = np.where(k > q, 0.0, NEG).astype(np.float32)
    c = np.arange(LANES)[:, None]
    t = np.arange(s)[None, :]
    cdist = t - (c * CMP_STRIDE + CMP_BLOCK - 1)
    cmp_map = np.where(cdist >= 0, _bucket_np(cdist), -1)
    ns = s // SLC_BLOCK
    nc = (s - CMP_BLOCK) // CMP_STRIDE + 1
    c_start = np.arange(LANES) * CMP_STRIDE
    s_start = np.arange(ns) * SLC_BLOCK
    ovt = ((c_start[None, :] <= s_start[:, None] + SLC_BLOCK - 1) &
           (c_start[None, :] + CMP_BLOCK - 1 >= s_start[:, None]) &
           (np.arange(LANES)[None, :] < nc)).astype(np.float32)
    e3 = np.zeros((3, B_HEADS * HEAD_DIM, LANES), np.float32)
    for h in range(B_HEADS):
        for br in range(3):
            e3[br, h * HEAD_DIM:(h + 1) * HEAD_DIM, h * 3 + br] = 1.0
    e3 = e3.reshape(3 * B_HEADS * HEAD_DIM, LANES)
    return near_swa, near_nsa, upper, cmp_map, ovt, e3


def _even_layer(x, b, s, rel_bias, norm, w_in, sinks, pos_k, pos_v, k_w1, k_w2, v_w1, v_w2, w_out):
    near_swa, near_nsa, upper, cmp_map, ovt, e3 = _static_tables(s)
    ta = _bias_tiles(rel_bias, near_swa, 0, A_HEADS)
    tb = _bias_tiles(rel_bias, near_nsa, A_HEADS, B_HEADS)
    cbias = _cmp_bias(rel_bias, cmp_map, A_HEADS, B_HEADS)

    pe = _norm_matmul(x, norm, _even_in_weight(w_in), IN_PROJ_TM, EVEN_COLS, BF16).reshape(b, s, EVEN_COLS)

    rows = s // (CMP_BLOCK // 2)
    hkv = jnp.stack([pe[:, :, SEG_KC * LANES:(SEG_KC + 1) * LANES],
                     pe[:, :, SEG_VC * LANES:(SEG_VC + 1) * LANES]]).reshape(2, b, rows, -1)
    ck = _compress_weights(pos_k, k_w1, k_w2)
    cv = _compress_weights(pos_v, v_w1, v_w2)
    kvc = _compress(hkv, *[jnp.stack([a, c]) for a, c in zip(ck, cv)])
    assert rows <= TQ
    kvc = jnp.pad(kvc, ((0, 0), (0, 0), (0, TQ - rows), (0, 0)))

    oa, ob = _even_attention(pe, kvc, rel_bias, sinks, ta, tb, jnp.asarray(upper), cbias,
                             jnp.asarray(e3, BF16), jnp.asarray(ovt, BF16))
    w_out = w_out.astype(BF16)
    n_a = A_HEADS * HEAD_DIM
    return _out_proj(x, [oa.reshape(b * s, -1), ob.reshape(b * s, -1)], [w_out[:n_a], w_out[n_a:]], OUT_PROJ_TM)


def _odd_layer(x, b, s, norm, w_in, q_norm, w_q_up, kv_norm, w_kv_up, w_out):
    w_in2, wqt, wk, wvt = _mla_weights(w_in, w_q_up, w_kv_up)
    cos, sin = _rope_tables(s)
    qt, k, vt = _mla_prep(x, norm, w_in2, q_norm, kv_norm, wqt, wk, wvt, cos, sin, b, s)
    o = _mla_attention(qt, k.reshape(b, s, -1), vt)
    return _out_proj(x, [o.reshape(b * s, -1)], [w_out.astype(BF16)], OUT_PROJ_TM)


def kernel(x, rel_bias, norm_mix_e, w_in_e, sinks, cmp_pos_k, cmp_pos_v, cmp_k_w1, cmp_k_w2, cmp_v_w1, cmp_v_w2, w_out_e, norm_mix_o, w_in_o, q_norm, w_q_up, kv_norm, w_kv_up, w_out_o, norm_mlp, w_up, w_down, norm_final):
    b, s, d = x.shape
    depth = norm_mlp.shape[0]
    assert s % MLA_TQ == 0 and s // (CMP_BLOCK // 2) <= TQ and s // SLC_BLOCK <= LANES
    assert (b * s) % MLP_TM == 0 and (b * s) % IN_PROJ_TM == 0 and w_up.shape[2] % MLP_TF == 0
    h = x.reshape(b * s, d)
    for layer in range(depth):
        i = layer // 2
        if layer % 2 == 0:
            h = _even_layer(h, b, s, rel_bias, norm_mix_e[i], w_in_e[i], sinks[i], cmp_pos_k[i], cmp_pos_v[i],
                            cmp_k_w1[i], cmp_k_w2[i], cmp_v_w1[i], cmp_v_w2[i], w_out_e[i])
        else:
            h = _odd_layer(h, b, s, norm_mix_o[i], w_in_o[i], q_norm[i], w_q_up[i], kv_norm[i], w_kv_up[i],
                           w_out_o[i])
        h = _mlp(h, norm_mlp[layer], w_up, w_down, layer, norm_final, MLP_TM, MLP_TF, layer == depth - 1)
    return h.reshape(b, s, d)
```
